```python
import math
import jax
import jax.numpy as jnp
from jax import lax
import numpy as np

D_MODEL = 2048
BATCH = 1
SEQ = 8192
DEPTH = 1
DEC_BATCH = 32
DEC_SEQ = 4
PAST_LEN = 16384
PAGE_SIZE = 128

D_MIX = D_MODEL
D_ATTN = D_MIX // 2
D_CHUNK = D_MIX - D_ATTN
HEAD_DIM = 64
N_HEADS_A = D_ATTN // HEAD_DIM
CHUNK = 128
GROUP_WIDTH_B = 128
N_GROUPS_B = D_CHUNK // GROUP_WIDTH_B
DILATIONS = ((128, 1), (512, 4), (2048, 16))
MAX_WINDOW = 2048
BLOCK = 128
ROPE_THETA = 10000.0
EPS = 1e-6
NEG_INF = -1e30
SPLITS = (D_ATTN, 2 * D_ATTN, 3 * D_ATTN, 4 * D_ATTN, 4 * D_ATTN + D_CHUNK, 4 * D_ATTN + 2 * D_CHUNK)
D_IN = 4 * D_ATTN + 3 * D_CHUNK

kernel_name = 'hybrid_dilated_attn_chunk_gmlp_step'


def _rms_norm(x, g):
    xf = x.astype(jnp.float32)
    y = xf * lax.rsqrt(jnp.mean(xf * xf, axis=-1, keepdims=True) + EPS)
    return (y * g.astype(jnp.float32)).astype(x.dtype)


def _layer_norm(x, g, b):
    xf = x.astype(jnp.float32)
    mu = jnp.mean(xf, axis=-1, keepdims=True)
    var = jnp.mean(jnp.square(xf - mu), axis=-1, keepdims=True)
    y = (xf - mu) * lax.rsqrt(var + EPS)
    return (y * g.astype(jnp.float32) + b.astype(jnp.float32)).astype(x.dtype)


def _rope(x, pos):
    half = HEAD_DIM // 2
    inv = jnp.exp(-math.log(ROPE_THETA) * jnp.arange(half, dtype=jnp.float32) / half)
    ang = pos.astype(jnp.float32)[:, None] * inv[None, :]
    cos = jnp.cos(ang)[None, :, None, :]
    sin = jnp.sin(ang)[None, :, None, :]
    xf = x.astype(jnp.float32)
    x1, x2 = xf[..., :half], xf[..., half:]
    return jnp.concatenate([x1 * cos - x2 * sin, x2 * cos + x1 * sin], axis=-1).astype(x.dtype)


def _branch_inputs(x, pos, norm_g, w_in, ln_g, ln_b):
    B, T, _ = x.shape
    z = _rms_norm(x, norm_g) @ w_in
    q, k, v, ga, u, vc, gb = jnp.split(z, SPLITS, axis=-1)
    hs = (B, T, N_HEADS_A, HEAD_DIM)
    q = _rope(q.reshape(hs), pos)
    k = _rope(k.reshape(hs), pos)
    v = v.reshape(hs)
    u = jax.nn.gelu(u)
    vc = _layer_norm(jax.nn.gelu(vc), ln_g, ln_b)
    return q, k, v, ga, u, vc, gb


def _softmax_lse(s, mask):
    s = jnp.where(mask, s, NEG_INF)
    m = jnp.max(s, axis=-1, keepdims=True)
    p = jnp.exp(s - m)
    den = jnp.sum(p, axis=-1, keepdims=True)
    return p / den, (m + jnp.log(den))[..., 0]


def _combine(outs, lses):
    w = jax.nn.softmax(jnp.stack(lses, axis=0), axis=0)
    return jnp.sum(w[..., None] * jnp.stack(outs, axis=0), axis=0)


def _dilated_attention_prompt(q, k, v):
    B, S, H, hd = q.shape
    scale = HEAD_DIM ** -0.5
    qi = jnp.arange(BLOCK)[:, None]
    si = jnp.arange(2 * BLOCK)[None, :]
    dist = qi + BLOCK - si
    outs, lses = [], []
    for window, d in DILATIONS:
        span = window // d
        L = S // d
        nb = -(-L // BLOCK)
        Lp = nb * BLOCK

        def to_res(a):
            a = a.reshape(B, L, d, H, hd).transpose(0, 2, 1, 3, 4)
            a = jnp.pad(a, ((0, 0), (0, 0), (0, Lp - L), (0, 0), (0, 0)))
            return a.reshape(B, d, nb, BLOCK, H, hd).astype(jnp.float32)

        qb, kb, vb = to_res(q), to_res(k), to_res(v)
        pad_prev = ((0, 0), (0, 0), (1, 0), (0, 0), (0, 0), (0, 0))
        kcat = jnp.concatenate([jnp.pad(kb, pad_prev)[:, :, :nb], kb], axis=3)
        vcat = jnp.concatenate([jnp.pad(vb, pad_prev)[:, :, :nb], vb], axis=3)
        s = jnp.einsum('brnqhk,brnshk->brnhqs', qb, kcat) * scale
        band = (dist >= 0) & (dist <= span)
        has_prev = (jnp.arange(nb) > 0)[:, None, None] | (si >= BLOCK)[None]
        mask = (band[None] & has_prev)[None, None, :, None]
        p, lse = _softmax_lse(s, mask)
        o = jnp.einsum('brnhqs,brnshk->brnqhk', p, vcat)
        o = o.reshape(B, d, Lp, H, hd)[:, :, :L].transpose(0, 2, 1, 3, 4).reshape(B, S, H, hd)
        lse = lse.transpose(0, 1, 2, 4, 3).reshape(B, d, Lp, H)[:, :, :L].transpose(0, 2, 1, 3).reshape(B, S, H)
        outs.append(o)
        lses.append(lse)
    return _combine(outs, lses)


def _dilated_attention_sample(q, k_all, v_all, wb):
    T = q.shape[1]
    scale = HEAD_DIM ** -0.5
    qf = q.astype(jnp.float32)
    outs, lses = [], []
    for window, d in DILATIONS:
        J = window // d + 1
        idx = wb + jnp.arange(T)[:, None] - jnp.arange(J)[None, :] * d
        valid = idx >= 0
        idxc = jnp.maximum(idx, 0)
        kg = k_all[:, idxc].astype(jnp.float32)
        vg = v_all[:, idxc].astype(jnp.float32)
        s = jnp.einsum('bthk,btjhk->bthj', qf, kg) * scale
        p, lse = _softmax_lse(s, valid[None, :, None, :])
        outs.append(jnp.einsum('bthj,btjhk->bthk', p, vg))
        lses.append(lse)
    return _combine(outs, lses)


def _masked_spatial(w_s):
    return w_s * jnp.tril(jnp.ones((CHUNK, CHUNK), w_s.dtype))[None]


def _chunk_mlp_prompt(u, vc, w_s, b_s):
    B, S, _ = u.shape
    vr = vc.reshape(B, S // CHUNK, CHUNK, N_GROUPS_B, GROUP_WIDTH_B)
    mixed = jnp.einsum('gts,bcsgk->bctgk', _masked_spatial(w_s), vr) + jnp.transpose(b_s)[:, :, None]
    return u * mixed.reshape(B, S, D_CHUNK)


def _chunk_mlp_sample(u, vc, w_s, b_s):
    B, T, _ = u.shape
    vr = vc.reshape(B, T, N_GROUPS_B, GROUP_WIDTH_B)
    wm = _masked_spatial(w_s)[:, :T, :T]
    mixed = jnp.einsum('gts,bsgk->btgk', wm, vr) + jnp.transpose(b_s[:, :T])[:, :, None]
    return u * mixed.reshape(B, T, D_CHUNK)


def _output(o_attn, ga, o_chunk, gb, x, w_out):
    B, T, _ = x.shape
    a = o_attn.reshape(B, T, D_ATTN).astype(x.dtype) * jax.nn.silu(ga)
    c = o_chunk * jax.nn.silu(gb)
    return x + jnp.concatenate([a, c], axis=-1) @ w_out


def setup_inputs(seed: int = 0) -> dict:
    key = jax.random.key(seed)
    ks = jax.random.split(key, 14)
    wb = min(MAX_WINDOW, PAST_LEN)
    nrm = jax.random.normal
    f32 = jnp.float32
    return {
        'x_prompt': nrm(ks[0], (BATCH, SEQ, D_MODEL), f32),
        'x_sample': nrm(ks[1], (DEC_BATCH, DEC_SEQ, D_MODEL), f32),
        'cache_k': nrm(ks[2], (DEPTH, DEC_BATCH, wb, N_HEADS_A, HEAD_DIM), f32),
        'cache_v': nrm(ks[3], (DEPTH, DEC_BATCH, wb, N_HEADS_A, HEAD_DIM), f32),
        'norm_g': 1.0 + 0.02 * nrm(ks[4], (DEPTH, D_MODEL), f32),
        'w_in': nrm(ks[5], (DEPTH, D_MODEL, D_IN), f32) * D_MODEL ** -0.5,
        'ln_g': 1.0 + 0.02 * nrm(ks[6], (DEPTH, D_CHUNK), f32),
        'ln_b': 0.02 * nrm(ks[7], (DEPTH, D_CHUNK), f32),
        'w_s': nrm(ks[8], (DEPTH, N_GROUPS_B, CHUNK, CHUNK), f32) * CHUNK ** -0.5,
        'b_s': 1.0 + 0.02 * nrm(ks[9], (DEPTH, N_GROUPS_B, CHUNK), f32),
        'w_out': nrm(ks[10], (DEPTH, D_MIX, D_MODEL), f32) * D_MIX ** -0.5,
        'final_g': 1.0 + 0.02 * nrm(ks[11], (D_MODEL,), f32),
    }


def reference(x_prompt, x_sample, cache_k, cache_v, norm_g, w_in, ln_g, ln_b, w_s, b_s, w_out, final_g):
    S = x_prompt.shape[1]
    T = x_sample.shape[1]
    wb = cache_k.shape[2]
    tail = min(MAX_WINDOW, S)
    pos_p = jnp.arange(S, dtype=jnp.int32)
    pos_s = PAST_LEN + jnp.arange(T, dtype=jnp.int32)
    xp, xs = x_prompt, x_sample
    kp_rows, vp_rows, ks_rows, vs_rows, cs_rows = [], [], [], [], []
    for l in range(DEPTH):
        q, k, v, ga, u, vc, gb = _branch_inputs(xp, pos_p, norm_g[l], w_in[l], ln_g[l], ln_b[l])
        oa = _dilated_attention_prompt(q, k, v)
        oc = _chunk_mlp_prompt(u, vc, w_s[l], b_s[l])
        xp = _output(oa, ga, oc, gb, xp, w_out[l])
        kp_rows.append(k[:, S - tail:])
        vp_rows.append(v[:, S - tail:])
        q, k, v, ga, u, vc, gb = _branch_inputs(xs, pos_s, norm_g[l], w_in[l], ln_g[l], ln_b[l])
        k_all = jnp.concatenate([cache_k[l].astype(k.dtype), k], axis=1)
        v_all = jnp.concatenate([cache_v[l].astype(v.dtype), v], axis=1)
        oa = _dilated_attention_sample(q, k_all, v_all, wb)
        oc = _chunk_mlp_sample(u, vc, w_s[l], b_s[l])
        xs = _output(oa, ga, oc, gb, xs, w_out[l])
        ks_rows.append(k)
        vs_rows.append(v)
        cs_rows.append(vc)
    y_prompt = _rms_norm(xp, final_g)
    y_sample = _rms_norm(xs, final_g)
    return (y_prompt, y_sample, jnp.stack(kp_rows), jnp.stack(vp_rows), jnp.stack(ks_rows), jnp.stack(vs_rows), jnp.stack(cs_rows))
```

```python
import functools
import math

import jax
import jax.numpy as jnp
from jax import lax
from jax.experimental import pallas as pl
from jax.experimental.pallas import tpu as pltpu

HEAD_DIM = 64
BLOCK = 128
CHUNK = 128
GROUP_WIDTH_B = 128
DILATIONS = ((128, 1), (512, 4), (2048, 16))
MAX_WINDOW = 2048
PAST_LEN = 16384
ROPE_THETA = 10000.0
EPS = 1e-6
NEG_INF = -1e30
N_SEGMENTS = 7

LANES = 128
VMEM_LIMIT_BYTES = 56 * 1024 * 1024

_BF16 = jnp.bfloat16
_F32 = jnp.float32


def _rope_store(z, cos, sin_signed, scale, out_refs):
    tm = z.shape[0]
    lane = lax.broadcasted_iota(jnp.int32, (tm, LANES), 1)
    first_half = (lane % HEAD_DIM) < (HEAD_DIM // 2)
    for g in range(z.shape[1] // LANES):
        sl = slice(g * LANES, (g + 1) * LANES)
        zg = z[:, sl]
        partner = jnp.where(first_half, pltpu.roll(zg, LANES - HEAD_DIM // 2, 1),
                            pltpu.roll(zg, HEAD_DIM // 2, 1))
        r = zg * cos + partner * sin_signed
        if scale != 1.0:
            r = r * scale
        for ref in out_refs:
            ref[:, sl] = r.astype(ref.dtype)


def _in_proj_kernel(x_ref, g_ref, w_ref, cos_ref, sin_ref, lng_ref, lnb_ref, *refs, tail_start):
    if tail_start is None:
        q_ref, k_ref, v_ref, ga_ref, u_ref, vc_ref, gb_ref, xn_ref = refs
        kt_ref = vt_ref = None
    else:
        q_ref, k_ref, v_ref, ga_ref, u_ref, vc_ref, gb_ref, kt_ref, vt_ref, xn_ref = refs
    i = pl.program_id(0)
    j = pl.program_id(1)

    @pl.when(j == 0)
    def _():
        xf = x_ref[...]
        ms = jnp.mean(xf * xf, axis=-1, keepdims=True)
        xn_ref[...] = (xf * lax.rsqrt(ms + EPS) * g_ref[...]).astype(xn_ref.dtype)

    z = jnp.dot(xn_ref[...], w_ref[...], preferred_element_type=_F32)

    @pl.when(j == 0)
    def _():
        _rope_store(z, cos_ref[...], sin_ref[...], HEAD_DIM ** -0.5, [q_ref])

    @pl.when(j == 1)
    def _():
        _rope_store(z, cos_ref[...], sin_ref[...], 1.0, [k_ref])
        if kt_ref is not None:
            @pl.when(i >= tail_start)
            def _():
                _rope_store(z, cos_ref[...], sin_ref[...], 1.0, [kt_ref])

    @pl.when(j == 2)
    def _():
        v_ref[...] = z.astype(v_ref.dtype)
        if vt_ref is not None:
            @pl.when(i >= tail_start)
            def _():
                vt_ref[...] = z

    @pl.when(j == 3)
    def _():
        ga_ref[...] = jax.nn.silu(z).astype(ga_ref.dtype)

    @pl.when(j == 4)
    def _():
        u_ref[...] = jax.nn.gelu(z).astype(u_ref.dtype)

    @pl.when(j == 5)
    def _():
        h = jax.nn.gelu(z)
        mu = jnp.mean(h, axis=-1, keepdims=True)
        hc = h - mu
        var = jnp.mean(hc * hc, axis=-1, keepdims=True)
        y = hc * lax.rsqrt(var + EPS) * lng_ref[...] + lnb_ref[...]
        vc_ref[...] = y.astype(vc_ref.dtype)

    @pl.when(j == 6)
    def _():
        gb_ref[...] = jax.nn.silu(z).astype(gb_ref.dtype)


def _in_proj(x, norm_g, w_bf16, cos_t, sin_t, ln_g, ln_b, *, tm, out_dtype, tail_rows):
    m, d_model = x.shape
    d_seg = w_bf16.shape[1] // N_SEGMENTS
    n_i = m // tm
    seg_shape = jax.ShapeDtypeStruct((m, d_seg), out_dtype)
    seg_spec = pl.BlockSpec((tm, d_seg), lambda i, j: (i, 0))
    out_shape = [seg_shape] * N_SEGMENTS
    out_specs = [seg_spec] * N_SEGMENTS
    tail_start = None
    if tail_rows:
        tail_start = n_i - tail_rows // tm
        tail_shape = jax.ShapeDtypeStruct((tail_rows, d_seg), _F32)
        tail_spec = pl.BlockSpec((tm, d_seg), lambda i, j: (jnp.maximum(i - tail_start, 0), 0))
        out_shape += [tail_shape] * 2
        out_specs += [tail_spec] * 2
    row_vec = lambda n: pl.BlockSpec((1, n), lambda i, j: (0, 0))
    return pl.pallas_call(
        functools.partial(_in_proj_kernel, tail_start=tail_start),
        grid=(n_i, N_SEGMENTS),
        in_specs=[
            pl.BlockSpec((tm, d_model), lambda i, j: (i, 0)),
            row_vec(d_model),
            pl.BlockSpec((d_model, d_seg), lambda i, j: (0, j)),
            pl.BlockSpec((tm, LANES), lambda i, j: (i, 0)),
            pl.BlockSpec((tm, LANES), lambda i, j: (i, 0)),
            row_vec(d_seg),
            row_vec(d_seg),
        ],
        out_specs=out_specs,
        out_shape=out_shape,
        scratch_shapes=[pltpu.VMEM((tm, d_model), _BF16)],
        compiler_params=pltpu.CompilerParams(
            dimension_semantics=("arbitrary", "arbitrary"), vmem_limit_bytes=VMEM_LIMIT_BYTES),
        name="in_proj",
    )(x, norm_g.reshape(1, -1), w_bf16, cos_t, sin_t, ln_g.reshape(1, -1), ln_b.reshape(1, -1))


def _rope_tables(pos):
    half = HEAD_DIM // 2
    inv = jnp.exp(-math.log(ROPE_THETA) * jnp.arange(half, dtype=_F32) / half)
    ang = pos.astype(_F32)[:, None] * inv[None, :]
    cos = jnp.cos(ang)
    sin = jnp.sin(ang)
    cos_t = jnp.concatenate([cos, cos, cos, cos], axis=-1)
    sin_t = jnp.concatenate([-sin, sin, -sin, sin], axis=-1)
    return cos_t, sin_t


def _attn_kernel(q_ref, kp_ref, kc_ref, vp_ref, vc_ref, o_ref, lse_ref):
    n = pl.program_id(1)
    two = 2 * BLOCK
    qi = lax.broadcasted_iota(jnp.int32, (two, two), 0) % BLOCK
    si = lax.broadcasted_iota(jnp.int32, (two, two), 1)
    dist = qi + BLOCK - si
    mask = (dist >= 0) & (dist <= BLOCK) & jnp.logical_or(si >= BLOCK, n > 0)
    lane = lax.broadcasted_iota(jnp.int32, (BLOCK, LANES), 1)
    head0 = lane < HEAD_DIM
    lse_acc = jnp.zeros((BLOCK, LANES), _F32)
    zero = jnp.zeros((BLOCK, LANES), q_ref.dtype)
    for hp in range(q_ref.shape[1] // LANES):
        sl = slice(hp * LANES, (hp + 1) * LANES)
        q2 = q_ref[:, sl]
        qs = jnp.concatenate([jnp.where(head0, q2, zero), jnp.where(head0, zero, q2)], axis=0)
        k2 = jnp.concatenate([kp_ref[:, sl], kc_ref[:, sl]], axis=0)
        v2 = jnp.concatenate([vp_ref[:, sl], vc_ref[:, sl]], axis=0)
        s = lax.dot_general(qs, k2, (((1,), (1,)), ((), ())), preferred_element_type=_F32)
        s = jnp.where(mask, s, NEG_INF)
        m = jnp.max(s, axis=-1, keepdims=True)
        p = jnp.exp(s - m)
        den = jnp.sum(p, axis=-1, keepdims=True)
        pv = jnp.dot(p.astype(v2.dtype), v2, preferred_element_type=_F32)
        o2 = pv / den
        o_ref[:, sl] = jnp.where(head0, o2[:BLOCK], o2[BLOCK:]).astype(o_ref.dtype)
        l2 = m + jnp.log(den)
        lse_acc = jnp.where(lane == 2 * hp, l2[:BLOCK],
                            jnp.where(lane == 2 * hp + 1, l2[BLOCK:], lse_acc))
    lse_ref[...] = lse_acc


def _dilated_attention(q, k, v, d):
    s_len, d_attn = q.shape
    sub_len = s_len // d
    nb = sub_len // BLOCK
    qr, kr, vr = (a.reshape(sub_len, d * d_attn) for a in (q, k, v))
    cur = pl.BlockSpec((BLOCK, d_attn), lambda r, n: (n, r))
    prev = pl.BlockSpec((BLOCK, d_attn), lambda r, n: (jnp.maximum(n - 1, 0), r))
    o, lse = pl.pallas_call(
        _attn_kernel,
        grid=(d, nb),
        in_specs=[cur, prev, cur, prev, cur],
        out_specs=[cur, pl.BlockSpec((BLOCK, LANES), lambda r, n: (n, r))],
        out_shape=[jax.ShapeDtypeStruct((sub_len, d * d_attn), q.dtype),
                   jax.ShapeDtypeStruct((sub_len, d * LANES), _F32)],
        compiler_params=pltpu.CompilerParams(
            dimension_semantics=("arbitrary", "arbitrary"), vmem_limit_bytes=VMEM_LIMIT_BYTES),
        name=f"dilated_attn_d{d}",
    )(qr, kr, kr, vr, vr)
    return o.reshape(s_len, d_attn), lse.reshape(s_len, LANES)


def _expand_heads(w, expand):
    hi = w.astype(_BF16)
    lo = (w - hi.astype(_F32)).astype(_BF16)
    return (jnp.dot(hi, expand, preferred_element_type=_F32)
            + jnp.dot(lo, expand, preferred_element_type=_F32))


def _out_proj_kernel(o1_ref, o2_ref, o3_ref, l1_ref, l2_ref, l3_ref, ga_ref, u_ref, vc_ref, gb_ref,
                     x_ref, wout_ref, ws_ref, bst_ref, fg_ref, y_ref, a_ref, c_ref):
    tm, d_attn = o1_ref.shape
    lses = [l1_ref[...], l2_ref[...], l3_ref[...]]
    mx = jnp.maximum(jnp.maximum(lses[0], lses[1]), lses[2])
    es = [jnp.exp(l - mx) for l in lses]
    tot = es[0] + es[1] + es[2]
    head = lax.broadcasted_iota(jnp.int32, (LANES, d_attn), 0)
    col_head = lax.broadcasted_iota(jnp.int32, (LANES, d_attn), 1) // HEAD_DIM
    expand = (head == col_head).astype(_BF16)
    a = jnp.zeros((tm, d_attn), _F32)
    for e, o_ref in zip(es, (o1_ref, o2_ref, o3_ref)):
        a = a + _expand_heads(e / tot, expand) * o_ref[...].astype(_F32)
    a_ref[...] = (a * ga_ref[...].astype(_F32)).astype(a_ref.dtype)

    n_groups = ws_ref.shape[0]
    row = lax.broadcasted_iota(jnp.int32, (CHUNK, CHUNK), 0)
    col = lax.broadcasted_iota(jnp.int32, (CHUNK, CHUNK), 1)
    tril = row >= col
    for g in range(n_groups):
        wm = jnp.where(tril, ws_ref[g], 0.0).astype(_BF16)
        bias = bst_ref[:, g:g + 1]
        gs = slice(g * GROUP_WIDTH_B, (g + 1) * GROUP_WIDTH_B)
        for c in range(tm // CHUNK):
            rs = slice(c * CHUNK, (c + 1) * CHUNK)
            mixed = jnp.dot(wm, vc_ref[rs, gs], preferred_element_type=_F32) + bias
            cv = u_ref[rs, gs].astype(_F32) * mixed * gb_ref[rs, gs].astype(_F32)
            c_ref[rs, gs] = cv.astype(c_ref.dtype)

    acc = jnp.dot(a_ref[...], wout_ref[:d_attn, :], preferred_element_type=_F32)
    acc = acc + jnp.dot(c_ref[...], wout_ref[d_attn:, :], preferred_element_type=_F32)
    xo = x_ref[...] + acc
    ms = jnp.mean(xo * xo, axis=-1, keepdims=True)
    y_ref[...] = xo * lax.rsqrt(ms + EPS) * fg_ref[...]


def _out_proj(os, lses, ga, u, vc, gb, x, wout_bf16, w_s, b_s_t, final_g, *, tm):
    m, d_model = x.shape
    d_attn = ga.shape[1]
    d_chunk = u.shape[1]
    rows = lambda n: pl.BlockSpec((tm, n), lambda i: (i, 0))
    whole = lambda a: pl.BlockSpec(a.shape, lambda i: (0,) * a.ndim)
    fg = final_g.reshape(1, -1)
    return pl.pallas_call(
        _out_proj_kernel,
        grid=(m // tm,),
        in_specs=[rows(d_attn)] * 3 + [rows(LANES)] * 3 + [rows(d_attn), rows(d_chunk), rows(d_chunk),
                                                          rows(d_chunk), rows(d_model),
                                                          whole(wout_bf16), whole(w_s), whole(b_s_t), whole(fg)],
        out_specs=rows(d_model),
        out_shape=jax.ShapeDtypeStruct((m, d_model), _F32),
        scratch_shapes=[pltpu.VMEM((tm, d_attn), _BF16), pltpu.VMEM((tm, d_chunk), _BF16)],
        compiler_params=pltpu.CompilerParams(
            dimension_semantics=("arbitrary",), vmem_limit_bytes=VMEM_LIMIT_BYTES),
        name="out_proj",
    )(*os, *lses, ga, u, vc, gb, x, wout_bf16, w_s, b_s_t, fg)


def _sample_attn_kernel(q_ref, kn_ref, vn_ref, k1_ref, k4_ref, k16_ref, v1_ref, v4_ref, v16_ref, o_ref):
    t_new, d_attn = q_ref.shape
    n_heads = d_attn // HEAD_DIM
    head_row = lax.broadcasted_iota(jnp.int32, (n_heads, d_attn), 0)
    head_col = lax.broadcasted_iota(jnp.int32, (n_heads, d_attn), 1) // HEAD_DIM
    own = head_row == head_col
    key_idx = lax.broadcasted_iota(jnp.int32, (n_heads, BLOCK), 1)
    kn = kn_ref[...]
    vn = vn_ref[...]
    k1 = k1_ref[...].astype(_BF16)
    v1 = v1_ref[...].astype(_BF16)
    nt = (((1,), (1,)), ((), ()))
    for t in range(t_new):
        qexp = jnp.where(own, q_ref[t:t + 1, :], 0.0)
        qexp_b = qexp.astype(_BF16)
        s_new = [jnp.sum(qexp * kn[tp:tp + 1, :], axis=-1, keepdims=True) for tp in range(t + 1)]
        cs = slice(t * d_attn, (t + 1) * d_attn)
        patterns = (
            (k1, v1, key_idx >= t, tuple(range(t + 1))),
            (k4_ref[:, cs].astype(_BF16), v4_ref[:, cs].astype(_BF16), None, (t,)),
            (k16_ref[:, cs].astype(_BF16), v16_ref[:, cs].astype(_BF16), None, (t,)),
        )
        nums, lses = [], []
        for kb, vb, valid, new_rows in patterns:
            s = lax.dot_general(qexp_b, kb, nt, preferred_element_type=_F32)
            if valid is not None:
                s = jnp.where(valid, s, NEG_INF)
            m = jnp.max(s, axis=-1, keepdims=True)
            for tp in new_rows:
                m = jnp.maximum(m, s_new[tp])
            p = jnp.exp(s - m)
            den = jnp.sum(p, axis=-1, keepdims=True)
            num = jnp.dot(p.astype(_BF16), vb, preferred_element_type=_F32)
            for tp in new_rows:
                pn = jnp.exp(s_new[tp] - m)
                den = den + pn
                num = num + pn * vn[tp:tp + 1, :]
            nums.append(num / den)
            lses.append(m + jnp.log(den))
        mx = jnp.maximum(jnp.maximum(lses[0], lses[1]), lses[2])
        es = [jnp.exp(l - mx) for l in lses]
        tot = es[0] + es[1] + es[2]
        merged = (es[0] * nums[0] + es[1] * nums[1] + es[2] * nums[2]) / tot
        o_ref[t:t + 1, :] = jnp.sum(jnp.where(own, merged, 0.0), axis=0, keepdims=True)


def _sample_attention(q, k_new, v_new, cache_k, cache_v):
    db, t_new, d_attn = q.shape
    wb = cache_k.shape[1]
    assert wb == MAX_WINDOW and t_new == 4 and wb % (16 * BLOCK) == 0
    new = pl.BlockSpec((None, t_new, d_attn), lambda b: (b, 0, 0))
    near1 = pl.BlockSpec((None, BLOCK, d_attn), lambda b: (b, wb // BLOCK - 1, 0))
    c4 = lambda a: a.reshape(db, wb // 4, 4 * d_attn)
    near4 = pl.BlockSpec((None, BLOCK, 4 * d_attn), lambda b: (b, wb // 4 // BLOCK - 1, 0))
    c16 = lambda a: a.reshape(db, wb // 16, 16 * d_attn)
    far16 = pl.BlockSpec((None, BLOCK, t_new * d_attn), lambda b: (b, 0, 0))
    return pl.pallas_call(
        _sample_attn_kernel,
        grid=(db,),
        in_specs=[new, new, new, near1, near4, far16, near1, near4, far16],
        out_specs=new,
        out_shape=jax.ShapeDtypeStruct((db, t_new, d_attn), _F32),
        compiler_params=pltpu.CompilerParams(
            dimension_semantics=("arbitrary",), vmem_limit_bytes=VMEM_LIMIT_BYTES),
        name="sample_attn",
    )(q, k_new, v_new, cache_k, c4(cache_k), c16(cache_k), cache_v, c4(cache_v), c16(cache_v))


def _sample_out_kernel(oa_ref, ga_ref, u_ref, vc_ref, gb_ref, x_ref, wout_ref, coef_ref, bias_ref, fg_ref,
                       y_ref, c_ref, *, t_new):
    m, d_attn = oa_ref.shape
    db = m // t_new
    a = (oa_ref[...] * ga_ref[...]).astype(_BF16)
    for t in range(t_new):
        mixed = jnp.zeros((db, c_ref.shape[1]), _F32) + bias_ref[t:t + 1, :]
        for s in range(t + 1):
            mixed = mixed + coef_ref[t * t_new + s:t * t_new + s + 1, :] * vc_ref[s * db:(s + 1) * db, :]
        rs = slice(t * db, (t + 1) * db)
        c_ref[rs, :] = (u_ref[rs, :] * mixed * gb_ref[rs, :]).astype(c_ref.dtype)
    acc = jnp.dot(a, wout_ref[:d_attn, :], preferred_element_type=_F32)
    acc = acc + jnp.dot(c_ref[...], wout_ref[d_attn:, :], preferred_element_type=_F32)
    xo = x_ref[...] + acc
    ms = jnp.mean(xo * xo, axis=-1, keepdims=True)
    y_ref[...] = xo * lax.rsqrt(ms + EPS) * fg_ref[...]


def _sample_out(oa, ga, u, vc, gb, x, wout_bf16, coef, bias, final_g, *, t_new):
    m, d_model = x.shape
    args = (oa, ga, u, vc, gb, x, wout_bf16, coef, bias, final_g.reshape(1, -1))
    whole = lambda a: pl.BlockSpec(a.shape, lambda i: (0,) * a.ndim)
    return pl.pallas_call(
        functools.partial(_sample_out_kernel, t_new=t_new),
        grid=(1,),
        in_specs=[whole(a) for a in args],
        out_specs=pl.BlockSpec((m, d_model), lambda i: (0, 0)),
        out_shape=jax.ShapeDtypeStruct((m, d_model), _F32),
        scratch_shapes=[pltpu.VMEM((m, u.shape[1]), _BF16)],
        compiler_params=pltpu.CompilerParams(
            dimension_semantics=("arbitrary",), vmem_limit_bytes=VMEM_LIMIT_BYTES),
        name="sample_out",
    )(*args)


def kernel(x_prompt, x_sample, cache_k, cache_v, norm_g, w_in, ln_g, ln_b, w_s, b_s, w_out, final_g):
    batch, s_len, d_model = x_prompt.shape
    db, t_new, _ = x_sample.shape
    depth, _, wb, n_heads, head_dim = cache_k.shape
    d_attn = n_heads * head_dim
    d_chunk = w_out.shape[1] - d_attn
    assert batch == 1 and depth == 1 and head_dim == HEAD_DIM
    assert w_in.shape[2] == 4 * d_attn + 3 * d_chunk and d_attn == d_chunk
    tail = min(MAX_WINDOW, s_len)

    w_in_b = w_in[0].astype(_BF16)
    w_out_b = w_out[0].astype(_BF16)
    ng, lg, lb = norm_g[0], ln_g[0], ln_b[0]

    xp = x_prompt.reshape(s_len, d_model)
    cos_p, sin_p = _rope_tables(jnp.arange(s_len, dtype=jnp.int32))
    q, k, v, ga, u, vc, gb, k_tail, v_tail = _in_proj(
        xp, ng, w_in_b, cos_p, sin_p, lg, lb, tm=512, out_dtype=_BF16, tail_rows=tail)
    outs, lses = zip(*[_dilated_attention(q, k, v, d) for _, d in DILATIONS])
    y_prompt = _out_proj(outs, lses, ga, u, vc, gb, xp, w_out_b, w_s[0], b_s[0].T, final_g, tm=256)

    xs = jnp.transpose(x_sample, (1, 0, 2)).reshape(t_new * db, d_model)
    pos_s = PAST_LEN + jnp.repeat(jnp.arange(t_new, dtype=jnp.int32), db)
    cos_s, sin_s = _rope_tables(pos_s)
    qs, ks, vs, gas, us, vcs, gbs = _in_proj(
        xs, ng, w_in_b, cos_s, sin_s, lg, lb, tm=t_new * db, out_dtype=_F32, tail_rows=0)
    to_b_major = lambda a: jnp.transpose(a.reshape(t_new, db, -1), (1, 0, 2))
    qs_b, ks_b, vs_b, vcs_b = (to_b_major(a) for a in (qs, ks, vs, vcs))
    oa = _sample_attention(qs_b, ks_b, vs_b, cache_k[0].reshape(db, wb, d_attn),
                           cache_v[0].reshape(db, wb, d_attn))
    oa_t = jnp.transpose(oa, (1, 0, 2)).reshape(t_new * db, d_attn)
    coef = jnp.repeat(jnp.transpose(w_s[0][:, :t_new, :t_new], (1, 2, 0)).reshape(t_new * t_new, -1),
                      GROUP_WIDTH_B, axis=1)
    bias = jnp.repeat(jnp.transpose(b_s[0][:, :t_new], (1, 0)), GROUP_WIDTH_B, axis=1)
    ys = _sample_out(oa_t, gas, us, vcs, gbs, xs, w_out_b, coef, bias, final_g, t_new=t_new)
    y_sample = jnp.transpose(ys.reshape(t_new, db, d_model), (1, 0, 2))

    hs = (n_heads, head_dim)
    return (
        y_prompt.reshape(batch, s_len, d_model),
        y_sample,
        k_tail.reshape(depth, batch, tail, *hs),
        v_tail.reshape(depth, batch, tail, *hs),
        ks_b.reshape(depth, db, t_new, *hs),
        vs_b.reshape(depth, db, t_new, *hs),
        vcs_b.reshape(depth, db, t_new, d_chunk),
    )
```

```python
import functools
import math

import jax
import jax.numpy as jnp
from jax import lax
from jax.experimental import pallas as pl
from jax.experimental.pallas import tpu as pltpu

HEAD_DIM = 64
BLOCK = 128
CHUNK = 128
GROUP_WIDTH_B = 128
DILATIONS = ((128, 1), (512, 4), (2048, 16))
MAX_WINDOW = 2048
PAST_LEN = 16384
ROPE_THETA = 10000.0
EPS = 1e-6
NEG_INF = -1e30
N_SEGMENTS = 7

LANES = 128
VMEM_LIMIT_BYTES = 56 * 1024 * 1024

_BF16 = jnp.bfloat16
_F32 = jnp.float32


def _params(*semantics):
    return pltpu.CompilerParams(dimension_semantics=semantics, vmem_limit_bytes=VMEM_LIMIT_BYTES)


def _rope_tables(pos):
    half = HEAD_DIM // 2
    inv = jnp.exp(-math.log(ROPE_THETA) * jnp.arange(half, dtype=_F32) / half)
    ang = pos.astype(_F32)[:, None] * inv[None, :]
    cos = jnp.cos(ang)
    sin = jnp.sin(ang)
    cos_t = jnp.concatenate([cos, cos, cos, cos], axis=-1)
    sin_t = jnp.concatenate([-sin, sin, -sin, sin], axis=-1)
    return cos_t, sin_t


def _rope_group(zg, cos, sin_signed):
    lane = lax.broadcasted_iota(jnp.int32, zg.shape, 1)
    first_half = (lane % HEAD_DIM) < (HEAD_DIM // 2)
    partner = jnp.where(first_half, pltpu.roll(zg, LANES - HEAD_DIM // 2, 1),
                        pltpu.roll(zg, HEAD_DIM // 2, 1))
    return zg * cos + partner * sin_signed


def _gelu_layer_norm(z, g, b):
    h = jax.nn.gelu(z)
    mu = jnp.mean(h, axis=-1, keepdims=True)
    hc = h - mu
    var = jnp.mean(hc * hc, axis=-1, keepdims=True)
    return hc * lax.rsqrt(var + EPS) * g + b


def _rms_norm_kernel(x_ref, g_ref, o_ref):
    xf = x_ref[...]
    ms = jnp.mean(xf * xf, axis=-1, keepdims=True)
    o_ref[...] = (xf * lax.rsqrt(ms + EPS) * g_ref[...]).astype(o_ref.dtype)


def _rms_norm(x, g, *, tm):
    m, d = x.shape
    return pl.pallas_call(
        _rms_norm_kernel,
        grid=(m // tm,),
        in_specs=[pl.BlockSpec((tm, d), lambda i: (i, 0)), pl.BlockSpec((1, d), lambda i: (0, 0))],
        out_specs=pl.BlockSpec((tm, d), lambda i: (i, 0)),
        out_shape=jax.ShapeDtypeStruct((m, d), _BF16),
        compiler_params=_params("arbitrary"),
        name="rms_norm",
    )(x, g.reshape(1, -1))


def _qkv_kernel(*refs, rope, scale, tail_start):
    it = iter(refs)
    xn_ref, w_ref = next(it), next(it)
    cos_ref = sin_ref = None
    if rope:
        cos_ref, sin_ref = next(it), next(it)
    d1_ref, d4_ref, d16_ref = next(it), next(it), next(it)
    tail_ref = next(it) if tail_start is not None else None
    nat_ref, res4_ref = next(it), next(it)

    tm = xn_ref.shape[0]
    z = jnp.dot(xn_ref[...], w_ref[...], preferred_element_type=_F32)
    n4, n16 = tm // 4, tm // 16
    for g in range(z.shape[1] // LANES):
        sl = slice(g * LANES, (g + 1) * LANES)
        r = z[:, sl]
        if rope:
            r = _rope_group(r, cos_ref[...], sin_ref[...])
        if tail_ref is not None:
            @pl.when(pl.program_id(0) >= tail_start)
            def _():
                tail_ref[sl, :] = r.T
        if scale != 1.0:
            r = r * scale
        d1_ref[:, sl] = r.astype(d1_ref.dtype)
        nat_ref[g] = r
        for r4 in range(4):
            blk = nat_ref[g, pl.ds(r4, n4, stride=4), :]
            d4_ref[r4, :, sl] = blk.astype(d4_ref.dtype)
            res4_ref[g, r4 * n4:(r4 + 1) * n4, :] = blk
        for r4 in range(4):
            for a in range(4):
                blk = res4_ref[g, pl.ds(r4 * n4 + a, n16, stride=4), :]
                d16_ref[r4 + 4 * a, :, sl] = blk.astype(d16_ref.dtype)


def _qkv_proj(xn, w_seg, tables, *, tm, scale, tail_rows):
    s_len, d_model = xn.shape
    d_seg = w_seg.shape[1]
    n_i = s_len // tm
    rope = tables is not None
    in_specs = [pl.BlockSpec((tm, d_model), lambda i: (i, 0)),
                pl.BlockSpec((d_model, d_seg), lambda i: (0, 0))]
    args = [xn, w_seg]
    if rope:
        in_specs += [pl.BlockSpec((tm, LANES), lambda i: (i, 0))] * 2
        args += list(tables)
    out_shape = [jax.ShapeDtypeStruct((s_len, d_seg), _BF16),
                 jax.ShapeDtypeStruct((4, s_len // 4, d_seg), _BF16),
                 jax.ShapeDtypeStruct((16, s_len // 16, d_seg), _BF16)]
    out_specs = [pl.BlockSpec((tm, d_seg), lambda i: (i, 0)),
                 pl.BlockSpec((4, tm // 4, d_seg), lambda i: (0, i, 0)),
                 pl.BlockSpec((16, tm // 16, d_seg), lambda i: (0, i, 0))]
    tail_start = None
    if tail_rows:
        tail_start = n_i - tail_rows // tm
        out_shape.append(jax.ShapeDtypeStruct((d_seg, tail_rows), _F32))
        out_specs.append(pl.BlockSpec((d_seg, tm), lambda i: (0, jnp.maximum(i - tail_start, 0))))
    slab = pltpu.VMEM((d_seg // LANES, tm, LANES), _F32)
    return pl.pallas_call(
        functools.partial(_qkv_kernel, rope=rope, scale=scale, tail_start=tail_start),
        grid=(n_i,),
        in_specs=in_specs,
        out_specs=out_specs,
        out_shape=out_shape,
        scratch_shapes=[slab, slab],
        compiler_params=_params("arbitrary"),
        name="qkv_proj",
    )(*args)


def _gate_kernel(*refs, mode):
    if mode == "gelu_ln":
        xn_ref, w_ref, lng_ref, lnb_ref, o_ref = refs
    else:
        xn_ref, w_ref, o_ref = refs
    z = jnp.dot(xn_ref[...], w_ref[...], preferred_element_type=_F32)
    if mode == "silu":
        y = jax.nn.silu(z)
    elif mode == "gelu":
        y = jax.nn.gelu(z)
    else:
        y = _gelu_layer_norm(z, lng_ref[...], lnb_ref[...])
    o_ref[...] = y.astype(o_ref.dtype)


def _gate_proj(xn, w_seg, *, tm, mode, ln=None):
    s_len, d_model = xn.shape
    d_seg = w_seg.shape[1]
    in_specs = [pl.BlockSpec((tm, d_model), lambda i: (i, 0)),
                pl.BlockSpec((d_model, d_seg), lambda i: (0, 0))]
    args = [xn, w_seg]
    if mode == "gelu_ln":
        in_specs += [pl.BlockSpec((1, d_seg), lambda i: (0, 0))] * 2
        args += [ln[0].reshape(1, -1), ln[1].reshape(1, -1)]
    return pl.pallas_call(
        functools.partial(_gate_kernel, mode=mode),
        grid=(s_len // tm,),
        in_specs=in_specs,
        out_specs=pl.BlockSpec((tm, d_seg), lambda i: (i, 0)),
        out_shape=jax.ShapeDtypeStruct((s_len, d_seg), _BF16),
        compiler_params=_params("arbitrary"),
        name=f"gate_proj_{mode}",
    )(*args)


def _attn_kernel(q_ref, kp_ref, kc_ref, vp_ref, vc_ref, o_ref, lse_ref):
    n = pl.program_id(1)
    two = 2 * BLOCK
    qi = lax.broadcasted_iota(jnp.int32, (two, two), 0) % BLOCK
    si = lax.broadcasted_iota(jnp.int32, (two, two), 1)
    dist = qi + BLOCK - si
    mask = (dist >= 0) & (dist <= BLOCK) & jnp.logical_or(si >= BLOCK, n > 0)
    lane = lax.broadcasted_iota(jnp.int32, (BLOCK, LANES), 1)
    head0 = lane < HEAD_DIM
    lse_acc = jnp.zeros((BLOCK, LANES), _F32)
    zero = jnp.zeros((BLOCK, LANES), q_ref.dtype)
    for hp in range(q_ref.shape[1] // LANES):
        sl = slice(hp * LANES, (hp + 1) * LANES)
        q2 = q_ref[:, sl]
        qs = jnp.concatenate([jnp.where(head0, q2, zero), jnp.where(head0, zero, q2)], axis=0)
        k2 = jnp.concatenate([kp_ref[:, sl], kc_ref[:, sl]], axis=0)
        v2 = jnp.concatenate([vp_ref[:, sl], vc_ref[:, sl]], axis=0)
        s = lax.dot_general(qs, k2, (((1,), (1,)), ((), ())), preferred_element_type=_F32)
        s = jnp.where(mask, s, NEG_INF)
        m = jnp.max(s, axis=-1, keepdims=True)
        p = jnp.exp(s - m)
        den = jnp.sum(p, axis=-1, keepdims=True)
        pv = jnp.dot(p.astype(v2.dtype), v2, preferred_element_type=_F32)
        o2 = pv / den
        o_ref[:, sl] = jnp.where(head0, o2[:BLOCK], o2[BLOCK:]).astype(o_ref.dtype)
        l2 = m + jnp.log(den)
        lse_acc = jnp.where(lane == 2 * hp, l2[:BLOCK],
                            jnp.where(lane == 2 * hp + 1, l2[BLOCK:], lse_acc))
    lse_ref[...] = lse_acc


def _dilated_attention(q, k, v):
    d, sub_len, d_attn = q.shape
    cur = pl.BlockSpec((None, BLOCK, d_attn), lambda r, n: (r, n, 0))
    prev = pl.BlockSpec((None, BLOCK, d_attn), lambda r, n: (r, jnp.maximum(n - 1, 0), 0))
    return pl.pallas_call(
        _attn_kernel,
        grid=(d, sub_len // BLOCK),
        in_specs=[cur, prev, cur, prev, cur],
        out_specs=[cur, pl.BlockSpec((None, BLOCK, LANES), lambda r, n: (r, n, 0))],
        out_shape=[jax.ShapeDtypeStruct((d, sub_len, d_attn), q.dtype),
                   jax.ShapeDtypeStruct((d, sub_len, LANES), _F32)],
        compiler_params=_params("arbitrary", "arbitrary"),
        name=f"dilated_attn_d{d}",
    )(q, k, k, v, v)


def _expand_heads(w, expand):
    hi = w.astype(_BF16)
    lo = (w - hi.astype(_F32)).astype(_BF16)
    return (jnp.dot(hi, expand, preferred_element_type=_F32)
            + jnp.dot(lo, expand, preferred_element_type=_F32))


def _out_proj_kernel(o1_ref, o4_ref, o16_ref, l1_ref, l4_ref, l16_ref, ga_ref, u_ref, vc_ref, gb_ref,
                     x_ref, wout_ref, ws_ref, bst_ref, fg_ref, y_ref,
                     a_ref, c_ref, n4_ref, n16_ref, ln4_ref, ln16_ref):
    tm, d_attn = o1_ref.shape
    n_lane_groups = d_attn // LANES
    for r in range(4):
        ln4_ref[pl.ds(r, tm // 4, stride=4), :] = l4_ref[r]
        for g in range(n_lane_groups):
            n4_ref[g, pl.ds(r, tm // 4, stride=4), :] = o4_ref[r, :, g * LANES:(g + 1) * LANES].astype(_F32)
    for r in range(16):
        ln16_ref[pl.ds(r, tm // 16, stride=16), :] = l16_ref[r]
        for g in range(n_lane_groups):
            n16_ref[g, pl.ds(r, tm // 16, stride=16), :] = o16_ref[r, :, g * LANES:(g + 1) * LANES].astype(_F32)

    lses = [l1_ref[...], ln4_ref[...], ln16_ref[...]]
    mx = jnp.maximum(jnp.maximum(lses[0], lses[1]), lses[2])
    es = [jnp.exp(l - mx) for l in lses]
    tot = es[0] + es[1] + es[2]
    head = lax.broadcasted_iota(jnp.int32, (LANES, d_attn), 0)
    col_head = lax.broadcasted_iota(jnp.int32, (LANES, d_attn), 1) // HEAD_DIM
    expand = (head == col_head).astype(_BF16)
    ws = [_expand_heads(e / tot, expand) for e in es]
    for g in range(n_lane_groups):
        sl = slice(g * LANES, (g + 1) * LANES)
        a = (ws[0][:, sl] * o1_ref[:, sl].astype(_F32) + ws[1][:, sl] * n4_ref[g] + ws[2][:, sl] * n16_ref[g])
        a_ref[:, sl] = (a * ga_ref[:, sl].astype(_F32)).astype(a_ref.dtype)

    row = lax.broadcasted_iota(jnp.int32, (CHUNK, CHUNK), 0)
    col = lax.broadcasted_iota(jnp.int32, (CHUNK, CHUNK), 1)
    tril = row >= col
    for g in range(ws_ref.shape[0]):
        wm = jnp.where(tril, ws_ref[g], 0.0).astype(_BF16)
        bias = bst_ref[:, g:g + 1]
        gs = slice(g * GROUP_WIDTH_B, (g + 1) * GROUP_WIDTH_B)
        for c in range(tm // CHUNK):
            rs = slice(c * CHUNK, (c + 1) * CHUNK)
            mixed = jnp.dot(wm, vc_ref[rs, gs], preferred_element_type=_F32) + bias
            cv = u_ref[rs, gs].astype(_F32) * mixed * gb_ref[rs, gs].astype(_F32)
            c_ref[rs, gs] = cv.astype(c_ref.dtype)

    acc = jnp.dot(a_ref[...], wout_ref[:d_attn, :], preferred_element_type=_F32)
    acc = acc + jnp.dot(c_ref[...], wout_ref[d_attn:, :], preferred_element_type=_F32)
    xo = x_ref[...] + acc
    ms = jnp.mean(xo * xo, axis=-1, keepdims=True)
    y_ref[...] = xo * lax.rsqrt(ms + EPS) * fg_ref[...]


def _out_proj(os, lses, ga, u, vc, gb, x, wout_bf16, w_s, b_s_t, final_g, *, tm):
    m, d_model = x.shape
    d_attn = ga.shape[1]
    d_chunk = u.shape[1]
    rows = lambda n: pl.BlockSpec((tm, n), lambda i: (i, 0))
    res = lambda d, n: pl.BlockSpec((d, tm // d, n), lambda i: (0, i, 0))
    whole = lambda a: pl.BlockSpec(a.shape, lambda i: (0,) * a.ndim)
    fg = final_g.reshape(1, -1)
    slab = pltpu.VMEM((d_attn // LANES, tm, LANES), _F32)
    return pl.pallas_call(
        _out_proj_kernel,
        grid=(m // tm,),
        in_specs=[rows(d_attn), res(4, d_attn), res(16, d_attn), rows(LANES), res(4, LANES), res(16, LANES),
                  rows(d_attn), rows(d_chunk), rows(d_chunk), rows(d_chunk), rows(d_model),
                  whole(wout_bf16), whole(w_s), whole(b_s_t), whole(fg)],
        out_specs=rows(d_model),
        out_shape=jax.ShapeDtypeStruct((m, d_model), _F32),
        scratch_shapes=[pltpu.VMEM((tm, d_attn), _BF16), pltpu.VMEM((tm, d_chunk), _BF16), slab, slab,
                        pltpu.VMEM((tm, LANES), _F32), pltpu.VMEM((tm, LANES), _F32)],
        compiler_params=_params("arbitrary"),
        name="out_proj",
    )(*os, *lses, ga, u, vc, gb, x, wout_bf16, w_s, b_s_t, fg)


def _sample_in_proj_kernel(x_ref, g_ref, w_ref, cos_ref, sin_ref, lng_ref, lnb_ref,
                           q_ref, k_ref, v_ref, ga_ref, u_ref, vc_ref, gb_ref, xn_ref):
    j = pl.program_id(0)

    @pl.when(j == 0)
    def _():
        xf = x_ref[...]
        ms = jnp.mean(xf * xf, axis=-1, keepdims=True)
        xn_ref[...] = (xf * lax.rsqrt(ms + EPS) * g_ref[...]).astype(xn_ref.dtype)

    z = jnp.dot(xn_ref[...], w_ref[...], preferred_element_type=_F32)

    def rope_to(ref, scale):
        for g in range(z.shape[1] // LANES):
            sl = slice(g * LANES, (g + 1) * LANES)
            ref[:, sl] = _rope_group(z[:, sl], cos_ref[...], sin_ref[...]) * scale

    @pl.when(j == 0)
    def _():
        rope_to(q_ref, HEAD_DIM ** -0.5)

    @pl.when(j == 1)
    def _():
        rope_to(k_ref, 1.0)

    @pl.when(j == 2)
    def _():
        v_ref[...] = z

    @pl.when(j == 3)
    def _():
        ga_ref[...] = jax.nn.silu(z)

    @pl.when(j == 4)
    def _():
        u_ref[...] = jax.nn.gelu(z)

    @pl.when(j == 5)
    def _():
        vc_ref[...] = _gelu_layer_norm(z, lng_ref[...], lnb_ref[...])

    @pl.when(j == 6)
    def _():
        gb_ref[...] = jax.nn.silu(z)


def _sample_in_proj(x, norm_g, w_bf16, cos_t, sin_t, ln_g, ln_b):
    m, d_model = x.shape
    d_seg = w_bf16.shape[1] // N_SEGMENTS
    whole = lambda r, c: pl.BlockSpec((r, c), lambda j: (0, 0))
    return pl.pallas_call(
        _sample_in_proj_kernel,
        grid=(N_SEGMENTS,),
        in_specs=[whole(m, d_model), whole(1, d_model),
                  pl.BlockSpec((d_model, d_seg), lambda j: (0, j)),
                  whole(m, LANES), whole(m, LANES), whole(1, d_seg), whole(1, d_seg)],
        out_specs=[whole(m, d_seg)] * N_SEGMENTS,
        out_shape=[jax.ShapeDtypeStruct((m, d_seg), _F32)] * N_SEGMENTS,
        scratch_shapes=[pltpu.VMEM((m, d_model), _BF16)],
        compiler_params=_params("arbitrary"),
        name="sample_in_proj",
    )(x, norm_g.reshape(1, -1), w_bf16, cos_t, sin_t, ln_g.reshape(1, -1), ln_b.reshape(1, -1))


def _sample_attn_kernel(q_ref, kn_ref, vn_ref, kt_ref, vt_ref, o_ref, *, wb):
    t_new, width = q_ref.shape
    n_heads = width // HEAD_DIM
    rows = t_new * n_heads
    head_row = lax.broadcasted_iota(jnp.int32, (n_heads, width), 0)
    head_col = lax.broadcasted_iota(jnp.int32, (n_heads, width), 1) // HEAD_DIM
    own = head_row == head_col
    qbd = jnp.concatenate([jnp.where(own, q_ref[t:t + 1, :], 0.0) for t in range(t_new)], axis=0)

    t_row = lax.broadcasted_iota(jnp.int32, (rows, wb), 0) // n_heads
    pos = lax.broadcasted_iota(jnp.int32, (rows, wb), 1)
    dist = wb + t_row - pos
    mult = jnp.zeros((rows, wb), _F32)
    for window, d in DILATIONS:
        mult = mult + jnp.where((dist % d == 0) & (dist <= window), 1.0, 0.0)
    valid = mult > 0.0

    s = jnp.dot(qbd.astype(_BF16), kt_ref[...].astype(_BF16), preferred_element_type=_F32)
    s = jnp.where(valid, s, NEG_INF)
    m = jnp.max(s, axis=-1, keepdims=True)
    t_col = lax.broadcasted_iota(jnp.int32, (rows, 1), 0) // n_heads
    s_new, mult_new = [], []
    for tp in range(t_new):
        s_new.append(jnp.sum(qbd * kn_ref[tp:tp + 1, :], axis=-1, keepdims=True))
        dn = t_col - tp
        mn = jnp.zeros((rows, 1), _F32)
        for window, d in DILATIONS:
            mn = mn + jnp.where((dn >= 0) & (dn % d == 0), 1.0, 0.0)
        mult_new.append(mn)
        m = jnp.maximum(m, jnp.where(mn > 0.0, s_new[tp], NEG_INF))
    p = jnp.exp(s - m) * mult
    den = jnp.sum(p, axis=-1, keepdims=True)
    num = lax.dot_general(p.astype(_BF16), vt_ref[...].astype(_BF16), (((1,), (1,)), ((), ())),
                          preferred_element_type=_F32)
    for tp in range(t_new):
        pn = jnp.where(mult_new[tp] > 0.0, jnp.exp(s_new[tp] - m), 0.0) * mult_new[tp]
        den = den + pn
        num = num + pn * vn_ref[tp:tp + 1, :]
    res = num / den
    for t in range(t_new):
        blk = res[t * n_heads:(t + 1) * n_heads, :]
        o_ref[t:t + 1, :] = jnp.sum(jnp.where(own, blk, 0.0), axis=0, keepdims=True)


def _sample_attention(q, k_new, v_new, cache_kt, cache_vt, *, head_groups):
    db, t_new, d_attn = q.shape
    wb = cache_kt.shape[2]
    assert wb == MAX_WINDOW
    width = d_attn // head_groups
    new = pl.BlockSpec((None, t_new, width), lambda b, g: (b, 0, g))
    cache = pl.BlockSpec((None, width, wb), lambda b, g: (b, g, 0))
    return pl.pallas_call(
        functools.partial(_sample_attn_kernel, wb=wb),
        grid=(db, head_groups),
        in_specs=[new, new, new, cache, cache],
        out_specs=new,
        out_shape=jax.ShapeDtypeStruct((db, t_new, d_attn), _F32),
        compiler_params=_params("arbitrary", "arbitrary"),
        name="sample_attn",
    )(q, k_new, v_new, cache_kt, cache_vt)


def _sample_out_kernel(oa_ref, ga_ref, u_ref, vc_ref, gb_ref, x_ref, wout_ref, coef_ref, bias_ref, fg_ref,
                       y_ref, c_ref, *, t_new):
    m, d_attn = oa_ref.shape
    db = m // t_new
    a = (oa_ref[...] * ga_ref[...]).astype(_BF16)
    for t in range(t_new):
        mixed = jnp.zeros((db, c_ref.shape[1]), _F32) + bias_ref[t:t + 1, :]
        for s in range(t + 1):
            mixed = mixed + coef_ref[t * t_new + s:t * t_new + s + 1, :] * vc_ref[s * db:(s + 1) * db, :]
        rs = slice(t * db, (t + 1) * db)
        c_ref[rs, :] = (u_ref[rs, :] * mixed * gb_ref[rs, :]).astype(c_ref.dtype)
    acc = jnp.dot(a, wout_ref[:d_attn, :], preferred_element_type=_F32)
    acc = acc + jnp.dot(c_ref[...], wout_ref[d_attn:, :], preferred_element_type=_F32)
    xo = x_ref[...] + acc
    ms = jnp.mean(xo * xo, axis=-1, keepdims=True)
    y_ref[...] = xo * lax.rsqrt(ms + EPS) * fg_ref[...]


def _sample_out(oa, ga, u, vc, gb, x, wout_bf16, coef, bias, final_g, *, t_new):
    m, d_model = x.shape
    args = (oa, ga, u, vc, gb, x, wout_bf16, coef, bias, final_g.reshape(1, -1))
    whole = lambda a: pl.BlockSpec(a.shape, lambda i: (0,) * a.ndim)
    return pl.pallas_call(
        functools.partial(_sample_out_kernel, t_new=t_new),
        grid=(1,),
        in_specs=[whole(a) for a in args],
        out_specs=pl.BlockSpec((m, d_model), lambda i: (0, 0)),
        out_shape=jax.ShapeDtypeStruct((m, d_model), _F32),
        scratch_shapes=[pltpu.VMEM((m, u.shape[1]), _BF16)],
        compiler_params=_params("arbitrary"),
        name="sample_out",
    )(*args)


def kernel(x_prompt, x_sample, cache_k, cache_v, norm_g, w_in, ln_g, ln_b, w_s, b_s, w_out, final_g):
    batch, s_len, d_model = x_prompt.shape
    db, t_new, _ = x_sample.shape
    depth, _, wb, n_heads, head_dim = cache_k.shape
    d_attn = n_heads * head_dim
    d_chunk = w_out.shape[1] - d_attn
    assert batch == 1 and depth == 1 and head_dim == HEAD_DIM
    assert w_in.shape[2] == 4 * d_attn + 3 * d_chunk and d_attn == d_chunk
    tail = min(MAX_WINDOW, s_len)

    w_in_b = w_in[0].astype(_BF16)
    w_out_b = w_out[0].astype(_BF16)
    seg = lambda j: w_in_b[:, j * d_attn:(j + 1) * d_attn]
    ng, lg, lb = norm_g[0], ln_g[0], ln_b[0]

    tm = 512
    xp = x_prompt.reshape(s_len, d_model)
    xn = _rms_norm(xp, ng, tm=tm)
    tables = _rope_tables(jnp.arange(s_len, dtype=jnp.int32))
    q1, q4, q16 = _qkv_proj(xn, seg(0), tables, tm=tm, scale=HEAD_DIM ** -0.5, tail_rows=0)
    k1, k4, k16, k_tail_t = _qkv_proj(xn, seg(1), tables, tm=tm, scale=1.0, tail_rows=tail)
    v1, v4, v16, v_tail_t = _qkv_proj(xn, seg(2), None, tm=tm, scale=1.0, tail_rows=tail)
    ga = _gate_proj(xn, seg(3), tm=tm, mode="silu")
    u = _gate_proj(xn, seg(4), tm=tm, mode="gelu")
    vc = _gate_proj(xn, seg(5), tm=tm, mode="gelu_ln", ln=(lg, lb))
    gb = _gate_proj(xn, seg(6), tm=tm, mode="silu")
    o1, l1 = _dilated_attention(q1[None], k1[None], v1[None])
    o4, l4 = _dilated_attention(q4, k4, v4)
    o16, l16 = _dilated_attention(q16, k16, v16)
    y_prompt = _out_proj((o1[0], o4, o16), (l1[0], l4, l16), ga, u, vc, gb, xp, w_out_b,
                         w_s[0], b_s[0].T, final_g, tm=256)
    untranspose = lambda a: jnp.transpose(a.reshape(n_heads, head_dim, tail), (2, 0, 1))

    xs = jnp.transpose(x_sample, (1, 0, 2)).reshape(t_new * db, d_model)
    pos_s = PAST_LEN + jnp.repeat(jnp.arange(t_new, dtype=jnp.int32), db)
    cos_s, sin_s = _rope_tables(pos_s)
    qs, ks, vs, gas, us, vcs, gbs = _sample_in_proj(xs, ng, w_in_b, cos_s, sin_s, lg, lb)
    to_b_major = lambda a: jnp.transpose(a.reshape(t_new, db, -1), (1, 0, 2))
    qs_b, ks_b, vs_b, vcs_b = (to_b_major(a) for a in (qs, ks, vs, vcs))
    cache_t = lambda c: jnp.transpose(c[0], (0, 2, 3, 1)).reshape(db, d_attn, wb)
    oa = _sample_attention(qs_b, ks_b, vs_b, cache_t(cache_k), cache_t(cache_v), head_groups=2)
    oa_t = jnp.transpose(oa, (1, 0, 2)).reshape(t_new * db, d_attn)
    coef = jnp.repeat(jnp.transpose(w_s[0][:, :t_new, :t_new], (1, 2, 0)).reshape(t_new * t_new, -1),
                      GROUP_WIDTH_B, axis=1)
    bias = jnp.repeat(jnp.transpose(b_s[0][:, :t_new], (1, 0)), GROUP_WIDTH_B, axis=1)
    ys = _sample_out(oa_t, gas, us, vcs, gbs, xs, w_out_b, coef, bias, final_g, t_new=t_new)
    y_sample = jnp.transpose(ys.reshape(t_new, db, d_model), (1, 0, 2))

    hs = (n_heads, head_dim)
    return (
        y_prompt.reshape(batch, s_len, d_model),
        y_sample,
        untranspose(k_tail_t).reshape(depth, batch, tail, *hs),
        untranspose(v_tail_t).reshape(depth, batch, tail, *hs),
        ks_b.reshape(depth, db, t_new, *hs),
        vs_b.reshape(depth, db, t_new, *hs),
        vcs_b.reshape(depth, db, t_new, d_chunk),
    )
```

```python
import functools
import math

import jax
import jax.numpy as jnp
from jax import lax
from jax.experimental import pallas as pl
from jax.experimental.pallas import tpu as pltpu

HEAD_DIM = 64
BLOCK = 128
CHUNK = 128
GROUP_WIDTH_B = 128
DILATIONS = ((128, 1), (512, 4), (2048, 16))
MAX_WINDOW = 2048
PAST_LEN = 16384
ROPE_THETA = 10000.0
EPS = 1e-6
NEG_INF = -1e30
N_SEGMENTS = 7

LANES = 128
VMEM_LIMIT_BYTES = 56 * 1024 * 1024

_BF16 = jnp.bfloat16
_F32 = jnp.float32


def _params(*semantics):
    return pltpu.CompilerParams(dimension_semantics=semantics, vmem_limit_bytes=VMEM_LIMIT_BYTES)


def _rope_tables(pos):
    half = HEAD_DIM // 2
    inv = jnp.exp(-math.log(ROPE_THETA) * jnp.arange(half, dtype=_F32) / half)
    ang = pos.astype(_F32)[:, None] * inv[None, :]
    cos = jnp.cos(ang)
    sin = jnp.sin(ang)
    cos_t = jnp.concatenate([cos, cos, cos, cos], axis=-1)
    sin_t = jnp.concatenate([-sin, sin, -sin, sin], axis=-1)
    return cos_t, sin_t


def _rope_group(zg, cos, sin_signed):
    lane = lax.broadcasted_iota(jnp.int32, zg.shape, 1)
    first_half = (lane % HEAD_DIM) < (HEAD_DIM // 2)
    partner = jnp.where(first_half, pltpu.roll(zg, LANES - HEAD_DIM // 2, 1),
                        pltpu.roll(zg, HEAD_DIM // 2, 1))
    return zg * cos + partner * sin_signed


def _gelu_layer_norm(z, g, b):
    h = jax.nn.gelu(z)
    mu = jnp.mean(h, axis=-1, keepdims=True)
    hc = h - mu
    var = jnp.mean(hc * hc, axis=-1, keepdims=True)
    return hc * lax.rsqrt(var + EPS) * g + b


def _rms_norm_kernel(x_ref, g_ref, o_ref):
    xf = x_ref[...]
    ms = jnp.mean(xf * xf, axis=-1, keepdims=True)
    o_ref[...] = (xf * lax.rsqrt(ms + EPS) * g_ref[...]).astype(o_ref.dtype)


def _rms_norm(x, g, *, tm):
    m, d = x.shape
    return pl.pallas_call(
        _rms_norm_kernel,
        grid=(m // tm,),
        in_specs=[pl.BlockSpec((tm, d), lambda i: (i, 0)), pl.BlockSpec((1, d), lambda i: (0, 0))],
        out_specs=pl.BlockSpec((tm, d), lambda i: (i, 0)),
        out_shape=jax.ShapeDtypeStruct((m, d), _BF16),
        compiler_params=_params("arbitrary"),
        name="rms_norm",
    )(x, g.reshape(1, -1))


def _qkv_kernel(*refs, rope, scale, tail_start):
    it = iter(refs)
    xn_ref, w_ref = next(it), next(it)
    cos_ref = sin_ref = None
    if rope:
        cos_ref, sin_ref = next(it), next(it)
    d1_ref, d4_ref, d16_ref = next(it), next(it), next(it)
    tail_ref = next(it) if tail_start is not None else None
    nat_ref, res4_ref = next(it), next(it)

    tm = xn_ref.shape[0]
    z = jnp.dot(xn_ref[...], w_ref[...], preferred_element_type=_F32)
    n4, n16 = tm // 4, tm // 16
    for g in range(z.shape[1] // LANES):
        sl = slice(g * LANES, (g + 1) * LANES)
        r = z[:, sl]
        if rope:
            r = _rope_group(r, cos_ref[...], sin_ref[...])
        if tail_ref is not None:
            @pl.when(pl.program_id(0) >= tail_start)
            def _():
                tail_ref[sl, :] = r.T
        if scale != 1.0:
            r = r * scale
        d1_ref[:, sl] = r.astype(d1_ref.dtype)
        nat_ref[g] = r
        for r4 in range(4):
            blk = nat_ref[g, pl.ds(r4, n4, stride=4), :]
            d4_ref[r4, :, sl] = blk.astype(d4_ref.dtype)
            res4_ref[g, r4 * n4:(r4 + 1) * n4, :] = blk
        for r4 in range(4):
            for a in range(4):
                blk = res4_ref[g, pl.ds(r4 * n4 + a, n16, stride=4), :]
                d16_ref[r4 + 4 * a, :, sl] = blk.astype(d16_ref.dtype)


def _qkv_proj(xn, w_seg, tables, *, tm, scale, tail_rows):
    s_len, d_model = xn.shape
    d_seg = w_seg.shape[1]
    n_i = s_len // tm
    rope = tables is not None
    in_specs = [pl.BlockSpec((tm, d_model), lambda i: (i, 0)),
                pl.BlockSpec((d_model, d_seg), lambda i: (0, 0))]
    args = [xn, w_seg]
    if rope:
        in_specs += [pl.BlockSpec((tm, LANES), lambda i: (i, 0))] * 2
        args += list(tables)
    out_shape = [jax.ShapeDtypeStruct((s_len, d_seg), _BF16),
                 jax.ShapeDtypeStruct((4, s_len // 4, d_seg), _BF16),
                 jax.ShapeDtypeStruct((16, s_len // 16, d_seg), _BF16)]
    out_specs = [pl.BlockSpec((tm, d_seg), lambda i: (i, 0)),
                 pl.BlockSpec((4, tm // 4, d_seg), lambda i: (0, i, 0)),
                 pl.BlockSpec((16, tm // 16, d_seg), lambda i: (0, i, 0))]
    tail_start = None
    if tail_rows:
        tail_start = n_i - tail_rows // tm
        out_shape.append(jax.ShapeDtypeStruct((d_seg, tail_rows), _F32))
        out_specs.append(pl.BlockSpec((d_seg, tm), lambda i: (0, jnp.maximum(i - tail_start, 0))))
    slab = pltpu.VMEM((d_seg // LANES, tm, LANES), _F32)
    return pl.pallas_call(
        functools.partial(_qkv_kernel, rope=rope, scale=scale, tail_start=tail_start),
        grid=(n_i,),
        in_specs=in_specs,
        out_specs=out_specs,
        out_shape=out_shape,
        scratch_shapes=[slab, slab],
        compiler_params=_params("arbitrary"),
        name="qkv_proj",
    )(*args)


def _gate_kernel(*refs, mode):
    if mode == "gelu_ln":
        xn_ref, w_ref, lng_ref, lnb_ref, o_ref = refs
    else:
        xn_ref, w_ref, o_ref = refs
    z = jnp.dot(xn_ref[...], w_ref[...], preferred_element_type=_F32)
    if mode == "silu":
        y = jax.nn.silu(z)
    elif mode == "gelu":
        y = jax.nn.gelu(z)
    else:
        y = _gelu_layer_norm(z, lng_ref[...], lnb_ref[...])
    o_ref[...] = y.astype(o_ref.dtype)


def _gate_proj(xn, w_seg, *, tm, mode, ln=None):
    s_len, d_model = xn.shape
    d_seg = w_seg.shape[1]
    in_specs = [pl.BlockSpec((tm, d_model), lambda i: (i, 0)),
                pl.BlockSpec((d_model, d_seg), lambda i: (0, 0))]
    args = [xn, w_seg]
    if mode == "gelu_ln":
        in_specs += [pl.BlockSpec((1, d_seg), lambda i: (0, 0))] * 2
        args += [ln[0].reshape(1, -1), ln[1].reshape(1, -1)]
    return pl.pallas_call(
        functools.partial(_gate_kernel, mode=mode),
        grid=(s_len // tm,),
        in_specs=in_specs,
        out_specs=pl.BlockSpec((tm, d_seg), lambda i: (i, 0)),
        out_shape=jax.ShapeDtypeStruct((s_len, d_seg), _BF16),
        compiler_params=_params("arbitrary"),
        name=f"gate_proj_{mode}",
    )(*args)


def _attn_kernel(q_ref, kp_ref, kc_ref, vp_ref, vc_ref, o_ref, m_ref, den_ref, kall_ref, vall_ref):
    n = pl.program_id(1)
    first_step = jnp.logical_and(pl.program_id(0) == 0, n == 0)
    n_pairs = q_ref.shape[1] // LANES
    two = 2 * BLOCK

    kall_ref[0:BLOCK, :] = kp_ref[...]
    kall_ref[BLOCK:, :] = kc_ref[...]
    for hp in range(n_pairs):
        sl = slice(hp * LANES, (hp + 1) * LANES)
        ext = slice(2 * hp * LANES, (2 * hp + 1) * LANES)
        vall_ref[0:BLOCK, ext] = vp_ref[:, sl]
        vall_ref[BLOCK:, ext] = vc_ref[:, sl]

    @pl.when(first_step)
    def _():
        ones = jnp.ones((vall_ref.shape[0], LANES), vall_ref.dtype)
        for hp in range(n_pairs):
            vall_ref[:, (2 * hp + 1) * LANES:(2 * hp + 2) * LANES] = ones

    qi = lax.broadcasted_iota(jnp.int32, (two, two), 0) % BLOCK
    si = lax.broadcasted_iota(jnp.int32, (two, two), 1)
    dist = qi + BLOCK - si
    band_bias = jnp.where((dist >= 0) & (dist <= BLOCK), 0.0, NEG_INF).astype(_F32)
    lane = lax.broadcasted_iota(jnp.int32, (BLOCK, LANES), 1)
    head0 = lane < HEAD_DIM
    zero = jnp.zeros((BLOCK, LANES), q_ref.dtype)

    for j in range(q_ref.shape[0] // BLOCK):
        rows = slice(j * BLOCK, (j + 1) * BLOCK)
        keys = slice(j * BLOCK, j * BLOCK + two)
        bias = band_bias
        if j == 0:
            bias = band_bias + jnp.where(jnp.logical_and(si < BLOCK, n == 0), NEG_INF, 0.0)
        m_acc = jnp.zeros((BLOCK, LANES), _F32)
        den_acc = jnp.ones((BLOCK, LANES), _F32)
        for hp in range(n_pairs):
            sl = slice(hp * LANES, (hp + 1) * LANES)
            q2 = q_ref[rows, sl]
            qs = jnp.concatenate([jnp.where(head0, q2, zero), jnp.where(head0, zero, q2)], axis=0)
            s = lax.dot_general(qs, kall_ref[keys, sl], (((1,), (1,)), ((), ())),
                                preferred_element_type=_F32) + bias
            m = jnp.max(s, axis=-1, keepdims=True)
            p = jnp.exp2(s - m).astype(vall_ref.dtype)
            pv = jnp.dot(p, vall_ref[keys, 2 * hp * LANES:(2 * hp + 2) * LANES], preferred_element_type=_F32)
            num, den = pv[:, :LANES], pv[:, LANES:]
            o_ref[rows, sl] = jnp.where(head0, num[:BLOCK], num[BLOCK:]).astype(o_ref.dtype)
            m_acc = jnp.where(lane == 2 * hp, m[:BLOCK], jnp.where(lane == 2 * hp + 1, m[BLOCK:], m_acc))
            den_acc = jnp.where(lane == 2 * hp, den[:BLOCK],
                                jnp.where(lane == 2 * hp + 1, den[BLOCK:], den_acc))
        m_ref[rows, :] = m_acc
        den_ref[rows, :] = den_acc


def _dilated_attention(q, k, v, *, q_rows):
    d, sub_len, d_attn = q.shape
    per = q_rows // BLOCK
    cur = pl.BlockSpec((None, q_rows, d_attn), lambda r, n: (r, n, 0))
    prev = pl.BlockSpec((None, BLOCK, d_attn), lambda r, n: (r, jnp.maximum(n * per - 1, 0), 0))
    stat = pl.BlockSpec((None, q_rows, LANES), lambda r, n: (r, n, 0))
    stat_shape = jax.ShapeDtypeStruct((d, sub_len, LANES), _F32)
    return pl.pallas_call(
        _attn_kernel,
        grid=(d, sub_len // q_rows),
        in_specs=[cur, prev, cur, prev, cur],
        out_specs=[cur, stat, stat],
        out_shape=[jax.ShapeDtypeStruct((d, sub_len, d_attn), q.dtype), stat_shape, stat_shape],
        scratch_shapes=[pltpu.VMEM((q_rows + BLOCK, d_attn), q.dtype),
                        pltpu.VMEM((q_rows + BLOCK, 2 * d_attn), q.dtype)],
        compiler_params=_params("arbitrary", "arbitrary"),
        name=f"dilated_attn_d{d}",
    )(q, k, k, v, v)


def _expand_heads(w, expand2):
    n_heads = expand2.shape[1] // HEAD_DIM
    hi = w.astype(_BF16).astype(_F32)
    lane = lax.broadcasted_iota(jnp.int32, w.shape, 1)
    lhs = jnp.where(lane < n_heads, hi, pltpu.roll(w - hi, n_heads, 1))
    return jnp.dot(lhs.astype(_BF16), expand2, preferred_element_type=_F32)


def _out_proj_kernel(o1_ref, o4_ref, o16_ref, m1_ref, m4_ref, m16_ref, d1_ref, d4_ref, d16_ref,
                     ga_ref, u_ref, vc_ref, gb_ref, x_ref, wout_ref, ws_ref, bst_ref, fg_ref, y_ref,
                     a_ref, c_ref, n4_ref, n16_ref, st_ref):
    tm, d_attn = o1_ref.shape
    n_heads = d_attn // HEAD_DIM
    n_lane_groups = d_attn // LANES
    for r in range(4):
        st_ref[0, pl.ds(r, tm // 4, stride=4), :] = m4_ref[r]
        st_ref[1, pl.ds(r, tm // 4, stride=4), :] = d4_ref[r]
        for g in range(n_lane_groups):
            n4_ref[g, pl.ds(r, tm // 4, stride=4), :] = o4_ref[r, :, g * LANES:(g + 1) * LANES].astype(_F32)
    for r in range(16):
        st_ref[2, pl.ds(r, tm // 16, stride=16), :] = m16_ref[r]
        st_ref[3, pl.ds(r, tm // 16, stride=16), :] = d16_ref[r]
        for g in range(n_lane_groups):
            n16_ref[g, pl.ds(r, tm // 16, stride=16), :] = o16_ref[r, :, g * LANES:(g + 1) * LANES].astype(_F32)

    ms = [m1_ref[...], st_ref[0], st_ref[2]]
    dens = [d1_ref[...], st_ref[1], st_ref[3]]
    mx = jnp.maximum(jnp.maximum(ms[0], ms[1]), ms[2])
    es = [jnp.exp2(m - mx) for m in ms]
    tot = es[0] * dens[0] + es[1] * dens[1] + es[2] * dens[2]
    row_head = lax.broadcasted_iota(jnp.int32, (LANES, d_attn), 0) % n_heads
    row_used = lax.broadcasted_iota(jnp.int32, (LANES, d_attn), 0) < 2 * n_heads
    col_head = lax.broadcasted_iota(jnp.int32, (LANES, d_attn), 1) // HEAD_DIM
    expand2 = jnp.logical_and(row_head == col_head, row_used).astype(_BF16)
    ws = [_expand_heads(e / tot, expand2) for e in es]
    for g in range(n_lane_groups):
        sl = slice(g * LANES, (g + 1) * LANES)
        a = (ws[0][:, sl] * o1_ref[:, sl].astype(_F32) + ws[1][:, sl] * n4_ref[g] + ws[2][:, sl] * n16_ref[g])
        a_ref[:, sl] = (a * ga_ref[:, sl].astype(_F32)).astype(a_ref.dtype)

    row = lax.broadcasted_iota(jnp.int32, (CHUNK, CHUNK), 0)
    col = lax.broadcasted_iota(jnp.int32, (CHUNK, CHUNK), 1)
    tril = row >= col
    for g in range(ws_ref.shape[0]):
        wm = jnp.where(tril, ws_ref[g], 0.0).astype(_BF16)
        bias = bst_ref[:, g:g + 1]
        gs = slice(g * GROUP_WIDTH_B, (g + 1) * GROUP_WIDTH_B)
        for c in range(tm // CHUNK):
            rs = slice(c * CHUNK, (c + 1) * CHUNK)
            mixed = jnp.dot(wm, vc_ref[rs, gs], preferred_element_type=_F32) + bias
            cv = u_ref[rs, gs].astype(_F32) * mixed * gb_ref[rs, gs].astype(_F32)
            c_ref[rs, gs] = cv.astype(c_ref.dtype)

    acc = jnp.dot(a_ref[...], wout_ref[:d_attn, :], preferred_element_type=_F32)
    acc = acc + jnp.dot(c_ref[...], wout_ref[d_attn:, :], preferred_element_type=_F32)
    xo = x_ref[...] + acc
    ms = jnp.mean(xo * xo, axis=-1, keepdims=True)
    y_ref[...] = xo * lax.rsqrt(ms + EPS) * fg_ref[...]


def _out_proj(os, ms, dens, ga, u, vc, gb, x, wout_bf16, w_s, b_s_t, final_g, *, tm):
    m, d_model = x.shape
    d_attn = ga.shape[1]
    d_chunk = u.shape[1]
    rows = lambda n: pl.BlockSpec((tm, n), lambda i: (i, 0))
    res = lambda d, n: pl.BlockSpec((d, tm // d, n), lambda i: (0, i, 0))
    whole = lambda a: pl.BlockSpec(a.shape, lambda i: (0,) * a.ndim)
    by_pattern = lambda n: [rows(n), res(4, n), res(16, n)]
    fg = final_g.reshape(1, -1)
    slab = pltpu.VMEM((d_attn // LANES, tm, LANES), _F32)
    return pl.pallas_call(
        _out_proj_kernel,
        grid=(m // tm,),
        in_specs=by_pattern(d_attn) + by_pattern(LANES) + by_pattern(LANES)
        + [rows(d_attn), rows(d_chunk), rows(d_chunk), rows(d_chunk), rows(d_model),
           whole(wout_bf16), whole(w_s), whole(b_s_t), whole(fg)],
        out_specs=rows(d_model),
        out_shape=jax.ShapeDtypeStruct((m, d_model), _F32),
        scratch_shapes=[pltpu.VMEM((tm, d_attn), _BF16), pltpu.VMEM((tm, d_chunk), _BF16), slab, slab,
                        pltpu.VMEM((4, tm, LANES), _F32)],
        compiler_params=_params("arbitrary"),
        name="out_proj",
    )(*os, *ms, *dens, ga, u, vc, gb, x, wout_bf16, w_s, b_s_t, fg)


def _sample_in_proj_kernel(x_ref, g_ref, w_ref, cos_ref, sin_ref, lng_ref, lnb_ref,
                           q_ref, k_ref, v_ref, ga_ref, u_ref, vc_ref, gb_ref, xn_ref):
    j = pl.program_id(0)

    @pl.when(j == 0)
    def _():
        xf = x_ref[...]
        ms = jnp.mean(xf * xf, axis=-1, keepdims=True)
        xn_ref[...] = (xf * lax.rsqrt(ms + EPS) * g_ref[...]).astype(xn_ref.dtype)

    z = jnp.dot(xn_ref[...], w_ref[...], preferred_element_type=_F32)

    def rope_to(ref, scale):
        for g in range(z.shape[1] // LANES):
            sl = slice(g * LANES, (g + 1) * LANES)
            ref[:, sl] = _rope_group(z[:, sl], cos_ref[...], sin_ref[...]) * scale

    @pl.when(j == 0)
    def _():
        rope_to(q_ref, HEAD_DIM ** -0.5)

    @pl.when(j == 1)
    def _():
        rope_to(k_ref, 1.0)

    @pl.when(j == 2)
    def _():
        v_ref[...] = z

    @pl.when(j == 3)
    def _():
        ga_ref[...] = jax.nn.silu(z)

    @pl.when(j == 4)
    def _():
        u_ref[...] = jax.nn.gelu(z)

    @pl.when(j == 5)
    def _():
        vc_ref[...] = _gelu_layer_norm(z, lng_ref[...], lnb_ref[...])

    @pl.when(j == 6)
    def _():
        gb_ref[...] = jax.nn.silu(z)


def _sample_in_proj(x, norm_g, w_bf16, cos_t, sin_t, ln_g, ln_b):
    m, d_model = x.shape
    d_seg = w_bf16.shape[1] // N_SEGMENTS
    whole = lambda r, c: pl.BlockSpec((r, c), lambda j: (0, 0))
    return pl.pallas_call(
        _sample_in_proj_kernel,
        grid=(N_SEGMENTS,),
        in_specs=[whole(m, d_model), whole(1, d_model),
                  pl.BlockSpec((d_model, d_seg), lambda j: (0, j)),
                  whole(m, LANES), whole(m, LANES), whole(1, d_seg), whole(1, d_seg)],
        out_specs=[whole(m, d_seg)] * N_SEGMENTS,
        out_shape=[jax.ShapeDtypeStruct((m, d_seg), _F32)] * N_SEGMENTS,
        scratch_shapes=[pltpu.VMEM((m, d_model), _BF16)],
        compiler_params=_params("arbitrary"),
        name="sample_in_proj",
    )(x, norm_g.reshape(1, -1), w_bf16, cos_t, sin_t, ln_g.reshape(1, -1), ln_b.reshape(1, -1))


def _sample_attn_kernel(q_ref, kn_ref, vn_ref, kt_ref, vt_ref, o_ref, *, wb):
    t_new, width = q_ref.shape
    n_heads = width // HEAD_DIM
    rows = t_new * n_heads
    head_row = lax.broadcasted_iota(jnp.int32, (n_heads, width), 0)
    head_col = lax.broadcasted_iota(jnp.int32, (n_heads, width), 1) // HEAD_DIM
    own = head_row == head_col
    qbd = jnp.concatenate([jnp.where(own, q_ref[t:t + 1, :], 0.0) for t in range(t_new)], axis=0)

    t_row = lax.broadcasted_iota(jnp.int32, (rows, wb), 0) // n_heads
    pos = lax.broadcasted_iota(jnp.int32, (rows, wb), 1)
    dist = wb + t_row - pos
    mult = jnp.zeros((rows, wb), _F32)
    for window, d in DILATIONS:
        mult = mult + jnp.where((dist % d == 0) & (dist <= window), 1.0, 0.0)
    valid = mult > 0.0

    s = jnp.dot(qbd.astype(_BF16), kt_ref[...].astype(_BF16), preferred_element_type=_F32)
    s = jnp.where(valid, s, NEG_INF)
    m = jnp.max(s, axis=-1, keepdims=True)
    t_col = lax.broadcasted_iota(jnp.int32, (rows, 1), 0) // n_heads
    s_new, mult_new = [], []
    for tp in range(t_new):
        s_new.append(jnp.sum(qbd * kn_ref[tp:tp + 1, :], axis=-1, keepdims=True))
        dn = t_col - tp
        mn = jnp.zeros((rows, 1), _F32)
        for window, d in DILATIONS:
            mn = mn + jnp.where((dn >= 0) & (dn % d == 0), 1.0, 0.0)
        mult_new.append(mn)
        m = jnp.maximum(m, jnp.where(mn > 0.0, s_new[tp], NEG_INF))
    p = jnp.exp(s - m) * mult
    den = jnp.sum(p, axis=-1, keepdims=True)
    num = lax.dot_general(p.astype(_BF16), vt_ref[...].astype(_BF16), (((1,), (1,)), ((), ())),
                          preferred_element_type=_F32)
    for tp in range(t_new):
        pn = jnp.where(mult_new[tp] > 0.0, jnp.exp(s_new[tp] - m), 0.0) * mult_new[tp]
        den = den + pn
        num = num + pn * vn_ref[tp:tp + 1, :]
    res = num / den
    for t in range(t_new):
        blk = res[t * n_heads:(t + 1) * n_heads, :]
        o_ref[t:t + 1, :] = jnp.sum(jnp.where(own, blk, 0.0), axis=0, keepdims=True)


def _sample_attention(q, k_new, v_new, cache_kt, cache_vt, *, head_groups):
    db, t_new, d_attn = q.shape
    wb = cache_kt.shape[2]
    assert wb == MAX_WINDOW
    width = d_attn // head_groups
    new = pl.BlockSpec((None, t_new, width), lambda b, g: (b, 0, g))
    cache = pl.BlockSpec((None, width, wb), lambda b, g: (b, g, 0))
    return pl.pallas_call(
        functools.partial(_sample_attn_kernel, wb=wb),
        grid=(db, head_groups),
        in_specs=[new, new, new, cache, cache],
        out_specs=new,
        out_shape=jax.ShapeDtypeStruct((db, t_new, d_attn), _F32),
        compiler_params=_params("arbitrary", "arbitrary"),
        name="sample_attn",
    )(q, k_new, v_new, cache_kt, cache_vt)


def _sample_out_kernel(oa_ref, ga_ref, u_ref, vc_ref, gb_ref, x_ref, wout_ref, coef_ref, bias_ref, fg_ref,
                       y_ref, c_ref, *, t_new):
    m, d_attn = oa_ref.shape
    db = m // t_new
    a = (oa_ref[...] * ga_ref[...]).astype(_BF16)
    for t in range(t_new):
        mixed = jnp.zeros((db, c_ref.shape[1]), _F32) + bias_ref[t:t + 1, :]
        for s in range(t + 1):
            mixed = mixed + coef_ref[t * t_new + s:t * t_new + s + 1, :] * vc_ref[s * db:(s + 1) * db, :]
        rs = slice(t * db, (t + 1) * db)
        c_ref[rs, :] = (u_ref[rs, :] * mixed * gb_ref[rs, :]).astype(c_ref.dtype)
    acc = jnp.dot(a, wout_ref[:d_attn, :], preferred_element_type=_F32)
    acc = acc + jnp.dot(c_ref[...], wout_ref[d_attn:, :], preferred_element_type=_F32)
    xo = x_ref[...] + acc
    ms = jnp.mean(xo * xo, axis=-1, keepdims=True)
    y_ref[...] = xo * lax.rsqrt(ms + EPS) * fg_ref[...]


def _sample_out(oa, ga, u, vc, gb, x, wout_bf16, coef, bias, final_g, *, t_new):
    m, d_model = x.shape
    args = (oa, ga, u, vc, gb, x, wout_bf16, coef, bias, final_g.reshape(1, -1))
    whole = lambda a: pl.BlockSpec(a.shape, lambda i: (0,) * a.ndim)
    return pl.pallas_call(
        functools.partial(_sample_out_kernel, t_new=t_new),
        grid=(1,),
        in_specs=[whole(a) for a in args],
        out_specs=pl.BlockSpec((m, d_model), lambda i: (0, 0)),
        out_shape=jax.ShapeDtypeStruct((m, d_model), _F32),
        scratch_shapes=[pltpu.VMEM((m, u.shape[1]), _BF16)],
        compiler_params=_params("arbitrary"),
        name="sample_out",
    )(*args)


def kernel(x_prompt, x_sample, cache_k, cache_v, norm_g, w_in, ln_g, ln_b, w_s, b_s, w_out, final_g):
    batch, s_len, d_model = x_prompt.shape
    db, t_new, _ = x_sample.shape
    depth, _, wb, n_heads, head_dim = cache_k.shape
    d_attn = n_heads * head_dim
    d_chunk = w_out.shape[1] - d_attn
    assert batch == 1 and depth == 1 and head_dim == HEAD_DIM
    assert w_in.shape[2] == 4 * d_attn + 3 * d_chunk and d_attn == d_chunk
    tail = min(MAX_WINDOW, s_len)

    w_in_b = w_in[0].astype(_BF16)
    w_out_b = w_out[0].astype(_BF16)
    seg = lambda j: w_in_b[:, j * d_attn:(j + 1) * d_attn]
    ng, lg, lb = norm_g[0], ln_g[0], ln_b[0]

    tm = 512
    xp = x_prompt.reshape(s_len, d_model)
    xn = _rms_norm(xp, ng, tm=tm)
    tables = _rope_tables(jnp.arange(s_len, dtype=jnp.int32))
    q1, q4, q16 = _qkv_proj(xn, seg(0), tables, tm=tm, scale=HEAD_DIM ** -0.5 * math.log2(math.e), tail_rows=0)
    k1, k4, k16, k_tail_t = _qkv_proj(xn, seg(1), tables, tm=tm, scale=1.0, tail_rows=tail)
    v1, v4, v16, v_tail_t = _qkv_proj(xn, seg(2), None, tm=tm, scale=1.0, tail_rows=tail)
    ga = _gate_proj(xn, seg(3), tm=tm, mode="silu")
    u = _gate_proj(xn, seg(4), tm=tm, mode="gelu")
    vc = _gate_proj(xn, seg(5), tm=tm, mode="gelu_ln", ln=(lg, lb))
    gb = _gate_proj(xn, seg(6), tm=tm, mode="silu")
    q_rows = 512
    o1, m1, d1 = _dilated_attention(q1[None], k1[None], v1[None], q_rows=q_rows)
    o4, m4, d4 = _dilated_attention(q4, k4, v4, q_rows=q_rows)
    o16, m16, d16 = _dilated_attention(q16, k16, v16, q_rows=q_rows)
    y_prompt = _out_proj((o1[0], o4, o16), (m1[0], m4, m16), (d1[0], d4, d16), ga, u, vc, gb, xp, w_out_b,
                         w_s[0], b_s[0].T, final_g, tm=256)
    untranspose = lambda a: jnp.transpose(a.reshape(n_heads, head_dim, tail), (2, 0, 1))

    xs = jnp.transpose(x_sample, (1, 0, 2)).reshape(t_new * db, d_model)
    pos_s = PAST_LEN + jnp.repeat(jnp.arange(t_new, dtype=jnp.int32), db)
    cos_s, sin_s = _rope_tables(pos_s)
    qs, ks, vs, gas, us, vcs, gbs = _sample_in_proj(xs, ng, w_in_b, cos_s, sin_s, lg, lb)
    to_b_major = lambda a: jnp.transpose(a.reshape(t_new, db, -1), (1, 0, 2))
    qs_b, ks_b, vs_b, vcs_b = (to_b_major(a) for a in (qs, ks, vs, vcs))
    cache_t = lambda c: jnp.transpose(c[0], (0, 2, 3, 1)).reshape(db, d_attn, wb)
    oa = _sample_attention(qs_b, ks_b, vs_b, cache_t(cache_k), cache_t(cache_v), head_groups=2)
    oa_t = jnp.transpose(oa, (1, 0, 2)).reshape(t_new * db, d_attn)
    coef = jnp.repeat(jnp.transpose(w_s[0][:, :t_new, :t_new], (1, 2, 0)).reshape(t_new * t_new, -1),
                      GROUP_WIDTH_B, axis=1)
    bias = jnp.repeat(jnp.transpose(b_s[0][:, :t_new], (1, 0)), GROUP_WIDTH_B, axis=1)
    ys = _sample_out(oa_t, gas, us, vcs, gbs, xs, w_out_b, coef, bias, final_g, t_new=t_new)
    y_sample = jnp.transpose(ys.reshape(t_new, db, d_model), (1, 0, 2))

    hs = (n_heads, head_dim)
    return (
        y_prompt.reshape(batch, s_len, d_model),
        y_sample,
        untranspose(k_tail_t).reshape(depth, batch, tail, *hs),
        untranspose(v_tail_t).reshape(depth, batch, tail, *hs),
        ks_b.reshape(depth, db, t_new, *hs),
        vs_b.reshape(depth, db, t_new, *hs),
        vcs_b.reshape(depth, db, t_new, d_chunk),
    )
```

```python
import functools
import math

import jax
import jax.numpy as jnp
from jax import lax
from jax.experimental import pallas as pl
from jax.experimental.pallas import tpu as pltpu

HEAD_DIM = 64
BLOCK = 128
CHUNK = 128
GROUP_WIDTH_B = 128
DILATIONS = ((128, 1), (512, 4), (2048, 16))
MAX_WINDOW = 2048
PAST_LEN = 16384
ROPE_THETA = 10000.0
EPS = 1e-6
NEG_INF = -1e30
N_SEGMENTS = 7
SEG_Q, SEG_K, SEG_V, SEG_GA, SEG_U, SEG_VC, SEG_GB = range(N_SEGMENTS)

LANES = 128
VMEM_LIMIT_BYTES = 56 * 1024 * 1024
SUB_ROWS = 256

_BF16 = jnp.bfloat16
_F32 = jnp.float32


def _params(*semantics):
    return pltpu.CompilerParams(dimension_semantics=semantics, vmem_limit_bytes=VMEM_LIMIT_BYTES)


def _rope_tables(pos):
    half = HEAD_DIM // 2
    inv = jnp.exp(-math.log(ROPE_THETA) * jnp.arange(half, dtype=_F32) / half)
    ang = pos.astype(_F32)[:, None] * inv[None, :]
    cos = jnp.cos(ang)
    sin = jnp.sin(ang)
    cos_t = jnp.concatenate([cos, cos, cos, cos], axis=-1)
    sin_t = jnp.concatenate([-sin, sin, -sin, sin], axis=-1)
    return cos_t, sin_t


def _rope_group(zg, cos, sin_signed):
    lane = lax.broadcasted_iota(jnp.int32, zg.shape, 1)
    first_half = (lane % HEAD_DIM) < (HEAD_DIM // 2)
    partner = jnp.where(first_half, pltpu.roll(zg, LANES - HEAD_DIM // 2, 1),
                        pltpu.roll(zg, HEAD_DIM // 2, 1))
    return zg * cos + partner * sin_signed


def _gelu_layer_norm(z, g, b):
    h = jax.nn.gelu(z)
    mu = jnp.mean(h, axis=-1, keepdims=True)
    hc = h - mu
    var = jnp.mean(hc * hc, axis=-1, keepdims=True)
    return hc * lax.rsqrt(var + EPS) * g + b


def _proj_kernel(*refs, seg, scale, tail_start):
    qkv = seg in (SEG_Q, SEG_K, SEG_V)
    rope = seg in (SEG_Q, SEG_K)
    it = iter(refs)
    x_ref = next(it)
    g_ref = next(it) if seg == SEG_Q else None
    w_ref = next(it)
    cos_ref, sin_ref = (next(it), next(it)) if rope else (None, None)
    lng_ref, lnb_ref = (next(it), next(it)) if seg == SEG_VC else (None, None)
    xn_out_ref = next(it) if seg == SEG_Q else None
    if qkv:
        d1_ref, d4_ref, d16_ref = next(it), next(it), next(it)
        tail_ref = next(it) if tail_start is not None else None
    else:
        o_ref = next(it)
    wb_ref = next(it)
    if qkv:
        nat_ref, res4_ref = next(it), next(it)

    @pl.when(pl.program_id(0) == 0)
    def _():
        wb_ref[...] = w_ref[...].astype(wb_ref.dtype)

    tm = x_ref.shape[0]
    sub = min(SUB_ROWS, tm)
    n4, n16 = sub // 4, sub // 16
    for t in range(tm // sub):
        rs = slice(t * sub, (t + 1) * sub)
        if seg == SEG_Q:
            xf = x_ref[rs, :]
            ms = jnp.mean(xf * xf, axis=-1, keepdims=True)
            xb = (xf * lax.rsqrt(ms + EPS) * g_ref[...]).astype(_BF16)
            xn_out_ref[rs, :] = xb
        else:
            xb = x_ref[rs, :]
        z = jnp.dot(xb, wb_ref[...], preferred_element_type=_F32)
        if seg in (SEG_GA, SEG_GB):
            o_ref[rs, :] = jax.nn.silu(z).astype(o_ref.dtype)
        elif seg == SEG_U:
            o_ref[rs, :] = jax.nn.gelu(z).astype(o_ref.dtype)
        elif seg == SEG_VC:
            o_ref[rs, :] = _gelu_layer_norm(z, lng_ref[...], lnb_ref[...]).astype(o_ref.dtype)
        else:
            for g in range(z.shape[1] // LANES):
                sl = slice(g * LANES, (g + 1) * LANES)
                r = z[:, sl]
                if rope:
                    r = _rope_group(r, cos_ref[rs, :], sin_ref[rs, :])
                if tail_ref is not None:
                    tail_ref[sl, rs] = r.T
                if scale != 1.0:
                    r = r * scale
                d1_ref[rs, sl] = r.astype(d1_ref.dtype)
                nat_ref[g, rs, :] = r
                for r4 in range(4):
                    blk = nat_ref[g, pl.ds(t * sub + r4, n4, stride=4), :]
                    d4_ref[r4, t * n4:(t + 1) * n4, sl] = blk.astype(d4_ref.dtype)
                    res4_ref[g, t * sub + r4 * n4:t * sub + (r4 + 1) * n4, :] = blk
                for r4 in range(4):
                    for a in range(4):
                        blk = res4_ref[g, pl.ds(t * sub + r4 * n4 + a, n16, stride=4), :]
                        d16_ref[r4 + 4 * a, t * n16:(t + 1) * n16, sl] = blk.astype(d16_ref.dtype)


def _proj(x, w_in, seg, *, tm, norm_g=None, tables=None, ln=None, scale=1.0, tail_rows=0):
    s_len, d_model = x.shape
    d_seg = w_in.shape[1] // N_SEGMENTS
    n_i = s_len // tm
    qkv = seg in (SEG_Q, SEG_K, SEG_V)
    row_vec = lambda n: pl.BlockSpec((1, n), lambda i: (0, 0))
    in_specs = [pl.BlockSpec((tm, d_model), lambda i: (i, 0))]
    args = [x]
    if seg == SEG_Q:
        in_specs.append(row_vec(d_model))
        args.append(norm_g.reshape(1, -1))
    in_specs.append(pl.BlockSpec((d_model, d_seg), lambda i: (0, seg), pipeline_mode=pl.Buffered(1)))
    args.append(w_in)
    if seg in (SEG_Q, SEG_K):
        in_specs += [pl.BlockSpec((tm, LANES), lambda i: (i, 0))] * 2
        args += list(tables)
    if seg == SEG_VC:
        in_specs += [row_vec(d_seg)] * 2
        args += [ln[0].reshape(1, -1), ln[1].reshape(1, -1)]
    out_shape, out_specs = [], []
    if seg == SEG_Q:
        out_shape.append(jax.ShapeDtypeStruct((s_len, d_model), _BF16))
        out_specs.append(pl.BlockSpec((tm, d_model), lambda i: (i, 0)))
    scratch = [pltpu.VMEM((d_model, d_seg), _BF16)]
    tail_start = None
    if qkv:
        out_shape += [jax.ShapeDtypeStruct((s_len, d_seg), _BF16),
                      jax.ShapeDtypeStruct((4, s_len // 4, d_seg), _BF16),
                      jax.ShapeDtypeStruct((16, s_len // 16, d_seg), _BF16)]
        out_specs += [pl.BlockSpec((tm, d_seg), lambda i: (i, 0)),
                      pl.BlockSpec((4, tm // 4, d_seg), lambda i: (0, i, 0)),
                      pl.BlockSpec((16, tm // 16, d_seg), lambda i: (0, i, 0))]
        if tail_rows:
            tail_start = n_i - tail_rows // tm
            out_shape.append(jax.ShapeDtypeStruct((d_seg, tail_rows), _F32))
            out_specs.append(pl.BlockSpec((d_seg, tm), lambda i: (0, jnp.maximum(i - tail_start, 0))))
        scratch += [pltpu.VMEM((d_seg // LANES, tm, LANES), _F32)] * 2
    else:
        out_shape.append(jax.ShapeDtypeStruct((s_len, d_seg), _BF16))
        out_specs.append(pl.BlockSpec((tm, d_seg), lambda i: (i, 0)))
    return pl.pallas_call(
        functools.partial(_proj_kernel, seg=seg, scale=scale, tail_start=tail_start),
        grid=(n_i,),
        in_specs=in_specs,
        out_specs=out_specs,
        out_shape=out_shape,
        scratch_shapes=scratch,
        compiler_params=_params("arbitrary"),
        name=f"proj_seg{seg}",
    )(*args)


def _attn_kernel(q_ref, kp_ref, kc_ref, vp_ref, vc_ref, o_ref, m_ref, den_ref, kall_ref, vall_ref):
    n = pl.program_id(1)
    first_step = jnp.logical_and(pl.program_id(0) == 0, n == 0)
    n_pairs = q_ref.shape[1] // LANES
    two = 2 * BLOCK

    kall_ref[0:BLOCK, :] = kp_ref[...]
    kall_ref[BLOCK:, :] = kc_ref[...]
    for hp in range(n_pairs):
        sl = slice(hp * LANES, (hp + 1) * LANES)
        ext = slice(2 * hp * LANES, (2 * hp + 1) * LANES)
        vall_ref[0:BLOCK, ext] = vp_ref[:, sl]
        vall_ref[BLOCK:, ext] = vc_ref[:, sl]

    @pl.when(first_step)
    def _():
        ones = jnp.ones((vall_ref.shape[0], LANES), vall_ref.dtype)
        for hp in range(n_pairs):
            vall_ref[:, (2 * hp + 1) * LANES:(2 * hp + 2) * LANES] = ones

    qi = lax.broadcasted_iota(jnp.int32, (two, two), 0) % BLOCK
    si = lax.broadcasted_iota(jnp.int32, (two, two), 1)
    dist = qi + BLOCK - si
    band_bias = jnp.where((dist >= 0) & (dist <= BLOCK), 0.0, NEG_INF).astype(_F32)
    lane = lax.broadcasted_iota(jnp.int32, (BLOCK, LANES), 1)
    head0 = lane < HEAD_DIM
    zero = jnp.zeros((BLOCK, LANES), q_ref.dtype)

    for j in range(q_ref.shape[0] // BLOCK):
        rows = slice(j * BLOCK, (j + 1) * BLOCK)
        keys = slice(j * BLOCK, j * BLOCK + two)
        bias = band_bias
        if j == 0:
            bias = band_bias + jnp.where(jnp.logical_and(si < BLOCK, n == 0), NEG_INF, 0.0)
        m_acc = jnp.zeros((BLOCK, LANES), _F32)
        den_acc = jnp.ones((BLOCK, LANES), _F32)
        for hp in range(n_pairs):
            sl = slice(hp * LANES, (hp + 1) * LANES)
            q2 = q_ref[rows, sl]
            qs = jnp.concatenate([jnp.where(head0, q2, zero), jnp.where(head0, zero, q2)], axis=0)
            s = lax.dot_general(qs, kall_ref[keys, sl], (((1,), (1,)), ((), ())),
                                preferred_element_type=_F32) + bias
            m = jnp.max(s, axis=-1, keepdims=True)
            p = jnp.exp2(s - m).astype(vall_ref.dtype)
            pv = jnp.dot(p, vall_ref[keys, 2 * hp * LANES:(2 * hp + 2) * LANES], preferred_element_type=_F32)
            num, den = pv[:, :LANES], pv[:, LANES:]
            o_ref[rows, sl] = jnp.where(head0, num[:BLOCK], num[BLOCK:]).astype(o_ref.dtype)
            m_acc = jnp.where(lane == 2 * hp, m[:BLOCK], jnp.where(lane == 2 * hp + 1, m[BLOCK:], m_acc))
            den_acc = jnp.where(lane == 2 * hp, den[:BLOCK],
                                jnp.where(lane == 2 * hp + 1, den[BLOCK:], den_acc))
        m_ref[rows, :] = m_acc
        den_ref[rows, :] = den_acc


def _dilated_attention(q, k, v, *, q_rows):
    d, sub_len, d_attn = q.shape
    per = q_rows // BLOCK
    cur = pl.BlockSpec((None, q_rows, d_attn), lambda r, n: (r, n, 0))
    prev = pl.BlockSpec((None, BLOCK, d_attn), lambda r, n: (r, jnp.maximum(n * per - 1, 0), 0))
    stat = pl.BlockSpec((None, q_rows, LANES), lambda r, n: (r, n, 0))
    stat_shape = jax.ShapeDtypeStruct((d, sub_len, LANES), _F32)
    return pl.pallas_call(
        _attn_kernel,
        grid=(d, sub_len // q_rows),
        in_specs=[cur, prev, cur, prev, cur],
        out_specs=[cur, stat, stat],
        out_shape=[jax.ShapeDtypeStruct((d, sub_len, d_attn), q.dtype), stat_shape, stat_shape],
        scratch_shapes=[pltpu.VMEM((q_rows + BLOCK, d_attn), q.dtype),
                        pltpu.VMEM((q_rows + BLOCK, 2 * d_attn), q.dtype)],
        compiler_params=_params("arbitrary", "arbitrary"),
        name=f"dilated_attn_d{d}",
    )(q, k, k, v, v)


def _expand_heads(w, expand2):
    n_heads = expand2.shape[1] // HEAD_DIM
    hi = w.astype(_BF16).astype(_F32)
    lane = lax.broadcasted_iota(jnp.int32, w.shape, 1)
    lhs = jnp.where(lane < n_heads, hi, pltpu.roll(w - hi, n_heads, 1))
    return jnp.dot(lhs.astype(_BF16), expand2, preferred_element_type=_F32)


def _out_proj_kernel(o1_ref, o4_ref, o16_ref, m1_ref, m4_ref, m16_ref, d1_ref, d4_ref, d16_ref,
                     ga_ref, u_ref, vc_ref, gb_ref, x_ref, wout_ref, ws_ref, bst_ref, fg_ref, y_ref,
                     a_ref, c_ref, n4_ref, n16_ref, st_ref, wb_ref):
    tm, d_attn = o1_ref.shape
    n_heads = d_attn // HEAD_DIM
    n_lane_groups = d_attn // LANES

    @pl.when(pl.program_id(0) == 0)
    def _():
        wb_ref[...] = wout_ref[...].astype(wb_ref.dtype)

    for r in range(4):
        st_ref[0, pl.ds(r, tm // 4, stride=4), :] = m4_ref[r]
        st_ref[1, pl.ds(r, tm // 4, stride=4), :] = d4_ref[r]
        for g in range(n_lane_groups):
            n4_ref[g, pl.ds(r, tm // 4, stride=4), :] = o4_ref[r, :, g * LANES:(g + 1) * LANES].astype(_F32)
    for r in range(16):
        st_ref[2, pl.ds(r, tm // 16, stride=16), :] = m16_ref[r]
        st_ref[3, pl.ds(r, tm // 16, stride=16), :] = d16_ref[r]
        for g in range(n_lane_groups):
            n16_ref[g, pl.ds(r, tm // 16, stride=16), :] = o16_ref[r, :, g * LANES:(g + 1) * LANES].astype(_F32)

    ms = [m1_ref[...], st_ref[0], st_ref[2]]
    dens = [d1_ref[...], st_ref[1], st_ref[3]]
    mx = jnp.maximum(jnp.maximum(ms[0], ms[1]), ms[2])
    es = [jnp.exp2(m - mx) for m in ms]
    tot = es[0] * dens[0] + es[1] * dens[1] + es[2] * dens[2]
    row_head = lax.broadcasted_iota(jnp.int32, (LANES, d_attn), 0) % n_heads
    row_used = lax.broadcasted_iota(jnp.int32, (LANES, d_attn), 0) < 2 * n_heads
    col_head = lax.broadcasted_iota(jnp.int32, (LANES, d_attn), 1) // HEAD_DIM
    expand2 = jnp.logical_and(row_head == col_head, row_used).astype(_BF16)
    ws = [_expand_heads(e / tot, expand2) for e in es]
    for g in range(n_lane_groups):
        sl = slice(g * LANES, (g + 1) * LANES)
        a = (ws[0][:, sl] * o1_ref[:, sl].astype(_F32) + ws[1][:, sl] * n4_ref[g] + ws[2][:, sl] * n16_ref[g])
        a_ref[:, sl] = (a * ga_ref[:, sl].astype(_F32)).astype(a_ref.dtype)

    row = lax.broadcasted_iota(jnp.int32, (CHUNK, CHUNK), 0)
    col = lax.broadcasted_iota(jnp.int32, (CHUNK, CHUNK), 1)
    tril = row >= col
    for g in range(ws_ref.shape[0]):
        wm = jnp.where(tril, ws_ref[g], 0.0).astype(_BF16)
        bias = bst_ref[:, g:g + 1]
        gs = slice(g * GROUP_WIDTH_B, (g + 1) * GROUP_WIDTH_B)
        for c in range(tm // CHUNK):
            rs = slice(c * CHUNK, (c + 1) * CHUNK)
            mixed = jnp.dot(wm, vc_ref[rs, gs], preferred_element_type=_F32) + bias
            cv = u_ref[rs, gs].astype(_F32) * mixed * gb_ref[rs, gs].astype(_F32)
            c_ref[rs, gs] = cv.astype(c_ref.dtype)

    acc = jnp.dot(a_ref[...], wb_ref[:d_attn, :], preferred_element_type=_F32)
    acc = acc + jnp.dot(c_ref[...], wb_ref[d_attn:, :], preferred_element_type=_F32)
    xo = x_ref[...] + acc
    ms = jnp.mean(xo * xo, axis=-1, keepdims=True)
    y_ref[...] = xo * lax.rsqrt(ms + EPS) * fg_ref[...]


def _out_proj(os, ms, dens, ga, u, vc, gb, x, w_out, w_s, b_s_t, final_g, *, tm):
    m, d_model = x.shape
    d_attn = ga.shape[1]
    d_chunk = u.shape[1]
    rows = lambda n: pl.BlockSpec((tm, n), lambda i: (i, 0))
    res = lambda d, n: pl.BlockSpec((d, tm // d, n), lambda i: (0, i, 0))
    whole = lambda a: pl.BlockSpec(a.shape, lambda i: (0,) * a.ndim)
    by_pattern = lambda n: [rows(n), res(4, n), res(16, n)]
    fg = final_g.reshape(1, -1)
    slab = pltpu.VMEM((d_attn // LANES, tm, LANES), _F32)
    return pl.pallas_call(
        _out_proj_kernel,
        grid=(m // tm,),
        in_specs=by_pattern(d_attn) + by_pattern(LANES) + by_pattern(LANES)
        + [rows(d_attn), rows(d_chunk), rows(d_chunk), rows(d_chunk), rows(d_model),
           pl.BlockSpec(w_out.shape, lambda i: (0, 0), pipeline_mode=pl.Buffered(1)),
           whole(w_s), whole(b_s_t), whole(fg)],
        out_specs=rows(d_model),
        out_shape=jax.ShapeDtypeStruct((m, d_model), _F32),
        scratch_shapes=[pltpu.VMEM((tm, d_attn), _BF16), pltpu.VMEM((tm, d_chunk), _BF16), slab, slab,
                        pltpu.VMEM((4, tm, LANES), _F32), pltpu.VMEM(w_out.shape, _BF16)],
        compiler_params=_params("arbitrary"),
        name="out_proj",
    )(*os, *ms, *dens, ga, u, vc, gb, x, w_out, w_s, b_s_t, fg)


def _sample_in_proj_kernel(x_ref, g_ref, w_ref, cos_ref, sin_ref, lng_ref, lnb_ref,
                           q_ref, k_ref, v_ref, ga_ref, u_ref, vc_ref, gb_ref, xn_ref):
    j = pl.program_id(0)

    @pl.when(j == 0)
    def _():
        xf = x_ref[...]
        ms = jnp.mean(xf * xf, axis=-1, keepdims=True)
        xn_ref[...] = (xf * lax.rsqrt(ms + EPS) * g_ref[...]).astype(xn_ref.dtype)

    z = jnp.dot(xn_ref[...], w_ref[...].astype(_BF16), preferred_element_type=_F32)

    def rope_to(ref, scale):
        for g in range(z.shape[1] // LANES):
            sl = slice(g * LANES, (g + 1) * LANES)
            ref[:, sl] = _rope_group(z[:, sl], cos_ref[...], sin_ref[...]) * scale

    @pl.when(j == SEG_Q)
    def _():
        rope_to(q_ref, HEAD_DIM ** -0.5)

    @pl.when(j == SEG_K)
    def _():
        rope_to(k_ref, 1.0)

    @pl.when(j == SEG_V)
    def _():
        v_ref[...] = z

    @pl.when(j == SEG_GA)
    def _():
        ga_ref[...] = jax.nn.silu(z)

    @pl.when(j == SEG_U)
    def _():
        u_ref[...] = jax.nn.gelu(z)

    @pl.when(j == SEG_VC)
    def _():
        vc_ref[...] = _gelu_layer_norm(z, lng_ref[...], lnb_ref[...])

    @pl.when(j == SEG_GB)
    def _():
        gb_ref[...] = jax.nn.silu(z)


def _sample_in_proj(x, norm_g, w_in, cos_t, sin_t, ln_g, ln_b):
    m, d_model = x.shape
    d_seg = w_in.shape[1] // N_SEGMENTS
    whole = lambda r, c: pl.BlockSpec((r, c), lambda j: (0, 0))
    return pl.pallas_call(
        _sample_in_proj_kernel,
        grid=(N_SEGMENTS,),
        in_specs=[whole(m, d_model), whole(1, d_model),
                  pl.BlockSpec((d_model, d_seg), lambda j: (0, j)),
                  whole(m, LANES), whole(m, LANES), whole(1, d_seg), whole(1, d_seg)],
        out_specs=[whole(m, d_seg)] * N_SEGMENTS,
        out_shape=[jax.ShapeDtypeStruct((m, d_seg), _F32)] * N_SEGMENTS,
        scratch_shapes=[pltpu.VMEM((m, d_model), _BF16)],
        compiler_params=_params("arbitrary"),
        name="sample_in_proj",
    )(x, norm_g.reshape(1, -1), w_in, cos_t, sin_t, ln_g.reshape(1, -1), ln_b.reshape(1, -1))


def _sample_attn_kernel(q_ref, kn_ref, vn_ref, kt_ref, vt_ref, o_ref, *, wb):
    t_new, width = q_ref.shape
    n_heads = width // HEAD_DIM
    rows = t_new * n_heads
    head_row = lax.broadcasted_iota(jnp.int32, (n_heads, width), 0)
    head_col = lax.broadcasted_iota(jnp.int32, (n_heads, width), 1) // HEAD_DIM
    own = head_row == head_col
    qbd = jnp.concatenate([jnp.where(own, q_ref[t:t + 1, :], 0.0) for t in range(t_new)], axis=0)

    t_row = lax.broadcasted_iota(jnp.int32, (rows, wb), 0) // n_heads
    pos = lax.broadcasted_iota(jnp.int32, (rows, wb), 1)
    dist = wb + t_row - pos
    mult = jnp.zeros((rows, wb), _F32)
    for window, d in DILATIONS:
        mult = mult + jnp.where((dist % d == 0) & (dist <= window), 1.0, 0.0)
    valid = mult > 0.0

    s = jnp.dot(qbd.astype(_BF16), kt_ref[...].astype(_BF16), preferred_element_type=_F32)
    s = jnp.where(valid, s, NEG_INF)
    m = jnp.max(s, axis=-1, keepdims=True)
    t_col = lax.broadcasted_iota(jnp.int32, (rows, 1), 0) // n_heads
    s_new, mult_new = [], []
    for tp in range(t_new):
        s_new.append(jnp.sum(qbd * kn_ref[tp:tp + 1, :], axis=-1, keepdims=True))
        dn = t_col - tp
        mn = jnp.zeros((rows, 1), _F32)
        for window, d in DILATIONS:
            mn = mn + jnp.where((dn >= 0) & (dn % d == 0), 1.0, 0.0)
        mult_new.append(mn)
        m = jnp.maximum(m, jnp.where(mn > 0.0, s_new[tp], NEG_INF))
    p = jnp.exp(s - m) * mult
    den = jnp.sum(p, axis=-1, keepdims=True)
    num = lax.dot_general(p.astype(_BF16), vt_ref[...].astype(_BF16), (((1,), (1,)), ((), ())),
                          preferred_element_type=_F32)
    for tp in range(t_new):
        pn = jnp.where(mult_new[tp] > 0.0, jnp.exp(s_new[tp] - m), 0.0) * mult_new[tp]
        den = den + pn
        num = num + pn * vn_ref[tp:tp + 1, :]
    res = num / den
    for t in range(t_new):
        blk = res[t * n_heads:(t + 1) * n_heads, :]
        o_ref[t:t + 1, :] = jnp.sum(jnp.where(own, blk, 0.0), axis=0, keepdims=True)


def _sample_attention(q, k_new, v_new, cache_kt, cache_vt, *, head_groups):
    db, t_new, d_attn = q.shape
    wb = cache_kt.shape[2]
    assert wb == MAX_WINDOW
    width = d_attn // head_groups
    new = pl.BlockSpec((None, t_new, width), lambda b, g: (b, 0, g))
    cache = pl.BlockSpec((None, width, wb), lambda b, g: (b, g, 0))
    return pl.pallas_call(
        functools.partial(_sample_attn_kernel, wb=wb),
        grid=(db, head_groups),
        in_specs=[new, new, new, cache, cache],
        out_specs=new,
        out_shape=jax.ShapeDtypeStruct((db, t_new, d_attn), _F32),
        compiler_params=_params("arbitrary", "arbitrary"),
        name="sample_attn",
    )(q, k_new, v_new, cache_kt, cache_vt)


def _sample_out_kernel(oa_ref, ga_ref, u_ref, vc_ref, gb_ref, x_ref, wout_ref, coef_ref, bias_ref, fg_ref,
                       y_ref, c_ref, *, t_new):
    m, d_attn = oa_ref.shape
    db = m // t_new
    a = (oa_ref[...] * ga_ref[...]).astype(_BF16)
    for t in range(t_new):
        mixed = jnp.zeros((db, c_ref.shape[1]), _F32) + bias_ref[t:t + 1, :]
        for s in range(t + 1):
            mixed = mixed + coef_ref[t * t_new + s:t * t_new + s + 1, :] * vc_ref[s * db:(s + 1) * db, :]
        rs = slice(t * db, (t + 1) * db)
        c_ref[rs, :] = (u_ref[rs, :] * mixed * gb_ref[rs, :]).astype(c_ref.dtype)
    acc = jnp.dot(a, wout_ref[:d_attn, :].astype(_BF16), preferred_element_type=_F32)
    acc = acc + jnp.dot(c_ref[...], wout_ref[d_attn:, :].astype(_BF16), preferred_element_type=_F32)
    xo = x_ref[...] + acc
    ms = jnp.mean(xo * xo, axis=-1, keepdims=True)
    y_ref[...] = xo * lax.rsqrt(ms + EPS) * fg_ref[...]


def _sample_out(oa, ga, u, vc, gb, x, w_out, coef, bias, final_g, *, t_new):
    m, d_model = x.shape
    args = (oa, ga, u, vc, gb, x, w_out, coef, bias, final_g.reshape(1, -1))
    whole = lambda a: pl.BlockSpec(a.shape, lambda i: (0,) * a.ndim)
    return pl.pallas_call(
        functools.partial(_sample_out_kernel, t_new=t_new),
        grid=(1,),
        in_specs=[whole(a) for a in args],
        out_specs=pl.BlockSpec((m, d_model), lambda i: (0, 0)),
        out_shape=jax.ShapeDtypeStruct((m, d_model), _F32),
        scratch_shapes=[pltpu.VMEM((m, u.shape[1]), _BF16)],
        compiler_params=_params("arbitrary"),
        name="sample_out",
    )(*args)


def kernel(x_prompt, x_sample, cache_k, cache_v, norm_g, w_in, ln_g, ln_b, w_s, b_s, w_out, final_g):
    batch, s_len, d_model = x_prompt.shape
    db, t_new, _ = x_sample.shape
    depth, _, wb, n_heads, head_dim = cache_k.shape
    d_attn = n_heads * head_dim
    d_chunk = w_out.shape[1] - d_attn
    assert batch == 1 and depth == 1 and head_dim == HEAD_DIM
    assert w_in.shape[2] == 4 * d_attn + 3 * d_chunk and d_attn == d_chunk
    tail = min(MAX_WINDOW, s_len)
    w_in0, w_out0 = w_in[0], w_out[0]
    ng, lg, lb = norm_g[0], ln_g[0], ln_b[0]

    xp = x_prompt.reshape(s_len, d_model)
    tables = _rope_tables(jnp.arange(s_len, dtype=jnp.int32))
    xn, q1, q4, q16 = _proj(xp, w_in0, SEG_Q, tm=512, norm_g=ng, tables=tables,
                            scale=HEAD_DIM ** -0.5 * math.log2(math.e))
    k1, k4, k16, k_tail_t = _proj(xn, w_in0, SEG_K, tm=512, tables=tables, tail_rows=tail)
    v1, v4, v16, v_tail_t = _proj(xn, w_in0, SEG_V, tm=512, tail_rows=tail)
    ga, = _proj(xn, w_in0, SEG_GA, tm=1024)
    u, = _proj(xn, w_in0, SEG_U, tm=1024)
    vc, = _proj(xn, w_in0, SEG_VC, tm=1024, ln=(lg, lb))
    gb, = _proj(xn, w_in0, SEG_GB, tm=1024)
    q_rows = 512
    o1, m1, d1 = _dilated_attention(q1[None], k1[None], v1[None], q_rows=q_rows)
    o4, m4, d4 = _dilated_attention(q4, k4, v4, q_rows=q_rows)
    o16, m16, d16 = _dilated_attention(q16, k16, v16, q_rows=q_rows)
    y_prompt = _out_proj((o1[0], o4, o16), (m1[0], m4, m16), (d1[0], d4, d16), ga, u, vc, gb, xp, w_out0,
                         w_s[0], b_s[0].T, final_g, tm=256)
    untranspose = lambda a: jnp.transpose(a.reshape(n_heads, head_dim, tail), (2, 0, 1))

    xs = jnp.transpose(x_sample, (1, 0, 2)).reshape(t_new * db, d_model)
    pos_s = PAST_LEN + jnp.repeat(jnp.arange(t_new, dtype=jnp.int32), db)
    cos_s, sin_s = _rope_tables(pos_s)
    qs, ks, vs, gas, us, vcs, gbs = _sample_in_proj(xs, ng, w_in0, cos_s, sin_s, lg, lb)
    to_b_major = lambda a: jnp.transpose(a.reshape(t_new, db, -1), (1, 0, 2))
    qs_b, ks_b, vs_b, vcs_b = (to_b_major(a) for a in (qs, ks, vs, vcs))
    cache_t = lambda c: jnp.transpose(c[0], (0, 2, 3, 1)).reshape(db, d_attn, wb)
    oa = _sample_attention(qs_b, ks_b, vs_b, cache_t(cache_k), cache_t(cache_v), head_groups=2)
    oa_t = jnp.transpose(oa, (1, 0, 2)).reshape(t_new * db, d_attn)
    coef = jnp.repeat(jnp.transpose(w_s[0][:, :t_new, :t_new], (1, 2, 0)).reshape(t_new * t_new, -1),
                      GROUP_WIDTH_B, axis=1)
    bias = jnp.repeat(jnp.transpose(b_s[0][:, :t_new], (1, 0)), GROUP_WIDTH_B, axis=1)
    ys = _sample_out(oa_t, gas, us, vcs, gbs, xs, w_out0, coef, bias, final_g, t_new=t_new)
    y_sample = jnp.transpose(ys.reshape(t_new, db, d_model), (1, 0, 2))

    hs = (n_heads, head_dim)
    return (
        y_prompt.reshape(batch, s_len, d_model),
        y_sample,
        untranspose(k_tail_t).reshape(depth, batch, tail, *hs),
        untranspose(v_tail_t).reshape(depth, batch, tail, *hs),
        ks_b.reshape(depth, db, t_new, *hs),
        vs_b.reshape(depth, db, t_new, *hs),
        vcs_b.reshape(depth, db, t_new, d_chunk),
    )
```

```python
import functools
import math

import jax
import jax.numpy as jnp
from jax import lax
from jax.experimental import pallas as pl
from jax.experimental.pallas import tpu as pltpu

HEAD_DIM = 64
BLOCK = 128
CHUNK = 128
GROUP_WIDTH_B = 128
DILATIONS = ((128, 1), (512, 4), (2048, 16))
MAX_WINDOW = 2048
PAST_LEN = 16384
ROPE_THETA = 10000.0
EPS = 1e-6
NEG_INF = -1e30
N_SEGMENTS = 7
SEG_Q, SEG_K, SEG_V, SEG_GA, SEG_U, SEG_VC, SEG_GB = range(N_SEGMENTS)

LANES = 128
VMEM_LIMIT_BYTES = 56 * 1024 * 1024
SUB_ROWS = 256
SAMPLE_HEAD_GROUP_WIDTH = 512

_BF16 = jnp.bfloat16
_F32 = jnp.float32


def _params(*semantics):
    return pltpu.CompilerParams(dimension_semantics=semantics, vmem_limit_bytes=VMEM_LIMIT_BYTES)


def _rope_tables(pos):
    half = HEAD_DIM // 2
    inv = jnp.exp(-math.log(ROPE_THETA) * jnp.arange(half, dtype=_F32) / half)
    ang = pos.astype(_F32)[:, None] * inv[None, :]
    cos = jnp.cos(ang)
    sin = jnp.sin(ang)
    cos_t = jnp.concatenate([cos, cos, cos, cos], axis=-1)
    sin_t = jnp.concatenate([-sin, sin, -sin, sin], axis=-1)
    return cos_t, sin_t


def _rope_group(zg, cos, sin_signed):
    lane = lax.broadcasted_iota(jnp.int32, zg.shape, 1)
    first_half = (lane % HEAD_DIM) < (HEAD_DIM // 2)
    partner = jnp.where(first_half, pltpu.roll(zg, LANES - HEAD_DIM // 2, 1),
                        pltpu.roll(zg, HEAD_DIM // 2, 1))
    return zg * cos + partner * sin_signed


def _gelu_layer_norm(z, g, b):
    h = jax.nn.gelu(z)
    mu = jnp.mean(h, axis=-1, keepdims=True)
    hc = h - mu
    var = jnp.mean(hc * hc, axis=-1, keepdims=True)
    return hc * lax.rsqrt(var + EPS) * g + b


def _proj_kernel(*refs, seg, scale, tail_start, side_job):
    qkv = seg in (SEG_Q, SEG_K, SEG_V)
    rope = seg in (SEG_Q, SEG_K)
    it = iter(refs)
    x_ref = next(it)
    g_ref = next(it) if seg == SEG_Q else None
    w_ref = next(it)
    cos_ref, sin_ref = (next(it), next(it)) if rope else (None, None)
    lng_ref, lnb_ref = (next(it), next(it)) if seg == SEG_VC else (None, None)
    side_in = [next(it) for _ in range(5)] if side_job else None
    xn_out_ref = next(it) if seg == SEG_Q else None
    if qkv:
        d1_ref, d4_ref, d16_ref = next(it), next(it), next(it)
        tail_ref = next(it) if tail_start is not None else None
    else:
        o_ref = next(it)
    side_out_ref = next(it) if side_job else None
    wb_ref = next(it)
    if qkv:
        nat_ref, res4_ref = next(it), next(it)
    mult_ref = next(it) if side_job else None

    @pl.when(pl.program_id(0) == 0)
    def _():
        wb_ref[...] = w_ref[...].astype(wb_ref.dtype)
        if side_job:
            mult_ref[...] = _key_multiplicity(mult_ref.shape[0], mult_ref.shape[1], side_in[0].shape[0])

    if side_job:
        _sample_attn_tile(*side_in, side_out_ref, mult_ref)

    tm = x_ref.shape[0]
    sub = min(SUB_ROWS, tm)
    n4, n16 = sub // 4, sub // 16
    for t in range(tm // sub):
        rs = slice(t * sub, (t + 1) * sub)
        if seg == SEG_Q:
            xf = x_ref[rs, :]
            ms = jnp.mean(xf * xf, axis=-1, keepdims=True)
            xb = (xf * lax.rsqrt(ms + EPS) * g_ref[...]).astype(_BF16)
            xn_out_ref[rs, :] = xb
        else:
            xb = x_ref[rs, :]
        z = jnp.dot(xb, wb_ref[...], preferred_element_type=_F32)
        if seg in (SEG_GA, SEG_GB):
            o_ref[rs, :] = jax.nn.silu(z).astype(o_ref.dtype)
        elif seg == SEG_U:
            o_ref[rs, :] = jax.nn.gelu(z).astype(o_ref.dtype)
        elif seg == SEG_VC:
            o_ref[rs, :] = _gelu_layer_norm(z, lng_ref[...], lnb_ref[...]).astype(o_ref.dtype)
        else:
            for g in range(z.shape[1] // LANES):
                sl = slice(g * LANES, (g + 1) * LANES)
                r = z[:, sl]
                if rope:
                    r = _rope_group(r, cos_ref[rs, :], sin_ref[rs, :])
                if tail_ref is not None:
                    tail_ref[sl, rs] = r.T
                if scale != 1.0:
                    r = r * scale
                d1_ref[rs, sl] = r.astype(d1_ref.dtype)
                nat_ref[g, rs, :] = r
                for r4 in range(4):
                    blk = nat_ref[g, pl.ds(t * sub + r4, n4, stride=4), :]
                    d4_ref[r4, t * n4:(t + 1) * n4, sl] = blk.astype(d4_ref.dtype)
                    res4_ref[g, t * sub + r4 * n4:t * sub + (r4 + 1) * n4, :] = blk
                for r4 in range(4):
                    for a in range(4):
                        blk = res4_ref[g, pl.ds(t * sub + r4 * n4 + a, n16, stride=4), :]
                        d16_ref[r4 + 4 * a, t * n16:(t + 1) * n16, sl] = blk.astype(d16_ref.dtype)


def _proj(x, w_in, seg, *, tm, norm_g=None, tables=None, ln=None, scale=1.0, tail_rows=0, side=None):
    s_len, d_model = x.shape
    d_seg = w_in.shape[1] // N_SEGMENTS
    n_i = s_len // tm
    qkv = seg in (SEG_Q, SEG_K, SEG_V)
    row_vec = lambda n: pl.BlockSpec((1, n), lambda i: (0, 0))
    in_specs = [pl.BlockSpec((tm, d_model), lambda i: (i, 0))]
    args = [x]
    if seg == SEG_Q:
        in_specs.append(row_vec(d_model))
        args.append(norm_g.reshape(1, -1))
    in_specs.append(pl.BlockSpec((d_model, d_seg), lambda i: (0, seg), pipeline_mode=pl.Buffered(1)))
    args.append(w_in)
    if seg in (SEG_Q, SEG_K):
        in_specs += [pl.BlockSpec((tm, LANES), lambda i: (i, 0))] * 2
        args += list(tables)
    if seg == SEG_VC:
        in_specs += [row_vec(d_seg)] * 2
        args += [ln[0].reshape(1, -1), ln[1].reshape(1, -1)]
    if side is not None:
        sq, skn, svn, skt, svt, first_tile = side
        t_new, d_attn = sq.shape[1:]
        width, wb = SAMPLE_HEAD_GROUP_WIDTH, skt.shape[2]
        hg = d_attn // width
        new = pl.BlockSpec((None, t_new, width), lambda i: ((first_tile + i) // hg, 0, (first_tile + i) % hg))
        cache = pl.BlockSpec((None, width, wb), lambda i: ((first_tile + i) // hg, (first_tile + i) % hg, 0))
        in_specs += [new, new, new, cache, cache]
        args += [sq, skn, svn, skt, svt]
    out_shape, out_specs = [], []
    if seg == SEG_Q:
        out_shape.append(jax.ShapeDtypeStruct((s_len, d_model), _BF16))
        out_specs.append(pl.BlockSpec((tm, d_model), lambda i: (i, 0)))
    scratch = [pltpu.VMEM((d_model, d_seg), _BF16)]
    tail_start = None
    if qkv:
        out_shape += [jax.ShapeDtypeStruct((s_len, d_seg), _BF16),
                      jax.ShapeDtypeStruct((4, s_len // 4, d_seg), _BF16),
                      jax.ShapeDtypeStruct((16, s_len // 16, d_seg), _BF16)]
        out_specs += [pl.BlockSpec((tm, d_seg), lambda i: (i, 0)),
                      pl.BlockSpec((4, tm // 4, d_seg), lambda i: (0, i, 0)),
                      pl.BlockSpec((16, tm // 16, d_seg), lambda i: (0, i, 0))]
        if tail_rows:
            tail_start = n_i - tail_rows // tm
            out_shape.append(jax.ShapeDtypeStruct((d_seg, tail_rows), _F32))
            out_specs.append(pl.BlockSpec((d_seg, tm), lambda i: (0, jnp.maximum(i - tail_start, 0))))
        scratch += [pltpu.VMEM((d_seg // LANES, tm, LANES), _F32)] * 2
    else:
        out_shape.append(jax.ShapeDtypeStruct((s_len, d_seg), _BF16))
        out_specs.append(pl.BlockSpec((tm, d_seg), lambda i: (i, 0)))
    if side is not None:
        assert first_tile % hg == 0 and n_i % hg == 0
        out_shape.append(jax.ShapeDtypeStruct((n_i // hg, t_new, d_attn), _F32))
        out_specs.append(pl.BlockSpec((None, t_new, width), lambda i: (i // hg, 0, i % hg)))
        scratch.append(pltpu.VMEM((t_new * width // HEAD_DIM, wb), _F32))
    return pl.pallas_call(
        functools.partial(_proj_kernel, seg=seg, scale=scale, tail_start=tail_start, side_job=side is not None),
        grid=(n_i,),
        in_specs=in_specs,
        out_specs=out_specs,
        out_shape=out_shape,
        scratch_shapes=scratch,
        compiler_params=_params("arbitrary"),
        name=f"proj_seg{seg}",
    )(*args)


def _attn_kernel(q_ref, kp_ref, kc_ref, vp_ref, vc_ref, o_ref, m_ref, den_ref, kall_ref, vall_ref):
    n = pl.program_id(1)
    first_step = jnp.logical_and(pl.program_id(0) == 0, n == 0)
    n_pairs = q_ref.shape[1] // LANES
    two = 2 * BLOCK

    kall_ref[0:BLOCK, :] = kp_ref[...]
    kall_ref[BLOCK:, :] = kc_ref[...]
    for hp in range(n_pairs):
        sl = slice(hp * LANES, (hp + 1) * LANES)
        ext = slice(2 * hp * LANES, (2 * hp + 1) * LANES)
        vall_ref[0:BLOCK, ext] = vp_ref[:, sl]
        vall_ref[BLOCK:, ext] = vc_ref[:, sl]

    @pl.when(first_step)
    def _():
        ones = jnp.ones((vall_ref.shape[0], LANES), vall_ref.dtype)
        for hp in range(n_pairs):
            vall_ref[:, (2 * hp + 1) * LANES:(2 * hp + 2) * LANES] = ones

    qi = lax.broadcasted_iota(jnp.int32, (two, two), 0) % BLOCK
    si = lax.broadcasted_iota(jnp.int32, (two, two), 1)
    dist = qi + BLOCK - si
    band_bias = jnp.where((dist >= 0) & (dist <= BLOCK), 0.0, NEG_INF).astype(_F32)
    lane = lax.broadcasted_iota(jnp.int32, (BLOCK, LANES), 1)
    head0 = lane < HEAD_DIM
    zero = jnp.zeros((BLOCK, LANES), q_ref.dtype)

    for j in range(q_ref.shape[0] // BLOCK):
        rows = slice(j * BLOCK, (j + 1) * BLOCK)
        keys = slice(j * BLOCK, j * BLOCK + two)
        bias = band_bias
        if j == 0:
            bias = band_bias + jnp.where(jnp.logical_and(si < BLOCK, n == 0), NEG_INF, 0.0)
        m_acc = jnp.zeros((BLOCK, LANES), _F32)
        den_acc = jnp.ones((BLOCK, LANES), _F32)
        for hp in range(n_pairs):
            sl = slice(hp * LANES, (hp + 1) * LANES)
            q2 = q_ref[rows, sl]
            qs = jnp.concatenate([jnp.where(head0, q2, zero), jnp.where(head0, zero, q2)], axis=0)
            s = lax.dot_general(qs, kall_ref[keys, sl], (((1,), (1,)), ((), ())),
                                preferred_element_type=_F32) + bias
            m = jnp.max(s, axis=-1, keepdims=True)
            p = jnp.exp2(s - m).astype(vall_ref.dtype)
            pv = jnp.dot(p, vall_ref[keys, 2 * hp * LANES:(2 * hp + 2) * LANES], preferred_element_type=_F32)
            num, den = pv[:, :LANES], pv[:, LANES:]
            o_ref[rows, sl] = jnp.where(head0, num[:BLOCK], num[BLOCK:]).astype(o_ref.dtype)
            m_acc = jnp.where(lane == 2 * hp, m[:BLOCK], jnp.where(lane == 2 * hp + 1, m[BLOCK:], m_acc))
            den_acc = jnp.where(lane == 2 * hp, den[:BLOCK],
                                jnp.where(lane == 2 * hp + 1, den[BLOCK:], den_acc))
        m_ref[rows, :] = m_acc
        den_ref[rows, :] = den_acc


def _dilated_attention(q, k, v, *, q_rows):
    d, sub_len, d_attn = q.shape
    per = q_rows // BLOCK
    cur = pl.BlockSpec((None, q_rows, d_attn), lambda r, n: (r, n, 0))
    prev = pl.BlockSpec((None, BLOCK, d_attn), lambda r, n: (r, jnp.maximum(n * per - 1, 0), 0))
    stat = pl.BlockSpec((None, q_rows, LANES), lambda r, n: (r, n, 0))
    stat_shape = jax.ShapeDtypeStruct((d, sub_len, LANES), _F32)
    return pl.pallas_call(
        _attn_kernel,
        grid=(d, sub_len // q_rows),
        in_specs=[cur, prev, cur, prev, cur],
        out_specs=[cur, stat, stat],
        out_shape=[jax.ShapeDtypeStruct((d, sub_len, d_attn), q.dtype), stat_shape, stat_shape],
        scratch_shapes=[pltpu.VMEM((q_rows + BLOCK, d_attn), q.dtype),
                        pltpu.VMEM((q_rows + BLOCK, 2 * d_attn), q.dtype)],
        compiler_params=_params("arbitrary", "arbitrary"),
        name=f"dilated_attn_d{d}",
    )(q, k, k, v, v)


def _expand_heads(w, expand2):
    n_heads = expand2.shape[1] // HEAD_DIM
    hi = w.astype(_BF16).astype(_F32)
    lane = lax.broadcasted_iota(jnp.int32, w.shape, 1)
    lhs = jnp.where(lane < n_heads, hi, pltpu.roll(w - hi, n_heads, 1))
    return jnp.dot(lhs.astype(_BF16), expand2, preferred_element_type=_F32)


def _out_proj_kernel(o1_ref, o4_ref, o16_ref, m1_ref, m4_ref, m16_ref, d1_ref, d4_ref, d16_ref,
                     ga_ref, u_ref, vc_ref, gb_ref, x_ref, wout_ref, ws_ref, bst_ref, fg_ref, y_ref,
                     a_ref, c_ref, n4_ref, n16_ref, st_ref, wb_ref):
    tm, d_attn = o1_ref.shape
    n_heads = d_attn // HEAD_DIM
    n_lane_groups = d_attn // LANES

    @pl.when(pl.program_id(0) == 0)
    def _():
        wb_ref[...] = wout_ref[...].astype(wb_ref.dtype)

    for r in range(4):
        st_ref[0, pl.ds(r, tm // 4, stride=4), :] = m4_ref[r]
        st_ref[1, pl.ds(r, tm // 4, stride=4), :] = d4_ref[r]
        for g in range(n_lane_groups):
            n4_ref[g, pl.ds(r, tm // 4, stride=4), :] = o4_ref[r, :, g * LANES:(g + 1) * LANES].astype(_F32)
    for r in range(16):
        st_ref[2, pl.ds(r, tm // 16, stride=16), :] = m16_ref[r]
        st_ref[3, pl.ds(r, tm // 16, stride=16), :] = d16_ref[r]
        for g in range(n_lane_groups):
            n16_ref[g, pl.ds(r, tm // 16, stride=16), :] = o16_ref[r, :, g * LANES:(g + 1) * LANES].astype(_F32)

    ms = [m1_ref[...], st_ref[0], st_ref[2]]
    dens = [d1_ref[...], st_ref[1], st_ref[3]]
    mx = jnp.maximum(jnp.maximum(ms[0], ms[1]), ms[2])
    es = [jnp.exp2(m - mx) for m in ms]
    tot = es[0] * dens[0] + es[1] * dens[1] + es[2] * dens[2]
    row_head = lax.broadcasted_iota(jnp.int32, (LANES, d_attn), 0) % n_heads
    row_used = lax.broadcasted_iota(jnp.int32, (LANES, d_attn), 0) < 2 * n_heads
    col_head = lax.broadcasted_iota(jnp.int32, (LANES, d_attn), 1) // HEAD_DIM
    expand2 = jnp.logical_and(row_head == col_head, row_used).astype(_BF16)
    ws = [_expand_heads(e / tot, expand2) for e in es]
    for g in range(n_lane_groups):
        sl = slice(g * LANES, (g + 1) * LANES)
        a = (ws[0][:, sl] * o1_ref[:, sl].astype(_F32) + ws[1][:, sl] * n4_ref[g] + ws[2][:, sl] * n16_ref[g])
        a_ref[:, sl] = (a * ga_ref[:, sl].astype(_F32)).astype(a_ref.dtype)

    row = lax.broadcasted_iota(jnp.int32, (CHUNK, CHUNK), 0)
    col = lax.broadcasted_iota(jnp.int32, (CHUNK, CHUNK), 1)
    tril = row >= col
    for g in range(ws_ref.shape[0]):
        wm = jnp.where(tril, ws_ref[g], 0.0).astype(_BF16)
        bias = bst_ref[:, g:g + 1]
        gs = slice(g * GROUP_WIDTH_B, (g + 1) * GROUP_WIDTH_B)
        for c in range(tm // CHUNK):
            rs = slice(c * CHUNK, (c + 1) * CHUNK)
            mixed = jnp.dot(wm, vc_ref[rs, gs], preferred_element_type=_F32) + bias
            cv = u_ref[rs, gs].astype(_F32) * mixed * gb_ref[rs, gs].astype(_F32)
            c_ref[rs, gs] = cv.astype(c_ref.dtype)

    acc = jnp.dot(a_ref[...], wb_ref[:d_attn, :], preferred_element_type=_F32)
    acc = acc + jnp.dot(c_ref[...], wb_ref[d_attn:, :], preferred_element_type=_F32)
    xo = x_ref[...] + acc
    ms = jnp.mean(xo * xo, axis=-1, keepdims=True)
    y_ref[...] = xo * lax.rsqrt(ms + EPS) * fg_ref[...]


def _out_proj(os, ms, dens, ga, u, vc, gb, x, w_out, w_s, b_s_t, final_g, *, tm):
    m, d_model = x.shape
    d_attn = ga.shape[1]
    d_chunk = u.shape[1]
    rows = lambda n: pl.BlockSpec((tm, n), lambda i: (i, 0))
    res = lambda d, n: pl.BlockSpec((d, tm // d, n), lambda i: (0, i, 0))
    whole = lambda a: pl.BlockSpec(a.shape, lambda i: (0,) * a.ndim)
    by_pattern = lambda n: [rows(n), res(4, n), res(16, n)]
    fg = final_g.reshape(1, -1)
    slab = pltpu.VMEM((d_attn // LANES, tm, LANES), _F32)
    return pl.pallas_call(
        _out_proj_kernel,
        grid=(m // tm,),
        in_specs=by_pattern(d_attn) + by_pattern(LANES) + by_pattern(LANES)
        + [rows(d_attn), rows(d_chunk), rows(d_chunk), rows(d_chunk), rows(d_model),
           pl.BlockSpec(w_out.shape, lambda i: (0, 0), pipeline_mode=pl.Buffered(1)),
           whole(w_s), whole(b_s_t), whole(fg)],
        out_specs=rows(d_model),
        out_shape=jax.ShapeDtypeStruct((m, d_model), _F32),
        scratch_shapes=[pltpu.VMEM((tm, d_attn), _BF16), pltpu.VMEM((tm, d_chunk), _BF16), slab, slab,
                        pltpu.VMEM((4, tm, LANES), _F32), pltpu.VMEM(w_out.shape, _BF16)],
        compiler_params=_params("arbitrary"),
        name="out_proj",
    )(*os, *ms, *dens, ga, u, vc, gb, x, w_out, w_s, b_s_t, fg)


def _sample_in_proj_kernel(x_ref, g_ref, w_ref, cos_ref, sin_ref, lng_ref, lnb_ref,
                           q_ref, k_ref, v_ref, ga_ref, u_ref, vc_ref, gb_ref, xn_ref):
    j = pl.program_id(0)

    @pl.when(j == 0)
    def _():
        xf = x_ref[...]
        ms = jnp.mean(xf * xf, axis=-1, keepdims=True)
        xn_ref[...] = (xf * lax.rsqrt(ms + EPS) * g_ref[...]).astype(xn_ref.dtype)

    z = jnp.dot(xn_ref[...], w_ref[...].astype(_BF16), preferred_element_type=_F32)

    def rope_to(ref, scale):
        for g in range(z.shape[1] // LANES):
            sl = slice(g * LANES, (g + 1) * LANES)
            ref[:, sl] = _rope_group(z[:, sl], cos_ref[...], sin_ref[...]) * scale

    @pl.when(j == SEG_Q)
    def _():
        rope_to(q_ref, HEAD_DIM ** -0.5)

    @pl.when(j == SEG_K)
    def _():
        rope_to(k_ref, 1.0)

    @pl.when(j == SEG_V)
    def _():
        v_ref[...] = z

    @pl.when(j == SEG_GA)
    def _():
        ga_ref[...] = jax.nn.silu(z)

    @pl.when(j == SEG_U)
    def _():
        u_ref[...] = jax.nn.gelu(z)

    @pl.when(j == SEG_VC)
    def _():
        vc_ref[...] = _gelu_layer_norm(z, lng_ref[...], lnb_ref[...])

    @pl.when(j == SEG_GB)
    def _():
        gb_ref[...] = jax.nn.silu(z)


def _sample_in_proj(x, norm_g, w_in, cos_t, sin_t, ln_g, ln_b):
    m, d_model = x.shape
    d_seg = w_in.shape[1] // N_SEGMENTS
    whole = lambda r, c: pl.BlockSpec((r, c), lambda j: (0, 0))
    return pl.pallas_call(
        _sample_in_proj_kernel,
        grid=(N_SEGMENTS,),
        in_specs=[whole(m, d_model), whole(1, d_model),
                  pl.BlockSpec((d_model, d_seg), lambda j: (0, j)),
                  whole(m, LANES), whole(m, LANES), whole(1, d_seg), whole(1, d_seg)],
        out_specs=[whole(m, d_seg)] * N_SEGMENTS,
        out_shape=[jax.ShapeDtypeStruct((m, d_seg), _F32)] * N_SEGMENTS,
        scratch_shapes=[pltpu.VMEM((m, d_model), _BF16)],
        compiler_params=_params("arbitrary"),
        name="sample_in_proj",
    )(x, norm_g.reshape(1, -1), w_in, cos_t, sin_t, ln_g.reshape(1, -1), ln_b.reshape(1, -1))


def _key_multiplicity(rows, wb, t_new):
    t_row = lax.broadcasted_iota(jnp.int32, (rows, wb), 0) // (rows // t_new)
    pos = lax.broadcasted_iota(jnp.int32, (rows, wb), 1)
    dist = wb + t_row - pos
    mult = jnp.zeros((rows, wb), _F32)
    for window, d in DILATIONS:
        mult = mult + jnp.where((dist % d == 0) & (dist <= window), 1.0, 0.0)
    return mult


def _sample_attn_tile(q_ref, kn_ref, vn_ref, kt_ref, vt_ref, o_ref, mult_ref):
    t_new, width = q_ref.shape
    n_heads = width // HEAD_DIM
    rows = t_new * n_heads
    head_row = lax.broadcasted_iota(jnp.int32, (n_heads, width), 0)
    head_col = lax.broadcasted_iota(jnp.int32, (n_heads, width), 1) // HEAD_DIM
    own = head_row == head_col
    qbd = jnp.concatenate([jnp.where(own, q_ref[t:t + 1, :], 0.0) for t in range(t_new)], axis=0)
    mult = mult_ref[...]

    s = jnp.dot(qbd.astype(_BF16), kt_ref[...].astype(_BF16), preferred_element_type=_F32)
    s = jnp.where(mult > 0.0, s, NEG_INF)
    m = jnp.max(s, axis=-1, keepdims=True)
    t_col = lax.broadcasted_iota(jnp.int32, (rows, 1), 0) // n_heads
    s_new, mult_new = [], []
    for tp in range(t_new):
        s_new.append(jnp.sum(qbd * kn_ref[tp:tp + 1, :], axis=-1, keepdims=True))
        dn = t_col - tp
        mn = jnp.zeros((rows, 1), _F32)
        for window, d in DILATIONS:
            mn = mn + jnp.where((dn >= 0) & (dn % d == 0), 1.0, 0.0)
        mult_new.append(mn)
        m = jnp.maximum(m, jnp.where(mn > 0.0, s_new[tp], NEG_INF))
    p = jnp.exp(s - m) * mult
    den = jnp.sum(p, axis=-1, keepdims=True)
    num = lax.dot_general(p.astype(_BF16), vt_ref[...].astype(_BF16), (((1,), (1,)), ((), ())),
                          preferred_element_type=_F32)
    for tp in range(t_new):
        pn = jnp.where(mult_new[tp] > 0.0, jnp.exp(s_new[tp] - m), 0.0) * mult_new[tp]
        den = den + pn
        num = num + pn * vn_ref[tp:tp + 1, :]
    res = num / den
    for t in range(t_new):
        blk = res[t * n_heads:(t + 1) * n_heads, :]
        o_ref[t:t + 1, :] = jnp.sum(jnp.where(own, blk, 0.0), axis=0, keepdims=True)


def _sample_out_kernel(oa_ref, ga_ref, u_ref, vc_ref, gb_ref, x_ref, wout_ref, coef_ref, bias_ref, fg_ref,
                       y_ref, c_ref, *, t_new):
    m, d_attn = oa_ref.shape
    db = m // t_new
    a = (oa_ref[...] * ga_ref[...]).astype(_BF16)
    for t in range(t_new):
        mixed = jnp.zeros((db, c_ref.shape[1]), _F32) + bias_ref[t:t + 1, :]
        for s in range(t + 1):
            mixed = mixed + coef_ref[t * t_new + s:t * t_new + s + 1, :] * vc_ref[s * db:(s + 1) * db, :]
        rs = slice(t * db, (t + 1) * db)
        c_ref[rs, :] = (u_ref[rs, :] * mixed * gb_ref[rs, :]).astype(c_ref.dtype)
    acc = jnp.dot(a, wout_ref[:d_attn, :].astype(_BF16), preferred_element_type=_F32)
    acc = acc + jnp.dot(c_ref[...], wout_ref[d_attn:, :].astype(_BF16), preferred_element_type=_F32)
    xo = x_ref[...] + acc
    ms = jnp.mean(xo * xo, axis=-1, keepdims=True)
    y_ref[...] = xo * lax.rsqrt(ms + EPS) * fg_ref[...]


def _sample_out(oa, ga, u, vc, gb, x, w_out, coef, bias, final_g, *, t_new):
    m, d_model = x.shape
    args = (oa, ga, u, vc, gb, x, w_out, coef, bias, final_g.reshape(1, -1))
    whole = lambda a: pl.BlockSpec(a.shape, lambda i: (0,) * a.ndim)
    return pl.pallas_call(
        functools.partial(_sample_out_kernel, t_new=t_new),
        grid=(1,),
        in_specs=[whole(a) for a in args],
        out_specs=pl.BlockSpec((m, d_model), lambda i: (0, 0)),
        out_shape=jax.ShapeDtypeStruct((m, d_model), _F32),
        scratch_shapes=[pltpu.VMEM((m, u.shape[1]), _BF16)],
        compiler_params=_params("arbitrary"),
        name="sample_out",
    )(*args)


def kernel(x_prompt, x_sample, cache_k, cache_v, norm_g, w_in, ln_g, ln_b, w_s, b_s, w_out, final_g):
    batch, s_len, d_model = x_prompt.shape
    db, t_new, _ = x_sample.shape
    depth, _, wb, n_heads, head_dim = cache_k.shape
    d_attn = n_heads * head_dim
    d_chunk = w_out.shape[1] - d_attn
    assert batch == 1 and depth == 1 and head_dim == HEAD_DIM
    assert w_in.shape[2] == 4 * d_attn + 3 * d_chunk and d_attn == d_chunk
    tail = min(MAX_WINDOW, s_len)
    w_in0, w_out0 = w_in[0], w_out[0]
    ng, lg, lb = norm_g[0], ln_g[0], ln_b[0]

    xs = jnp.transpose(x_sample, (1, 0, 2)).reshape(t_new * db, d_model)
    pos_s = PAST_LEN + jnp.repeat(jnp.arange(t_new, dtype=jnp.int32), db)
    cos_s, sin_s = _rope_tables(pos_s)
    qs, ks, vs, gas, us, vcs, gbs = _sample_in_proj(xs, ng, w_in0, cos_s, sin_s, lg, lb)
    to_b_major = lambda a: jnp.transpose(a.reshape(t_new, db, -1), (1, 0, 2))
    qs_b, ks_b, vs_b, vcs_b = (to_b_major(a) for a in (qs, ks, vs, vcs))
    cache_t = lambda c: jnp.transpose(c[0], (0, 2, 3, 1)).reshape(db, d_attn, wb)
    cache_kt, cache_vt = cache_t(cache_k), cache_t(cache_v)
    assert wb == MAX_WINDOW

    xp = x_prompt.reshape(s_len, d_model)
    tables = _rope_tables(jnp.arange(s_len, dtype=jnp.int32))
    xn, q1, q4, q16 = _proj(xp, w_in0, SEG_Q, tm=512, norm_g=ng, tables=tables,
                            scale=HEAD_DIM ** -0.5 * math.log2(math.e))
    k1, k4, k16, k_tail_t = _proj(xn, w_in0, SEG_K, tm=512, tables=tables, tail_rows=tail)
    v1, v4, v16, v_tail_t = _proj(xn, w_in0, SEG_V, tm=512, tail_rows=tail)
    tm_gate = 512
    tiles_per_call = s_len // tm_gate
    assert 4 * tiles_per_call == db * (d_attn // SAMPLE_HEAD_GROUP_WIDTH)
    side = lambda c: (qs_b, ks_b, vs_b, cache_kt, cache_vt, c * tiles_per_call)
    ga, oa0 = _proj(xn, w_in0, SEG_GA, tm=tm_gate, side=side(0))
    u, oa1 = _proj(xn, w_in0, SEG_U, tm=tm_gate, side=side(1))
    vc, oa2 = _proj(xn, w_in0, SEG_VC, tm=tm_gate, ln=(lg, lb), side=side(2))
    gb, oa3 = _proj(xn, w_in0, SEG_GB, tm=tm_gate, side=side(3))
    oa = jnp.concatenate([oa0, oa1, oa2, oa3], axis=0)
    q_rows = 512
    o1, m1, d1 = _dilated_attention(q1[None], k1[None], v1[None], q_rows=q_rows)
    o4, m4, d4 = _dilated_attention(q4, k4, v4, q_rows=q_rows)
    o16, m16, d16 = _dilated_attention(q16, k16, v16, q_rows=q_rows)
    y_prompt = _out_proj((o1[0], o4, o16), (m1[0], m4, m16), (d1[0], d4, d16), ga, u, vc, gb, xp, w_out0,
                         w_s[0], b_s[0].T, final_g, tm=256)
    untranspose = lambda a: jnp.transpose(a.reshape(n_heads, head_dim, tail), (2, 0, 1))

    oa_t = jnp.transpose(oa, (1, 0, 2)).reshape(t_new * db, d_attn)
    coef = jnp.repeat(jnp.transpose(w_s[0][:, :t_new, :t_new], (1, 2, 0)).reshape(t_new * t_new, -1),
                      GROUP_WIDTH_B, axis=1)
    bias = jnp.repeat(jnp.transpose(b_s[0][:, :t_new], (1, 0)), GROUP_WIDTH_B, axis=1)
    ys = _sample_out(oa_t, gas, us, vcs, gbs, xs, w_out0, coef, bias, final_g, t_new=t_new)
    y_sample = jnp.transpose(ys.reshape(t_new, db, d_model), (1, 0, 2))

    hs = (n_heads, head_dim)
    return (
        y_prompt.reshape(batch, s_len, d_model),
        y_sample,
        untranspose(k_tail_t).reshape(depth, batch, tail, *hs),
        untranspose(v_tail_t).reshape(depth, batch, tail, *hs),
        ks_b.reshape(depth, db, t_new, *hs),
        vs_b.reshape(depth, db, t_new, *hs),
        vcs_b.reshape(depth, db, t_new, d_chunk),
    )
```

```python
import functools
import math

import jax
import jax.numpy as jnp
from jax import lax
from jax.experimental import pallas as pl
from jax.experimental.pallas import tpu as pltpu

HEAD_DIM = 64
BLOCK = 128
CHUNK = 128
GROUP_WIDTH_B = 128
DILATIONS = ((128, 1), (512, 4), (2048, 16))
MAX_WINDOW = 2048
PAST_LEN = 16384
ROPE_THETA = 10000.0
EPS = 1e-6
NEG_INF = -1e30
N_SEGMENTS = 7
SEG_Q, SEG_K, SEG_V, SEG_GA, SEG_U, SEG_VC, SEG_GB = range(N_SEGMENTS)

LANES = 128
VMEM_LIMIT_BYTES = 56 * 1024 * 1024
SUB_ROWS = 256
SAMPLE_HEAD_GROUP_WIDTH = 512

_BF16 = jnp.bfloat16
_F32 = jnp.float32


def _params(*semantics):
    return pltpu.CompilerParams(dimension_semantics=semantics, vmem_limit_bytes=VMEM_LIMIT_BYTES)


def _rope_tables(pos):
    half = HEAD_DIM // 2
    inv = jnp.exp(-math.log(ROPE_THETA) * jnp.arange(half, dtype=_F32) / half)
    ang = pos.astype(_F32)[:, None] * inv[None, :]
    cos = jnp.cos(ang)
    sin = jnp.sin(ang)
    cos_t = jnp.concatenate([cos, cos, cos, cos], axis=-1)
    sin_t = jnp.concatenate([-sin, sin, -sin, sin], axis=-1)
    return cos_t, sin_t


def _rope_tables_blocked(s_len, tm):
    half = HEAD_DIM // 2
    inv = jnp.exp(-math.log(ROPE_THETA) * jnp.arange(half, dtype=_F32) / half)
    tile4 = lambda a: jnp.concatenate([a, a, a, a], axis=-1)
    ang_r = jnp.arange(tm, dtype=jnp.int32).astype(_F32)[:, None] * inv[None, :]
    ang_b = (jnp.arange(s_len // tm, dtype=jnp.int32) * tm).astype(_F32)[:, None] * inv[None, :]
    sign = jnp.concatenate([-jnp.ones((1, half), _F32), jnp.ones((1, half), _F32)] * 2, axis=-1)
    return (tile4(jnp.cos(ang_r)), tile4(jnp.sin(ang_r)), tile4(jnp.cos(ang_b)), tile4(jnp.sin(ang_b)), sign)


def _rope_group(zg, cos, sin_signed):
    lane = lax.broadcasted_iota(jnp.int32, zg.shape, 1)
    first_half = (lane % HEAD_DIM) < (HEAD_DIM // 2)
    partner = jnp.where(first_half, pltpu.roll(zg, LANES - HEAD_DIM // 2, 1),
                        pltpu.roll(zg, HEAD_DIM // 2, 1))
    return zg * cos + partner * sin_signed


def _gelu_layer_norm(z, g, b):
    h = jax.nn.gelu(z)
    mu = jnp.mean(h, axis=-1, keepdims=True)
    hc = h - mu
    var = jnp.mean(hc * hc, axis=-1, keepdims=True)
    return hc * lax.rsqrt(var + EPS) * g + b


def _proj_kernel(*refs, seg, scale, tail_start, side_job):
    qkv = seg in (SEG_Q, SEG_K, SEG_V)
    rope = seg in (SEG_Q, SEG_K)
    it = iter(refs)
    x_ref = next(it)
    g_ref = next(it) if seg == SEG_Q else None
    w_ref = next(it)
    rope_refs = [next(it) for _ in range(5)] if rope else None
    lng_ref, lnb_ref = (next(it), next(it)) if seg == SEG_VC else (None, None)
    side_in = [next(it) for _ in range(5)] if side_job else None
    xn_out_ref = next(it) if seg == SEG_Q else None
    if qkv:
        d1_ref, d4_ref, d16_ref = next(it), next(it), next(it)
        tail_ref = next(it) if tail_start is not None else None
    else:
        o_ref = next(it)
    side_out_ref = next(it) if side_job else None
    wb_ref = next(it)
    if qkv:
        nat_ref, res4_ref = next(it), next(it)
    mult_ref = next(it) if side_job else None

    @pl.when(pl.program_id(0) == 0)
    def _():
        wb_ref[...] = w_ref[...].astype(wb_ref.dtype)
        if side_job:
            mult_ref[...] = _key_multiplicity(mult_ref.shape[0], mult_ref.shape[1], side_in[0].shape[0])

    if side_job:
        _sample_attn_tile(*side_in, side_out_ref, mult_ref)

    tm = x_ref.shape[0]
    sub = min(SUB_ROWS, tm)
    n4, n16 = sub // 4, sub // 16
    for t in range(tm // sub):
        rs = slice(t * sub, (t + 1) * sub)
        if seg == SEG_Q:
            xf = x_ref[rs, :]
            ms = jnp.mean(xf * xf, axis=-1, keepdims=True)
            xb = (xf * lax.rsqrt(ms + EPS) * g_ref[...]).astype(_BF16)
            xn_out_ref[rs, :] = xb
        else:
            xb = x_ref[rs, :]
        z = jnp.dot(xb, wb_ref[...], preferred_element_type=_F32)
        if rope:
            cos_r_ref, sin_r_ref, cos_b_ref, sin_b_ref, sign_ref = rope_refs
            step = pl.ds(pl.program_id(0), 1)
            cb, sb = cos_b_ref[step, :], sin_b_ref[step, :]
            cr, sr = cos_r_ref[rs, :], sin_r_ref[rs, :]
            cos = cr * cb - sr * sb
            sin_signed = (sr * cb + cr * sb) * sign_ref[...]
        if seg in (SEG_GA, SEG_GB):
            o_ref[rs, :] = jax.nn.silu(z).astype(o_ref.dtype)
        elif seg == SEG_U:
            o_ref[rs, :] = jax.nn.gelu(z).astype(o_ref.dtype)
        elif seg == SEG_VC:
            o_ref[rs, :] = _gelu_layer_norm(z, lng_ref[...], lnb_ref[...]).astype(o_ref.dtype)
        else:
            for g in range(z.shape[1] // LANES):
                sl = slice(g * LANES, (g + 1) * LANES)
                r = z[:, sl]
                if rope:
                    r = _rope_group(r, cos, sin_signed)
                if tail_ref is not None:
                    tail_ref[sl, rs] = r.T
                if scale != 1.0:
                    r = r * scale
                d1_ref[rs, sl] = r.astype(d1_ref.dtype)
                nat_ref[g, rs, :] = r
                for r4 in range(4):
                    blk = nat_ref[g, pl.ds(t * sub + r4, n4, stride=4), :]
                    d4_ref[r4, t * n4:(t + 1) * n4, sl] = blk.astype(d4_ref.dtype)
                    res4_ref[g, t * sub + r4 * n4:t * sub + (r4 + 1) * n4, :] = blk
                for r4 in range(4):
                    for a in range(4):
                        blk = res4_ref[g, pl.ds(t * sub + r4 * n4 + a, n16, stride=4), :]
                        d16_ref[r4 + 4 * a, t * n16:(t + 1) * n16, sl] = blk.astype(d16_ref.dtype)


def _proj(x, w_in, seg, *, tm, norm_g=None, tables=None, ln=None, scale=1.0, tail_rows=0, side=None):
    s_len, d_model = x.shape
    d_seg = w_in.shape[1] // N_SEGMENTS
    n_i = s_len // tm
    qkv = seg in (SEG_Q, SEG_K, SEG_V)
    row_vec = lambda n: pl.BlockSpec((1, n), lambda i: (0, 0))
    in_specs = [pl.BlockSpec((tm, d_model), lambda i: (i, 0))]
    args = [x]
    if seg == SEG_Q:
        in_specs.append(row_vec(d_model))
        args.append(norm_g.reshape(1, -1))
    in_specs.append(pl.BlockSpec((d_model, d_seg), lambda i: (0, seg), pipeline_mode=pl.Buffered(1)))
    args.append(w_in)
    if seg in (SEG_Q, SEG_K):
        in_specs += [pl.BlockSpec(t.shape, lambda i: (0, 0)) for t in tables]
        args += list(tables)
    if seg == SEG_VC:
        in_specs += [row_vec(d_seg)] * 2
        args += [ln[0].reshape(1, -1), ln[1].reshape(1, -1)]
    if side is not None:
        sq, skn, svn, skt, svt, first_tile = side
        t_new, d_attn = sq.shape[1:]
        width, wb = SAMPLE_HEAD_GROUP_WIDTH, skt.shape[2]
        hg = d_attn // width
        new = pl.BlockSpec((None, t_new, width), lambda i: ((first_tile + i) // hg, 0, (first_tile + i) % hg))
        cache = pl.BlockSpec((None, width, wb), lambda i: ((first_tile + i) // hg, (first_tile + i) % hg, 0))
        in_specs += [new, new, new, cache, cache]
        args += [sq, skn, svn, skt, svt]
    out_shape, out_specs = [], []
    if seg == SEG_Q:
        out_shape.append(jax.ShapeDtypeStruct((s_len, d_model), _BF16))
        out_specs.append(pl.BlockSpec((tm, d_model), lambda i: (i, 0)))
    scratch = [pltpu.VMEM((d_model, d_seg), _BF16)]
    tail_start = None
    if qkv:
        out_shape += [jax.ShapeDtypeStruct((s_len, d_seg), _BF16),
                      jax.ShapeDtypeStruct((4, s_len // 4, d_seg), _BF16),
                      jax.ShapeDtypeStruct((16, s_len // 16, d_seg), _BF16)]
        out_specs += [pl.BlockSpec((tm, d_seg), lambda i: (i, 0)),
                      pl.BlockSpec((4, tm // 4, d_seg), lambda i: (0, i, 0)),
                      pl.BlockSpec((16, tm // 16, d_seg), lambda i: (0, i, 0))]
        if tail_rows:
            tail_start = n_i - tail_rows // tm
            out_shape.append(jax.ShapeDtypeStruct((d_seg, tail_rows), _F32))
            out_specs.append(pl.BlockSpec((d_seg, tm), lambda i: (0, jnp.maximum(i - tail_start, 0))))
        scratch += [pltpu.VMEM((d_seg // LANES, tm, LANES), _F32)] * 2
    else:
        out_shape.append(jax.ShapeDtypeStruct((s_len, d_seg), _BF16))
        out_specs.append(pl.BlockSpec((tm, d_seg), lambda i: (i, 0)))
    if side is not None:
        assert first_tile % hg == 0 and n_i % hg == 0
        out_shape.append(jax.ShapeDtypeStruct((n_i // hg, t_new, d_attn), _F32))
        out_specs.append(pl.BlockSpec((None, t_new, width), lambda i: (i // hg, 0, i % hg)))
        scratch.append(pltpu.VMEM((t_new * width // HEAD_DIM, wb), _F32))
    return pl.pallas_call(
        functools.partial(_proj_kernel, seg=seg, scale=scale, tail_start=tail_start, side_job=side is not None),
        grid=(n_i,),
        in_specs=in_specs,
        out_specs=out_specs,
        out_shape=out_shape,
        scratch_shapes=scratch,
        compiler_params=_params("arbitrary"),
        name=f"proj_seg{seg}",
    )(*args)


def _attn_kernel(q_ref, kp_ref, kc_ref, vp_ref, vc_ref, o_ref, m_ref, den_ref, kall_ref, vall_ref):
    n = pl.program_id(1)
    first_step = jnp.logical_and(pl.program_id(0) == 0, n == 0)
    n_pairs = q_ref.shape[1] // LANES
    two = 2 * BLOCK

    kall_ref[0:BLOCK, :] = kp_ref[...]
    kall_ref[BLOCK:, :] = kc_ref[...]
    for hp in range(n_pairs):
        sl = slice(hp * LANES, (hp + 1) * LANES)
        ext = slice(2 * hp * LANES, (2 * hp + 1) * LANES)
        vall_ref[0:BLOCK, ext] = vp_ref[:, sl]
        vall_ref[BLOCK:, ext] = vc_ref[:, sl]

    @pl.when(first_step)
    def _():
        ones = jnp.ones((vall_ref.shape[0], LANES), vall_ref.dtype)
        for hp in range(n_pairs):
            vall_ref[:, (2 * hp + 1) * LANES:(2 * hp + 2) * LANES] = ones

    qi = lax.broadcasted_iota(jnp.int32, (two, two), 0) % BLOCK
    si = lax.broadcasted_iota(jnp.int32, (two, two), 1)
    dist = qi + BLOCK - si
    band_bias = jnp.where((dist >= 0) & (dist <= BLOCK), 0.0, NEG_INF).astype(_F32)
    lane = lax.broadcasted_iota(jnp.int32, (BLOCK, LANES), 1)
    head0 = lane < HEAD_DIM
    zero = jnp.zeros((BLOCK, LANES), q_ref.dtype)

    for j in range(q_ref.shape[0] // BLOCK):
        rows = slice(j * BLOCK, (j + 1) * BLOCK)
        keys = slice(j * BLOCK, j * BLOCK + two)
        bias = band_bias
        if j == 0:
            bias = band_bias + jnp.where(jnp.logical_and(si < BLOCK, n == 0), NEG_INF, 0.0)
        m_acc = jnp.zeros((BLOCK, LANES), _F32)
        den_acc = jnp.ones((BLOCK, LANES), _F32)
        for hp in range(n_pairs):
            sl = slice(hp * LANES, (hp + 1) * LANES)
            q2 = q_ref[rows, sl]
            qs = jnp.concatenate([jnp.where(head0, q2, zero), jnp.where(head0, zero, q2)], axis=0)
            s = lax.dot_general(qs, kall_ref[keys, sl], (((1,), (1,)), ((), ())),
                                preferred_element_type=_F32) + bias
            m = jnp.max(s, axis=-1, keepdims=True)
            p = jnp.exp2(s - m).astype(vall_ref.dtype)
            pv = jnp.dot(p, vall_ref[keys, 2 * hp * LANES:(2 * hp + 2) * LANES], preferred_element_type=_F32)
            num, den = pv[:, :LANES], pv[:, LANES:]
            o_ref[rows, sl] = jnp.where(head0, num[:BLOCK], num[BLOCK:]).astype(o_ref.dtype)
            m_acc = jnp.where(lane == 2 * hp, m[:BLOCK], jnp.where(lane == 2 * hp + 1, m[BLOCK:], m_acc))
            den_acc = jnp.where(lane == 2 * hp, den[:BLOCK],
                                jnp.where(lane == 2 * hp + 1, den[BLOCK:], den_acc))
        m_ref[rows, :] = m_acc
        den_ref[rows, :] = den_acc


def _dilated_attention(q, k, v, *, q_rows):
    d, sub_len, d_attn = q.shape
    per = q_rows // BLOCK
    cur = pl.BlockSpec((None, q_rows, d_attn), lambda r, n: (r, n, 0))
    prev = pl.BlockSpec((None, BLOCK, d_attn), lambda r, n: (r, jnp.maximum(n * per - 1, 0), 0))
    stat = pl.BlockSpec((None, q_rows, LANES), lambda r, n: (r, n, 0))
    stat_shape = jax.ShapeDtypeStruct((d, sub_len, LANES), _F32)
    return pl.pallas_call(
        _attn_kernel,
        grid=(d, sub_len // q_rows),
        in_specs=[cur, prev, cur, prev, cur],
        out_specs=[cur, stat, stat],
        out_shape=[jax.ShapeDtypeStruct((d, sub_len, d_attn), q.dtype), stat_shape, stat_shape],
        scratch_shapes=[pltpu.VMEM((q_rows + BLOCK, d_attn), q.dtype),
                        pltpu.VMEM((q_rows + BLOCK, 2 * d_attn), q.dtype)],
        compiler_params=_params("arbitrary", "arbitrary"),
        name=f"dilated_attn_d{d}",
    )(q, k, k, v, v)


def _expand_heads(w, expand2):
    n_heads = expand2.shape[1] // HEAD_DIM
    hi = w.astype(_BF16).astype(_F32)
    lane = lax.broadcasted_iota(jnp.int32, w.shape, 1)
    lhs = jnp.where(lane < n_heads, hi, pltpu.roll(w - hi, n_heads, 1))
    return jnp.dot(lhs.astype(_BF16), expand2, preferred_element_type=_F32)


def _out_proj_kernel(o1_ref, o4_ref, o16_ref, m1_ref, m4_ref, m16_ref, d1_ref, d4_ref, d16_ref,
                     ga_ref, u_ref, vc_ref, gb_ref, x_ref, wb_ref, ws_ref, bst_ref, fg_ref, y_ref,
                     a_ref, c_ref, n4_ref, n16_ref, st_ref):
    tm, d_attn = o1_ref.shape
    n_heads = d_attn // HEAD_DIM
    n_lane_groups = d_attn // LANES
    sub = 2 * CHUNK
    n4, n16 = sub // 4, sub // 16

    row_head = lax.broadcasted_iota(jnp.int32, (LANES, d_attn), 0) % n_heads
    row_used = lax.broadcasted_iota(jnp.int32, (LANES, d_attn), 0) < 2 * n_heads
    col_head = lax.broadcasted_iota(jnp.int32, (LANES, d_attn), 1) // HEAD_DIM
    expand2 = jnp.logical_and(row_head == col_head, row_used).astype(_BF16)
    tril = (lax.broadcasted_iota(jnp.int32, (CHUNK, CHUNK), 0)
            >= lax.broadcasted_iota(jnp.int32, (CHUNK, CHUNK), 1))

    for t in range(tm // sub):
        rs = slice(t * sub, (t + 1) * sub)
        for r in range(4):
            dst = pl.ds(t * sub + r, n4, stride=4)
            src = slice(t * n4, (t + 1) * n4)
            st_ref[0, dst, :] = m4_ref[r, src, :]
            st_ref[1, dst, :] = d4_ref[r, src, :]
            for g in range(n_lane_groups):
                n4_ref[g, dst, :] = o4_ref[r, src, g * LANES:(g + 1) * LANES].astype(_F32)
        for r in range(16):
            dst = pl.ds(t * sub + r, n16, stride=16)
            src = slice(t * n16, (t + 1) * n16)
            st_ref[2, dst, :] = m16_ref[r, src, :]
            st_ref[3, dst, :] = d16_ref[r, src, :]
            for g in range(n_lane_groups):
                n16_ref[g, dst, :] = o16_ref[r, src, g * LANES:(g + 1) * LANES].astype(_F32)

        ms = [m1_ref[rs, :], st_ref[0, rs, :], st_ref[2, rs, :]]
        dens = [d1_ref[rs, :], st_ref[1, rs, :], st_ref[3, rs, :]]
        mx = jnp.maximum(jnp.maximum(ms[0], ms[1]), ms[2])
        es = [jnp.exp2(m - mx) for m in ms]
        tot = es[0] * dens[0] + es[1] * dens[1] + es[2] * dens[2]
        ws = [_expand_heads(e / tot, expand2) for e in es]
        for g in range(n_lane_groups):
            sl = slice(g * LANES, (g + 1) * LANES)
            a = (ws[0][:, sl] * o1_ref[rs, sl].astype(_F32) + ws[1][:, sl] * n4_ref[g, rs, :]
                 + ws[2][:, sl] * n16_ref[g, rs, :])
            a_ref[rs, sl] = (a * ga_ref[rs, sl].astype(_F32)).astype(a_ref.dtype)

        c0 = slice(t * sub, t * sub + CHUNK)
        c1 = slice(t * sub + CHUNK, (t + 1) * sub)
        for g in range(ws_ref.shape[0]):
            wm = jnp.where(tril, ws_ref[g], 0.0).astype(_BF16)
            gs = slice(g * GROUP_WIDTH_B, (g + 1) * GROUP_WIDTH_B)
            vc2 = jnp.concatenate([vc_ref[c0, gs], vc_ref[c1, gs]], axis=1)
            mixed = jnp.dot(wm, vc2, preferred_element_type=_F32) + bst_ref[:, g:g + 1]
            for c, half in ((c0, slice(0, GROUP_WIDTH_B)), (c1, slice(GROUP_WIDTH_B, 2 * GROUP_WIDTH_B))):
                cv = u_ref[c, gs].astype(_F32) * mixed[:, half] * gb_ref[c, gs].astype(_F32)
                c_ref[c, gs] = cv.astype(c_ref.dtype)

        acc = jnp.dot(a_ref[rs, :], wb_ref[:d_attn, :], preferred_element_type=_F32)
        acc = acc + jnp.dot(c_ref[rs, :], wb_ref[d_attn:, :], preferred_element_type=_F32)
        xo = x_ref[rs, :] + acc
        msq = jnp.mean(xo * xo, axis=-1, keepdims=True)
        y_ref[rs, :] = xo * lax.rsqrt(msq + EPS) * fg_ref[...]


def _out_proj(os, ms, dens, ga, u, vc, gb, x, w_out, w_s, b_s_t, final_g, *, tm):
    m, d_model = x.shape
    d_attn = ga.shape[1]
    d_chunk = u.shape[1]
    rows = lambda n: pl.BlockSpec((tm, n), lambda i: (i, 0))
    res = lambda d, n: pl.BlockSpec((d, tm // d, n), lambda i: (0, i, 0))
    whole = lambda a: pl.BlockSpec(a.shape, lambda i: (0,) * a.ndim)
    by_pattern = lambda n: [rows(n), res(4, n), res(16, n)]
    fg = final_g.reshape(1, -1)
    slab = pltpu.VMEM((d_attn // LANES, tm, LANES), _F32)
    return pl.pallas_call(
        _out_proj_kernel,
        grid=(m // tm,),
        in_specs=by_pattern(d_attn) + by_pattern(LANES) + by_pattern(LANES)
        + [rows(d_attn), rows(d_chunk), rows(d_chunk), rows(d_chunk), rows(d_model),
           pl.BlockSpec(w_out.shape, lambda i: (0, 0), pipeline_mode=pl.Buffered(1)),
           whole(w_s), whole(b_s_t), whole(fg)],
        out_specs=rows(d_model),
        out_shape=jax.ShapeDtypeStruct((m, d_model), _F32),
        scratch_shapes=[pltpu.VMEM((tm, d_attn), _BF16), pltpu.VMEM((tm, d_chunk), _BF16), slab, slab,
                        pltpu.VMEM((4, tm, LANES), _F32)],
        compiler_params=_params("arbitrary"),
        name="out_proj",
    )(*os, *ms, *dens, ga, u, vc, gb, x, w_out, w_s, b_s_t, fg)


def _sample_in_proj_kernel(x_ref, g_ref, w_ref, cos_ref, sin_ref, lng_ref, lnb_ref,
                           q_ref, k_ref, v_ref, ga_ref, u_ref, vc_ref, gb_ref, xn_ref):
    j = pl.program_id(0)

    @pl.when(j == 0)
    def _():
        xf = x_ref[...]
        ms = jnp.mean(xf * xf, axis=-1, keepdims=True)
        xn_ref[...] = (xf * lax.rsqrt(ms + EPS) * g_ref[...]).astype(xn_ref.dtype)

    z = jnp.dot(xn_ref[...], w_ref[...].astype(_BF16), preferred_element_type=_F32)

    def rope_to(ref, scale):
        for g in range(z.shape[1] // LANES):
            sl = slice(g * LANES, (g + 1) * LANES)
            ref[:, sl] = _rope_group(z[:, sl], cos_ref[...], sin_ref[...]) * scale

    @pl.when(j == SEG_Q)
    def _():
        rope_to(q_ref, HEAD_DIM ** -0.5)

    @pl.when(j == SEG_K)
    def _():
        rope_to(k_ref, 1.0)

    @pl.when(j == SEG_V)
    def _():
        v_ref[...] = z

    @pl.when(j == SEG_GA)
    def _():
        ga_ref[...] = jax.nn.silu(z)

    @pl.when(j == SEG_U)
    def _():
        u_ref[...] = jax.nn.gelu(z)

    @pl.when(j == SEG_VC)
    def _():
        vc_ref[...] = _gelu_layer_norm(z, lng_ref[...], lnb_ref[...])

    @pl.when(j == SEG_GB)
    def _():
        gb_ref[...] = jax.nn.silu(z)


def _sample_in_proj(x, norm_g, w_in, cos_t, sin_t, ln_g, ln_b):
    m, d_model = x.shape
    d_seg = w_in.shape[1] // N_SEGMENTS
    whole = lambda r, c: pl.BlockSpec((r, c), lambda j: (0, 0))
    return pl.pallas_call(
        _sample_in_proj_kernel,
        grid=(N_SEGMENTS,),
        in_specs=[whole(m, d_model), whole(1, d_model),
                  pl.BlockSpec((d_model, d_seg), lambda j: (0, j)),
                  whole(m, LANES), whole(m, LANES), whole(1, d_seg), whole(1, d_seg)],
        out_specs=[whole(m, d_seg)] * N_SEGMENTS,
        out_shape=[jax.ShapeDtypeStruct((m, d_seg), _F32)] * N_SEGMENTS,
        scratch_shapes=[pltpu.VMEM((m, d_model), _BF16)],
        compiler_params=_params("arbitrary"),
        name="sample_in_proj",
    )(x, norm_g.reshape(1, -1), w_in, cos_t, sin_t, ln_g.reshape(1, -1), ln_b.reshape(1, -1))


def _key_multiplicity(rows, wb, t_new):
    t_row = lax.broadcasted_iota(jnp.int32, (rows, wb), 0) // (rows // t_new)
    pos = lax.broadcasted_iota(jnp.int32, (rows, wb), 1)
    dist = wb + t_row - pos
    mult = jnp.zeros((rows, wb), _F32)
    for window, d in DILATIONS:
        mult = mult + jnp.where((dist % d == 0) & (dist <= window), 1.0, 0.0)
    return mult


def _sample_attn_tile(q_ref, kn_ref, vn_ref, kt_ref, vt_ref, o_ref, mult_ref):
    t_new, width = q_ref.shape
    n_heads = width // HEAD_DIM
    rows = t_new * n_heads
    head_row = lax.broadcasted_iota(jnp.int32, (n_heads, width), 0)
    head_col = lax.broadcasted_iota(jnp.int32, (n_heads, width), 1) // HEAD_DIM
    own = head_row == head_col
    qbd = jnp.concatenate([jnp.where(own, q_ref[t:t + 1, :], 0.0) for t in range(t_new)], axis=0)
    mult = mult_ref[...]

    s = jnp.dot(qbd.astype(_BF16), kt_ref[...].astype(_BF16), preferred_element_type=_F32)
    s = jnp.where(mult > 0.0, s, NEG_INF)
    m = jnp.max(s, axis=-1, keepdims=True)
    t_col = lax.broadcasted_iota(jnp.int32, (rows, 1), 0) // n_heads
    s_new, mult_new = [], []
    for tp in range(t_new):
        s_new.append(jnp.sum(qbd * kn_ref[tp:tp + 1, :], axis=-1, keepdims=True))
        dn = t_col - tp
        mn = jnp.zeros((rows, 1), _F32)
        for window, d in DILATIONS:
            mn = mn + jnp.where((dn >= 0) & (dn % d == 0), 1.0, 0.0)
        mult_new.append(mn)
        m = jnp.maximum(m, jnp.where(mn > 0.0, s_new[tp], NEG_INF))
    p = jnp.exp(s - m) * mult
    den = jnp.sum(p, axis=-1, keepdims=True)
    num = lax.dot_general(p.astype(_BF16), vt_ref[...].astype(_BF16), (((1,), (1,)), ((), ())),
                          preferred_element_type=_F32)
    for tp in range(t_new):
        pn = jnp.where(mult_new[tp] > 0.0, jnp.exp(s_new[tp] - m), 0.0) * mult_new[tp]
        den = den + pn
        num = num + pn * vn_ref[tp:tp + 1, :]
    res = num / den
    for t in range(t_new):
        blk = res[t * n_heads:(t + 1) * n_heads, :]
        o_ref[t:t + 1, :] = jnp.sum(jnp.where(own, blk, 0.0), axis=0, keepdims=True)


def _sample_out_kernel(oa_ref, ga_ref, u_ref, vc_ref, gb_ref, x_ref, wout_ref, coef_ref, bias_ref, fg_ref,
                       y_ref, wb_ref, c_ref, *, t_new):
    m, d_attn = oa_ref.shape
    db = m // t_new
    wb_ref[...] = wout_ref[...].astype(wb_ref.dtype)
    a = (oa_ref[...] * ga_ref[...]).astype(_BF16)
    for t in range(t_new):
        mixed = jnp.zeros((db, c_ref.shape[1]), _F32) + bias_ref[t:t + 1, :]
        for s in range(t + 1):
            mixed = mixed + coef_ref[t * t_new + s:t * t_new + s + 1, :] * vc_ref[s * db:(s + 1) * db, :]
        rs = slice(t * db, (t + 1) * db)
        c_ref[rs, :] = (u_ref[rs, :] * mixed * gb_ref[rs, :]).astype(c_ref.dtype)
    acc = jnp.dot(a, wb_ref[:d_attn, :], preferred_element_type=_F32)
    acc = acc + jnp.dot(c_ref[...], wb_ref[d_attn:, :], preferred_element_type=_F32)
    xo = x_ref[...] + acc
    ms = jnp.mean(xo * xo, axis=-1, keepdims=True)
    y_ref[...] = xo * lax.rsqrt(ms + EPS) * fg_ref[...]


def _sample_out(oa, ga, u, vc, gb, x, w_out, coef, bias, final_g, *, t_new):
    m, d_model = x.shape
    args = (oa, ga, u, vc, gb, x, w_out, coef, bias, final_g.reshape(1, -1))
    whole = lambda a: pl.BlockSpec(a.shape, lambda i: (0,) * a.ndim)
    return pl.pallas_call(
        functools.partial(_sample_out_kernel, t_new=t_new),
        grid=(1,),
        in_specs=[whole(a) for a in args],
        out_specs=[pl.BlockSpec((m, d_model), lambda i: (0, 0)), whole(w_out)],
        out_shape=[jax.ShapeDtypeStruct((m, d_model), _F32), jax.ShapeDtypeStruct(w_out.shape, _BF16)],
        scratch_shapes=[pltpu.VMEM((m, u.shape[1]), _BF16)],
        compiler_params=_params("arbitrary"),
        name="sample_out",
    )(*args)


def kernel(x_prompt, x_sample, cache_k, cache_v, norm_g, w_in, ln_g, ln_b, w_s, b_s, w_out, final_g):
    batch, s_len, d_model = x_prompt.shape
    db, t_new, _ = x_sample.shape
    depth, _, wb, n_heads, head_dim = cache_k.shape
    d_attn = n_heads * head_dim
    d_chunk = w_out.shape[1] - d_attn
    assert batch == 1 and depth == 1 and head_dim == HEAD_DIM
    assert w_in.shape[2] == 4 * d_attn + 3 * d_chunk and d_attn == d_chunk
    tail = min(MAX_WINDOW, s_len)
    w_in0, w_out0 = w_in[0], w_out[0]
    ng, lg, lb = norm_g[0], ln_g[0], ln_b[0]

    xs = jnp.transpose(x_sample, (1, 0, 2)).reshape(t_new * db, d_model)
    pos_s = PAST_LEN + jnp.repeat(jnp.arange(t_new, dtype=jnp.int32), db)
    cos_s, sin_s = _rope_tables(pos_s)
    qs, ks, vs, gas, us, vcs, gbs = _sample_in_proj(xs, ng, w_in0, cos_s, sin_s, lg, lb)
    to_b_major = lambda a: jnp.transpose(a.reshape(t_new, db, -1), (1, 0, 2))
    qs_b, ks_b, vs_b, vcs_b = (to_b_major(a) for a in (qs, ks, vs, vcs))
    cache_t = lambda c: jnp.transpose(c[0], (0, 2, 3, 1)).reshape(db, d_attn, wb)
    cache_kt, cache_vt = cache_t(cache_k), cache_t(cache_v)
    assert wb == MAX_WINDOW

    xp = x_prompt.reshape(s_len, d_model)
    tables = _rope_tables_blocked(s_len, 512)
    xn, q1, q4, q16 = _proj(xp, w_in0, SEG_Q, tm=512, norm_g=ng, tables=tables,
                            scale=HEAD_DIM ** -0.5 * math.log2(math.e))
    k1, k4, k16, k_tail_t = _proj(xn, w_in0, SEG_K, tm=512, tables=tables, tail_rows=tail)
    v1, v4, v16, v_tail_t = _proj(xn, w_in0, SEG_V, tm=512, tail_rows=tail)
    tm_gate = 512
    tiles_per_call = s_len // tm_gate
    assert 4 * tiles_per_call == db * (d_attn // SAMPLE_HEAD_GROUP_WIDTH)
    side = lambda c: (qs_b, ks_b, vs_b, cache_kt, cache_vt, c * tiles_per_call)
    ga, oa0 = _proj(xn, w_in0, SEG_GA, tm=tm_gate, side=side(0))
    u, oa1 = _proj(xn, w_in0, SEG_U, tm=tm_gate, side=side(1))
    vc, oa2 = _proj(xn, w_in0, SEG_VC, tm=tm_gate, ln=(lg, lb), side=side(2))
    gb, oa3 = _proj(xn, w_in0, SEG_GB, tm=tm_gate, side=side(3))
    oa = jnp.concatenate([oa0, oa1, oa2, oa3], axis=0)
    q_rows = 512
    o1, m1, d1 = _dilated_attention(q1[None], k1[None], v1[None], q_rows=q_rows)
    o4, m4, d4 = _dilated_attention(q4, k4, v4, q_rows=q_rows)
    o16, m16, d16 = _dilated_attention(q16, k16, v16, q_rows=q_rows)

    oa_t = jnp.transpose(oa, (1, 0, 2)).reshape(t_new * db, d_attn)
    coef = jnp.repeat(jnp.transpose(w_s[0][:, :t_new, :t_new], (1, 2, 0)).reshape(t_new * t_new, -1),
                      GROUP_WIDTH_B, axis=1)
    bias = jnp.repeat(jnp.transpose(b_s[0][:, :t_new], (1, 0)), GROUP_WIDTH_B, axis=1)
    ys, w_out_b = _sample_out(oa_t, gas, us, vcs, gbs, xs, w_out0, coef, bias, final_g, t_new=t_new)
    y_sample = jnp.transpose(ys.reshape(t_new, db, d_model), (1, 0, 2))

    y_prompt = _out_proj((o1[0], o4, o16), (m1[0], m4, m16), (d1[0], d4, d16), ga, u, vc, gb, xp, w_out_b,
                         w_s[0], b_s[0].T, final_g, tm=512)
    untranspose = lambda a: jnp.transpose(a.reshape(n_heads, head_dim, tail), (2, 0, 1))

    hs = (n_heads, head_dim)
    return (
        y_prompt.reshape(batch, s_len, d_model),
        y_sample,
        untranspose(k_tail_t).reshape(depth, batch, tail, *hs),
        untranspose(v_tail_t).reshape(depth, batch, tail, *hs),
        ks_b.reshape(depth, db, t_new, *hs),
        vs_b.reshape(depth, db, t_new, *hs),
        vcs_b.reshape(depth, db, t_new, d_chunk),
    )
```

```python
import functools
import math

import jax
import jax.numpy as jnp
from jax import lax
from jax.experimental import pallas as pl
from jax.experimental.pallas import tpu as pltpu

HEAD_DIM = 64
BLOCK = 128
CHUNK = 128
GROUP_WIDTH_B = 128
DILATIONS = ((128, 1), (512, 4), (2048, 16))
MAX_WINDOW = 2048
PAST_LEN = 16384
ROPE_THETA = 10000.0
EPS = 1e-6
NEG_INF = -1e30
N_SEGMENTS = 7
SEG_Q, SEG_K, SEG_V, SEG_GA, SEG_U, SEG_VC, SEG_GB = range(N_SEGMENTS)

LANES = 128
VMEM_LIMIT_BYTES = 56 * 1024 * 1024
SUB_ROWS = 256
SAMPLE_HEAD_GROUP_WIDTH = 512

_BF16 = jnp.bfloat16
_F32 = jnp.float32


def _params(*semantics):
    return pltpu.CompilerParams(dimension_semantics=semantics, vmem_limit_bytes=VMEM_LIMIT_BYTES)


def _rope_tables(pos):
    half = HEAD_DIM // 2
    inv = jnp.exp(-math.log(ROPE_THETA) * jnp.arange(half, dtype=_F32) / half)
    ang = pos.astype(_F32)[:, None] * inv[None, :]
    cos = jnp.cos(ang)
    sin = jnp.sin(ang)
    cos_t = jnp.concatenate([cos, cos, cos, cos], axis=-1)
    sin_t = jnp.concatenate([-sin, sin, -sin, sin], axis=-1)
    return cos_t, sin_t


def _rope_tables_blocked(s_len, tm):
    half = HEAD_DIM // 2
    inv = jnp.exp(-math.log(ROPE_THETA) * jnp.arange(half, dtype=_F32) / half)
    tile4 = lambda a: jnp.concatenate([a, a, a, a], axis=-1)
    ang_r = jnp.arange(tm, dtype=jnp.int32).astype(_F32)[:, None] * inv[None, :]
    ang_b = (jnp.arange(s_len // tm, dtype=jnp.int32) * tm).astype(_F32)[:, None] * inv[None, :]
    sign = jnp.concatenate([-jnp.ones((1, half), _F32), jnp.ones((1, half), _F32)] * 2, axis=-1)
    return (tile4(jnp.cos(ang_r)), tile4(jnp.sin(ang_r)), tile4(jnp.cos(ang_b)), tile4(jnp.sin(ang_b)), sign)


def _rope_group(zg, cos, sin_signed):
    lane = lax.broadcasted_iota(jnp.int32, zg.shape, 1)
    first_half = (lane % HEAD_DIM) < (HEAD_DIM // 2)
    partner = jnp.where(first_half, pltpu.roll(zg, LANES - HEAD_DIM // 2, 1),
                        pltpu.roll(zg, HEAD_DIM // 2, 1))
    return zg * cos + partner * sin_signed


def _gelu_layer_norm(z, g, b):
    h = jax.nn.gelu(z)
    mu = jnp.mean(h, axis=-1, keepdims=True)
    hc = h - mu
    var = jnp.mean(hc * hc, axis=-1, keepdims=True)
    return hc * lax.rsqrt(var + EPS) * g + b


def _proj_kernel(*refs, seg, scale, tail_start, side_job):
    qkv = seg in (SEG_Q, SEG_K, SEG_V)
    rope = seg in (SEG_Q, SEG_K)
    it = iter(refs)
    x_ref = next(it)
    g_ref = next(it) if seg == SEG_Q else None
    w_ref = next(it)
    rope_refs = [next(it) for _ in range(5)] if rope else None
    lng_ref, lnb_ref = (next(it), next(it)) if seg == SEG_VC else (None, None)
    side_in = [next(it) for _ in range(5)] if side_job else None
    xn_out_ref = next(it) if seg == SEG_Q else None
    if qkv:
        d1_ref, d4_ref, d16_ref = next(it), next(it), next(it)
        tail_ref = next(it) if tail_start is not None else None
    else:
        o_ref = next(it)
    side_out_ref = next(it) if side_job else None
    wb_ref = next(it)
    if qkv:
        nat_ref, res4_ref = next(it), next(it)
    mult_ref = next(it) if side_job else None

    @pl.when(pl.program_id(0) == 0)
    def _():
        wb_ref[...] = w_ref[...].astype(wb_ref.dtype)
        if side_job:
            mult_ref[...] = _key_multiplicity(mult_ref.shape[0], mult_ref.shape[1], side_in[0].shape[0])

    if side_job:
        _sample_attn_tile(*side_in, side_out_ref, mult_ref)

    tm = x_ref.shape[0]
    sub = min(SUB_ROWS, tm)
    n4, n16 = sub // 4, sub // 16
    for t in range(tm // sub):
        rs = slice(t * sub, (t + 1) * sub)
        if seg == SEG_Q:
            xf = x_ref[rs, :]
            ms = jnp.mean(xf * xf, axis=-1, keepdims=True)
            xb = (xf * lax.rsqrt(ms + EPS) * g_ref[...]).astype(_BF16)
            xn_out_ref[rs, :] = xb
        else:
            xb = x_ref[rs, :]
        z = jnp.dot(xb, wb_ref[...], preferred_element_type=_F32)
        if rope:
            cos_r_ref, sin_r_ref, cos_b_ref, sin_b_ref, sign_ref = rope_refs
            step = pl.ds(pl.program_id(0), 1)
            cb, sb = cos_b_ref[step, :], sin_b_ref[step, :]
            cr, sr = cos_r_ref[rs, :], sin_r_ref[rs, :]
            cos = cr * cb - sr * sb
            sin_signed = (sr * cb + cr * sb) * sign_ref[...]
        if seg in (SEG_GA, SEG_GB):
            o_ref[rs, :] = jax.nn.silu(z).astype(o_ref.dtype)
        elif seg == SEG_U:
            o_ref[rs, :] = jax.nn.gelu(z).astype(o_ref.dtype)
        elif seg == SEG_VC:
            o_ref[rs, :] = _gelu_layer_norm(z, lng_ref[...], lnb_ref[...]).astype(o_ref.dtype)
        else:
            for g in range(z.shape[1] // LANES):
                sl = slice(g * LANES, (g + 1) * LANES)
                r = z[:, sl]
                if rope:
                    r = _rope_group(r, cos, sin_signed)
                if tail_ref is not None:
                    tail_ref[sl, rs] = r.T
                if scale != 1.0:
                    r = r * scale
                d1_ref[rs, sl] = r.astype(d1_ref.dtype)
                nat_ref[g, rs, :] = r
                for r4 in range(4):
                    blk = nat_ref[g, pl.ds(t * sub + r4, n4, stride=4), :]
                    d4_ref[r4, t * n4:(t + 1) * n4, sl] = blk.astype(d4_ref.dtype)
                    res4_ref[g, t * sub + r4 * n4:t * sub + (r4 + 1) * n4, :] = blk
                for r4 in range(4):
                    for a in range(4):
                        blk = res4_ref[g, pl.ds(t * sub + r4 * n4 + a, n16, stride=4), :]
                        d16_ref[r4 + 4 * a, t * n16:(t + 1) * n16, sl] = blk.astype(d16_ref.dtype)


def _proj(x, w_in, seg, *, tm, norm_g=None, tables=None, ln=None, scale=1.0, tail_rows=0, side=None):
    s_len, d_model = x.shape
    d_seg = w_in.shape[1] // N_SEGMENTS
    n_i = s_len // tm
    qkv = seg in (SEG_Q, SEG_K, SEG_V)
    row_vec = lambda n: pl.BlockSpec((1, n), lambda i: (0, 0))
    in_specs = [pl.BlockSpec((tm, d_model), lambda i: (i, 0))]
    args = [x]
    if seg == SEG_Q:
        in_specs.append(row_vec(d_model))
        args.append(norm_g.reshape(1, -1))
    in_specs.append(pl.BlockSpec((d_model, d_seg), lambda i: (0, seg), pipeline_mode=pl.Buffered(1)))
    args.append(w_in)
    if seg in (SEG_Q, SEG_K):
        in_specs += [pl.BlockSpec(t.shape, lambda i: (0, 0)) for t in tables]
        args += list(tables)
    if seg == SEG_VC:
        in_specs += [row_vec(d_seg)] * 2
        args += [ln[0].reshape(1, -1), ln[1].reshape(1, -1)]
    if side is not None:
        sq, skn, svn, skt, svt, first_tile = side
        t_new, d_attn = sq.shape[1:]
        width, wb = SAMPLE_HEAD_GROUP_WIDTH, skt.shape[2]
        hg = d_attn // width
        new = pl.BlockSpec((None, t_new, width), lambda i: ((first_tile + i) // hg, 0, (first_tile + i) % hg))
        cache = pl.BlockSpec((None, width, wb), lambda i: ((first_tile + i) // hg, (first_tile + i) % hg, 0))
        in_specs += [new, new, new, cache, cache]
        args += [sq, skn, svn, skt, svt]
    out_shape, out_specs = [], []
    if seg == SEG_Q:
        out_shape.append(jax.ShapeDtypeStruct((s_len, d_model), _BF16))
        out_specs.append(pl.BlockSpec((tm, d_model), lambda i: (i, 0)))
    scratch = [pltpu.VMEM((d_model, d_seg), _BF16)]
    tail_start = None
    if qkv:
        out_shape += [jax.ShapeDtypeStruct((s_len, d_seg), _BF16),
                      jax.ShapeDtypeStruct((4, s_len // 4, d_seg), _BF16),
                      jax.ShapeDtypeStruct((16, s_len // 16, d_seg), _BF16)]
        out_specs += [pl.BlockSpec((tm, d_seg), lambda i: (i, 0)),
                      pl.BlockSpec((4, tm // 4, d_seg), lambda i: (0, i, 0)),
                      pl.BlockSpec((16, tm // 16, d_seg), lambda i: (0, i, 0))]
        if tail_rows:
            tail_start = n_i - tail_rows // tm
            out_shape.append(jax.ShapeDtypeStruct((d_seg, tail_rows), _F32))
            out_specs.append(pl.BlockSpec((d_seg, tm), lambda i: (0, jnp.maximum(i - tail_start, 0))))
        scratch += [pltpu.VMEM((d_seg // LANES, tm, LANES), _F32)] * 2
    else:
        out_shape.append(jax.ShapeDtypeStruct((s_len, d_seg), _BF16))
        out_specs.append(pl.BlockSpec((tm, d_seg), lambda i: (i, 0)))
    if side is not None:
        assert first_tile % hg == 0 and n_i % hg == 0
        out_shape.append(jax.ShapeDtypeStruct((n_i // hg, t_new, d_attn), _F32))
        out_specs.append(pl.BlockSpec((None, t_new, width), lambda i: (i // hg, 0, i % hg)))
        scratch.append(pltpu.VMEM((t_new * width // HEAD_DIM, wb), _F32))
    return pl.pallas_call(
        functools.partial(_proj_kernel, seg=seg, scale=scale, tail_start=tail_start, side_job=side is not None),
        grid=(n_i,),
        in_specs=in_specs,
        out_specs=out_specs,
        out_shape=out_shape,
        scratch_shapes=scratch,
        compiler_params=_params("arbitrary"),
        name=f"proj_seg{seg}",
    )(*args)


def _attn_kernel(q_ref, kp_ref, kc_ref, vp_ref, vc_ref, o_ref, m_ref, den_ref, k0_ref, *vext_refs):
    n = pl.program_id(1)
    first_step = jnp.logical_and(pl.program_id(0) == 0, n == 0)
    n_pairs = q_ref.shape[1] // LANES
    two = 2 * BLOCK

    @pl.when(first_step)
    def _():
        ones = jnp.ones((two, LANES), k0_ref.dtype)
        for vext_ref in vext_refs:
            for hp in range(n_pairs):
                vext_ref[:, (2 * hp + 1) * LANES:(2 * hp + 2) * LANES] = ones

    k0_ref[0:BLOCK, :] = kp_ref[...]
    k0_ref[BLOCK:, :] = kc_ref[0:BLOCK, :]

    qi = lax.broadcasted_iota(jnp.int32, (two, two), 0) % BLOCK
    si = lax.broadcasted_iota(jnp.int32, (two, two), 1)
    dist = qi + BLOCK - si
    band_bias = jnp.where((dist >= 0) & (dist <= BLOCK), 0.0, NEG_INF).astype(_F32)
    lane = lax.broadcasted_iota(jnp.int32, (BLOCK, LANES), 1)
    head0 = lane < HEAD_DIM
    zero = jnp.zeros((BLOCK, LANES), q_ref.dtype)

    for j, vext_ref in enumerate(vext_refs):
        rows = slice(j * BLOCK, (j + 1) * BLOCK)
        prev_keys = slice((j - 1) * BLOCK, (j + 1) * BLOCK)
        bias = band_bias
        if j == 0:
            bias = band_bias + jnp.where(jnp.logical_and(si < BLOCK, n == 0), NEG_INF, 0.0)
        for hp in range(n_pairs):
            sl = slice(hp * LANES, (hp + 1) * LANES)
            ext = slice(2 * hp * LANES, (2 * hp + 1) * LANES)
            if j == 0:
                vext_ref[0:BLOCK, ext] = vp_ref[:, sl]
                vext_ref[BLOCK:, ext] = vc_ref[0:BLOCK, sl]
            else:
                vext_ref[:, ext] = vc_ref[prev_keys, sl]
        m_acc = jnp.zeros((BLOCK, LANES), _F32)
        den_acc = jnp.ones((BLOCK, LANES), _F32)
        for hp in range(n_pairs):
            sl = slice(hp * LANES, (hp + 1) * LANES)
            q2 = q_ref[rows, sl]
            qs = jnp.concatenate([jnp.where(head0, q2, zero), jnp.where(head0, zero, q2)], axis=0)
            k2 = k0_ref[:, sl] if j == 0 else kc_ref[prev_keys, sl]
            s = lax.dot_general(qs, k2, (((1,), (1,)), ((), ())), preferred_element_type=_F32) + bias
            m = jnp.max(s, axis=-1, keepdims=True)
            p = jnp.exp2(s - m).astype(vext_ref.dtype)
            pv = jnp.dot(p, vext_ref[:, 2 * hp * LANES:(2 * hp + 2) * LANES], preferred_element_type=_F32)
            num, den = pv[:, :LANES], pv[:, LANES:]
            o_ref[rows, sl] = jnp.where(head0, num[:BLOCK], num[BLOCK:]).astype(o_ref.dtype)
            m_acc = jnp.where(lane == 2 * hp, m[:BLOCK], jnp.where(lane == 2 * hp + 1, m[BLOCK:], m_acc))
            den_acc = jnp.where(lane == 2 * hp, den[:BLOCK],
                                jnp.where(lane == 2 * hp + 1, den[BLOCK:], den_acc))
        m_ref[rows, :] = m_acc
        den_ref[rows, :] = den_acc


def _dilated_attention(q, k, v, *, q_rows):
    d, sub_len, d_attn = q.shape
    per = q_rows // BLOCK
    cur = pl.BlockSpec((None, q_rows, d_attn), lambda r, n: (r, n, 0))
    prev = pl.BlockSpec((None, BLOCK, d_attn), lambda r, n: (r, jnp.maximum(n * per - 1, 0), 0))
    stat = pl.BlockSpec((None, q_rows, LANES), lambda r, n: (r, n, 0))
    stat_shape = jax.ShapeDtypeStruct((d, sub_len, LANES), _F32)
    return pl.pallas_call(
        _attn_kernel,
        grid=(d, sub_len // q_rows),
        in_specs=[cur, prev, cur, prev, cur],
        out_specs=[cur, stat, stat],
        out_shape=[jax.ShapeDtypeStruct((d, sub_len, d_attn), q.dtype), stat_shape, stat_shape],
        scratch_shapes=[pltpu.VMEM((2 * BLOCK, d_attn), q.dtype)]
        + [pltpu.VMEM((2 * BLOCK, 2 * d_attn), q.dtype)] * per,
        compiler_params=_params("arbitrary", "arbitrary"),
        name=f"dilated_attn_d{d}",
    )(q, k, k, v, v)


def _expand_heads(w, expand2):
    n_heads = expand2.shape[1] // HEAD_DIM
    hi = w.astype(_BF16).astype(_F32)
    lane = lax.broadcasted_iota(jnp.int32, w.shape, 1)
    lhs = jnp.where(lane < n_heads, hi, pltpu.roll(w - hi, n_heads, 1))
    return jnp.dot(lhs.astype(_BF16), expand2, preferred_element_type=_F32)


def _out_proj_kernel(o1_ref, o4_ref, o16_ref, m1_ref, m4_ref, m16_ref, d1_ref, d4_ref, d16_ref,
                     ga_ref, u_ref, vc_ref, gb_ref, x_ref, wb_ref, ws_ref, bst_ref, fg_ref, y_ref,
                     a_ref, c_ref, n4_ref, n16_ref, tmp_ref, st_ref):
    tm, d_attn = o1_ref.shape
    n_heads = d_attn // HEAD_DIM
    n_lane_groups = d_attn // LANES
    sub = 2 * CHUNK
    n4, n16 = sub // 4, sub // 16

    row_head = lax.broadcasted_iota(jnp.int32, (LANES, d_attn), 0) % n_heads
    row_used = lax.broadcasted_iota(jnp.int32, (LANES, d_attn), 0) < 2 * n_heads
    col_head = lax.broadcasted_iota(jnp.int32, (LANES, d_attn), 1) // HEAD_DIM
    expand2 = jnp.logical_and(row_head == col_head, row_used).astype(_BF16)
    tril = (lax.broadcasted_iota(jnp.int32, (CHUNK, CHUNK), 0)
            >= lax.broadcasted_iota(jnp.int32, (CHUNK, CHUNK), 1))

    for t in range(tm // sub):
        rs = slice(t * sub, (t + 1) * sub)
        for r in range(4):
            dst = pl.ds(t * sub + r, n4, stride=4)
            src = slice(t * n4, (t + 1) * n4)
            st_ref[0, dst, :] = m4_ref[r, src, :]
            st_ref[1, dst, :] = d4_ref[r, src, :]
            for g in range(n_lane_groups):
                n4_ref[g, dst, :] = o4_ref[r, src, g * LANES:(g + 1) * LANES].astype(_F32)
        for r in range(16):
            dst = pl.ds(t * sub + r, n16, stride=16)
            src = slice(t * n16, (t + 1) * n16)
            st_ref[2, dst, :] = m16_ref[r, src, :]
            st_ref[3, dst, :] = d16_ref[r, src, :]
        for g in range(n_lane_groups):
            sl = slice(g * LANES, (g + 1) * LANES)
            for r4 in range(4):
                for a in range(4):
                    blk = o16_ref[r4 + 4 * a, t * n16:(t + 1) * n16, sl].astype(_F32)
                    tmp_ref[g, pl.ds(t * sub + r4 * n4 + a, n16, stride=4), :] = blk
                n16_ref[g, pl.ds(t * sub + r4, n4, stride=4), :] = tmp_ref[g, t * sub + r4 * n4:
                                                                          t * sub + (r4 + 1) * n4, :]

        ms = [m1_ref[rs, :], st_ref[0, rs, :], st_ref[2, rs, :]]
        dens = [d1_ref[rs, :], st_ref[1, rs, :], st_ref[3, rs, :]]
        mx = jnp.maximum(jnp.maximum(ms[0], ms[1]), ms[2])
        es = [jnp.exp2(m - mx) for m in ms]
        tot = es[0] * dens[0] + es[1] * dens[1] + es[2] * dens[2]
        ws = [_expand_heads(e / tot, expand2) for e in es]
        for g in range(n_lane_groups):
            sl = slice(g * LANES, (g + 1) * LANES)
            a = (ws[0][:, sl] * o1_ref[rs, sl].astype(_F32) + ws[1][:, sl] * n4_ref[g, rs, :]
                 + ws[2][:, sl] * n16_ref[g, rs, :])
            a_ref[rs, sl] = (a * ga_ref[rs, sl].astype(_F32)).astype(a_ref.dtype)

        c0 = slice(t * sub, t * sub + CHUNK)
        c1 = slice(t * sub + CHUNK, (t + 1) * sub)
        for g in range(ws_ref.shape[0]):
            wm = jnp.where(tril, ws_ref[g], 0.0).astype(_BF16)
            gs = slice(g * GROUP_WIDTH_B, (g + 1) * GROUP_WIDTH_B)
            vc2 = jnp.concatenate([vc_ref[c0, gs], vc_ref[c1, gs]], axis=1)
            mixed = jnp.dot(wm, vc2, preferred_element_type=_F32) + bst_ref[:, g:g + 1]
            for c, half in ((c0, slice(0, GROUP_WIDTH_B)), (c1, slice(GROUP_WIDTH_B, 2 * GROUP_WIDTH_B))):
                cv = u_ref[c, gs].astype(_F32) * mixed[:, half] * gb_ref[c, gs].astype(_F32)
                c_ref[c, gs] = cv.astype(c_ref.dtype)

        acc = jnp.dot(a_ref[rs, :], wb_ref[:d_attn, :], preferred_element_type=_F32)
        acc = acc + jnp.dot(c_ref[rs, :], wb_ref[d_attn:, :], preferred_element_type=_F32)
        xo = x_ref[rs, :] + acc
        msq = jnp.mean(xo * xo, axis=-1, keepdims=True)
        y_ref[rs, :] = xo * lax.rsqrt(msq + EPS) * fg_ref[...]


def _out_proj(os, ms, dens, ga, u, vc, gb, x, w_out, w_s, b_s_t, final_g, *, tm):
    m, d_model = x.shape
    d_attn = ga.shape[1]
    d_chunk = u.shape[1]
    rows = lambda n: pl.BlockSpec((tm, n), lambda i: (i, 0))
    res = lambda d, n: pl.BlockSpec((d, tm // d, n), lambda i: (0, i, 0))
    whole = lambda a: pl.BlockSpec(a.shape, lambda i: (0,) * a.ndim)
    by_pattern = lambda n: [rows(n), res(4, n), res(16, n)]
    fg = final_g.reshape(1, -1)
    slab = pltpu.VMEM((d_attn // LANES, tm, LANES), _F32)
    return pl.pallas_call(
        _out_proj_kernel,
        grid=(m // tm,),
        in_specs=by_pattern(d_attn) + by_pattern(LANES) + by_pattern(LANES)
        + [rows(d_attn), rows(d_chunk), rows(d_chunk), rows(d_chunk), rows(d_model),
           pl.BlockSpec(w_out.shape, lambda i: (0, 0), pipeline_mode=pl.Buffered(1)),
           whole(w_s), whole(b_s_t), whole(fg)],
        out_specs=rows(d_model),
        out_shape=jax.ShapeDtypeStruct((m, d_model), _F32),
        scratch_shapes=[pltpu.VMEM((tm, d_attn), _BF16), pltpu.VMEM((tm, d_chunk), _BF16), slab, slab, slab,
                        pltpu.VMEM((4, tm, LANES), _F32)],
        compiler_params=_params("arbitrary"),
        name="out_proj",
    )(*os, *ms, *dens, ga, u, vc, gb, x, w_out, w_s, b_s_t, fg)


def _sample_in_proj_kernel(x_ref, g_ref, w_ref, cos_ref, sin_ref, lng_ref, lnb_ref,
                           q_ref, k_ref, v_ref, ga_ref, u_ref, vc_ref, gb_ref, xn_ref):
    j = pl.program_id(0)

    @pl.when(j == 0)
    def _():
        xf = x_ref[...]
        ms = jnp.mean(xf * xf, axis=-1, keepdims=True)
        xn_ref[...] = (xf * lax.rsqrt(ms + EPS) * g_ref[...]).astype(xn_ref.dtype)

    z = jnp.dot(xn_ref[...], w_ref[...].astype(_BF16), preferred_element_type=_F32)

    def rope_to(ref, scale):
        for g in range(z.shape[1] // LANES):
            sl = slice(g * LANES, (g + 1) * LANES)
            ref[:, sl] = _rope_group(z[:, sl], cos_ref[...], sin_ref[...]) * scale

    @pl.when(j == SEG_Q)
    def _():
        rope_to(q_ref, HEAD_DIM ** -0.5)

    @pl.when(j == SEG_K)
    def _():
        rope_to(k_ref, 1.0)

    @pl.when(j == SEG_V)
    def _():
        v_ref[...] = z

    @pl.when(j == SEG_GA)
    def _():
        ga_ref[...] = jax.nn.silu(z)

    @pl.when(j == SEG_U)
    def _():
        u_ref[...] = jax.nn.gelu(z)

    @pl.when(j == SEG_VC)
    def _():
        vc_ref[...] = _gelu_layer_norm(z, lng_ref[...], lnb_ref[...])

    @pl.when(j == SEG_GB)
    def _():
        gb_ref[...] = jax.nn.silu(z)


def _sample_in_proj(x, norm_g, w_in, cos_t, sin_t, ln_g, ln_b):
    m, d_model = x.shape
    d_seg = w_in.shape[1] // N_SEGMENTS
    whole = lambda r, c: pl.BlockSpec((r, c), lambda j: (0, 0))
    return pl.pallas_call(
        _sample_in_proj_kernel,
        grid=(N_SEGMENTS,),
        in_specs=[whole(m, d_model), whole(1, d_model),
                  pl.BlockSpec((d_model, d_seg), lambda j: (0, j)),
                  whole(m, LANES), whole(m, LANES), whole(1, d_seg), whole(1, d_seg)],
        out_specs=[whole(m, d_seg)] * N_SEGMENTS,
        out_shape=[jax.ShapeDtypeStruct((m, d_seg), _F32)] * N_SEGMENTS,
        scratch_shapes=[pltpu.VMEM((m, d_model), _BF16)],
        compiler_params=_params("arbitrary"),
        name="sample_in_proj",
    )(x, norm_g.reshape(1, -1), w_in, cos_t, sin_t, ln_g.reshape(1, -1), ln_b.reshape(1, -1))


def _key_multiplicity(rows, wb, t_new):
    t_row = lax.broadcasted_iota(jnp.int32, (rows, wb), 0) // (rows // t_new)
    pos = lax.broadcasted_iota(jnp.int32, (rows, wb), 1)
    dist = wb + t_row - pos
    mult = jnp.zeros((rows, wb), _F32)
    for window, d in DILATIONS:
        mult = mult + jnp.where((dist % d == 0) & (dist <= window), 1.0, 0.0)
    return mult


def _sample_attn_tile(q_ref, kn_ref, vn_ref, kt_ref, vt_ref, o_ref, mult_ref):
    t_new, width = q_ref.shape
    n_heads = width // HEAD_DIM
    rows = t_new * n_heads
    head_row = lax.broadcasted_iota(jnp.int32, (n_heads, width), 0)
    head_col = lax.broadcasted_iota(jnp.int32, (n_heads, width), 1) // HEAD_DIM
    own = head_row == head_col
    qbd = jnp.concatenate([jnp.where(own, q_ref[t:t + 1, :], 0.0) for t in range(t_new)], axis=0)
    mult = mult_ref[...]

    s = jnp.dot(qbd.astype(_BF16), kt_ref[...].astype(_BF16), preferred_element_type=_F32)
    s = jnp.where(mult > 0.0, s, NEG_INF)
    m = jnp.max(s, axis=-1, keepdims=True)
    t_col = lax.broadcasted_iota(jnp.int32, (rows, 1), 0) // n_heads
    s_new, mult_new = [], []
    for tp in range(t_new):
        s_new.append(jnp.sum(qbd * kn_ref[tp:tp + 1, :], axis=-1, keepdims=True))
        dn = t_col - tp
        mn = jnp.zeros((rows, 1), _F32)
        for window, d in DILATIONS:
            mn = mn + jnp.where((dn >= 0) & (dn % d == 0), 1.0, 0.0)
        mult_new.append(mn)
        m = jnp.maximum(m, jnp.where(mn > 0.0, s_new[tp], NEG_INF))
    p = jnp.exp(s - m) * mult
    den = jnp.sum(p, axis=-1, keepdims=True)
    num = lax.dot_general(p.astype(_BF16), vt_ref[...].astype(_BF16), (((1,), (1,)), ((), ())),
                          preferred_element_type=_F32)
    for tp in range(t_new):
        pn = jnp.where(mult_new[tp] > 0.0, jnp.exp(s_new[tp] - m), 0.0) * mult_new[tp]
        den = den + pn
        num = num + pn * vn_ref[tp:tp + 1, :]
    res = num / den
    for t in range(t_new):
        blk = res[t * n_heads:(t + 1) * n_heads, :]
        o_ref[t:t + 1, :] = jnp.sum(jnp.where(own, blk, 0.0), axis=0, keepdims=True)


def _sample_out_kernel(oa_ref, ga_ref, u_ref, vc_ref, gb_ref, x_ref, wout_ref, coef_ref, bias_ref, fg_ref,
                       y_ref, wb_ref, c_ref, *, t_new):
    m, d_attn = oa_ref.shape
    db = m // t_new
    wb_ref[...] = wout_ref[...].astype(wb_ref.dtype)
    a = (oa_ref[...] * ga_ref[...]).astype(_BF16)
    for t in range(t_new):
        mixed = jnp.zeros((db, c_ref.shape[1]), _F32) + bias_ref[t:t + 1, :]
        for s in range(t + 1):
            mixed = mixed + coef_ref[t * t_new + s:t * t_new + s + 1, :] * vc_ref[s * db:(s + 1) * db, :]
        rs = slice(t * db, (t + 1) * db)
        c_ref[rs, :] = (u_ref[rs, :] * mixed * gb_ref[rs, :]).astype(c_ref.dtype)
    acc = jnp.dot(a, wb_ref[:d_attn, :], preferred_element_type=_F32)
    acc = acc + jnp.dot(c_ref[...], wb_ref[d_attn:, :], preferred_element_type=_F32)
    xo = x_ref[...] + acc
    ms = jnp.mean(xo * xo, axis=-1, keepdims=True)
    y_ref[...] = xo * lax.rsqrt(ms + EPS) * fg_ref[...]


def _sample_out(oa, ga, u, vc, gb, x, w_out, coef, bias, final_g, *, t_new):
    m, d_model = x.shape
    args = (oa, ga, u, vc, gb, x, w_out, coef, bias, final_g.reshape(1, -1))
    whole = lambda a: pl.BlockSpec(a.shape, lambda i: (0,) * a.ndim)
    return pl.pallas_call(
        functools.partial(_sample_out_kernel, t_new=t_new),
        grid=(1,),
        in_specs=[whole(a) for a in args],
        out_specs=[pl.BlockSpec((m, d_model), lambda i: (0, 0)), whole(w_out)],
        out_shape=[jax.ShapeDtypeStruct((m, d_model), _F32), jax.ShapeDtypeStruct(w_out.shape, _BF16)],
        scratch_shapes=[pltpu.VMEM((m, u.shape[1]), _BF16)],
        compiler_params=_params("arbitrary"),
        name="sample_out",
    )(*args)


def kernel(x_prompt, x_sample, cache_k, cache_v, norm_g, w_in, ln_g, ln_b, w_s, b_s, w_out, final_g):
    batch, s_len, d_model = x_prompt.shape
    db, t_new, _ = x_sample.shape
    depth, _, wb, n_heads, head_dim = cache_k.shape
    d_attn = n_heads * head_dim
    d_chunk = w_out.shape[1] - d_attn
    assert batch == 1 and depth == 1 and head_dim == HEAD_DIM
    assert w_in.shape[2] == 4 * d_attn + 3 * d_chunk and d_attn == d_chunk
    tail = min(MAX_WINDOW, s_len)
    w_in0, w_out0 = w_in[0], w_out[0]
    ng, lg, lb = norm_g[0], ln_g[0], ln_b[0]

    xs = jnp.transpose(x_sample, (1, 0, 2)).reshape(t_new * db, d_model)
    pos_s = PAST_LEN + jnp.repeat(jnp.arange(t_new, dtype=jnp.int32), db)
    cos_s, sin_s = _rope_tables(pos_s)
    qs, ks, vs, gas, us, vcs, gbs = _sample_in_proj(xs, ng, w_in0, cos_s, sin_s, lg, lb)
    to_b_major = lambda a: jnp.transpose(a.reshape(t_new, db, -1), (1, 0, 2))
    qs_b, ks_b, vs_b, vcs_b = (to_b_major(a) for a in (qs, ks, vs, vcs))
    cache_t = lambda c: jnp.transpose(c[0], (0, 2, 3, 1)).reshape(db, d_attn, wb)
    cache_kt, cache_vt = cache_t(cache_k), cache_t(cache_v)
    assert wb == MAX_WINDOW

    xp = x_prompt.reshape(s_len, d_model)
    tables = _rope_tables_blocked(s_len, 512)
    xn, q1, q4, q16 = _proj(xp, w_in0, SEG_Q, tm=512, norm_g=ng, tables=tables,
                            scale=HEAD_DIM ** -0.5 * math.log2(math.e))
    k1, k4, k16, k_tail_t = _proj(xn, w_in0, SEG_K, tm=512, tables=tables, tail_rows=tail)
    v1, v4, v16, v_tail_t = _proj(xn, w_in0, SEG_V, tm=512, tail_rows=tail)
    tm_gate = 512
    tiles_per_call = s_len // tm_gate
    assert 4 * tiles_per_call == db * (d_attn // SAMPLE_HEAD_GROUP_WIDTH)
    side = lambda c: (qs_b, ks_b, vs_b, cache_kt, cache_vt, c * tiles_per_call)
    ga, oa0 = _proj(xn, w_in0, SEG_GA, tm=tm_gate, side=side(0))
    u, oa1 = _proj(xn, w_in0, SEG_U, tm=tm_gate, side=side(1))
    vc, oa2 = _proj(xn, w_in0, SEG_VC, tm=tm_gate, ln=(lg, lb), side=side(2))
    gb, oa3 = _proj(xn, w_in0, SEG_GB, tm=tm_gate, side=side(3))
    oa = jnp.concatenate([oa0, oa1, oa2, oa3], axis=0)
    o1, m1, d1 = _dilated_attention(q1[None], k1[None], v1[None], q_rows=1024)
    o4, m4, d4 = _dilated_attention(q4, k4, v4, q_rows=1024)
    o16, m16, d16 = _dilated_attention(q16, k16, v16, q_rows=512)

    oa_t = jnp.transpose(oa, (1, 0, 2)).reshape(t_new * db, d_attn)
    coef = jnp.repeat(jnp.transpose(w_s[0][:, :t_new, :t_new], (1, 2, 0)).reshape(t_new * t_new, -1),
                      GROUP_WIDTH_B, axis=1)
    bias = jnp.repeat(jnp.transpose(b_s[0][:, :t_new], (1, 0)), GROUP_WIDTH_B, axis=1)
    ys, w_out_b = _sample_out(oa_t, gas, us, vcs, gbs, xs, w_out0, coef, bias, final_g, t_new=t_new)
    y_sample = jnp.transpose(ys.reshape(t_new, db, d_model), (1, 0, 2))

    y_prompt = _out_proj((o1[0], o4, o16), (m1[0], m4, m16), (d1[0], d4, d16), ga, u, vc, gb, xp, w_out_b,
                         w_s[0], b_s[0].T, final_g, tm=512)
    untranspose = lambda a: jnp.transpose(a.reshape(n_heads, head_dim, tail), (2, 0, 1))

    hs = (n_heads, head_dim)
    return (
        y_prompt.reshape(batch, s_len, d_model),
        y_sample,
        untranspose(k_tail_t).reshape(depth, batch, tail, *hs),
        untranspose(v_tail_t).reshape(depth, batch, tail, *hs),
        ks_b.reshape(depth, db, t_new, *hs),
        vs_b.reshape(depth, db, t_new, *hs),
        vcs_b.reshape(depth, db, t_new, d_chunk),
    )
```

```python
import functools
import math

import jax
import jax.numpy as jnp
from jax import lax
from jax.experimental import pallas as pl
from jax.experimental.pallas import tpu as pltpu

HEAD_DIM = 64
BLOCK = 128
CHUNK = 128
GROUP_WIDTH_B = 128
DILATIONS = ((128, 1), (512, 4), (2048, 16))
MAX_WINDOW = 2048
PAST_LEN = 16384
ROPE_THETA = 10000.0
EPS = 1e-6
NEG_INF = -1e30
N_SEGMENTS = 7
SEG_Q, SEG_K, SEG_V, SEG_GA, SEG_U, SEG_VC, SEG_GB = range(N_SEGMENTS)

LANES = 128
VMEM_LIMIT_BYTES = 56 * 1024 * 1024
SUB_ROWS = 256
SAMPLE_HEAD_GROUP_WIDTH = 512

_BF16 = jnp.bfloat16
_F32 = jnp.float32


def _params(*semantics):
    return pltpu.CompilerParams(dimension_semantics=semantics, vmem_limit_bytes=VMEM_LIMIT_BYTES)


def _rope_tables(pos):
    half = HEAD_DIM // 2
    inv = jnp.exp(-math.log(ROPE_THETA) * jnp.arange(half, dtype=_F32) / half)
    ang = pos.astype(_F32)[:, None] * inv[None, :]
    cos = jnp.cos(ang)
    sin = jnp.sin(ang)
    cos_t = jnp.concatenate([cos, cos, cos, cos], axis=-1)
    sin_t = jnp.concatenate([-sin, sin, -sin, sin], axis=-1)
    return cos_t, sin_t


def _rope_tables_blocked(s_len, tm):
    half = HEAD_DIM // 2
    inv = jnp.exp(-math.log(ROPE_THETA) * jnp.arange(half, dtype=_F32) / half)
    tile4 = lambda a: jnp.concatenate([a, a, a, a], axis=-1)
    ang_r = jnp.arange(tm, dtype=jnp.int32).astype(_F32)[:, None] * inv[None, :]
    ang_b = (jnp.arange(s_len // tm, dtype=jnp.int32) * tm).astype(_F32)[:, None] * inv[None, :]
    sign = jnp.concatenate([-jnp.ones((1, half), _F32), jnp.ones((1, half), _F32)] * 2, axis=-1)
    return (tile4(jnp.cos(ang_r)), tile4(jnp.sin(ang_r)), tile4(jnp.cos(ang_b)), tile4(jnp.sin(ang_b)), sign)


def _rope_group(zg, cos, sin_signed):
    lane = lax.broadcasted_iota(jnp.int32, zg.shape, 1)
    first_half = (lane % HEAD_DIM) < (HEAD_DIM // 2)
    partner = jnp.where(first_half, pltpu.roll(zg, LANES - HEAD_DIM // 2, 1),
                        pltpu.roll(zg, HEAD_DIM // 2, 1))
    return zg * cos + partner * sin_signed


def _gelu_layer_norm(z, g, b):
    h = jax.nn.gelu(z)
    mu = jnp.mean(h, axis=-1, keepdims=True)
    hc = h - mu
    var = jnp.mean(hc * hc, axis=-1, keepdims=True)
    return hc * lax.rsqrt(var + EPS) * g + b


def _proj_kernel(*refs, seg, scale, tail_start, side_job):
    qkv = seg in (SEG_Q, SEG_K, SEG_V)
    rope = seg in (SEG_Q, SEG_K)
    it = iter(refs)
    x_ref = next(it)
    g_ref = next(it) if seg == SEG_Q else None
    w_ref = next(it)
    rope_refs = [next(it) for _ in range(5)] if rope else None
    lng_ref, lnb_ref = (next(it), next(it)) if seg == SEG_VC else (None, None)
    side_in = [next(it) for _ in range(5)] if side_job else None
    xn_out_ref = next(it) if seg == SEG_Q else None
    if qkv:
        d1_ref, d4_ref, d16_ref = next(it), next(it), next(it)
        tail_ref = next(it) if tail_start is not None else None
    else:
        o_ref = next(it)
    side_out_ref = next(it) if side_job else None
    if qkv:
        nat_ref, res4_ref = next(it), next(it)
    mult_ref = next(it) if side_job else None

    if side_job:
        @pl.when(pl.program_id(0) == 0)
        def _():
            mult_ref[...] = _key_multiplicity(mult_ref.shape[0], mult_ref.shape[1], side_in[0].shape[0])

        _sample_attn_tile(*side_in, side_out_ref, mult_ref)

    tm = x_ref.shape[0]
    sub = min(SUB_ROWS, tm)
    n4, n16 = sub // 4, sub // 16
    for t in range(tm // sub):
        rs = slice(t * sub, (t + 1) * sub)
        if seg == SEG_Q:
            xf = x_ref[rs, :]
            ms = jnp.mean(xf * xf, axis=-1, keepdims=True)
            xb = (xf * lax.rsqrt(ms + EPS) * g_ref[...]).astype(_BF16)
            xn_out_ref[rs, :] = xb
        else:
            xb = x_ref[rs, :]
        z = jnp.dot(xb, w_ref[...], preferred_element_type=_F32)
        if rope:
            cos_r_ref, sin_r_ref, cos_b_ref, sin_b_ref, sign_ref = rope_refs
            step = pl.ds(pl.program_id(0), 1)
            cb, sb = cos_b_ref[step, :], sin_b_ref[step, :]
            cr, sr = cos_r_ref[rs, :], sin_r_ref[rs, :]
            cos = cr * cb - sr * sb
            sin_signed = (sr * cb + cr * sb) * sign_ref[...]
        if seg in (SEG_GA, SEG_GB):
            o_ref[rs, :] = jax.nn.silu(z).astype(o_ref.dtype)
        elif seg == SEG_U:
            o_ref[rs, :] = jax.nn.gelu(z).astype(o_ref.dtype)
        elif seg == SEG_VC:
            o_ref[rs, :] = _gelu_layer_norm(z, lng_ref[...], lnb_ref[...]).astype(o_ref.dtype)
        else:
            for g in range(z.shape[1] // LANES):
                sl = slice(g * LANES, (g + 1) * LANES)
                r = z[:, sl]
                if rope:
                    r = _rope_group(r, cos, sin_signed)
                if tail_ref is not None:
                    tail_ref[sl, rs] = r.T
                if scale != 1.0:
                    r = r * scale
                d1_ref[rs, sl] = r.astype(d1_ref.dtype)
                nat_ref[g, rs, :] = r
                for r4 in range(4):
                    blk = nat_ref[g, pl.ds(t * sub + r4, n4, stride=4), :]
                    d4_ref[r4, t * n4:(t + 1) * n4, sl] = blk.astype(d4_ref.dtype)
                    res4_ref[g, t * sub + r4 * n4:t * sub + (r4 + 1) * n4, :] = blk
                for r4 in range(4):
                    for a in range(4):
                        blk = res4_ref[g, pl.ds(t * sub + r4 * n4 + a, n16, stride=4), :]
                        d16_ref[r4 + 4 * a, t * n16:(t + 1) * n16, sl] = blk.astype(d16_ref.dtype)


def _proj(x, w_in, seg, *, tm, norm_g=None, tables=None, ln=None, scale=1.0, tail_rows=0, side=None):
    s_len, d_model = x.shape
    d_seg = w_in.shape[1] // N_SEGMENTS
    n_i = s_len // tm
    qkv = seg in (SEG_Q, SEG_K, SEG_V)
    row_vec = lambda n: pl.BlockSpec((1, n), lambda i: (0, 0))
    in_specs = [pl.BlockSpec((tm, d_model), lambda i: (i, 0))]
    args = [x]
    if seg == SEG_Q:
        in_specs.append(row_vec(d_model))
        args.append(norm_g.reshape(1, -1))
    in_specs.append(pl.BlockSpec((d_model, d_seg), lambda i: (0, seg), pipeline_mode=pl.Buffered(1)))
    args.append(w_in)
    if seg in (SEG_Q, SEG_K):
        in_specs += [pl.BlockSpec(t.shape, lambda i: (0, 0)) for t in tables]
        args += list(tables)
    if seg == SEG_VC:
        in_specs += [row_vec(d_seg)] * 2
        args += [ln[0].reshape(1, -1), ln[1].reshape(1, -1)]
    if side is not None:
        sq, skn, svn, skt, svt, first_tile = side
        t_new, d_attn = sq.shape[1:]
        width, wb = SAMPLE_HEAD_GROUP_WIDTH, skt.shape[2]
        hg = d_attn // width
        new = pl.BlockSpec((None, t_new, width), lambda i: ((first_tile + i) // hg, 0, (first_tile + i) % hg))
        cache = pl.BlockSpec((None, width, wb), lambda i: ((first_tile + i) // hg, (first_tile + i) % hg, 0))
        in_specs += [new, new, new, cache, cache]
        args += [sq, skn, svn, skt, svt]
    out_shape, out_specs = [], []
    if seg == SEG_Q:
        out_shape.append(jax.ShapeDtypeStruct((s_len, d_model), _BF16))
        out_specs.append(pl.BlockSpec((tm, d_model), lambda i: (i, 0)))
    scratch = []
    tail_start = None
    if qkv:
        out_shape += [jax.ShapeDtypeStruct((s_len, d_seg), _BF16),
                      jax.ShapeDtypeStruct((4, s_len // 4, d_seg), _BF16),
                      jax.ShapeDtypeStruct((16, s_len // 16, d_seg), _BF16)]
        out_specs += [pl.BlockSpec((tm, d_seg), lambda i: (i, 0)),
                      pl.BlockSpec((4, tm // 4, d_seg), lambda i: (0, i, 0)),
                      pl.BlockSpec((16, tm // 16, d_seg), lambda i: (0, i, 0))]
        if tail_rows:
            tail_start = n_i - tail_rows // tm
            out_shape.append(jax.ShapeDtypeStruct((d_seg, tail_rows), _F32))
            out_specs.append(pl.BlockSpec((d_seg, tm), lambda i: (0, jnp.maximum(i - tail_start, 0))))
        scratch += [pltpu.VMEM((d_seg // LANES, tm, LANES), _F32)] * 2
    else:
        out_shape.append(jax.ShapeDtypeStruct((s_len, d_seg), _BF16))
        out_specs.append(pl.BlockSpec((tm, d_seg), lambda i: (i, 0)))
    if side is not None:
        assert first_tile % hg == 0 and n_i % hg == 0
        out_shape.append(jax.ShapeDtypeStruct((n_i // hg, t_new, d_attn), _F32))
        out_specs.append(pl.BlockSpec((None, t_new, width), lambda i: (i // hg, 0, i % hg)))
        scratch.append(pltpu.VMEM((t_new * width // HEAD_DIM, wb), _F32))
    return pl.pallas_call(
        functools.partial(_proj_kernel, seg=seg, scale=scale, tail_start=tail_start, side_job=side is not None),
        grid=(n_i,),
        in_specs=in_specs,
        out_specs=out_specs,
        out_shape=out_shape,
        scratch_shapes=scratch,
        compiler_params=_params("arbitrary"),
        name=f"proj_seg{seg}",
    )(*args)


def _attn_kernel(q_ref, kp_ref, kc_ref, vp_ref, vc_ref, o_ref, m_ref, den_ref, k0_ref, *vext_refs):
    n = pl.program_id(1)
    first_step = jnp.logical_and(pl.program_id(0) == 0, n == 0)
    n_pairs = q_ref.shape[1] // LANES
    two = 2 * BLOCK

    @pl.when(first_step)
    def _():
        ones = jnp.ones((two, LANES), k0_ref.dtype)
        for vext_ref in vext_refs:
            for hp in range(n_pairs):
                vext_ref[:, (2 * hp + 1) * LANES:(2 * hp + 2) * LANES] = ones

    k0_ref[0:BLOCK, :] = kp_ref[...]
    k0_ref[BLOCK:, :] = kc_ref[0:BLOCK, :]

    qi = lax.broadcasted_iota(jnp.int32, (two, two), 0) % BLOCK
    si = lax.broadcasted_iota(jnp.int32, (two, two), 1)
    dist = qi + BLOCK - si
    band_bias = jnp.where((dist >= 0) & (dist <= BLOCK), 0.0, NEG_INF).astype(_F32)
    lane = lax.broadcasted_iota(jnp.int32, (BLOCK, LANES), 1)
    head0 = lane < HEAD_DIM
    zero = jnp.zeros((BLOCK, LANES), q_ref.dtype)

    for j, vext_ref in enumerate(vext_refs):
        rows = slice(j * BLOCK, (j + 1) * BLOCK)
        prev_keys = slice((j - 1) * BLOCK, (j + 1) * BLOCK)
        bias = band_bias
        if j == 0:
            bias = band_bias + jnp.where(jnp.logical_and(si < BLOCK, n == 0), NEG_INF, 0.0)
        for hp in range(n_pairs):
            sl = slice(hp * LANES, (hp + 1) * LANES)
            ext = slice(2 * hp * LANES, (2 * hp + 1) * LANES)
            if j == 0:
                vext_ref[0:BLOCK, ext] = vp_ref[:, sl]
                vext_ref[BLOCK:, ext] = vc_ref[0:BLOCK, sl]
            else:
                vext_ref[:, ext] = vc_ref[prev_keys, sl]
        m_acc = jnp.zeros((BLOCK, LANES), _F32)
        den_acc = jnp.ones((BLOCK, LANES), _F32)
        for hp in range(n_pairs):
            sl = slice(hp * LANES, (hp + 1) * LANES)
            q2 = q_ref[rows, sl]
            qs = jnp.concatenate([jnp.where(head0, q2, zero), jnp.where(head0, zero, q2)], axis=0)
            k2 = k0_ref[:, sl] if j == 0 else kc_ref[prev_keys, sl]
            s = lax.dot_general(qs, k2, (((1,), (1,)), ((), ())), preferred_element_type=_F32) + bias
            m = jnp.max(s, axis=-1, keepdims=True)
            p = jnp.exp2(s - m).astype(vext_ref.dtype)
            pv = jnp.dot(p, vext_ref[:, 2 * hp * LANES:(2 * hp + 2) * LANES], preferred_element_type=_F32)
            num, den = pv[:, :LANES], pv[:, LANES:]
            o_ref[rows, sl] = jnp.where(head0, num[:BLOCK], num[BLOCK:]).astype(o_ref.dtype)
            m_acc = jnp.where(lane == 2 * hp, m[:BLOCK], jnp.where(lane == 2 * hp + 1, m[BLOCK:], m_acc))
            den_acc = jnp.where(lane == 2 * hp, den[:BLOCK],
                                jnp.where(lane == 2 * hp + 1, den[BLOCK:], den_acc))
        m_ref[rows, :] = m_acc
        den_ref[rows, :] = den_acc


def _dilated_attention(q, k, v, *, q_rows):
    d, sub_len, d_attn = q.shape
    per = q_rows // BLOCK
    cur = pl.BlockSpec((None, q_rows, d_attn), lambda r, n: (r, n, 0))
    prev = pl.BlockSpec((None, BLOCK, d_attn), lambda r, n: (r, jnp.maximum(n * per - 1, 0), 0))
    stat = pl.BlockSpec((None, q_rows, LANES), lambda r, n: (r, n, 0))
    stat_shape = jax.ShapeDtypeStruct((d, sub_len, LANES), _F32)
    return pl.pallas_call(
        _attn_kernel,
        grid=(d, sub_len // q_rows),
        in_specs=[cur, prev, cur, prev, cur],
        out_specs=[cur, stat, stat],
        out_shape=[jax.ShapeDtypeStruct((d, sub_len, d_attn), q.dtype), stat_shape, stat_shape],
        scratch_shapes=[pltpu.VMEM((2 * BLOCK, d_attn), q.dtype)]
        + [pltpu.VMEM((2 * BLOCK, 2 * d_attn), q.dtype)] * per,
        compiler_params=_params("arbitrary", "arbitrary"),
        name=f"dilated_attn_d{d}",
    )(q, k, k, v, v)


def _expand_heads(w, expand2):
    n_heads = expand2.shape[1] // HEAD_DIM
    hi = w.astype(_BF16).astype(_F32)
    lane = lax.broadcasted_iota(jnp.int32, w.shape, 1)
    lhs = jnp.where(lane < n_heads, hi, pltpu.roll(w - hi, n_heads, 1))
    return jnp.dot(lhs.astype(_BF16), expand2, preferred_element_type=_F32)


def _out_proj_kernel(o1_ref, o4_ref, o16_ref, m1_ref, m4_ref, m16_ref, d1_ref, d4_ref, d16_ref,
                     ga_ref, u_ref, vc_ref, gb_ref, x_ref, wb_ref, ws_ref, bst_ref, fg_ref, y_ref,
                     a_ref, c_ref, n4_ref, n16_ref, tmp_ref, st_ref):
    tm, d_attn = o1_ref.shape
    n_heads = d_attn // HEAD_DIM
    n_lane_groups = d_attn // LANES
    sub = 2 * CHUNK
    n4, n16 = sub // 4, sub // 16

    row_head = lax.broadcasted_iota(jnp.int32, (LANES, d_attn), 0) % n_heads
    row_used = lax.broadcasted_iota(jnp.int32, (LANES, d_attn), 0) < 2 * n_heads
    col_head = lax.broadcasted_iota(jnp.int32, (LANES, d_attn), 1) // HEAD_DIM
    expand2 = jnp.logical_and(row_head == col_head, row_used).astype(_BF16)
    tril = (lax.broadcasted_iota(jnp.int32, (CHUNK, CHUNK), 0)
            >= lax.broadcasted_iota(jnp.int32, (CHUNK, CHUNK), 1))

    for t in range(tm // sub):
        rs = slice(t * sub, (t + 1) * sub)
        for r in range(4):
            dst = pl.ds(t * sub + r, n4, stride=4)
            src = slice(t * n4, (t + 1) * n4)
            st_ref[0, dst, :] = m4_ref[r, src, :]
            st_ref[1, dst, :] = d4_ref[r, src, :]
            for g in range(n_lane_groups):
                n4_ref[g, dst, :] = o4_ref[r, src, g * LANES:(g + 1) * LANES].astype(_F32)
        for r in range(16):
            dst = pl.ds(t * sub + r, n16, stride=16)
            src = slice(t * n16, (t + 1) * n16)
            st_ref[2, dst, :] = m16_ref[r, src, :]
            st_ref[3, dst, :] = d16_ref[r, src, :]
        for g in range(n_lane_groups):
            sl = slice(g * LANES, (g + 1) * LANES)
            for r4 in range(4):
                for a in range(4):
                    blk = o16_ref[r4 + 4 * a, t * n16:(t + 1) * n16, sl].astype(_F32)
                    tmp_ref[g, pl.ds(t * sub + r4 * n4 + a, n16, stride=4), :] = blk
                n16_ref[g, pl.ds(t * sub + r4, n4, stride=4), :] = tmp_ref[g, t * sub + r4 * n4:
                                                                          t * sub + (r4 + 1) * n4, :]

        ms = [m1_ref[rs, :], st_ref[0, rs, :], st_ref[2, rs, :]]
        dens = [d1_ref[rs, :], st_ref[1, rs, :], st_ref[3, rs, :]]
        mx = jnp.maximum(jnp.maximum(ms[0], ms[1]), ms[2])
        es = [jnp.exp2(m - mx) for m in ms]
        tot = es[0] * dens[0] + es[1] * dens[1] + es[2] * dens[2]
        ws = [_expand_heads(e / tot, expand2) for e in es]
        for g in range(n_lane_groups):
            sl = slice(g * LANES, (g + 1) * LANES)
            a = (ws[0][:, sl] * o1_ref[rs, sl].astype(_F32) + ws[1][:, sl] * n4_ref[g, rs, :]
                 + ws[2][:, sl] * n16_ref[g, rs, :])
            a_ref[rs, sl] = (a * ga_ref[rs, sl].astype(_F32)).astype(a_ref.dtype)

        c0 = slice(t * sub, t * sub + CHUNK)
        c1 = slice(t * sub + CHUNK, (t + 1) * sub)
        for g in range(ws_ref.shape[0]):
            wm = jnp.where(tril, ws_ref[g], 0.0).astype(_BF16)
            gs = slice(g * GROUP_WIDTH_B, (g + 1) * GROUP_WIDTH_B)
            vc2 = jnp.concatenate([vc_ref[c0, gs], vc_ref[c1, gs]], axis=1)
            mixed = jnp.dot(wm, vc2, preferred_element_type=_F32) + bst_ref[:, g:g + 1]
            for c, half in ((c0, slice(0, GROUP_WIDTH_B)), (c1, slice(GROUP_WIDTH_B, 2 * GROUP_WIDTH_B))):
                cv = u_ref[c, gs].astype(_F32) * mixed[:, half] * gb_ref[c, gs].astype(_F32)
                c_ref[c, gs] = cv.astype(c_ref.dtype)

        acc = jnp.dot(a_ref[rs, :], wb_ref[:d_attn, :], preferred_element_type=_F32)
        acc = acc + jnp.dot(c_ref[rs, :], wb_ref[d_attn:, :], preferred_element_type=_F32)
        xo = x_ref[rs, :] + acc
        msq = jnp.mean(xo * xo, axis=-1, keepdims=True)
        y_ref[rs, :] = xo * lax.rsqrt(msq + EPS) * fg_ref[...]


def _out_proj(os, ms, dens, ga, u, vc, gb, x, w_out, w_s, b_s_t, final_g, *, tm):
    m, d_model = x.shape
    d_attn = ga.shape[1]
    d_chunk = u.shape[1]
    rows = lambda n: pl.BlockSpec((tm, n), lambda i: (i, 0))
    res = lambda d, n: pl.BlockSpec((d, tm // d, n), lambda i: (0, i, 0))
    whole = lambda a: pl.BlockSpec(a.shape, lambda i: (0,) * a.ndim)
    by_pattern = lambda n: [rows(n), res(4, n), res(16, n)]
    fg = final_g.reshape(1, -1)
    slab = pltpu.VMEM((d_attn // LANES, tm, LANES), _F32)
    return pl.pallas_call(
        _out_proj_kernel,
        grid=(m // tm,),
        in_specs=by_pattern(d_attn) + by_pattern(LANES) + by_pattern(LANES)
        + [rows(d_attn), rows(d_chunk), rows(d_chunk), rows(d_chunk), rows(d_model),
           pl.BlockSpec(w_out.shape, lambda i: (0, 0), pipeline_mode=pl.Buffered(1)),
           whole(w_s), whole(b_s_t), whole(fg)],
        out_specs=rows(d_model),
        out_shape=jax.ShapeDtypeStruct((m, d_model), _F32),
        scratch_shapes=[pltpu.VMEM((tm, d_attn), _BF16), pltpu.VMEM((tm, d_chunk), _BF16), slab, slab, slab,
                        pltpu.VMEM((4, tm, LANES), _F32)],
        compiler_params=_params("arbitrary"),
        name="out_proj",
    )(*os, *ms, *dens, ga, u, vc, gb, x, w_out, w_s, b_s_t, fg)


def _sample_in_proj_kernel(x_ref, g_ref, w_ref, cos_ref, sin_ref, lng_ref, lnb_ref,
                           q_ref, k_ref, v_ref, ga_ref, u_ref, vc_ref, gb_ref, wb_ref, xn_ref):
    j = pl.program_id(0)

    @pl.when(j == 0)
    def _():
        xf = x_ref[...]
        ms = jnp.mean(xf * xf, axis=-1, keepdims=True)
        xn_ref[...] = (xf * lax.rsqrt(ms + EPS) * g_ref[...]).astype(xn_ref.dtype)

    wb_ref[...] = w_ref[...].astype(wb_ref.dtype)
    z = jnp.dot(xn_ref[...], wb_ref[...], preferred_element_type=_F32)

    def rope_to(ref, scale):
        for g in range(z.shape[1] // LANES):
            sl = slice(g * LANES, (g + 1) * LANES)
            ref[:, sl] = _rope_group(z[:, sl], cos_ref[...], sin_ref[...]) * scale

    @pl.when(j == SEG_Q)
    def _():
        rope_to(q_ref, HEAD_DIM ** -0.5)

    @pl.when(j == SEG_K)
    def _():
        rope_to(k_ref, 1.0)

    @pl.when(j == SEG_V)
    def _():
        v_ref[...] = z

    @pl.when(j == SEG_GA)
    def _():
        ga_ref[...] = jax.nn.silu(z)

    @pl.when(j == SEG_U)
    def _():
        u_ref[...] = jax.nn.gelu(z)

    @pl.when(j == SEG_VC)
    def _():
        vc_ref[...] = _gelu_layer_norm(z, lng_ref[...], lnb_ref[...])

    @pl.when(j == SEG_GB)
    def _():
        gb_ref[...] = jax.nn.silu(z)


def _sample_in_proj(x, norm_g, w_in, cos_t, sin_t, ln_g, ln_b):
    m, d_model = x.shape
    d_seg = w_in.shape[1] // N_SEGMENTS
    whole = lambda r, c: pl.BlockSpec((r, c), lambda j: (0, 0))
    return pl.pallas_call(
        _sample_in_proj_kernel,
        grid=(N_SEGMENTS,),
        in_specs=[whole(m, d_model), whole(1, d_model),
                  pl.BlockSpec((d_model, d_seg), lambda j: (0, j)),
                  whole(m, LANES), whole(m, LANES), whole(1, d_seg), whole(1, d_seg)],
        out_specs=[whole(m, d_seg)] * N_SEGMENTS + [pl.BlockSpec((d_model, d_seg), lambda j: (0, j))],
        out_shape=[jax.ShapeDtypeStruct((m, d_seg), _F32)] * N_SEGMENTS
        + [jax.ShapeDtypeStruct(w_in.shape, _BF16)],
        scratch_shapes=[pltpu.VMEM((m, d_model), _BF16)],
        compiler_params=_params("arbitrary"),
        name="sample_in_proj",
    )(x, norm_g.reshape(1, -1), w_in, cos_t, sin_t, ln_g.reshape(1, -1), ln_b.reshape(1, -1))


def _key_multiplicity(rows, wb, t_new):
    t_row = lax.broadcasted_iota(jnp.int32, (rows, wb), 0) // (rows // t_new)
    pos = lax.broadcasted_iota(jnp.int32, (rows, wb), 1)
    dist = wb + t_row - pos
    mult = jnp.zeros((rows, wb), _F32)
    for window, d in DILATIONS:
        mult = mult + jnp.where((dist % d == 0) & (dist <= window), 1.0, 0.0)
    return mult


def _sample_attn_tile(q_ref, kn_ref, vn_ref, kt_ref, vt_ref, o_ref, mult_ref):
    t_new, width = q_ref.shape
    n_heads = width // HEAD_DIM
    rows = t_new * n_heads
    head_row = lax.broadcasted_iota(jnp.int32, (n_heads, width), 0)
    head_col = lax.broadcasted_iota(jnp.int32, (n_heads, width), 1) // HEAD_DIM
    own = head_row == head_col
    qbd = jnp.concatenate([jnp.where(own, q_ref[t:t + 1, :], 0.0) for t in range(t_new)], axis=0)
    mult = mult_ref[...]

    s = jnp.dot(qbd.astype(_BF16), kt_ref[...].astype(_BF16), preferred_element_type=_F32)
    s = jnp.where(mult > 0.0, s, NEG_INF)
    m = jnp.max(s, axis=-1, keepdims=True)
    t_col = lax.broadcasted_iota(jnp.int32, (rows, 1), 0) // n_heads
    s_new, mult_new = [], []
    for tp in range(t_new):
        s_new.append(jnp.sum(qbd * kn_ref[tp:tp + 1, :], axis=-1, keepdims=True))
        dn = t_col - tp
        mn = jnp.zeros((rows, 1), _F32)
        for window, d in DILATIONS:
            mn = mn + jnp.where((dn >= 0) & (dn % d == 0), 1.0, 0.0)
        mult_new.append(mn)
        m = jnp.maximum(m, jnp.where(mn > 0.0, s_new[tp], NEG_INF))
    p = jnp.exp(s - m) * mult
    den = jnp.sum(p, axis=-1, keepdims=True)
    num = lax.dot_general(p.astype(_BF16), vt_ref[...].astype(_BF16), (((1,), (1,)), ((), ())),
                          preferred_element_type=_F32)
    for tp in range(t_new):
        pn = jnp.where(mult_new[tp] > 0.0, jnp.exp(s_new[tp] - m), 0.0) * mult_new[tp]
        den = den + pn
        num = num + pn * vn_ref[tp:tp + 1, :]
    res = num / den
    for t in range(t_new):
        blk = res[t * n_heads:(t + 1) * n_heads, :]
        o_ref[t:t + 1, :] = jnp.sum(jnp.where(own, blk, 0.0), axis=0, keepdims=True)


def _sample_out_kernel(oa_ref, ga_ref, u_ref, vc_ref, gb_ref, x_ref, wout_ref, coef_ref, bias_ref, fg_ref,
                       y_ref, wb_ref, c_ref, *, t_new):
    m, d_attn = oa_ref.shape
    db = m // t_new
    wb_ref[...] = wout_ref[...].astype(wb_ref.dtype)
    a = (oa_ref[...] * ga_ref[...]).astype(_BF16)
    for t in range(t_new):
        mixed = jnp.zeros((db, c_ref.shape[1]), _F32) + bias_ref[t:t + 1, :]
        for s in range(t + 1):
            mixed = mixed + coef_ref[t * t_new + s:t * t_new + s + 1, :] * vc_ref[s * db:(s + 1) * db, :]
        rs = slice(t * db, (t + 1) * db)
        c_ref[rs, :] = (u_ref[rs, :] * mixed * gb_ref[rs, :]).astype(c_ref.dtype)
    acc = jnp.dot(a, wb_ref[:d_attn, :], preferred_element_type=_F32)
    acc = acc + jnp.dot(c_ref[...], wb_ref[d_attn:, :], preferred_element_type=_F32)
    xo = x_ref[...] + acc
    ms = jnp.mean(xo * xo, axis=-1, keepdims=True)
    y_ref[...] = xo * lax.rsqrt(ms + EPS) * fg_ref[...]


def _sample_out(oa, ga, u, vc, gb, x, w_out, coef, bias, final_g, *, t_new):
    m, d_model = x.shape
    args = (oa, ga, u, vc, gb, x, w_out, coef, bias, final_g.reshape(1, -1))
    whole = lambda a: pl.BlockSpec(a.shape, lambda i: (0,) * a.ndim)
    return pl.pallas_call(
        functools.partial(_sample_out_kernel, t_new=t_new),
        grid=(1,),
        in_specs=[whole(a) for a in args],
        out_specs=[pl.BlockSpec((m, d_model), lambda i: (0, 0)), whole(w_out)],
        out_shape=[jax.ShapeDtypeStruct((m, d_model), _F32), jax.ShapeDtypeStruct(w_out.shape, _BF16)],
        scratch_shapes=[pltpu.VMEM((m, u.shape[1]), _BF16)],
        compiler_params=_params("arbitrary"),
        name="sample_out",
    )(*args)


def kernel(x_prompt, x_sample, cache_k, cache_v, norm_g, w_in, ln_g, ln_b, w_s, b_s, w_out, final_g):
    batch, s_len, d_model = x_prompt.shape
    db, t_new, _ = x_sample.shape
    depth, _, wb, n_heads, head_dim = cache_k.shape
    d_attn = n_heads * head_dim
    d_chunk = w_out.shape[1] - d_attn
    assert batch == 1 and depth == 1 and head_dim == HEAD_DIM
    assert w_in.shape[2] == 4 * d_attn + 3 * d_chunk and d_attn == d_chunk
    tail = min(MAX_WINDOW, s_len)
    w_in0, w_out0 = w_in[0], w_out[0]
    ng, lg, lb = norm_g[0], ln_g[0], ln_b[0]

    xs = jnp.transpose(x_sample, (1, 0, 2)).reshape(t_new * db, d_model)
    pos_s = PAST_LEN + jnp.repeat(jnp.arange(t_new, dtype=jnp.int32), db)
    cos_s, sin_s = _rope_tables(pos_s)
    qs, ks, vs, gas, us, vcs, gbs, w_in_b = _sample_in_proj(xs, ng, w_in0, cos_s, sin_s, lg, lb)
    to_b_major = lambda a: jnp.transpose(a.reshape(t_new, db, -1), (1, 0, 2))
    qs_b, ks_b, vs_b, vcs_b = (to_b_major(a) for a in (qs, ks, vs, vcs))
    cache_t = lambda c: jnp.transpose(c[0], (0, 2, 3, 1)).reshape(db, d_attn, wb)
    cache_kt, cache_vt = cache_t(cache_k), cache_t(cache_v)
    assert wb == MAX_WINDOW

    xp = x_prompt.reshape(s_len, d_model)
    tm_qkv = 1024
    tables = _rope_tables_blocked(s_len, tm_qkv)
    xn, q1, q4, q16 = _proj(xp, w_in_b, SEG_Q, tm=tm_qkv, norm_g=ng, tables=tables,
                            scale=HEAD_DIM ** -0.5 * math.log2(math.e))
    k1, k4, k16, k_tail_t = _proj(xn, w_in_b, SEG_K, tm=tm_qkv, tables=tables, tail_rows=tail)
    v1, v4, v16, v_tail_t = _proj(xn, w_in_b, SEG_V, tm=tm_qkv, tail_rows=tail)
    tm_gate = 512
    tiles_per_call = s_len // tm_gate
    assert 4 * tiles_per_call == db * (d_attn // SAMPLE_HEAD_GROUP_WIDTH)
    side = lambda c: (qs_b, ks_b, vs_b, cache_kt, cache_vt, c * tiles_per_call)
    ga, oa0 = _proj(xn, w_in_b, SEG_GA, tm=tm_gate, side=side(0))
    u, oa1 = _proj(xn, w_in_b, SEG_U, tm=tm_gate, side=side(1))
    vc, oa2 = _proj(xn, w_in_b, SEG_VC, tm=tm_gate, ln=(lg, lb), side=side(2))
    gb, oa3 = _proj(xn, w_in_b, SEG_GB, tm=tm_gate, side=side(3))
    oa = jnp.concatenate([oa0, oa1, oa2, oa3], axis=0)
    o1, m1, d1 = _dilated_attention(q1[None], k1[None], v1[None], q_rows=1024)
    o4, m4, d4 = _dilated_attention(q4, k4, v4, q_rows=1024)
    o16, m16, d16 = _dilated_attention(q16, k16, v16, q_rows=512)

    oa_t = jnp.transpose(oa, (1, 0, 2)).reshape(t_new * db, d_attn)
    coef = jnp.repeat(jnp.transpose(w_s[0][:, :t_new, :t_new], (1, 2, 0)).reshape(t_new * t_new, -1),
                      GROUP_WIDTH_B, axis=1)
    bias = jnp.repeat(jnp.transpose(b_s[0][:, :t_new], (1, 0)), GROUP_WIDTH_B, axis=1)
    ys, w_out_b = _sample_out(oa_t, gas, us, vcs, gbs, xs, w_out0, coef, bias, final_g, t_new=t_new)
    y_sample = jnp.transpose(ys.reshape(t_new, db, d_model), (1, 0, 2))

    y_prompt = _out_proj((o1[0], o4, o16), (m1[0], m4, m16), (d1[0], d4, d16), ga, u, vc, gb, xp, w_out_b,
                         w_s[0], b_s[0].T, final_g, tm=512)
    untranspose = lambda a: jnp.transpose(a.reshape(n_heads, head_dim, tail), (2, 0, 1))

    hs = (n_heads, head_dim)
    return (
        y_prompt.reshape(batch, s_len, d_model),
        y_sample,
        untranspose(k_tail_t).reshape(depth, batch, tail, *hs),
        untranspose(v_tail_t).reshape(depth, batch, tail, *hs),
        ks_b.reshape(depth, db, t_new, *hs),
        vs_b.reshape(depth, db, t_new, *hs),
        vcs_b.reshape(depth, db, t_new, d_chunk),
    )
```

```python
import functools
import math

import jax
import jax.numpy as jnp
from jax import lax
from jax.experimental import pallas as pl
from jax.experimental.pallas import tpu as pltpu

HEAD_DIM = 64
BLOCK = 128
CHUNK = 128
GROUP_WIDTH_B = 128
DILATIONS = ((128, 1), (512, 4), (2048, 16))
MAX_WINDOW = 2048
PAST_LEN = 16384
ROPE_THETA = 10000.0
EPS = 1e-6
NEG_INF = -1e30
N_SEGMENTS = 7
SEG_Q, SEG_K, SEG_V, SEG_GA, SEG_U, SEG_VC, SEG_GB = range(N_SEGMENTS)

LANES = 128
VMEM_LIMIT_BYTES = 56 * 1024 * 1024
SUB_ROWS = 256
SAMPLE_HEAD_GROUP_WIDTH = 512

_BF16 = jnp.bfloat16
_F32 = jnp.float32


def _params(*semantics):
    return pltpu.CompilerParams(dimension_semantics=semantics, vmem_limit_bytes=VMEM_LIMIT_BYTES)


def _rope_tables(pos):
    half = HEAD_DIM // 2
    inv = jnp.exp(-math.log(ROPE_THETA) * jnp.arange(half, dtype=_F32) / half)
    ang = pos.astype(_F32)[:, None] * inv[None, :]
    cos = jnp.cos(ang)
    sin = jnp.sin(ang)
    cos_t = jnp.concatenate([cos, cos, cos, cos], axis=-1)
    sin_t = jnp.concatenate([-sin, sin, -sin, sin], axis=-1)
    return cos_t, sin_t


def _rope_tables_blocked(s_len, tm):
    half = HEAD_DIM // 2
    inv = jnp.exp(-math.log(ROPE_THETA) * jnp.arange(half, dtype=_F32) / half)
    tile4 = lambda a: jnp.concatenate([a, a, a, a], axis=-1)
    ang_r = jnp.arange(tm, dtype=jnp.int32).astype(_F32)[:, None] * inv[None, :]
    ang_b = (jnp.arange(s_len // tm, dtype=jnp.int32) * tm).astype(_F32)[:, None] * inv[None, :]
    sign = jnp.concatenate([-jnp.ones((1, half), _F32), jnp.ones((1, half), _F32)] * 2, axis=-1)
    return (tile4(jnp.cos(ang_r)), tile4(jnp.sin(ang_r)), tile4(jnp.cos(ang_b)), tile4(jnp.sin(ang_b)), sign)


def _rope_group(zg, cos, sin_signed):
    lane = lax.broadcasted_iota(jnp.int32, zg.shape, 1)
    first_half = (lane % HEAD_DIM) < (HEAD_DIM // 2)
    partner = jnp.where(first_half, pltpu.roll(zg, LANES - HEAD_DIM // 2, 1),
                        pltpu.roll(zg, HEAD_DIM // 2, 1))
    return zg * cos + partner * sin_signed


def _gelu_layer_norm(z, g, b):
    h = jax.nn.gelu(z)
    mu = jnp.mean(h, axis=-1, keepdims=True)
    hc = h - mu
    var = jnp.mean(hc * hc, axis=-1, keepdims=True)
    return hc * lax.rsqrt(var + EPS) * g + b


def _proj_kernel(*refs, seg, scale, with_tail):
    rope = seg in (SEG_Q, SEG_K)
    it = iter(refs)
    x_ref = next(it)
    g_ref = next(it) if seg == SEG_Q else None
    w_ref = next(it)
    rope_refs = [next(it) for _ in range(5)] if rope else None
    xn_out_ref = next(it) if seg == SEG_Q else None
    d1_ref, d4_ref, d16_ref = next(it), next(it), next(it)
    tail_ref = next(it) if with_tail else None
    nat_ref, res4_ref = next(it), next(it)

    tm = x_ref.shape[0]
    sub = min(SUB_ROWS, tm)
    n4, n16 = sub // 4, sub // 16
    for t in range(tm // sub):
        rs = slice(t * sub, (t + 1) * sub)
        if seg == SEG_Q:
            xf = x_ref[rs, :]
            ms = jnp.mean(xf * xf, axis=-1, keepdims=True)
            xb = (xf * lax.rsqrt(ms + EPS) * g_ref[...]).astype(_BF16)
            xn_out_ref[rs, :] = xb
        else:
            xb = x_ref[rs, :]
        z = jnp.dot(xb, w_ref[...], preferred_element_type=_F32)
        if rope:
            cos_r_ref, sin_r_ref, cos_b_ref, sin_b_ref, sign_ref = rope_refs
            step = pl.ds(pl.program_id(0), 1)
            cb, sb = cos_b_ref[step, :], sin_b_ref[step, :]
            cr, sr = cos_r_ref[rs, :], sin_r_ref[rs, :]
            cos = cr * cb - sr * sb
            sin_signed = (sr * cb + cr * sb) * sign_ref[...]
        for g in range(z.shape[1] // LANES):
            sl = slice(g * LANES, (g + 1) * LANES)
            r = z[:, sl]
            if rope:
                r = _rope_group(r, cos, sin_signed)
            if tail_ref is not None:
                tail_ref[sl, rs] = r.T
            if scale != 1.0:
                r = r * scale
            d1_ref[rs, sl] = r.astype(d1_ref.dtype)
            nat_ref[g, rs, :] = r
            for r4 in range(4):
                blk = nat_ref[g, pl.ds(t * sub + r4, n4, stride=4), :]
                d4_ref[r4, t * n4:(t + 1) * n4, sl] = blk.astype(d4_ref.dtype)
                res4_ref[g, t * sub + r4 * n4:t * sub + (r4 + 1) * n4, :] = blk
            for r4 in range(4):
                for a in range(4):
                    blk = res4_ref[g, pl.ds(t * sub + r4 * n4 + a, n16, stride=4), :]
                    d16_ref[r4 + 4 * a, t * n16:(t + 1) * n16, sl] = blk.astype(d16_ref.dtype)


def _gates_kernel(xn_ref, w_ref, lng_ref, lnb_ref, q_ref, kn_ref, vn_ref, kt_ref, vt_ref,
                  ga_ref, u_ref, vc_ref, gb_ref, oa_ref, mult_ref):
    j = pl.program_id(1)

    @pl.when(jnp.logical_and(pl.program_id(0) == 0, j == 0))
    def _():
        mult_ref[...] = _key_multiplicity(mult_ref.shape[0], mult_ref.shape[1], q_ref.shape[0])

    tm = xn_ref.shape[0]
    sub = min(SUB_ROWS, tm)
    epilogues = (
        (ga_ref, jax.nn.silu),
        (u_ref, jax.nn.gelu),
        (vc_ref, lambda z: _gelu_layer_norm(z, lng_ref[...], lnb_ref[...])),
        (gb_ref, jax.nn.silu),
    )
    for s, (o_ref, epilogue) in enumerate(epilogues):
        @pl.when(j == s)
        def _(s=s, o_ref=o_ref, epilogue=epilogue):
            _sample_attn_tile(q_ref, kn_ref, vn_ref, kt_ref, vt_ref, oa_ref, mult_ref)
            for t in range(tm // sub):
                rs = slice(t * sub, (t + 1) * sub)
                z = jnp.dot(xn_ref[rs, :], w_ref[s], preferred_element_type=_F32)
                o_ref[rs, :] = epilogue(z).astype(o_ref.dtype)


def _gates_proj(xn, w_gate, ln, side, *, tm):
    s_len, d_model = xn.shape
    n_seg, _, d_seg = w_gate.shape
    sq, skn, svn, skt, svt = side
    db, t_new, d_attn = sq.shape
    width, wb = SAMPLE_HEAD_GROUP_WIDTH, skt.shape[2]
    hg = d_attn // width
    n_i = s_len // tm
    assert n_i * n_seg == db * hg
    tile = lambda i, j: i * n_seg + j
    new = pl.BlockSpec((None, t_new, width), lambda i, j: (tile(i, j) // hg, 0, tile(i, j) % hg))
    cache = pl.BlockSpec((None, width, wb), lambda i, j: (tile(i, j) // hg, tile(i, j) % hg, 0))
    row_vec = pl.BlockSpec((1, d_seg), lambda i, j: (0, 0))
    rows = pl.BlockSpec((tm, d_seg), lambda i, j: (i, 0))
    return pl.pallas_call(
        _gates_kernel,
        grid=(n_i, n_seg),
        in_specs=[pl.BlockSpec((tm, d_model), lambda i, j: (i, 0)),
                  pl.BlockSpec(w_gate.shape, lambda i, j: (0, 0, 0), pipeline_mode=pl.Buffered(1)),
                  row_vec, row_vec, new, new, new, cache, cache],
        out_specs=[rows] * n_seg + [new],
        out_shape=[jax.ShapeDtypeStruct((s_len, d_seg), _BF16)] * n_seg
        + [jax.ShapeDtypeStruct((db, t_new, d_attn), _F32)],
        scratch_shapes=[pltpu.VMEM((t_new * width // HEAD_DIM, wb), _F32)],
        compiler_params=_params("arbitrary", "arbitrary"),
        name="gates_proj",
    )(xn, w_gate, ln[0].reshape(1, -1), ln[1].reshape(1, -1), sq, skn, svn, skt, svt)


def _proj(x, w_qkv, seg, *, tm, norm_g=None, tables=None, scale=1.0, tail_rows=0):
    s_len, d_model = x.shape
    d_seg = w_qkv.shape[1] // 3
    n_i = s_len // tm
    in_specs = [pl.BlockSpec((tm, d_model), lambda i: (i, 0))]
    args = [x]
    if seg == SEG_Q:
        in_specs.append(pl.BlockSpec((1, d_model), lambda i: (0, 0)))
        args.append(norm_g.reshape(1, -1))
    in_specs.append(pl.BlockSpec((d_model, d_seg), lambda i: (0, seg), pipeline_mode=pl.Buffered(1)))
    args.append(w_qkv)
    if seg in (SEG_Q, SEG_K):
        in_specs += [pl.BlockSpec(t.shape, lambda i: (0, 0)) for t in tables]
        args += list(tables)
    out_shape, out_specs = [], []
    if seg == SEG_Q:
        out_shape.append(jax.ShapeDtypeStruct((s_len, d_model), _BF16))
        out_specs.append(pl.BlockSpec((tm, d_model), lambda i: (i, 0)))
    out_shape += [jax.ShapeDtypeStruct((s_len, d_seg), _BF16),
                  jax.ShapeDtypeStruct((4, s_len // 4, d_seg), _BF16),
                  jax.ShapeDtypeStruct((16, s_len // 16, d_seg), _BF16)]
    out_specs += [pl.BlockSpec((tm, d_seg), lambda i: (i, 0)),
                  pl.BlockSpec((4, tm // 4, d_seg), lambda i: (0, i, 0)),
                  pl.BlockSpec((16, tm // 16, d_seg), lambda i: (0, i, 0))]
    if tail_rows:
        tail_start = n_i - tail_rows // tm
        out_shape.append(jax.ShapeDtypeStruct((d_seg, tail_rows), _F32))
        out_specs.append(pl.BlockSpec((d_seg, tm), lambda i: (0, jnp.maximum(i - tail_start, 0))))
    scratch = [pltpu.VMEM((d_seg // LANES, tm, LANES), _F32)] * 2
    return pl.pallas_call(
        functools.partial(_proj_kernel, seg=seg, scale=scale, with_tail=bool(tail_rows)),
        grid=(n_i,),
        in_specs=in_specs,
        out_specs=out_specs,
        out_shape=out_shape,
        scratch_shapes=scratch,
        compiler_params=_params("arbitrary"),
        name=f"proj_seg{seg}",
    )(*args)


def _attn_kernel(q_ref, kp_ref, kc_ref, vp_ref, vc_ref, o_ref, m_ref, den_ref, k0_ref, *vext_refs):
    n = pl.program_id(1)
    first_step = jnp.logical_and(pl.program_id(0) == 0, n == 0)
    n_pairs = q_ref.shape[1] // LANES
    two = 2 * BLOCK

    @pl.when(first_step)
    def _():
        ones = jnp.ones((two, LANES), k0_ref.dtype)
        for vext_ref in vext_refs:
            for hp in range(n_pairs):
                vext_ref[:, (2 * hp + 1) * LANES:(2 * hp + 2) * LANES] = ones

    k0_ref[0:BLOCK, :] = kp_ref[...]
    k0_ref[BLOCK:, :] = kc_ref[0:BLOCK, :]

    qi = lax.broadcasted_iota(jnp.int32, (two, two), 0) % BLOCK
    si = lax.broadcasted_iota(jnp.int32, (two, two), 1)
    dist = qi + BLOCK - si
    band_bias = jnp.where((dist >= 0) & (dist <= BLOCK), 0.0, NEG_INF).astype(_F32)
    lane = lax.broadcasted_iota(jnp.int32, (BLOCK, LANES), 1)
    head0 = lane < HEAD_DIM
    zero = jnp.zeros((BLOCK, LANES), q_ref.dtype)

    for j, vext_ref in enumerate(vext_refs):
        rows = slice(j * BLOCK, (j + 1) * BLOCK)
        prev_keys = slice((j - 1) * BLOCK, (j + 1) * BLOCK)
        bias = band_bias
        if j == 0:
            bias = band_bias + jnp.where(jnp.logical_and(si < BLOCK, n == 0), NEG_INF, 0.0)
        for hp in range(n_pairs):
            sl = slice(hp * LANES, (hp + 1) * LANES)
            ext = slice(2 * hp * LANES, (2 * hp + 1) * LANES)
            if j == 0:
                vext_ref[0:BLOCK, ext] = vp_ref[:, sl]
                vext_ref[BLOCK:, ext] = vc_ref[0:BLOCK, sl]
            else:
                vext_ref[:, ext] = vc_ref[prev_keys, sl]
        m_acc = jnp.zeros((BLOCK, LANES), _F32)
        den_acc = jnp.ones((BLOCK, LANES), _F32)
        for hp in range(n_pairs):
            sl = slice(hp * LANES, (hp + 1) * LANES)
            q2 = q_ref[rows, sl]
            qs = jnp.concatenate([jnp.where(head0, q2, zero), jnp.where(head0, zero, q2)], axis=0)
            k2 = k0_ref[:, sl] if j == 0 else kc_ref[prev_keys, sl]
            s = lax.dot_general(qs, k2, (((1,), (1,)), ((), ())), preferred_element_type=_F32) + bias
            m = jnp.max(s, axis=-1, keepdims=True)
            p = jnp.exp2(s - m).astype(vext_ref.dtype)
            pv = jnp.dot(p, vext_ref[:, 2 * hp * LANES:(2 * hp + 2) * LANES], preferred_element_type=_F32)
            num, den = pv[:, :LANES], pv[:, LANES:]
            o_ref[rows, sl] = jnp.where(head0, num[:BLOCK], num[BLOCK:]).astype(o_ref.dtype)
            m_acc = jnp.where(lane == 2 * hp, m[:BLOCK], jnp.where(lane == 2 * hp + 1, m[BLOCK:], m_acc))
            den_acc = jnp.where(lane == 2 * hp, den[:BLOCK],
                                jnp.where(lane == 2 * hp + 1, den[BLOCK:], den_acc))
        m_ref[rows, :] = m_acc
        den_ref[rows, :] = den_acc


def _dilated_attention(q, k, v, *, q_rows):
    d, sub_len, d_attn = q.shape
    per = q_rows // BLOCK
    cur = pl.BlockSpec((None, q_rows, d_attn), lambda r, n: (r, n, 0))
    prev = pl.BlockSpec((None, BLOCK, d_attn), lambda r, n: (r, jnp.maximum(n * per - 1, 0), 0))
    stat = pl.BlockSpec((None, q_rows, LANES), lambda r, n: (r, n, 0))
    stat_shape = jax.ShapeDtypeStruct((d, sub_len, LANES), _F32)
    return pl.pallas_call(
        _attn_kernel,
        grid=(d, sub_len // q_rows),
        in_specs=[cur, prev, cur, prev, cur],
        out_specs=[cur, stat, stat],
        out_shape=[jax.ShapeDtypeStruct((d, sub_len, d_attn), q.dtype), stat_shape, stat_shape],
        scratch_shapes=[pltpu.VMEM((2 * BLOCK, d_attn), q.dtype)]
        + [pltpu.VMEM((2 * BLOCK, 2 * d_attn), q.dtype)] * per,
        compiler_params=_params("arbitrary", "arbitrary"),
        name=f"dilated_attn_d{d}",
    )(q, k, k, v, v)


def _expand_heads(w, expand2):
    n_heads = expand2.shape[1] // HEAD_DIM
    hi = w.astype(_BF16).astype(_F32)
    lane = lax.broadcasted_iota(jnp.int32, w.shape, 1)
    lhs = jnp.where(lane < n_heads, hi, pltpu.roll(w - hi, n_heads, 1))
    return jnp.dot(lhs.astype(_BF16), expand2, preferred_element_type=_F32)


def _out_proj_kernel(o1_ref, o4_ref, o16_ref, m1_ref, m4_ref, m16_ref, d1_ref, d4_ref, d16_ref,
                     ga_ref, u_ref, vc_ref, gb_ref, x_ref, wb_ref, ws_ref, bst_ref, fg_ref, y_ref,
                     a_ref, c_ref, n4_ref, n16_ref, tmp_ref, st_ref):
    tm, d_attn = o1_ref.shape
    n_heads = d_attn // HEAD_DIM
    n_lane_groups = d_attn // LANES
    sub = 2 * CHUNK
    n4, n16 = sub // 4, sub // 16

    row_head = lax.broadcasted_iota(jnp.int32, (LANES, d_attn), 0) % n_heads
    row_used = lax.broadcasted_iota(jnp.int32, (LANES, d_attn), 0) < 2 * n_heads
    col_head = lax.broadcasted_iota(jnp.int32, (LANES, d_attn), 1) // HEAD_DIM
    expand2 = jnp.logical_and(row_head == col_head, row_used).astype(_BF16)
    tril = (lax.broadcasted_iota(jnp.int32, (CHUNK, CHUNK), 0)
            >= lax.broadcasted_iota(jnp.int32, (CHUNK, CHUNK), 1))

    for t in range(tm // sub):
        rs = slice(t * sub, (t + 1) * sub)
        for r in range(4):
            dst = pl.ds(t * sub + r, n4, stride=4)
            src = slice(t * n4, (t + 1) * n4)
            st_ref[0, dst, :] = m4_ref[r, src, :]
            st_ref[1, dst, :] = d4_ref[r, src, :]
            for g in range(n_lane_groups):
                n4_ref[g, dst, :] = o4_ref[r, src, g * LANES:(g + 1) * LANES].astype(_F32)
        for r in range(16):
            dst = pl.ds(t * sub + r, n16, stride=16)
            src = slice(t * n16, (t + 1) * n16)
            st_ref[2, dst, :] = m16_ref[r, src, :]
            st_ref[3, dst, :] = d16_ref[r, src, :]
        for g in range(n_lane_groups):
            sl = slice(g * LANES, (g + 1) * LANES)
            for r4 in range(4):
                for a in range(4):
                    blk = o16_ref[r4 + 4 * a, t * n16:(t + 1) * n16, sl].astype(_F32)
                    tmp_ref[g, pl.ds(t * sub + r4 * n4 + a, n16, stride=4), :] = blk
                n16_ref[g, pl.ds(t * sub + r4, n4, stride=4), :] = tmp_ref[g, t * sub + r4 * n4:
                                                                          t * sub + (r4 + 1) * n4, :]

        ms = [m1_ref[rs, :], st_ref[0, rs, :], st_ref[2, rs, :]]
        dens = [d1_ref[rs, :], st_ref[1, rs, :], st_ref[3, rs, :]]
        mx = jnp.maximum(jnp.maximum(ms[0], ms[1]), ms[2])
        es = [jnp.exp2(m - mx) for m in ms]
        tot = es[0] * dens[0] + es[1] * dens[1] + es[2] * dens[2]
        ws = [_expand_heads(e / tot, expand2) for e in es]
        for g in range(n_lane_groups):
            sl = slice(g * LANES, (g + 1) * LANES)
            a = (ws[0][:, sl] * o1_ref[rs, sl].astype(_F32) + ws[1][:, sl] * n4_ref[g, rs, :]
                 + ws[2][:, sl] * n16_ref[g, rs, :])
            a_ref[rs, sl] = (a * ga_ref[rs, sl].astype(_F32)).astype(a_ref.dtype)

        c0 = slice(t * sub, t * sub + CHUNK)
        c1 = slice(t * sub + CHUNK, (t + 1) * sub)
        for g in range(ws_ref.shape[0]):
            wm = jnp.where(tril, ws_ref[g], 0.0).astype(_BF16)
            gs = slice(g * GROUP_WIDTH_B, (g + 1) * GROUP_WIDTH_B)
            vc2 = jnp.concatenate([vc_ref[c0, gs], vc_ref[c1, gs]], axis=1)
            mixed = jnp.dot(wm, vc2, preferred_element_type=_F32) + bst_ref[:, g:g + 1]
            for c, half in ((c0, slice(0, GROUP_WIDTH_B)), (c1, slice(GROUP_WIDTH_B, 2 * GROUP_WIDTH_B))):
                cv = u_ref[c, gs].astype(_F32) * mixed[:, half] * gb_ref[c, gs].astype(_F32)
                c_ref[c, gs] = cv.astype(c_ref.dtype)

        acc = jnp.dot(a_ref[rs, :], wb_ref[:d_attn, :], preferred_element_type=_F32)
        acc = acc + jnp.dot(c_ref[rs, :], wb_ref[d_attn:, :], preferred_element_type=_F32)
        xo = x_ref[rs, :] + acc
        msq = jnp.mean(xo * xo, axis=-1, keepdims=True)
        y_ref[rs, :] = xo * lax.rsqrt(msq + EPS) * fg_ref[...]


def _out_proj(os, ms, dens, ga, u, vc, gb, x, w_out, w_s, b_s_t, final_g, *, tm):
    m, d_model = x.shape
    d_attn = ga.shape[1]
    d_chunk = u.shape[1]
    rows = lambda n: pl.BlockSpec((tm, n), lambda i: (i, 0))
    res = lambda d, n: pl.BlockSpec((d, tm // d, n), lambda i: (0, i, 0))
    whole = lambda a: pl.BlockSpec(a.shape, lambda i: (0,) * a.ndim)
    by_pattern = lambda n: [rows(n), res(4, n), res(16, n)]
    fg = final_g.reshape(1, -1)
    slab = pltpu.VMEM((d_attn // LANES, tm, LANES), _F32)
    return pl.pallas_call(
        _out_proj_kernel,
        grid=(m // tm,),
        in_specs=by_pattern(d_attn) + by_pattern(LANES) + by_pattern(LANES)
        + [rows(d_attn), rows(d_chunk), rows(d_chunk), rows(d_chunk), rows(d_model),
           pl.BlockSpec(w_out.shape, lambda i: (0, 0), pipeline_mode=pl.Buffered(1)),
           whole(w_s), whole(b_s_t), whole(fg)],
        out_specs=rows(d_model),
        out_shape=jax.ShapeDtypeStruct((m, d_model), _F32),
        scratch_shapes=[pltpu.VMEM((tm, d_attn), _BF16), pltpu.VMEM((tm, d_chunk), _BF16), slab, slab, slab,
                        pltpu.VMEM((4, tm, LANES), _F32)],
        compiler_params=_params("arbitrary"),
        name="out_proj",
    )(*os, *ms, *dens, ga, u, vc, gb, x, w_out, w_s, b_s_t, fg)


def _sample_in_proj_kernel(x_ref, g_ref, w_ref, cos_ref, sin_ref, lng_ref, lnb_ref,
                           q_ref, k_ref, v_ref, ga_ref, u_ref, vc_ref, gb_ref, wqkv_ref, wgate_ref, xn_ref):
    j = pl.program_id(0)

    @pl.when(j == 0)
    def _():
        xf = x_ref[...]
        ms = jnp.mean(xf * xf, axis=-1, keepdims=True)
        xn_ref[...] = (xf * lax.rsqrt(ms + EPS) * g_ref[...]).astype(xn_ref.dtype)

    wb = w_ref[...].astype(_BF16)

    @pl.when(j <= SEG_V)
    def _():
        wqkv_ref[...] = wb

    @pl.when(j > SEG_V)
    def _():
        wgate_ref[...] = wb

    z = jnp.dot(xn_ref[...], wb, preferred_element_type=_F32)

    def rope_to(ref, scale):
        for g in range(z.shape[1] // LANES):
            sl = slice(g * LANES, (g + 1) * LANES)
            ref[:, sl] = _rope_group(z[:, sl], cos_ref[...], sin_ref[...]) * scale

    @pl.when(j == SEG_Q)
    def _():
        rope_to(q_ref, HEAD_DIM ** -0.5)

    @pl.when(j == SEG_K)
    def _():
        rope_to(k_ref, 1.0)

    @pl.when(j == SEG_V)
    def _():
        v_ref[...] = z

    @pl.when(j == SEG_GA)
    def _():
        ga_ref[...] = jax.nn.silu(z)

    @pl.when(j == SEG_U)
    def _():
        u_ref[...] = jax.nn.gelu(z)

    @pl.when(j == SEG_VC)
    def _():
        vc_ref[...] = _gelu_layer_norm(z, lng_ref[...], lnb_ref[...])

    @pl.when(j == SEG_GB)
    def _():
        gb_ref[...] = jax.nn.silu(z)


def _sample_in_proj(x, norm_g, w_in, cos_t, sin_t, ln_g, ln_b):
    m, d_model = x.shape
    d_seg = w_in.shape[1] // N_SEGMENTS
    whole = lambda r, c: pl.BlockSpec((r, c), lambda j: (0, 0))
    return pl.pallas_call(
        _sample_in_proj_kernel,
        grid=(N_SEGMENTS,),
        in_specs=[whole(m, d_model), whole(1, d_model),
                  pl.BlockSpec((d_model, d_seg), lambda j: (0, j)),
                  whole(m, LANES), whole(m, LANES), whole(1, d_seg), whole(1, d_seg)],
        out_specs=[whole(m, d_seg)] * N_SEGMENTS
        + [pl.BlockSpec((d_model, d_seg), lambda j: (0, jnp.minimum(j, SEG_V))),
           pl.BlockSpec((None, d_model, d_seg), lambda j: (jnp.maximum(j - SEG_GA, 0), 0, 0))],
        out_shape=[jax.ShapeDtypeStruct((m, d_seg), _F32)] * N_SEGMENTS
        + [jax.ShapeDtypeStruct((d_model, (SEG_V + 1) * d_seg), _BF16),
           jax.ShapeDtypeStruct((N_SEGMENTS - SEG_GA, d_model, d_seg), _BF16)],
        scratch_shapes=[pltpu.VMEM((m, d_model), _BF16)],
        compiler_params=_params("arbitrary"),
        name="sample_in_proj",
    )(x, norm_g.reshape(1, -1), w_in, cos_t, sin_t, ln_g.reshape(1, -1), ln_b.reshape(1, -1))


def _key_multiplicity(rows, wb, t_new):
    t_row = lax.broadcasted_iota(jnp.int32, (rows, wb), 0) // (rows // t_new)
    pos = lax.broadcasted_iota(jnp.int32, (rows, wb), 1)
    dist = wb + t_row - pos
    mult = jnp.zeros((rows, wb), _F32)
    for window, d in DILATIONS:
        mult = mult + jnp.where((dist % d == 0) & (dist <= window), 1.0, 0.0)
    return mult


def _sample_attn_tile(q_ref, kn_ref, vn_ref, kt_ref, vt_ref, o_ref, mult_ref):
    t_new, width = q_ref.shape
    n_heads = width // HEAD_DIM
    rows = t_new * n_heads
    head_row = lax.broadcasted_iota(jnp.int32, (n_heads, width), 0)
    head_col = lax.broadcasted_iota(jnp.int32, (n_heads, width), 1) // HEAD_DIM
    own = head_row == head_col
    qbd = jnp.concatenate([jnp.where(own, q_ref[t:t + 1, :], 0.0) for t in range(t_new)], axis=0)
    mult = mult_ref[...]

    s = jnp.dot(qbd.astype(_BF16), kt_ref[...].astype(_BF16), preferred_element_type=_F32)
    s = jnp.where(mult > 0.0, s, NEG_INF)
    m = jnp.max(s, axis=-1, keepdims=True)
    t_col = lax.broadcasted_iota(jnp.int32, (rows, 1), 0) // n_heads
    s_new, mult_new = [], []
    for tp in range(t_new):
        s_new.append(jnp.sum(qbd * kn_ref[tp:tp + 1, :], axis=-1, keepdims=True))
        dn = t_col - tp
        mn = jnp.zeros((rows, 1), _F32)
        for window, d in DILATIONS:
            mn = mn + jnp.where((dn >= 0) & (dn % d == 0), 1.0, 0.0)
        mult_new.append(mn)
        m = jnp.maximum(m, jnp.where(mn > 0.0, s_new[tp], NEG_INF))
    p = jnp.exp(s - m) * mult
    den = jnp.sum(p, axis=-1, keepdims=True)
    num = lax.dot_general(p.astype(_BF16), vt_ref[...].astype(_BF16), (((1,), (1,)), ((), ())),
                          preferred_element_type=_F32)
    for tp in range(t_new):
        pn = jnp.where(mult_new[tp] > 0.0, jnp.exp(s_new[tp] - m), 0.0) * mult_new[tp]
        den = den + pn
        num = num + pn * vn_ref[tp:tp + 1, :]
    res = num / den
    for t in range(t_new):
        blk = res[t * n_heads:(t + 1) * n_heads, :]
        o_ref[t:t + 1, :] = jnp.sum(jnp.where(own, blk, 0.0), axis=0, keepdims=True)


def _sample_out_kernel(oa_ref, ga_ref, u_ref, vc_ref, gb_ref, x_ref, wout_ref, coef_ref, bias_ref, fg_ref,
                       y_ref, wb_ref, c_ref, *, t_new):
    m, d_attn = oa_ref.shape
    db = m // t_new
    wb_ref[...] = wout_ref[...].astype(wb_ref.dtype)
    a = (oa_ref[...] * ga_ref[...]).astype(_BF16)
    for t in range(t_new):
        mixed = jnp.zeros((db, c_ref.shape[1]), _F32) + bias_ref[t:t + 1, :]
        for s in range(t + 1):
            mixed = mixed + coef_ref[t * t_new + s:t * t_new + s + 1, :] * vc_ref[s * db:(s + 1) * db, :]
        rs = slice(t * db, (t + 1) * db)
        c_ref[rs, :] = (u_ref[rs, :] * mixed * gb_ref[rs, :]).astype(c_ref.dtype)
    acc = jnp.dot(a, wb_ref[:d_attn, :], preferred_element_type=_F32)
    acc = acc + jnp.dot(c_ref[...], wb_ref[d_attn:, :], preferred_element_type=_F32)
    xo = x_ref[...] + acc
    ms = jnp.mean(xo * xo, axis=-1, keepdims=True)
    y_ref[...] = xo * lax.rsqrt(ms + EPS) * fg_ref[...]


def _sample_out(oa, ga, u, vc, gb, x, w_out, coef, bias, final_g, *, t_new):
    m, d_model = x.shape
    args = (oa, ga, u, vc, gb, x, w_out, coef, bias, final_g.reshape(1, -1))
    whole = lambda a: pl.BlockSpec(a.shape, lambda i: (0,) * a.ndim)
    return pl.pallas_call(
        functools.partial(_sample_out_kernel, t_new=t_new),
        grid=(1,),
        in_specs=[whole(a) for a in args],
        out_specs=[pl.BlockSpec((m, d_model), lambda i: (0, 0)), whole(w_out)],
        out_shape=[jax.ShapeDtypeStruct((m, d_model), _F32), jax.ShapeDtypeStruct(w_out.shape, _BF16)],
        scratch_shapes=[pltpu.VMEM((m, u.shape[1]), _BF16)],
        compiler_params=_params("arbitrary"),
        name="sample_out",
    )(*args)


def kernel(x_prompt, x_sample, cache_k, cache_v, norm_g, w_in, ln_g, ln_b, w_s, b_s, w_out, final_g):
    batch, s_len, d_model = x_prompt.shape
    db, t_new, _ = x_sample.shape
    depth, _, wb, n_heads, head_dim = cache_k.shape
    d_attn = n_heads * head_dim
    d_chunk = w_out.shape[1] - d_attn
    assert batch == 1 and depth == 1 and head_dim == HEAD_DIM
    assert w_in.shape[2] == 4 * d_attn + 3 * d_chunk and d_attn == d_chunk
    tail = min(MAX_WINDOW, s_len)
    w_in0, w_out0 = w_in[0], w_out[0]
    ng, lg, lb = norm_g[0], ln_g[0], ln_b[0]

    xs = jnp.transpose(x_sample, (1, 0, 2)).reshape(t_new * db, d_model)
    pos_s = PAST_LEN + jnp.repeat(jnp.arange(t_new, dtype=jnp.int32), db)
    cos_s, sin_s = _rope_tables(pos_s)
    qs, ks, vs, gas, us, vcs, gbs, w_qkv_b, w_gate_b = _sample_in_proj(xs, ng, w_in0, cos_s, sin_s, lg, lb)
    to_b_major = lambda a: jnp.transpose(a.reshape(t_new, db, -1), (1, 0, 2))
    qs_b, ks_b, vs_b, vcs_b = (to_b_major(a) for a in (qs, ks, vs, vcs))
    cache_t = lambda c: jnp.transpose(c[0], (0, 2, 3, 1)).reshape(db, d_attn, wb)
    cache_kt, cache_vt = cache_t(cache_k), cache_t(cache_v)
    assert wb == MAX_WINDOW

    xp = x_prompt.reshape(s_len, d_model)
    tm_qkv = 1024
    tables = _rope_tables_blocked(s_len, tm_qkv)
    xn, q1, q4, q16 = _proj(xp, w_qkv_b, SEG_Q, tm=tm_qkv, norm_g=ng, tables=tables,
                            scale=HEAD_DIM ** -0.5 * math.log2(math.e))
    k1, k4, k16, k_tail_t = _proj(xn, w_qkv_b, SEG_K, tm=tm_qkv, tables=tables, tail_rows=tail)
    v1, v4, v16, v_tail_t = _proj(xn, w_qkv_b, SEG_V, tm=tm_qkv, tail_rows=tail)
    ga, u, vc, gb, oa = _gates_proj(xn, w_gate_b, (lg, lb), (qs_b, ks_b, vs_b, cache_kt, cache_vt), tm=512)
    o1, m1, d1 = _dilated_attention(q1[None], k1[None], v1[None], q_rows=1024)
    o4, m4, d4 = _dilated_attention(q4, k4, v4, q_rows=1024)
    o16, m16, d16 = _dilated_attention(q16, k16, v16, q_rows=512)

    oa_t = jnp.transpose(oa, (1, 0, 2)).reshape(t_new * db, d_attn)
    coef = jnp.repeat(jnp.transpose(w_s[0][:, :t_new, :t_new], (1, 2, 0)).reshape(t_new * t_new, -1),
                      GROUP_WIDTH_B, axis=1)
    bias = jnp.repeat(jnp.transpose(b_s[0][:, :t_new], (1, 0)), GROUP_WIDTH_B, axis=1)
    ys, w_out_b = _sample_out(oa_t, gas, us, vcs, gbs, xs, w_out0, coef, bias, final_g, t_new=t_new)
    y_sample = jnp.transpose(ys.reshape(t_new, db, d_model), (1, 0, 2))

    y_prompt = _out_proj((o1[0], o4, o16), (m1[0], m4, m16), (d1[0], d4, d16), ga, u, vc, gb, xp, w_out_b,
                         w_s[0], b_s[0].T, final_g, tm=512)
    untranspose = lambda a: jnp.transpose(a.reshape(n_heads, head_dim, tail), (2, 0, 1))

    hs = (n_heads, head_dim)
    return (
        y_prompt.reshape(batch, s_len, d_model),
        y_sample,
        untranspose(k_tail_t).reshape(depth, batch, tail, *hs),
        untranspose(v_tail_t).reshape(depth, batch, tail, *hs),
        ks_b.reshape(depth, db, t_new, *hs),
        vs_b.reshape(depth, db, t_new, *hs),
        vcs_b.reshape(depth, db, t_new, d_chunk),
    )
```

```python
import functools
import math

import jax
import jax.numpy as jnp
from jax import lax
from jax.experimental import pallas as pl
from jax.experimental.pallas import tpu as pltpu

HEAD_DIM = 64
BLOCK = 128
CHUNK = 128
GROUP_WIDTH_B = 128
DILATIONS = ((128, 1), (512, 4), (2048, 16))
MAX_WINDOW = 2048
PAST_LEN = 16384
ROPE_THETA = 10000.0
EPS = 1e-6
NEG_INF = -1e30
N_SEGMENTS = 7
SEG_Q, SEG_K, SEG_V, SEG_GA, SEG_U, SEG_VC, SEG_GB = range(N_SEGMENTS)

LANES = 128
VMEM_LIMIT_BYTES = 56 * 1024 * 1024
SUB_ROWS = 256
SAMPLE_HEAD_GROUP_WIDTH = 512

_BF16 = jnp.bfloat16
_F32 = jnp.float32


def _params(*semantics, flags=None):
    return pltpu.CompilerParams(dimension_semantics=semantics, vmem_limit_bytes=VMEM_LIMIT_BYTES, flags=flags)


def _rope_tables(pos):
    half = HEAD_DIM // 2
    inv = jnp.exp(-math.log(ROPE_THETA) * jnp.arange(half, dtype=_F32) / half)
    ang = pos.astype(_F32)[:, None] * inv[None, :]
    cos = jnp.cos(ang)
    sin = jnp.sin(ang)
    cos_t = jnp.concatenate([cos, cos, cos, cos], axis=-1)
    sin_t = jnp.concatenate([-sin, sin, -sin, sin], axis=-1)
    return cos_t, sin_t


def _rope_tables_blocked(s_len, tm):
    half = HEAD_DIM // 2
    inv = jnp.exp(-math.log(ROPE_THETA) * jnp.arange(half, dtype=_F32) / half)
    tile4 = lambda a: jnp.concatenate([a, a, a, a], axis=-1)
    ang_r = jnp.arange(tm, dtype=jnp.int32).astype(_F32)[:, None] * inv[None, :]
    ang_b = (jnp.arange(s_len // tm, dtype=jnp.int32) * tm).astype(_F32)[:, None] * inv[None, :]
    sign = jnp.concatenate([-jnp.ones((1, half), _F32), jnp.ones((1, half), _F32)] * 2, axis=-1)
    return (tile4(jnp.cos(ang_r)), tile4(jnp.sin(ang_r)), tile4(jnp.cos(ang_b)), tile4(jnp.sin(ang_b)), sign)


def _rope_group(zg, cos, sin_signed):
    lane = lax.broadcasted_iota(jnp.int32, zg.shape, 1)
    first_half = (lane % HEAD_DIM) < (HEAD_DIM // 2)
    partner = jnp.where(first_half, pltpu.roll(zg, LANES - HEAD_DIM // 2, 1),
                        pltpu.roll(zg, HEAD_DIM // 2, 1))
    return zg * cos + partner * sin_signed


def _gelu_layer_norm(z, g, b):
    h = jax.nn.gelu(z)
    mu = jnp.mean(h, axis=-1, keepdims=True)
    hc = h - mu
    var = jnp.mean(hc * hc, axis=-1, keepdims=True)
    return hc * lax.rsqrt(var + EPS) * g + b


def _qkv_kernel(x_ref, g_ref, w_ref, cos_r_ref, sin_r_ref, cos_b_ref, sin_b_ref, sign_ref,
                xn_ref, q1_ref, q4_ref, q16_ref, k1_ref, k4_ref, k16_ref, v1_ref, v4_ref, v16_ref,
                ktail_ref, vtail_ref, *slab_refs, q_scale):
    tm = x_ref.shape[0]
    d_seg = q1_ref.shape[1]
    n4, n16 = tm // 4, tm // 16
    xf = x_ref[...]
    ms = jnp.mean(xf * xf, axis=-1, keepdims=True)
    xb = (xf * lax.rsqrt(ms + EPS) * g_ref[...]).astype(_BF16)
    xn_ref[...] = xb
    step = pl.ds(pl.program_id(0), 1)
    cb, sb = cos_b_ref[step, :], sin_b_ref[step, :]
    cr, sr = cos_r_ref[...], sin_r_ref[...]
    cos = cr * cb - sr * sb
    sin_signed = (sr * cb + cr * sb) * sign_ref[...]

    segments = (
        (SEG_Q, (q1_ref, q4_ref, q16_ref), None, q_scale),
        (SEG_K, (k1_ref, k4_ref, k16_ref), ktail_ref, 1.0),
        (SEG_V, (v1_ref, v4_ref, v16_ref), vtail_ref, 1.0),
    )
    for seg, (d1_ref, d4_ref, d16_ref), tail_ref, scale in segments:
        nat_ref, res4_ref = slab_refs[2 * seg], slab_refs[2 * seg + 1]
        z = jnp.dot(xb, w_ref[:, seg * d_seg:(seg + 1) * d_seg], preferred_element_type=_F32)
        for g in range(d_seg // LANES):
            sl = slice(g * LANES, (g + 1) * LANES)
            r = z[:, sl]
            if seg != SEG_V:
                r = _rope_group(r, cos, sin_signed)
            if tail_ref is not None:
                tail_ref[sl, :] = r.T
            if scale != 1.0:
                r = r * scale
            d1_ref[:, sl] = r.astype(d1_ref.dtype)
            nat_ref[g] = r
            for r4 in range(4):
                blk = nat_ref[g, pl.ds(r4, n4, stride=4), :]
                d4_ref[r4, :, sl] = blk.astype(d4_ref.dtype)
                res4_ref[g, r4 * n4:(r4 + 1) * n4, :] = blk
            for r4 in range(4):
                for a in range(4):
                    blk = res4_ref[g, pl.ds(r4 * n4 + a, n16, stride=4), :]
                    d16_ref[r4 + 4 * a, :, sl] = blk.astype(d16_ref.dtype)


def _gates_kernel(xn_ref, w_ref, lng_ref, lnb_ref, q_ref, kn_ref, vn_ref, kt_ref, vt_ref,
                  ga_ref, u_ref, vc_ref, gb_ref, oa_ref, mult_ref):
    j = pl.program_id(1)

    @pl.when(jnp.logical_and(pl.program_id(0) == 0, j == 0))
    def _():
        mult_ref[...] = _key_multiplicity(mult_ref.shape[0], mult_ref.shape[1], q_ref.shape[0])

    tm = xn_ref.shape[0]
    sub = min(SUB_ROWS, tm)
    epilogues = (
        (ga_ref, jax.nn.silu),
        (u_ref, jax.nn.gelu),
        (vc_ref, lambda z: _gelu_layer_norm(z, lng_ref[...], lnb_ref[...])),
        (gb_ref, jax.nn.silu),
    )
    for s, (o_ref, epilogue) in enumerate(epilogues):
        @pl.when(j == s)
        def _(s=s, o_ref=o_ref, epilogue=epilogue):
            _sample_attn_tile(q_ref, kn_ref, vn_ref, kt_ref, vt_ref, oa_ref, mult_ref)
            for t in range(tm // sub):
                rs = slice(t * sub, (t + 1) * sub)
                z = jnp.dot(xn_ref[rs, :], w_ref[s], preferred_element_type=_F32)
                o_ref[rs, :] = epilogue(z).astype(o_ref.dtype)


def _gates_proj(xn, w_gate, ln, side, *, tm):
    s_len, d_model = xn.shape
    n_seg, _, d_seg = w_gate.shape
    sq, skn, svn, skt, svt = side
    db, t_new, d_attn = sq.shape
    width, wb = SAMPLE_HEAD_GROUP_WIDTH, skt.shape[2]
    hg = d_attn // width
    n_i = s_len // tm
    assert n_i * n_seg == db * hg
    tile = lambda i, j: i * n_seg + j
    new = pl.BlockSpec((None, t_new, width), lambda i, j: (tile(i, j) // hg, 0, tile(i, j) % hg))
    cache = pl.BlockSpec((None, width, wb), lambda i, j: (tile(i, j) // hg, tile(i, j) % hg, 0))
    row_vec = pl.BlockSpec((1, d_seg), lambda i, j: (0, 0))
    rows = pl.BlockSpec((tm, d_seg), lambda i, j: (i, 0))
    return pl.pallas_call(
        _gates_kernel,
        grid=(n_i, n_seg),
        in_specs=[pl.BlockSpec((tm, d_model), lambda i, j: (i, 0)),
                  pl.BlockSpec(w_gate.shape, lambda i, j: (0, 0, 0), pipeline_mode=pl.Buffered(1)),
                  row_vec, row_vec, new, new, new, cache, cache],
        out_specs=[rows] * n_seg + [new],
        out_shape=[jax.ShapeDtypeStruct((s_len, d_seg), _BF16)] * n_seg
        + [jax.ShapeDtypeStruct((db, t_new, d_attn), _F32)],
        scratch_shapes=[pltpu.VMEM((t_new * width // HEAD_DIM, wb), _F32)],
        compiler_params=_params("arbitrary", "arbitrary"),
        name="gates_proj",
    )(xn, w_gate, ln[0].reshape(1, -1), ln[1].reshape(1, -1), sq, skn, svn, skt, svt)


def _qkv_proj(x, norm_g, w_qkv, *, tm, q_scale, tail_rows):
    s_len, d_model = x.shape
    d_seg = w_qkv.shape[1] // 3
    n_i = s_len // tm
    tail_start = n_i - tail_rows // tm
    tables = _rope_tables_blocked(s_len, tm)
    const = lambda a: pl.BlockSpec(a.shape, lambda i: (0, 0))
    layouts_shape = [jax.ShapeDtypeStruct((s_len, d_seg), _BF16),
                     jax.ShapeDtypeStruct((4, s_len // 4, d_seg), _BF16),
                     jax.ShapeDtypeStruct((16, s_len // 16, d_seg), _BF16)]
    layouts_spec = [pl.BlockSpec((tm, d_seg), lambda i: (i, 0)),
                    pl.BlockSpec((4, tm // 4, d_seg), lambda i: (0, i, 0)),
                    pl.BlockSpec((16, tm // 16, d_seg), lambda i: (0, i, 0))]
    tail_spec = pl.BlockSpec((d_seg, tm), lambda i: (0, jnp.maximum(i - tail_start, 0)))
    tail_shape = jax.ShapeDtypeStruct((d_seg, tail_rows), _F32)
    return pl.pallas_call(
        functools.partial(_qkv_kernel, q_scale=q_scale),
        grid=(n_i,),
        in_specs=[pl.BlockSpec((tm, d_model), lambda i: (i, 0)), pl.BlockSpec((1, d_model), lambda i: (0, 0)),
                  pl.BlockSpec(w_qkv.shape, lambda i: (0, 0), pipeline_mode=pl.Buffered(1))]
        + [const(t) for t in tables],
        out_specs=[pl.BlockSpec((tm, d_model), lambda i: (i, 0))] + layouts_spec * 3 + [tail_spec] * 2,
        out_shape=[jax.ShapeDtypeStruct((s_len, d_model), _BF16)] + layouts_shape * 3 + [tail_shape] * 2,
        scratch_shapes=[pltpu.VMEM((d_seg // LANES, tm, LANES), _F32)] * 6,
        compiler_params=_params("arbitrary"),
        name="qkv_proj",
    )(x, norm_g.reshape(1, -1), w_qkv, *tables)


def _attn_kernel(q_ref, kp_ref, kc_ref, vp_ref, vc_ref, o_ref, m_ref, den_ref, k0_ref, *vext_refs):
    n = pl.program_id(1)
    first_step = jnp.logical_and(pl.program_id(0) == 0, n == 0)
    n_pairs = q_ref.shape[1] // LANES
    two = 2 * BLOCK

    @pl.when(first_step)
    def _():
        ones = jnp.ones((two, LANES), k0_ref.dtype)
        for vext_ref in vext_refs:
            for hp in range(n_pairs):
                vext_ref[:, (2 * hp + 1) * LANES:(2 * hp + 2) * LANES] = ones

    k0_ref[0:BLOCK, :] = kp_ref[...]
    k0_ref[BLOCK:, :] = kc_ref[0:BLOCK, :]

    qi = lax.broadcasted_iota(jnp.int32, (two, two), 0) % BLOCK
    si = lax.broadcasted_iota(jnp.int32, (two, two), 1)
    dist = qi + BLOCK - si
    band_bias = jnp.where((dist >= 0) & (dist <= BLOCK), 0.0, NEG_INF).astype(_F32)
    lane = lax.broadcasted_iota(jnp.int32, (BLOCK, LANES), 1)
    head0 = lane < HEAD_DIM
    zero = jnp.zeros((BLOCK, LANES), q_ref.dtype)

    for j, vext_ref in enumerate(vext_refs):
        rows = slice(j * BLOCK, (j + 1) * BLOCK)
        prev_keys = slice((j - 1) * BLOCK, (j + 1) * BLOCK)
        bias = band_bias
        if j == 0:
            bias = band_bias + jnp.where(jnp.logical_and(si < BLOCK, n == 0), NEG_INF, 0.0)
        for hp in range(n_pairs):
            sl = slice(hp * LANES, (hp + 1) * LANES)
            ext = slice(2 * hp * LANES, (2 * hp + 1) * LANES)
            if j == 0:
                vext_ref[0:BLOCK, ext] = vp_ref[:, sl]
                vext_ref[BLOCK:, ext] = vc_ref[0:BLOCK, sl]
            else:
                vext_ref[:, ext] = vc_ref[prev_keys, sl]
        m_acc = jnp.zeros((BLOCK, LANES), _F32)
        den_acc = jnp.ones((BLOCK, LANES), _F32)
        for hp in range(n_pairs):
            sl = slice(hp * LANES, (hp + 1) * LANES)
            q2 = q_ref[rows, sl]
            qs = jnp.concatenate([jnp.where(head0, q2, zero), jnp.where(head0, zero, q2)], axis=0)
            k2 = k0_ref[:, sl] if j == 0 else kc_ref[prev_keys, sl]
            s = lax.dot_general(qs, k2, (((1,), (1,)), ((), ())), preferred_element_type=_F32) + bias
            m = jnp.max(s, axis=-1, keepdims=True)
            p = jnp.exp2(s - m).astype(vext_ref.dtype)
            pv = jnp.dot(p, vext_ref[:, 2 * hp * LANES:(2 * hp + 2) * LANES], preferred_element_type=_F32)
            num, den = pv[:, :LANES], pv[:, LANES:]
            o_ref[rows, sl] = jnp.where(head0, num[:BLOCK], num[BLOCK:]).astype(o_ref.dtype)
            m_acc = jnp.where(lane == 2 * hp, m[:BLOCK], jnp.where(lane == 2 * hp + 1, m[BLOCK:], m_acc))
            den_acc = jnp.where(lane == 2 * hp, den[:BLOCK],
                                jnp.where(lane == 2 * hp + 1, den[BLOCK:], den_acc))
        m_ref[rows, :] = m_acc
        den_ref[rows, :] = den_acc


def _dilated_attention(q, k, v, *, q_rows):
    d, sub_len, d_attn = q.shape
    per = q_rows // BLOCK
    cur = pl.BlockSpec((None, q_rows, d_attn), lambda r, n: (r, n, 0))
    prev = pl.BlockSpec((None, BLOCK, d_attn), lambda r, n: (r, jnp.maximum(n * per - 1, 0), 0))
    stat = pl.BlockSpec((None, q_rows, LANES), lambda r, n: (r, n, 0))
    stat_shape = jax.ShapeDtypeStruct((d, sub_len, LANES), _F32)
    return pl.pallas_call(
        _attn_kernel,
        grid=(d, sub_len // q_rows),
        in_specs=[cur, prev, cur, prev, cur],
        out_specs=[cur, stat, stat],
        out_shape=[jax.ShapeDtypeStruct((d, sub_len, d_attn), q.dtype), stat_shape, stat_shape],
        scratch_shapes=[pltpu.VMEM((2 * BLOCK, d_attn), q.dtype)]
        + [pltpu.VMEM((2 * BLOCK, 2 * d_attn), q.dtype)] * per,
        compiler_params=_params("arbitrary", "arbitrary"),
        name=f"dilated_attn_d{d}",
    )(q, k, k, v, v)


def _out_proj_kernel(o1_ref, o4_ref, o16_ref, m1_ref, m4_ref, m16_ref, d1_ref, d4_ref, d16_ref,
                     ga_ref, u_ref, vc_ref, gb_ref, x_ref, wb_ref, ws_ref, bst_ref, fg_ref, y_ref,
                     ac_ref, n4_ref, n16_ref, tmp_ref, st_ref):
    tm, d_attn = o1_ref.shape
    n_lane_groups = d_attn // LANES
    sub = 2 * CHUNK
    n4, n16 = sub // 4, sub // 16

    pair_idx = lax.broadcasted_iota(jnp.int32, (sub, LANES), 1) // HEAD_DIM
    tril = (lax.broadcasted_iota(jnp.int32, (CHUNK, CHUNK), 0)
            >= lax.broadcasted_iota(jnp.int32, (CHUNK, CHUNK), 1))

    for t in range(tm // sub):
        rs = slice(t * sub, (t + 1) * sub)
        for r in range(4):
            dst = pl.ds(t * sub + r, n4, stride=4)
            src = slice(t * n4, (t + 1) * n4)
            st_ref[0, dst, :] = m4_ref[r, src, :]
            st_ref[1, dst, :] = d4_ref[r, src, :]
            for g in range(n_lane_groups):
                n4_ref[g, dst, :] = o4_ref[r, src, g * LANES:(g + 1) * LANES].astype(_F32)
        for r in range(16):
            dst = pl.ds(t * sub + r, n16, stride=16)
            src = slice(t * n16, (t + 1) * n16)
            st_ref[2, dst, :] = m16_ref[r, src, :]
            st_ref[3, dst, :] = d16_ref[r, src, :]
        for g in range(n_lane_groups):
            sl = slice(g * LANES, (g + 1) * LANES)
            for r4 in range(4):
                for a in range(4):
                    blk = o16_ref[r4 + 4 * a, t * n16:(t + 1) * n16, sl].astype(_F32)
                    tmp_ref[g, pl.ds(t * sub + r4 * n4 + a, n16, stride=4), :] = blk
                n16_ref[g, pl.ds(t * sub + r4, n4, stride=4), :] = tmp_ref[g, t * sub + r4 * n4:
                                                                          t * sub + (r4 + 1) * n4, :]

        ms = [m1_ref[rs, :], st_ref[0, rs, :], st_ref[2, rs, :]]
        dens = [d1_ref[rs, :], st_ref[1, rs, :], st_ref[3, rs, :]]
        mx = jnp.maximum(jnp.maximum(ms[0], ms[1]), ms[2])
        es = [jnp.exp2(m - mx) for m in ms]
        tot = es[0] * dens[0] + es[1] * dens[1] + es[2] * dens[2]
        ws = [e / tot for e in es]
        for g in range(n_lane_groups):
            sl = slice(g * LANES, (g + 1) * LANES)
            w1, w4, w16 = (jnp.take_along_axis(w, pair_idx + 2 * g, axis=1) for w in ws)
            a = w1 * o1_ref[rs, sl].astype(_F32) + w4 * n4_ref[g, rs, :] + w16 * n16_ref[g, rs, :]
            ac_ref[rs, sl] = (a * ga_ref[rs, sl].astype(_F32)).astype(ac_ref.dtype)

        c0 = slice(t * sub, t * sub + CHUNK)
        c1 = slice(t * sub + CHUNK, (t + 1) * sub)
        for g in range(ws_ref.shape[0]):
            wm = jnp.where(tril, ws_ref[g], 0.0).astype(_BF16)
            gs = slice(g * GROUP_WIDTH_B, (g + 1) * GROUP_WIDTH_B)
            vc2 = jnp.concatenate([vc_ref[c0, gs], vc_ref[c1, gs]], axis=1)
            mixed = jnp.dot(wm, vc2, preferred_element_type=_F32) + bst_ref[:, g:g + 1]
            for c, half in ((c0, slice(0, GROUP_WIDTH_B)), (c1, slice(GROUP_WIDTH_B, 2 * GROUP_WIDTH_B))):
                cv = u_ref[c, gs].astype(_F32) * mixed[:, half] * gb_ref[c, gs].astype(_F32)
                ac_ref[c, d_attn + g * GROUP_WIDTH_B:d_attn + (g + 1) * GROUP_WIDTH_B] = cv.astype(ac_ref.dtype)

        acc = jnp.dot(ac_ref[rs, :], wb_ref[...], preferred_element_type=_F32)
        xo = x_ref[rs, :] + acc
        msq = jnp.mean(xo * xo, axis=-1, keepdims=True)
        y_ref[rs, :] = xo * lax.rsqrt(msq + EPS) * fg_ref[...]


def _out_proj(os, ms, dens, ga, u, vc, gb, x, w_out, w_s, b_s_t, final_g, *, tm):
    m, d_model = x.shape
    d_attn = ga.shape[1]
    d_chunk = u.shape[1]
    rows = lambda n: pl.BlockSpec((tm, n), lambda i: (i, 0))
    res = lambda d, n: pl.BlockSpec((d, tm // d, n), lambda i: (0, i, 0))
    whole = lambda a: pl.BlockSpec(a.shape, lambda i: (0,) * a.ndim)
    by_pattern = lambda n: [rows(n), res(4, n), res(16, n)]
    fg = final_g.reshape(1, -1)
    slab = pltpu.VMEM((d_attn // LANES, tm, LANES), _F32)
    return pl.pallas_call(
        _out_proj_kernel,
        grid=(m // tm,),
        in_specs=by_pattern(d_attn) + by_pattern(LANES) + by_pattern(LANES)
        + [rows(d_attn), rows(d_chunk), rows(d_chunk), rows(d_chunk), rows(d_model),
           pl.BlockSpec(w_out.shape, lambda i: (0, 0), pipeline_mode=pl.Buffered(1)),
           whole(w_s), whole(b_s_t), whole(fg)],
        out_specs=rows(d_model),
        out_shape=jax.ShapeDtypeStruct((m, d_model), _F32),
        scratch_shapes=[pltpu.VMEM((tm, d_attn + d_chunk), _BF16), slab, slab, slab,
                        pltpu.VMEM((4, tm, LANES), _F32)],
        compiler_params=_params("arbitrary"),
        name="out_proj",
    )(*os, *ms, *dens, ga, u, vc, gb, x, w_out, w_s, b_s_t, fg)


def _sample_in_proj_kernel(x_ref, g_ref, w_ref, cos_ref, sin_ref, lng_ref, lnb_ref,
                           q_ref, k_ref, v_ref, ga_ref, u_ref, vc_ref, gb_ref, wqkv_ref, wgate_ref, xn_ref):
    j = pl.program_id(0)

    @pl.when(j == 0)
    def _():
        xf = x_ref[...]
        ms = jnp.mean(xf * xf, axis=-1, keepdims=True)
        xn_ref[...] = (xf * lax.rsqrt(ms + EPS) * g_ref[...]).astype(xn_ref.dtype)

    wb = w_ref[...].astype(_BF16)

    @pl.when(j <= SEG_V)
    def _():
        wqkv_ref[...] = wb

    @pl.when(j > SEG_V)
    def _():
        wgate_ref[...] = wb

    z = jnp.dot(xn_ref[...], wb, preferred_element_type=_F32)

    def rope_to(ref, scale):
        for g in range(z.shape[1] // LANES):
            sl = slice(g * LANES, (g + 1) * LANES)
            ref[:, sl] = _rope_group(z[:, sl], cos_ref[...], sin_ref[...]) * scale

    @pl.when(j == SEG_Q)
    def _():
        rope_to(q_ref, HEAD_DIM ** -0.5)

    @pl.when(j == SEG_K)
    def _():
        rope_to(k_ref, 1.0)

    @pl.when(j == SEG_V)
    def _():
        v_ref[...] = z

    @pl.when(j == SEG_GA)
    def _():
        ga_ref[...] = jax.nn.silu(z)

    @pl.when(j == SEG_U)
    def _():
        u_ref[...] = jax.nn.gelu(z)

    @pl.when(j == SEG_VC)
    def _():
        vc_ref[...] = _gelu_layer_norm(z, lng_ref[...], lnb_ref[...])

    @pl.when(j == SEG_GB)
    def _():
        gb_ref[...] = jax.nn.silu(z)


def _sample_in_proj(x, norm_g, w_in, cos_t, sin_t, ln_g, ln_b):
    m, d_model = x.shape
    d_seg = w_in.shape[1] // N_SEGMENTS
    whole = lambda r, c: pl.BlockSpec((r, c), lambda j: (0, 0))
    return pl.pallas_call(
        _sample_in_proj_kernel,
        grid=(N_SEGMENTS,),
        in_specs=[whole(m, d_model), whole(1, d_model),
                  pl.BlockSpec((d_model, d_seg), lambda j: (0, j)),
                  whole(m, LANES), whole(m, LANES), whole(1, d_seg), whole(1, d_seg)],
        out_specs=[whole(m, d_seg)] * N_SEGMENTS
        + [pl.BlockSpec((d_model, d_seg), lambda j: (0, jnp.minimum(j, SEG_V))),
           pl.BlockSpec((None, d_model, d_seg), lambda j: (jnp.maximum(j - SEG_GA, 0), 0, 0))],
        out_shape=[jax.ShapeDtypeStruct((m, d_seg), _F32)] * N_SEGMENTS
        + [jax.ShapeDtypeStruct((d_model, (SEG_V + 1) * d_seg), _BF16),
           jax.ShapeDtypeStruct((N_SEGMENTS - SEG_GA, d_model, d_seg), _BF16)],
        scratch_shapes=[pltpu.VMEM((m, d_model), _BF16)],
        compiler_params=_params("arbitrary"),
        name="sample_in_proj",
    )(x, norm_g.reshape(1, -1), w_in, cos_t, sin_t, ln_g.reshape(1, -1), ln_b.reshape(1, -1))


def _key_multiplicity(rows, wb, t_new):
    t_row = lax.broadcasted_iota(jnp.int32, (rows, wb), 0) // (rows // t_new)
    pos = lax.broadcasted_iota(jnp.int32, (rows, wb), 1)
    dist = wb + t_row - pos
    mult = jnp.zeros((rows, wb), _F32)
    for window, d in DILATIONS:
        mult = mult + jnp.where((dist % d == 0) & (dist <= window), 1.0, 0.0)
    return mult


def _sample_attn_tile(q_ref, kn_ref, vn_ref, kt_ref, vt_ref, o_ref, mult_ref):
    t_new, width = q_ref.shape
    n_heads = width // HEAD_DIM
    rows = t_new * n_heads
    head_row = lax.broadcasted_iota(jnp.int32, (n_heads, width), 0)
    head_col = lax.broadcasted_iota(jnp.int32, (n_heads, width), 1) // HEAD_DIM
    own = head_row == head_col
    qbd = jnp.concatenate([jnp.where(own, q_ref[t:t + 1, :], 0.0) for t in range(t_new)], axis=0)
    mult = mult_ref[...]

    s = jnp.dot(qbd.astype(_BF16), kt_ref[...].astype(_BF16), preferred_element_type=_F32)
    s = jnp.where(mult > 0.0, s, NEG_INF)
    m = jnp.max(s, axis=-1, keepdims=True)
    t_col = lax.broadcasted_iota(jnp.int32, (rows, 1), 0) // n_heads
    s_new, mult_new = [], []
    for tp in range(t_new):
        s_new.append(jnp.sum(qbd * kn_ref[tp:tp + 1, :], axis=-1, keepdims=True))
        dn = t_col - tp
        mn = jnp.zeros((rows, 1), _F32)
        for window, d in DILATIONS:
            mn = mn + jnp.where((dn >= 0) & (dn % d == 0), 1.0, 0.0)
        mult_new.append(mn)
        m = jnp.maximum(m, jnp.where(mn > 0.0, s_new[tp], NEG_INF))
    p = jnp.exp(s - m) * mult
    den = jnp.sum(p, axis=-1, keepdims=True)
    num = lax.dot_general(p.astype(_BF16), vt_ref[...].astype(_BF16), (((1,), (1,)), ((), ())),
                          preferred_element_type=_F32)
    for tp in range(t_new):
        pn = jnp.where(mult_new[tp] > 0.0, jnp.exp(s_new[tp] - m), 0.0) * mult_new[tp]
        den = den + pn
        num = num + pn * vn_ref[tp:tp + 1, :]
    res = num / den
    for t in range(t_new):
        blk = res[t * n_heads:(t + 1) * n_heads, :]
        o_ref[t:t + 1, :] = jnp.sum(jnp.where(own, blk, 0.0), axis=0, keepdims=True)


def _sample_out_kernel(oa_ref, ga_ref, u_ref, vc_ref, gb_ref, x_ref, wout_ref, coef_ref, bias_ref, fg_ref,
                       y_ref, wb_ref, c_ref, *, t_new):
    m, d_attn = oa_ref.shape
    db = m // t_new
    wb_ref[...] = wout_ref[...].astype(wb_ref.dtype)
    a = (oa_ref[...] * ga_ref[...]).astype(_BF16)
    for t in range(t_new):
        mixed = jnp.zeros((db, c_ref.shape[1]), _F32) + bias_ref[t:t + 1, :]
        for s in range(t + 1):
            mixed = mixed + coef_ref[t * t_new + s:t * t_new + s + 1, :] * vc_ref[s * db:(s + 1) * db, :]
        rs = slice(t * db, (t + 1) * db)
        c_ref[rs, :] = (u_ref[rs, :] * mixed * gb_ref[rs, :]).astype(c_ref.dtype)
    acc = jnp.dot(a, wb_ref[:d_attn, :], preferred_element_type=_F32)
    acc = acc + jnp.dot(c_ref[...], wb_ref[d_attn:, :], preferred_element_type=_F32)
    xo = x_ref[...] + acc
    ms = jnp.mean(xo * xo, axis=-1, keepdims=True)
    y_ref[...] = xo * lax.rsqrt(ms + EPS) * fg_ref[...]


def _sample_out(oa, ga, u, vc, gb, x, w_out, coef, bias, final_g, *, t_new):
    m, d_model = x.shape
    args = (oa, ga, u, vc, gb, x, w_out, coef, bias, final_g.reshape(1, -1))
    whole = lambda a: pl.BlockSpec(a.shape, lambda i: (0,) * a.ndim)
    return pl.pallas_call(
        functools.partial(_sample_out_kernel, t_new=t_new),
        grid=(1,),
        in_specs=[whole(a) for a in args],
        out_specs=[pl.BlockSpec((m, d_model), lambda i: (0, 0)), whole(w_out)],
        out_shape=[jax.ShapeDtypeStruct((m, d_model), _F32), jax.ShapeDtypeStruct(w_out.shape, _BF16)],
        scratch_shapes=[pltpu.VMEM((m, u.shape[1]), _BF16)],
        compiler_params=_params("arbitrary"),
        name="sample_out",
    )(*args)


def kernel(x_prompt, x_sample, cache_k, cache_v, norm_g, w_in, ln_g, ln_b, w_s, b_s, w_out, final_g):
    batch, s_len, d_model = x_prompt.shape
    db, t_new, _ = x_sample.shape
    depth, _, wb, n_heads, head_dim = cache_k.shape
    d_attn = n_heads * head_dim
    d_chunk = w_out.shape[1] - d_attn
    assert batch == 1 and depth == 1 and head_dim == HEAD_DIM
    assert w_in.shape[2] == 4 * d_attn + 3 * d_chunk and d_attn == d_chunk
    tail = min(MAX_WINDOW, s_len)
    w_in0, w_out0 = w_in[0], w_out[0]
    ng, lg, lb = norm_g[0], ln_g[0], ln_b[0]

    xs = jnp.transpose(x_sample, (1, 0, 2)).reshape(t_new * db, d_model)
    pos_s = PAST_LEN + jnp.repeat(jnp.arange(t_new, dtype=jnp.int32), db)
    cos_s, sin_s = _rope_tables(pos_s)
    qs, ks, vs, gas, us, vcs, gbs, w_qkv_b, w_gate_b = _sample_in_proj(xs, ng, w_in0, cos_s, sin_s, lg, lb)
    to_b_major = lambda a: jnp.transpose(a.reshape(t_new, db, -1), (1, 0, 2))
    qs_b, ks_b, vs_b, vcs_b = (to_b_major(a) for a in (qs, ks, vs, vcs))
    cache_t = lambda c: jnp.transpose(c[0], (0, 2, 3, 1)).reshape(db, d_attn, wb)
    cache_kt, cache_vt = cache_t(cache_k), cache_t(cache_v)
    assert wb == MAX_WINDOW

    xp = x_prompt.reshape(s_len, d_model)
    (xn, q1, q4, q16, k1, k4, k16, v1, v4, v16, k_tail_t, v_tail_t) = _qkv_proj(
        xp, ng, w_qkv_b, tm=256, q_scale=HEAD_DIM ** -0.5 * math.log2(math.e), tail_rows=tail)
    ga, u, vc, gb, oa = _gates_proj(xn, w_gate_b, (lg, lb), (qs_b, ks_b, vs_b, cache_kt, cache_vt), tm=512)
    o1, m1, d1 = _dilated_attention(q1[None], k1[None], v1[None], q_rows=1024)
    o4, m4, d4 = _dilated_attention(q4, k4, v4, q_rows=1024)
    o16, m16, d16 = _dilated_attention(q16, k16, v16, q_rows=512)

    oa_t = jnp.transpose(oa, (1, 0, 2)).reshape(t_new * db, d_attn)
    coef = jnp.repeat(jnp.transpose(w_s[0][:, :t_new, :t_new], (1, 2, 0)).reshape(t_new * t_new, -1),
                      GROUP_WIDTH_B, axis=1)
    bias = jnp.repeat(jnp.transpose(b_s[0][:, :t_new], (1, 0)), GROUP_WIDTH_B, axis=1)
    ys, w_out_b = _sample_out(oa_t, gas, us, vcs, gbs, xs, w_out0, coef, bias, final_g, t_new=t_new)
    y_sample = jnp.transpose(ys.reshape(t_new, db, d_model), (1, 0, 2))

    y_prompt = _out_proj((o1[0], o4, o16), (m1[0], m4, m16), (d1[0], d4, d16), ga, u, vc, gb, xp, w_out_b,
                         w_s[0], b_s[0].T, final_g, tm=512)
    untranspose = lambda a: jnp.transpose(a.reshape(n_heads, head_dim, tail), (2, 0, 1))

    hs = (n_heads, head_dim)
    return (
        y_prompt.reshape(batch, s_len, d_model),
        y_sample,
        untranspose(k_tail_t).reshape(depth, batch, tail, *hs),
        untranspose(v_tail_t).reshape(depth, batch, tail, *hs),
        ks_b.reshape(depth, db, t_new, *hs),
        vs_b.reshape(depth, db, t_new, *hs),
        vcs_b.reshape(depth, db, t_new, d_chunk),
    )
```

```python
import functools
import math

import jax
import jax.numpy as jnp
from jax import lax
from jax.experimental import pallas as pl
from jax.experimental.pallas import tpu as pltpu

HEAD_DIM = 64
BLOCK = 128
CHUNK = 128
GROUP_WIDTH_B = 128
DILATIONS = ((128, 1), (512, 4), (2048, 16))
MAX_WINDOW = 2048
PAST_LEN = 16384
ROPE_THETA = 10000.0
EPS = 1e-6
NEG_INF = -1e30
N_SEGMENTS = 7
SEG_Q, SEG_K, SEG_V, SEG_GA, SEG_U, SEG_VC, SEG_GB = range(N_SEGMENTS)

LANES = 128
VMEM_LIMIT_BYTES = 56 * 1024 * 1024
SUB_ROWS = 256
SAMPLE_HEAD_GROUP_WIDTH = 512

_BF16 = jnp.bfloat16
_F32 = jnp.float32


def _params(*semantics, flags=None):
    return pltpu.CompilerParams(dimension_semantics=semantics, vmem_limit_bytes=VMEM_LIMIT_BYTES, flags=flags)


def _rope_tables(pos):
    half = HEAD_DIM // 2
    inv = jnp.exp(-math.log(ROPE_THETA) * jnp.arange(half, dtype=_F32) / half)
    ang = pos.astype(_F32)[:, None] * inv[None, :]
    cos = jnp.cos(ang)
    sin = jnp.sin(ang)
    cos_t = jnp.concatenate([cos, cos, cos, cos], axis=-1)
    sin_t = jnp.concatenate([-sin, sin, -sin, sin], axis=-1)
    return cos_t, sin_t


def _rope_tables_blocked(s_len, tm):
    half = HEAD_DIM // 2
    inv = jnp.exp(-math.log(ROPE_THETA) * jnp.arange(half, dtype=_F32) / half)
    tile4 = lambda a: jnp.concatenate([a, a, a, a], axis=-1)
    ang_r = jnp.arange(tm, dtype=jnp.int32).astype(_F32)[:, None] * inv[None, :]
    ang_b = (jnp.arange(s_len // tm, dtype=jnp.int32) * tm).astype(_F32)[:, None] * inv[None, :]
    sign = jnp.concatenate([-jnp.ones((1, half), _F32), jnp.ones((1, half), _F32)] * 2, axis=-1)
    return (tile4(jnp.cos(ang_r)), tile4(jnp.sin(ang_r)), tile4(jnp.cos(ang_b)), tile4(jnp.sin(ang_b)), sign)


def _rope_group(zg, cos, sin_signed):
    lane = lax.broadcasted_iota(jnp.int32, zg.shape, 1)
    first_half = (lane % HEAD_DIM) < (HEAD_DIM // 2)
    partner = jnp.where(first_half, pltpu.roll(zg, LANES - HEAD_DIM // 2, 1),
                        pltpu.roll(zg, HEAD_DIM // 2, 1))
    return zg * cos + partner * sin_signed


def _gelu_layer_norm(z, g, b):
    h = jax.nn.gelu(z)
    mu = jnp.mean(h, axis=-1, keepdims=True)
    hc = h - mu
    var = jnp.mean(hc * hc, axis=-1, keepdims=True)
    return hc * lax.rsqrt(var + EPS) * g + b


def _qkv_kernel(x_ref, g_ref, w_ref, cos_r_ref, sin_r_ref, cos_b_ref, sin_b_ref, sign_ref,
                xn_ref, q1_ref, q4_ref, q16_ref, k1_ref, k4_ref, k16_ref, v1_ref, v4_ref, v16_ref,
                ktail_ref, vtail_ref, *slab_refs, q_scale):
    tm = x_ref.shape[0]
    d_seg = q1_ref.shape[1]
    n4, n16 = tm // 4, tm // 16
    xf = x_ref[...]
    ms = jnp.mean(xf * xf, axis=-1, keepdims=True)
    xb = (xf * lax.rsqrt(ms + EPS) * g_ref[...]).astype(_BF16)
    xn_ref[...] = xb
    step = pl.ds(pl.program_id(0), 1)
    cb, sb = cos_b_ref[step, :], sin_b_ref[step, :]
    cr, sr = cos_r_ref[...], sin_r_ref[...]
    cos = cr * cb - sr * sb
    sin_signed = (sr * cb + cr * sb) * sign_ref[...]

    segments = (
        (SEG_Q, (q1_ref, q4_ref, q16_ref), None, q_scale),
        (SEG_K, (k1_ref, k4_ref, k16_ref), ktail_ref, 1.0),
        (SEG_V, (v1_ref, v4_ref, v16_ref), vtail_ref, 1.0),
    )
    for seg, (d1_ref, d4_ref, d16_ref), tail_ref, scale in segments:
        nat_ref, res4_ref = slab_refs[2 * seg], slab_refs[2 * seg + 1]
        z = jnp.dot(xb, w_ref[:, seg * d_seg:(seg + 1) * d_seg], preferred_element_type=_F32)
        for g in range(d_seg // LANES):
            sl = slice(g * LANES, (g + 1) * LANES)
            r = z[:, sl]
            if seg != SEG_V:
                r = _rope_group(r, cos, sin_signed)
            if tail_ref is not None:
                tail_ref[sl, :] = r.T
            if scale != 1.0:
                r = r * scale
            d1_ref[:, sl] = r.astype(d1_ref.dtype)
            nat_ref[g] = r
            for r4 in range(4):
                blk = nat_ref[g, pl.ds(r4, n4, stride=4), :]
                d4_ref[r4, :, sl] = blk.astype(d4_ref.dtype)
                res4_ref[g, r4 * n4:(r4 + 1) * n4, :] = blk
            for r4 in range(4):
                for a in range(4):
                    blk = res4_ref[g, pl.ds(r4 * n4 + a, n16, stride=4), :]
                    d16_ref[r4 + 4 * a, :, sl] = blk.astype(d16_ref.dtype)


def _gates_kernel(xn_ref, w_ref, lng_ref, lnb_ref, q_ref, kn_ref, vn_ref, kt_ref, vt_ref,
                  ga_ref, u_ref, vc_ref, gb_ref, oa_ref, mult_ref):
    j = pl.program_id(1)

    @pl.when(jnp.logical_and(pl.program_id(0) == 0, j == 0))
    def _():
        mult_ref[...] = _key_multiplicity(mult_ref.shape[0], mult_ref.shape[1], q_ref.shape[0])

    tm = xn_ref.shape[0]
    sub = min(SUB_ROWS, tm)
    epilogues = (
        (ga_ref, jax.nn.silu),
        (u_ref, jax.nn.gelu),
        (vc_ref, lambda z: _gelu_layer_norm(z, lng_ref[...], lnb_ref[...])),
        (gb_ref, jax.nn.silu),
    )
    for s, (o_ref, epilogue) in enumerate(epilogues):
        @pl.when(j == s)
        def _(s=s, o_ref=o_ref, epilogue=epilogue):
            side = _sample_attn_tile(q_ref, kn_ref, vn_ref, kt_ref, vt_ref, oa_ref, mult_ref)
            for t in range(tm // sub):
                rs = slice(t * sub, (t + 1) * sub)
                z = jnp.dot(xn_ref[rs, :], w_ref[s], preferred_element_type=_F32)
                next(side, None)
                o_ref[rs, :] = epilogue(z).astype(o_ref.dtype)
            for _ in side:
                pass


def _gates_proj(xn, w_gate, ln, side, *, tm):
    s_len, d_model = xn.shape
    n_seg, _, d_seg = w_gate.shape
    sq, skn, svn, skt, svt = side
    db, t_new, d_attn = sq.shape
    width, wb = SAMPLE_HEAD_GROUP_WIDTH, skt.shape[2]
    hg = d_attn // width
    n_i = s_len // tm
    assert n_i * n_seg == db * hg
    tile = lambda i, j: i * n_seg + j
    new = pl.BlockSpec((None, t_new, width), lambda i, j: (tile(i, j) // hg, 0, tile(i, j) % hg))
    cache = pl.BlockSpec((None, width, wb), lambda i, j: (tile(i, j) // hg, tile(i, j) % hg, 0))
    row_vec = pl.BlockSpec((1, d_seg), lambda i, j: (0, 0))
    rows = pl.BlockSpec((tm, d_seg), lambda i, j: (i, 0))
    return pl.pallas_call(
        _gates_kernel,
        grid=(n_i, n_seg),
        in_specs=[pl.BlockSpec((tm, d_model), lambda i, j: (i, 0)),
                  pl.BlockSpec(w_gate.shape, lambda i, j: (0, 0, 0), pipeline_mode=pl.Buffered(1)),
                  row_vec, row_vec, new, new, new, cache, cache],
        out_specs=[rows] * n_seg + [new],
        out_shape=[jax.ShapeDtypeStruct((s_len, d_seg), _BF16)] * n_seg
        + [jax.ShapeDtypeStruct((db, t_new, d_attn), _F32)],
        scratch_shapes=[pltpu.VMEM((t_new * width // HEAD_DIM, wb), _F32)],
        compiler_params=_params("arbitrary", "arbitrary"),
        name="gates_proj",
    )(xn, w_gate, ln[0].reshape(1, -1), ln[1].reshape(1, -1), sq, skn, svn, skt, svt)


def _qkv_proj(x, norm_g, w_qkv, *, tm, q_scale, tail_rows):
    s_len, d_model = x.shape
    d_seg = w_qkv.shape[1] // 3
    n_i = s_len // tm
    tail_start = n_i - tail_rows // tm
    tables = _rope_tables_blocked(s_len, tm)
    const = lambda a: pl.BlockSpec(a.shape, lambda i: (0, 0))
    layouts_shape = [jax.ShapeDtypeStruct((s_len, d_seg), _BF16),
                     jax.ShapeDtypeStruct((4, s_len // 4, d_seg), _BF16),
                     jax.ShapeDtypeStruct((16, s_len // 16, d_seg), _BF16)]
    layouts_spec = [pl.BlockSpec((tm, d_seg), lambda i: (i, 0)),
                    pl.BlockSpec((4, tm // 4, d_seg), lambda i: (0, i, 0)),
                    pl.BlockSpec((16, tm // 16, d_seg), lambda i: (0, i, 0))]
    tail_spec = pl.BlockSpec((d_seg, tm), lambda i: (0, jnp.maximum(i - tail_start, 0)))
    tail_shape = jax.ShapeDtypeStruct((d_seg, tail_rows), _F32)
    return pl.pallas_call(
        functools.partial(_qkv_kernel, q_scale=q_scale),
        grid=(n_i,),
        in_specs=[pl.BlockSpec((tm, d_model), lambda i: (i, 0)), pl.BlockSpec((1, d_model), lambda i: (0, 0)),
                  pl.BlockSpec(w_qkv.shape, lambda i: (0, 0), pipeline_mode=pl.Buffered(1))]
        + [const(t) for t in tables],
        out_specs=[pl.BlockSpec((tm, d_model), lambda i: (i, 0))] + layouts_spec * 3 + [tail_spec] * 2,
        out_shape=[jax.ShapeDtypeStruct((s_len, d_model), _BF16)] + layouts_shape * 3 + [tail_shape] * 2,
        scratch_shapes=[pltpu.VMEM((d_seg // LANES, tm, LANES), _F32)] * 6,
        compiler_params=_params("arbitrary"),
        name="qkv_proj",
    )(x, norm_g.reshape(1, -1), w_qkv, *tables)


def _attn_kernel(q_ref, kp_ref, kc_ref, vp_ref, vc_ref, o_ref, m_ref, den_ref, k0_ref, *vext_refs):
    n = pl.program_id(1)
    first_step = jnp.logical_and(pl.program_id(0) == 0, n == 0)
    n_pairs = q_ref.shape[1] // LANES
    two = 2 * BLOCK

    @pl.when(first_step)
    def _():
        ones = jnp.ones((two, LANES), k0_ref.dtype)
        for vext_ref in vext_refs:
            for hp in range(n_pairs):
                vext_ref[:, (2 * hp + 1) * LANES:(2 * hp + 2) * LANES] = ones

    k0_ref[0:BLOCK, :] = kp_ref[...]
    k0_ref[BLOCK:, :] = kc_ref[0:BLOCK, :]

    qi = lax.broadcasted_iota(jnp.int32, (two, two), 0) % BLOCK
    si = lax.broadcasted_iota(jnp.int32, (two, two), 1)
    dist = qi + BLOCK - si
    band_bias = jnp.where((dist >= 0) & (dist <= BLOCK), 0.0, NEG_INF).astype(_F32)
    lane = lax.broadcasted_iota(jnp.int32, (BLOCK, LANES), 1)
    head0 = lane < HEAD_DIM
    zero = jnp.zeros((BLOCK, LANES), q_ref.dtype)

    for j, vext_ref in enumerate(vext_refs):
        rows = slice(j * BLOCK, (j + 1) * BLOCK)
        prev_keys = slice((j - 1) * BLOCK, (j + 1) * BLOCK)
        bias = band_bias
        if j == 0:
            bias = band_bias + jnp.where(jnp.logical_and(si < BLOCK, n == 0), NEG_INF, 0.0)
        for hp in range(n_pairs):
            sl = slice(hp * LANES, (hp + 1) * LANES)
            ext = slice(2 * hp * LANES, (2 * hp + 1) * LANES)
            if j == 0:
                vext_ref[0:BLOCK, ext] = vp_ref[:, sl]
                vext_ref[BLOCK:, ext] = vc_ref[0:BLOCK, sl]
            else:
                vext_ref[:, ext] = vc_ref[prev_keys, sl]
        m_acc = jnp.zeros((BLOCK, LANES), _F32)
        den_acc = jnp.ones((BLOCK, LANES), _F32)
        for hp in range(n_pairs):
            sl = slice(hp * LANES, (hp + 1) * LANES)
            q2 = q_ref[rows, sl]
            qs = jnp.concatenate([jnp.where(head0, q2, zero), jnp.where(head0, zero, q2)], axis=0)
            k2 = k0_ref[:, sl] if j == 0 else kc_ref[prev_keys, sl]
            s = lax.dot_general(qs, k2, (((1,), (1,)), ((), ())), preferred_element_type=_F32) + bias
            m = jnp.max(s, axis=-1, keepdims=True)
            p = jnp.exp2(s - m).astype(vext_ref.dtype)
            pv = jnp.dot(p, vext_ref[:, 2 * hp * LANES:(2 * hp + 2) * LANES], preferred_element_type=_F32)
            num, den = pv[:, :LANES], pv[:, LANES:]
            o_ref[rows, sl] = jnp.where(head0, num[:BLOCK], num[BLOCK:]).astype(o_ref.dtype)
            m_acc = jnp.where(lane == 2 * hp, m[:BLOCK], jnp.where(lane == 2 * hp + 1, m[BLOCK:], m_acc))
            den_acc = jnp.where(lane == 2 * hp, den[:BLOCK],
                                jnp.where(lane == 2 * hp + 1, den[BLOCK:], den_acc))
        m_ref[rows, :] = m_acc
        den_ref[rows, :] = den_acc


def _dilated_attention(q, k, v, *, q_rows):
    d, sub_len, d_attn = q.shape
    per = q_rows // BLOCK
    cur = pl.BlockSpec((None, q_rows, d_attn), lambda r, n: (r, n, 0))
    prev = pl.BlockSpec((None, BLOCK, d_attn), lambda r, n: (r, jnp.maximum(n * per - 1, 0), 0))
    stat = pl.BlockSpec((None, q_rows, LANES), lambda r, n: (r, n, 0))
    stat_shape = jax.ShapeDtypeStruct((d, sub_len, LANES), _F32)
    return pl.pallas_call(
        _attn_kernel,
        grid=(d, sub_len // q_rows),
        in_specs=[cur, prev, cur, prev, cur],
        out_specs=[cur, stat, stat],
        out_shape=[jax.ShapeDtypeStruct((d, sub_len, d_attn), q.dtype), stat_shape, stat_shape],
        scratch_shapes=[pltpu.VMEM((2 * BLOCK, d_attn), q.dtype)]
        + [pltpu.VMEM((2 * BLOCK, 2 * d_attn), q.dtype)] * per,
        compiler_params=_params("arbitrary", "arbitrary"),
        name=f"dilated_attn_d{d}",
    )(q, k, k, v, v)


def _out_proj_kernel(o1_ref, o4_ref, o16_ref, m1_ref, m4_ref, m16_ref, d1_ref, d4_ref, d16_ref,
                     ga_ref, u_ref, vc_ref, gb_ref, x_ref, wb_ref, ws_ref, bst_ref, fg_ref, y_ref,
                     ac_ref, n4_ref, n16_ref, tmp_ref, st_ref):
    tm, d_attn = o1_ref.shape
    n_lane_groups = d_attn // LANES
    sub = 2 * CHUNK
    n4, n16 = sub // 4, sub // 16

    pair_idx = lax.broadcasted_iota(jnp.int32, (sub, LANES), 1) // HEAD_DIM
    tril = (lax.broadcasted_iota(jnp.int32, (CHUNK, CHUNK), 0)
            >= lax.broadcasted_iota(jnp.int32, (CHUNK, CHUNK), 1))

    for t in range(tm // sub):
        rs = slice(t * sub, (t + 1) * sub)
        for r in range(4):
            dst = pl.ds(t * sub + r, n4, stride=4)
            src = slice(t * n4, (t + 1) * n4)
            st_ref[0, dst, :] = m4_ref[r, src, :]
            st_ref[1, dst, :] = d4_ref[r, src, :]
            for g in range(n_lane_groups):
                n4_ref[g, dst, :] = o4_ref[r, src, g * LANES:(g + 1) * LANES].astype(_F32)
        for r in range(16):
            dst = pl.ds(t * sub + r, n16, stride=16)
            src = slice(t * n16, (t + 1) * n16)
            st_ref[2, dst, :] = m16_ref[r, src, :]
            st_ref[3, dst, :] = d16_ref[r, src, :]
        for g in range(n_lane_groups):
            sl = slice(g * LANES, (g + 1) * LANES)
            for r4 in range(4):
                for a in range(4):
                    blk = o16_ref[r4 + 4 * a, t * n16:(t + 1) * n16, sl].astype(_F32)
                    tmp_ref[g, pl.ds(t * sub + r4 * n4 + a, n16, stride=4), :] = blk
                n16_ref[g, pl.ds(t * sub + r4, n4, stride=4), :] = tmp_ref[g, t * sub + r4 * n4:
                                                                          t * sub + (r4 + 1) * n4, :]

        ms = [m1_ref[rs, :], st_ref[0, rs, :], st_ref[2, rs, :]]
        dens = [d1_ref[rs, :], st_ref[1, rs, :], st_ref[3, rs, :]]
        mx = jnp.maximum(jnp.maximum(ms[0], ms[1]), ms[2])
        es = [jnp.exp2(m - mx) for m in ms]
        tot = es[0] * dens[0] + es[1] * dens[1] + es[2] * dens[2]
        ws = [e / tot for e in es]
        for g in range(n_lane_groups):
            sl = slice(g * LANES, (g + 1) * LANES)
            w1, w4, w16 = (jnp.take_along_axis(w, pair_idx + 2 * g, axis=1) for w in ws)
            a = w1 * o1_ref[rs, sl].astype(_F32) + w4 * n4_ref[g, rs, :] + w16 * n16_ref[g, rs, :]
            ac_ref[rs, sl] = (a * ga_ref[rs, sl].astype(_F32)).astype(ac_ref.dtype)

        c0 = slice(t * sub, t * sub + CHUNK)
        c1 = slice(t * sub + CHUNK, (t + 1) * sub)
        for g in range(ws_ref.shape[0]):
            wm = jnp.where(tril, ws_ref[g], 0.0).astype(_BF16)
            gs = slice(g * GROUP_WIDTH_B, (g + 1) * GROUP_WIDTH_B)
            vc2 = jnp.concatenate([vc_ref[c0, gs], vc_ref[c1, gs]], axis=1)
            mixed = jnp.dot(wm, vc2, preferred_element_type=_F32) + bst_ref[:, g:g + 1]
            for c, half in ((c0, slice(0, GROUP_WIDTH_B)), (c1, slice(GROUP_WIDTH_B, 2 * GROUP_WIDTH_B))):
                cv = u_ref[c, gs].astype(_F32) * mixed[:, half] * gb_ref[c, gs].astype(_F32)
                ac_ref[c, d_attn + g * GROUP_WIDTH_B:d_attn + (g + 1) * GROUP_WIDTH_B] = cv.astype(ac_ref.dtype)

    for t in range(tm // sub):
        rs = slice(t * sub, (t + 1) * sub)
        acc = jnp.dot(ac_ref[rs, :], wb_ref[...], preferred_element_type=_F32)
        xo = x_ref[rs, :] + acc
        msq = jnp.mean(xo * xo, axis=-1, keepdims=True)
        y_ref[rs, :] = xo * lax.rsqrt(msq + EPS) * fg_ref[...]


def _out_proj(os, ms, dens, ga, u, vc, gb, x, w_out, w_s, b_s_t, final_g, *, tm):
    m, d_model = x.shape
    d_attn = ga.shape[1]
    d_chunk = u.shape[1]
    rows = lambda n: pl.BlockSpec((tm, n), lambda i: (i, 0))
    res = lambda d, n: pl.BlockSpec((d, tm // d, n), lambda i: (0, i, 0))
    whole = lambda a: pl.BlockSpec(a.shape, lambda i: (0,) * a.ndim)
    by_pattern = lambda n: [rows(n), res(4, n), res(16, n)]
    fg = final_g.reshape(1, -1)
    slab = pltpu.VMEM((d_attn // LANES, tm, LANES), _F32)
    return pl.pallas_call(
        _out_proj_kernel,
        grid=(m // tm,),
        in_specs=by_pattern(d_attn) + by_pattern(LANES) + by_pattern(LANES)
        + [rows(d_attn), rows(d_chunk), rows(d_chunk), rows(d_chunk), rows(d_model),
           pl.BlockSpec(w_out.shape, lambda i: (0, 0), pipeline_mode=pl.Buffered(1)),
           whole(w_s), whole(b_s_t), whole(fg)],
        out_specs=rows(d_model),
        out_shape=jax.ShapeDtypeStruct((m, d_model), _F32),
        scratch_shapes=[pltpu.VMEM((tm, d_attn + d_chunk), _BF16), slab, slab, slab,
                        pltpu.VMEM((4, tm, LANES), _F32)],
        compiler_params=_params("arbitrary"),
        name="out_proj",
    )(*os, *ms, *dens, ga, u, vc, gb, x, w_out, w_s, b_s_t, fg)


def _sample_in_proj_kernel(x_ref, g_ref, w_ref, cos_ref, sin_ref, lng_ref, lnb_ref,
                           q_ref, k_ref, v_ref, ga_ref, u_ref, vc_ref, gb_ref, wqkv_ref, wgate_ref, xn_ref):
    j = pl.program_id(0)

    @pl.when(j == 0)
    def _():
        xf = x_ref[...]
        ms = jnp.mean(xf * xf, axis=-1, keepdims=True)
        xn_ref[...] = (xf * lax.rsqrt(ms + EPS) * g_ref[...]).astype(xn_ref.dtype)

    wb = w_ref[...].astype(_BF16)

    @pl.when(j <= SEG_V)
    def _():
        wqkv_ref[...] = wb

    @pl.when(j > SEG_V)
    def _():
        wgate_ref[...] = wb

    z = jnp.dot(xn_ref[...], wb, preferred_element_type=_F32)

    def rope_to(ref, scale):
        for g in range(z.shape[1] // LANES):
            sl = slice(g * LANES, (g + 1) * LANES)
            ref[:, sl] = _rope_group(z[:, sl], cos_ref[...], sin_ref[...]) * scale

    @pl.when(j == SEG_Q)
    def _():
        rope_to(q_ref, HEAD_DIM ** -0.5)

    @pl.when(j == SEG_K)
    def _():
        rope_to(k_ref, 1.0)

    @pl.when(j == SEG_V)
    def _():
        v_ref[...] = z

    @pl.when(j == SEG_GA)
    def _():
        ga_ref[...] = jax.nn.silu(z)

    @pl.when(j == SEG_U)
    def _():
        u_ref[...] = jax.nn.gelu(z)

    @pl.when(j == SEG_VC)
    def _():
        vc_ref[...] = _gelu_layer_norm(z, lng_ref[...], lnb_ref[...])

    @pl.when(j == SEG_GB)
    def _():
        gb_ref[...] = jax.nn.silu(z)


def _sample_in_proj(x, norm_g, w_in, cos_t, sin_t, ln_g, ln_b):
    m, d_model = x.shape
    d_seg = w_in.shape[1] // N_SEGMENTS
    whole = lambda r, c: pl.BlockSpec((r, c), lambda j: (0, 0))
    return pl.pallas_call(
        _sample_in_proj_kernel,
        grid=(N_SEGMENTS,),
        in_specs=[whole(m, d_model), whole(1, d_model),
                  pl.BlockSpec((d_model, d_seg), lambda j: (0, j)),
                  whole(m, LANES), whole(m, LANES), whole(1, d_seg), whole(1, d_seg)],
        out_specs=[whole(m, d_seg)] * N_SEGMENTS
        + [pl.BlockSpec((d_model, d_seg), lambda j: (0, jnp.minimum(j, SEG_V))),
           pl.BlockSpec((None, d_model, d_seg), lambda j: (jnp.maximum(j - SEG_GA, 0), 0, 0))],
        out_shape=[jax.ShapeDtypeStruct((m, d_seg), _F32)] * N_SEGMENTS
        + [jax.ShapeDtypeStruct((d_model, (SEG_V + 1) * d_seg), _BF16),
           jax.ShapeDtypeStruct((N_SEGMENTS - SEG_GA, d_model, d_seg), _BF16)],
        scratch_shapes=[pltpu.VMEM((m, d_model), _BF16)],
        compiler_params=_params("arbitrary"),
        name="sample_in_proj",
    )(x, norm_g.reshape(1, -1), w_in, cos_t, sin_t, ln_g.reshape(1, -1), ln_b.reshape(1, -1))


def _key_multiplicity(rows, wb, t_new):
    t_row = lax.broadcasted_iota(jnp.int32, (rows, wb), 0) // (rows // t_new)
    pos = lax.broadcasted_iota(jnp.int32, (rows, wb), 1)
    dist = wb + t_row - pos
    mult = jnp.zeros((rows, wb), _F32)
    for window, d in DILATIONS:
        mult = mult + jnp.where((dist % d == 0) & (dist <= window), 1.0, 0.0)
    return mult


def _sample_attn_tile(q_ref, kn_ref, vn_ref, kt_ref, vt_ref, o_ref, mult_ref):
    t_new, width = q_ref.shape
    n_heads = width // HEAD_DIM
    rows = t_new * n_heads
    head_row = lax.broadcasted_iota(jnp.int32, (n_heads, width), 0)
    head_col = lax.broadcasted_iota(jnp.int32, (n_heads, width), 1) // HEAD_DIM
    own = head_row == head_col
    qbd = jnp.concatenate([jnp.where(own, q_ref[t:t + 1, :], 0.0) for t in range(t_new)], axis=0)
    mult = mult_ref[...]

    s = jnp.dot(qbd.astype(_BF16), kt_ref[...].astype(_BF16), preferred_element_type=_F32)
    s = jnp.where(mult > 0.0, s, NEG_INF)
    m = jnp.max(s, axis=-1, keepdims=True)
    t_col = lax.broadcasted_iota(jnp.int32, (rows, 1), 0) // n_heads
    s_new, mult_new = [], []
    for tp in range(t_new):
        s_new.append(jnp.sum(qbd * kn_ref[tp:tp + 1, :], axis=-1, keepdims=True))
        dn = t_col - tp
        mn = jnp.zeros((rows, 1), _F32)
        for window, d in DILATIONS:
            mn = mn + jnp.where((dn >= 0) & (dn % d == 0), 1.0, 0.0)
        mult_new.append(mn)
        m = jnp.maximum(m, jnp.where(mn > 0.0, s_new[tp], NEG_INF))
    p = jnp.exp(s - m) * mult
    den = jnp.sum(p, axis=-1, keepdims=True)
    yield
    num = lax.dot_general(p.astype(_BF16), vt_ref[...].astype(_BF16), (((1,), (1,)), ((), ())),
                          preferred_element_type=_F32)
    for tp in range(t_new):
        pn = jnp.where(mult_new[tp] > 0.0, jnp.exp(s_new[tp] - m), 0.0) * mult_new[tp]
        den = den + pn
        num = num + pn * vn_ref[tp:tp + 1, :]
    res = num / den
    for t in range(t_new):
        blk = res[t * n_heads:(t + 1) * n_heads, :]
        o_ref[t:t + 1, :] = jnp.sum(jnp.where(own, blk, 0.0), axis=0, keepdims=True)


def _sample_out_kernel(oa_ref, ga_ref, u_ref, vc_ref, gb_ref, x_ref, wout_ref, coef_ref, bias_ref, fg_ref,
                       y_ref, wb_ref, c_ref, *, t_new):
    m, d_attn = oa_ref.shape
    db = m // t_new
    wb_ref[...] = wout_ref[...].astype(wb_ref.dtype)
    a = (oa_ref[...] * ga_ref[...]).astype(_BF16)
    for t in range(t_new):
        mixed = jnp.zeros((db, c_ref.shape[1]), _F32) + bias_ref[t:t + 1, :]
        for s in range(t + 1):
            mixed = mixed + coef_ref[t * t_new + s:t * t_new + s + 1, :] * vc_ref[s * db:(s + 1) * db, :]
        rs = slice(t * db, (t + 1) * db)
        c_ref[rs, :] = (u_ref[rs, :] * mixed * gb_ref[rs, :]).astype(c_ref.dtype)
    acc = jnp.dot(a, wb_ref[:d_attn, :], preferred_element_type=_F32)
    acc = acc + jnp.dot(c_ref[...], wb_ref[d_attn:, :], preferred_element_type=_F32)
    xo = x_ref[...] + acc
    ms = jnp.mean(xo * xo, axis=-1, keepdims=True)
    y_ref[...] = xo * lax.rsqrt(ms + EPS) * fg_ref[...]


def _sample_out(oa, ga, u, vc, gb, x, w_out, coef, bias, final_g, *, t_new):
    m, d_model = x.shape
    args = (oa, ga, u, vc, gb, x, w_out, coef, bias, final_g.reshape(1, -1))
    whole = lambda a: pl.BlockSpec(a.shape, lambda i: (0,) * a.ndim)
    return pl.pallas_call(
        functools.partial(_sample_out_kernel, t_new=t_new),
        grid=(1,),
        in_specs=[whole(a) for a in args],
        out_specs=[pl.BlockSpec((m, d_model), lambda i: (0, 0)), whole(w_out)],
        out_shape=[jax.ShapeDtypeStruct((m, d_model), _F32), jax.ShapeDtypeStruct(w_out.shape, _BF16)],
        scratch_shapes=[pltpu.VMEM((m, u.shape[1]), _BF16)],
        compiler_params=_params("arbitrary"),
        name="sample_out",
    )(*args)


def kernel(x_prompt, x_sample, cache_k, cache_v, norm_g, w_in, ln_g, ln_b, w_s, b_s, w_out, final_g):
    batch, s_len, d_model = x_prompt.shape
    db, t_new, _ = x_sample.shape
    depth, _, wb, n_heads, head_dim = cache_k.shape
    d_attn = n_heads * head_dim
    d_chunk = w_out.shape[1] - d_attn
    assert batch == 1 and depth == 1 and head_dim == HEAD_DIM
    assert w_in.shape[2] == 4 * d_attn + 3 * d_chunk and d_attn == d_chunk
    tail = min(MAX_WINDOW, s_len)
    w_in0, w_out0 = w_in[0], w_out[0]
    ng, lg, lb = norm_g[0], ln_g[0], ln_b[0]

    xs = jnp.transpose(x_sample, (1, 0, 2)).reshape(t_new * db, d_model)
    pos_s = PAST_LEN + jnp.repeat(jnp.arange(t_new, dtype=jnp.int32), db)
    cos_s, sin_s = _rope_tables(pos_s)
    qs, ks, vs, gas, us, vcs, gbs, w_qkv_b, w_gate_b = _sample_in_proj(xs, ng, w_in0, cos_s, sin_s, lg, lb)
    to_b_major = lambda a: jnp.transpose(a.reshape(t_new, db, -1), (1, 0, 2))
    qs_b, ks_b, vs_b, vcs_b = (to_b_major(a) for a in (qs, ks, vs, vcs))
    cache_t = lambda c: jnp.transpose(c[0], (0, 2, 3, 1)).reshape(db, d_attn, wb)
    cache_kt, cache_vt = cache_t(cache_k), cache_t(cache_v)
    assert wb == MAX_WINDOW

    xp = x_prompt.reshape(s_len, d_model)
    (xn, q1, q4, q16, k1, k4, k16, v1, v4, v16, k_tail_t, v_tail_t) = _qkv_proj(
        xp, ng, w_qkv_b, tm=256, q_scale=HEAD_DIM ** -0.5 * math.log2(math.e), tail_rows=tail)
    ga, u, vc, gb, oa = _gates_proj(xn, w_gate_b, (lg, lb), (qs_b, ks_b, vs_b, cache_kt, cache_vt), tm=512)
    o1, m1, d1 = _dilated_attention(q1[None], k1[None], v1[None], q_rows=1024)
    o4, m4, d4 = _dilated_attention(q4, k4, v4, q_rows=1024)
    o16, m16, d16 = _dilated_attention(q16, k16, v16, q_rows=512)

    oa_t = jnp.transpose(oa, (1, 0, 2)).reshape(t_new * db, d_attn)
    coef = jnp.repeat(jnp.transpose(w_s[0][:, :t_new, :t_new], (1, 2, 0)).reshape(t_new * t_new, -1),
                      GROUP_WIDTH_B, axis=1)
    bias = jnp.repeat(jnp.transpose(b_s[0][:, :t_new], (1, 0)), GROUP_WIDTH_B, axis=1)
    ys, w_out_b = _sample_out(oa_t, gas, us, vcs, gbs, xs, w_out0, coef, bias, final_g, t_new=t_new)
    y_sample = jnp.transpose(ys.reshape(t_new, db, d_model), (1, 0, 2))

    y_prompt = _out_proj((o1[0], o4, o16), (m1[0], m4, m16), (d1[0], d4, d16), ga, u, vc, gb, xp, w_out_b,
                         w_s[0], b_s[0].T, final_g, tm=512)
    untranspose = lambda a: jnp.transpose(a.reshape(n_heads, head_dim, tail), (2, 0, 1))

    hs = (n_heads, head_dim)
    return (
        y_prompt.reshape(batch, s_len, d_model),
        y_sample,
        untranspose(k_tail_t).reshape(depth, batch, tail, *hs),
        untranspose(v_tail_t).reshape(depth, batch, tail, *hs),
        ks_b.reshape(depth, db, t_new, *hs),
        vs_b.reshape(depth, db, t_new, *hs),
        vcs_b.reshape(depth, db, t_new, d_chunk),
    )
```

```python
import functools
import math

import jax
import jax.numpy as jnp
from jax import lax
from jax.experimental import pallas as pl
from jax.experimental.pallas import tpu as pltpu

HEAD_DIM = 64
BLOCK = 128
CHUNK = 128
GROUP_WIDTH_B = 128
DILATIONS = ((128, 1), (512, 4), (2048, 16))
MAX_WINDOW = 2048
PAST_LEN = 16384
ROPE_THETA = 10000.0
EPS = 1e-6
NEG_INF = -1e30
N_SEGMENTS = 7
SEG_Q, SEG_K, SEG_V, SEG_GA, SEG_U, SEG_VC, SEG_GB = range(N_SEGMENTS)

LANES = 128
VMEM_LIMIT_BYTES = 56 * 1024 * 1024
SUB_ROWS = 256
SAMPLE_HEAD_GROUP_WIDTH = 512

_BF16 = jnp.bfloat16
_F32 = jnp.float32


def _params(*semantics, flags=None):
    return pltpu.CompilerParams(dimension_semantics=semantics, vmem_limit_bytes=VMEM_LIMIT_BYTES, flags=flags)


def _rope_tables(pos):
    half = HEAD_DIM // 2
    inv = jnp.exp(-math.log(ROPE_THETA) * jnp.arange(half, dtype=_F32) / half)
    ang = pos.astype(_F32)[:, None] * inv[None, :]
    cos = jnp.cos(ang)
    sin = jnp.sin(ang)
    cos_t = jnp.concatenate([cos, cos, cos, cos], axis=-1)
    sin_t = jnp.concatenate([-sin, sin, -sin, sin], axis=-1)
    return cos_t, sin_t


def _rope_tables_blocked(s_len, tm):
    half = HEAD_DIM // 2
    inv = jnp.exp(-math.log(ROPE_THETA) * jnp.arange(half, dtype=_F32) / half)
    tile4 = lambda a: jnp.concatenate([a, a, a, a], axis=-1)
    ang_r = jnp.arange(tm, dtype=jnp.int32).astype(_F32)[:, None] * inv[None, :]
    ang_b = (jnp.arange(s_len // tm, dtype=jnp.int32) * tm).astype(_F32)[:, None] * inv[None, :]
    sign = jnp.concatenate([-jnp.ones((1, half), _F32), jnp.ones((1, half), _F32)] * 2, axis=-1)
    return (tile4(jnp.cos(ang_r)), tile4(jnp.sin(ang_r)), tile4(jnp.cos(ang_b)), tile4(jnp.sin(ang_b)), sign)


def _rope_group(zg, cos, sin_signed):
    lane = lax.broadcasted_iota(jnp.int32, zg.shape, 1)
    first_half = (lane % HEAD_DIM) < (HEAD_DIM // 2)
    partner = jnp.where(first_half, pltpu.roll(zg, LANES - HEAD_DIM // 2, 1),
                        pltpu.roll(zg, HEAD_DIM // 2, 1))
    return zg * cos + partner * sin_signed


def _gelu_layer_norm(z, g, b):
    h = jax.nn.gelu(z)
    mu = jnp.mean(h, axis=-1, keepdims=True)
    hc = h - mu
    var = jnp.mean(hc * hc, axis=-1, keepdims=True)
    return hc * lax.rsqrt(var + EPS) * g + b


def _qkv_kernel(x_ref, g_ref, w_ref, cos_r_ref, sin_r_ref, cos_b_ref, sin_b_ref, sign_ref,
                xn_ref, q1_ref, q4_ref, q16_ref, k1_ref, k4_ref, k16_ref, v1_ref, v4_ref, v16_ref,
                ktail_ref, vtail_ref, *slab_refs, q_scale):
    tm = x_ref.shape[0]
    d_seg = q1_ref.shape[1]
    n4, n16 = tm // 4, tm // 16
    xf = x_ref[...]
    ms = jnp.mean(xf * xf, axis=-1, keepdims=True)
    xb = (xf * lax.rsqrt(ms + EPS) * g_ref[...]).astype(_BF16)
    xn_ref[...] = xb
    step = pl.ds(pl.program_id(0), 1)
    cb, sb = cos_b_ref[step, :], sin_b_ref[step, :]
    cr, sr = cos_r_ref[...], sin_r_ref[...]
    cos = cr * cb - sr * sb
    sin_signed = (sr * cb + cr * sb) * sign_ref[...]

    segments = (
        (SEG_Q, (q1_ref, q4_ref, q16_ref), None, q_scale),
        (SEG_K, (k1_ref, k4_ref, k16_ref), ktail_ref, 1.0),
        (SEG_V, (v1_ref, v4_ref, v16_ref), vtail_ref, 1.0),
    )
    for seg, (d1_ref, d4_ref, d16_ref), tail_ref, scale in segments:
        nat_ref, res4_ref = slab_refs[2 * seg], slab_refs[2 * seg + 1]
        z = jnp.dot(xb, w_ref[:, seg * d_seg:(seg + 1) * d_seg], preferred_element_type=_F32)
        for g in range(d_seg // LANES):
            sl = slice(g * LANES, (g + 1) * LANES)
            r = z[:, sl]
            if seg != SEG_V:
                r = _rope_group(r, cos, sin_signed)
            if tail_ref is not None:
                tail_ref[sl, :] = r.T
            if scale != 1.0:
                r = r * scale
            d1_ref[:, sl] = r.astype(d1_ref.dtype)
            nat_ref[g] = r
            for r4 in range(4):
                blk = nat_ref[g, pl.ds(r4, n4, stride=4), :]
                d4_ref[r4, :, sl] = blk.astype(d4_ref.dtype)
                res4_ref[g, r4 * n4:(r4 + 1) * n4, :] = blk
            for r4 in range(4):
                for a in range(4):
                    blk = res4_ref[g, pl.ds(r4 * n4 + a, n16, stride=4), :]
                    d16_ref[r4 + 4 * a, :, sl] = blk.astype(d16_ref.dtype)


def _gates_kernel(xn_ref, w_ref, lng_ref, lnb_ref, q_ref, kn_ref, vn_ref, kt_ref, vt_ref,
                  ga_ref, u_ref, vc_ref, gb_ref, oa_ref, mult_ref):
    j = pl.program_id(1)

    @pl.when(jnp.logical_and(pl.program_id(0) == 0, j == 0))
    def _():
        mult_ref[...] = _key_multiplicity(mult_ref.shape[0], mult_ref.shape[1], q_ref.shape[0])

    tm = xn_ref.shape[0]
    sub = min(SUB_ROWS, tm)
    epilogues = (
        (ga_ref, jax.nn.silu),
        (u_ref, jax.nn.gelu),
        (vc_ref, lambda z: _gelu_layer_norm(z, lng_ref[...], lnb_ref[...])),
        (gb_ref, jax.nn.silu),
    )
    for s, (o_ref, epilogue) in enumerate(epilogues):
        @pl.when(j == s)
        def _(s=s, o_ref=o_ref, epilogue=epilogue):
            side = _sample_attn_tile(q_ref, kn_ref, vn_ref, kt_ref, vt_ref, oa_ref, mult_ref)
            for t in range(tm // sub):
                rs = slice(t * sub, (t + 1) * sub)
                z = jnp.dot(xn_ref[rs, :], w_ref[s], preferred_element_type=_F32)
                next(side, None)
                o_ref[rs, :] = epilogue(z).astype(o_ref.dtype)
                if t == 0:
                    next(side, None)
            for _ in side:
                pass


def _gates_proj(xn, w_gate, ln, side, *, tm):
    s_len, d_model = xn.shape
    n_seg, _, d_seg = w_gate.shape
    sq, skn, svn, skt, svt = side
    db, t_new, d_attn = sq.shape
    width, wb = SAMPLE_HEAD_GROUP_WIDTH, skt.shape[2]
    hg = d_attn // width
    n_i = s_len // tm
    assert n_i * n_seg == db * hg
    tile = lambda i, j: i * n_seg + j
    new = pl.BlockSpec((None, t_new, width), lambda i, j: (tile(i, j) // hg, 0, tile(i, j) % hg))
    cache = pl.BlockSpec((None, width, wb), lambda i, j: (tile(i, j) // hg, tile(i, j) % hg, 0))
    row_vec = pl.BlockSpec((1, d_seg), lambda i, j: (0, 0))
    rows = pl.BlockSpec((tm, d_seg), lambda i, j: (i, 0))
    return pl.pallas_call(
        _gates_kernel,
        grid=(n_i, n_seg),
        in_specs=[pl.BlockSpec((tm, d_model), lambda i, j: (i, 0)),
                  pl.BlockSpec(w_gate.shape, lambda i, j: (0, 0, 0), pipeline_mode=pl.Buffered(1)),
                  row_vec, row_vec, new, new, new, cache, cache],
        out_specs=[rows] * n_seg + [new],
        out_shape=[jax.ShapeDtypeStruct((s_len, d_seg), _BF16)] * n_seg
        + [jax.ShapeDtypeStruct((db, t_new, d_attn), _F32)],
        scratch_shapes=[pltpu.VMEM((t_new * width // HEAD_DIM, wb), _F32)],
        compiler_params=_params("arbitrary", "arbitrary"),
        name="gates_proj",
    )(xn, w_gate, ln[0].reshape(1, -1), ln[1].reshape(1, -1), sq, skn, svn, skt, svt)


def _qkv_proj(x, norm_g, w_qkv, *, tm, q_scale, tail_rows):
    s_len, d_model = x.shape
    d_seg = w_qkv.shape[1] // 3
    n_i = s_len // tm
    tail_start = n_i - tail_rows // tm
    tables = _rope_tables_blocked(s_len, tm)
    const = lambda a: pl.BlockSpec(a.shape, lambda i: (0, 0))
    layouts_shape = [jax.ShapeDtypeStruct((s_len, d_seg), _BF16),
                     jax.ShapeDtypeStruct((4, s_len // 4, d_seg), _BF16),
                     jax.ShapeDtypeStruct((16, s_len // 16, d_seg), _BF16)]
    layouts_spec = [pl.BlockSpec((tm, d_seg), lambda i: (i, 0)),
                    pl.BlockSpec((4, tm // 4, d_seg), lambda i: (0, i, 0)),
                    pl.BlockSpec((16, tm // 16, d_seg), lambda i: (0, i, 0))]
    tail_spec = pl.BlockSpec((d_seg, tm), lambda i: (0, jnp.maximum(i - tail_start, 0)))
    tail_shape = jax.ShapeDtypeStruct((d_seg, tail_rows), _F32)
    return pl.pallas_call(
        functools.partial(_qkv_kernel, q_scale=q_scale),
        grid=(n_i,),
        in_specs=[pl.BlockSpec((tm, d_model), lambda i: (i, 0)), pl.BlockSpec((1, d_model), lambda i: (0, 0)),
                  pl.BlockSpec(w_qkv.shape, lambda i: (0, 0), pipeline_mode=pl.Buffered(1))]
        + [const(t) for t in tables],
        out_specs=[pl.BlockSpec((tm, d_model), lambda i: (i, 0))] + layouts_spec * 3 + [tail_spec] * 2,
        out_shape=[jax.ShapeDtypeStruct((s_len, d_model), _BF16)] + layouts_shape * 3 + [tail_shape] * 2,
        scratch_shapes=[pltpu.VMEM((d_seg // LANES, tm, LANES), _F32)] * 6,
        compiler_params=_params("arbitrary"),
        name="qkv_proj",
    )(x, norm_g.reshape(1, -1), w_qkv, *tables)


def _attn_kernel(q_ref, kp_ref, kc_ref, vp_ref, vc_ref, o_ref, m_ref, den_ref, k0_ref, *vext_refs, res_blocks):
    n = pl.program_id(1)
    first_step = jnp.logical_and(pl.program_id(0) == 0, n == 0)
    n_pairs = q_ref.shape[1] // LANES
    two = 2 * BLOCK

    @pl.when(first_step)
    def _():
        ones = jnp.ones((two, LANES), k0_ref.dtype)
        for vext_ref in vext_refs:
            for hp in range(n_pairs):
                vext_ref[:, (2 * hp + 1) * LANES:(2 * hp + 2) * LANES] = ones

    k0_ref[0:BLOCK, :] = kp_ref[...]
    k0_ref[BLOCK:, :] = kc_ref[0:BLOCK, :]

    qi = lax.broadcasted_iota(jnp.int32, (two, two), 0) % BLOCK
    si = lax.broadcasted_iota(jnp.int32, (two, two), 1)
    dist = qi + BLOCK - si
    band_bias = jnp.where((dist >= 0) & (dist <= BLOCK), 0.0, NEG_INF).astype(_F32)
    lane = lax.broadcasted_iota(jnp.int32, (BLOCK, LANES), 1)
    head0 = lane < HEAD_DIM
    zero = jnp.zeros((BLOCK, LANES), q_ref.dtype)

    for j, vext_ref in enumerate(vext_refs):
        rows = slice(j * BLOCK, (j + 1) * BLOCK)
        prev_keys = slice((j - 1) * BLOCK, (j + 1) * BLOCK)
        bias = band_bias
        if len(vext_refs) % res_blocks == 0:
            if j % res_blocks == 0:
                bias = band_bias + jnp.where(si < BLOCK, NEG_INF, 0.0)
        elif j == 0:
            bias = band_bias + jnp.where(jnp.logical_and(si < BLOCK, n == 0), NEG_INF, 0.0)
        for hp in range(n_pairs):
            sl = slice(hp * LANES, (hp + 1) * LANES)
            ext = slice(2 * hp * LANES, (2 * hp + 1) * LANES)
            if j == 0:
                vext_ref[0:BLOCK, ext] = vp_ref[:, sl]
                vext_ref[BLOCK:, ext] = vc_ref[0:BLOCK, sl]
            else:
                vext_ref[:, ext] = vc_ref[prev_keys, sl]
        m_acc = jnp.zeros((BLOCK, LANES), _F32)
        den_acc = jnp.ones((BLOCK, LANES), _F32)
        for hp in range(n_pairs):
            sl = slice(hp * LANES, (hp + 1) * LANES)
            q2 = q_ref[rows, sl]
            qs = jnp.concatenate([jnp.where(head0, q2, zero), jnp.where(head0, zero, q2)], axis=0)
            k2 = k0_ref[:, sl] if j == 0 else kc_ref[prev_keys, sl]
            s = lax.dot_general(qs, k2, (((1,), (1,)), ((), ())), preferred_element_type=_F32) + bias
            m = jnp.max(s, axis=-1, keepdims=True)
            p = jnp.exp2(s - m).astype(vext_ref.dtype)
            pv = jnp.dot(p, vext_ref[:, 2 * hp * LANES:(2 * hp + 2) * LANES], preferred_element_type=_F32)
            num, den = pv[:, :LANES], pv[:, LANES:]
            o_ref[rows, sl] = jnp.where(head0, num[:BLOCK], num[BLOCK:]).astype(o_ref.dtype)
            m_acc = jnp.where(lane == 2 * hp, m[:BLOCK], jnp.where(lane == 2 * hp + 1, m[BLOCK:], m_acc))
            den_acc = jnp.where(lane == 2 * hp, den[:BLOCK],
                                jnp.where(lane == 2 * hp + 1, den[BLOCK:], den_acc))
        m_ref[rows, :] = m_acc
        den_ref[rows, :] = den_acc


def _dilated_attention(q, k, v, *, q_rows):
    d_orig, res_len, d_attn = q.shape
    res_blocks = res_len // BLOCK
    if q_rows > res_len:
        q, k, v = (a.reshape(d_orig * res_len // q_rows, q_rows, d_attn) for a in (q, k, v))
    d, sub_len, _ = q.shape
    per = q_rows // BLOCK
    assert per % res_blocks == 0 or res_blocks % per == 0
    cur = pl.BlockSpec((None, q_rows, d_attn), lambda r, n: (r, n, 0))
    prev = pl.BlockSpec((None, BLOCK, d_attn), lambda r, n: (r, jnp.maximum(n * per - 1, 0), 0))
    stat = pl.BlockSpec((None, q_rows, LANES), lambda r, n: (r, n, 0))
    stat_shape = jax.ShapeDtypeStruct((d, sub_len, LANES), _F32)
    o, m, den = pl.pallas_call(
        functools.partial(_attn_kernel, res_blocks=res_blocks),
        grid=(d, sub_len // q_rows),
        in_specs=[cur, prev, cur, prev, cur],
        out_specs=[cur, stat, stat],
        out_shape=[jax.ShapeDtypeStruct((d, sub_len, d_attn), q.dtype), stat_shape, stat_shape],
        scratch_shapes=[pltpu.VMEM((2 * BLOCK, d_attn), q.dtype)]
        + [pltpu.VMEM((2 * BLOCK, 2 * d_attn), q.dtype)] * per,
        compiler_params=_params("arbitrary", "arbitrary"),
        name=f"dilated_attn_d{d_orig}",
    )(q, k, k, v, v)
    return (o.reshape(d_orig, res_len, d_attn), m.reshape(d_orig, res_len, LANES),
            den.reshape(d_orig, res_len, LANES))


def _out_proj_kernel(o1_ref, o4_ref, o16_ref, m1_ref, m4_ref, m16_ref, d1_ref, d4_ref, d16_ref,
                     ga_ref, u_ref, vc_ref, gb_ref, x_ref, wb_ref, ws_ref, bst_ref, fg_ref, y_ref,
                     ac_ref, n4_ref, n16_ref, tmp_ref, st_ref):
    tm, d_attn = o1_ref.shape
    n_lane_groups = d_attn // LANES
    sub = 2 * CHUNK
    n4, n16 = sub // 4, sub // 16

    pair_idx = lax.broadcasted_iota(jnp.int32, (sub, LANES), 1) // HEAD_DIM
    tril = (lax.broadcasted_iota(jnp.int32, (CHUNK, CHUNK), 0)
            >= lax.broadcasted_iota(jnp.int32, (CHUNK, CHUNK), 1))

    for t in range(tm // sub):
        rs = slice(t * sub, (t + 1) * sub)
        for r in range(4):
            dst = pl.ds(t * sub + r, n4, stride=4)
            src = slice(t * n4, (t + 1) * n4)
            st_ref[0, dst, :] = m4_ref[r, src, :]
            st_ref[1, dst, :] = d4_ref[r, src, :]
            for g in range(n_lane_groups):
                n4_ref[g, dst, :] = o4_ref[r, src, g * LANES:(g + 1) * LANES].astype(_F32)
        for r in range(16):
            dst = pl.ds(t * sub + r, n16, stride=16)
            src = slice(t * n16, (t + 1) * n16)
            st_ref[2, dst, :] = m16_ref[r, src, :]
            st_ref[3, dst, :] = d16_ref[r, src, :]
        for g in range(n_lane_groups):
            sl = slice(g * LANES, (g + 1) * LANES)
            for r4 in range(4):
                for a in range(4):
                    blk = o16_ref[r4 + 4 * a, t * n16:(t + 1) * n16, sl].astype(_F32)
                    tmp_ref[g, pl.ds(t * sub + r4 * n4 + a, n16, stride=4), :] = blk
                n16_ref[g, pl.ds(t * sub + r4, n4, stride=4), :] = tmp_ref[g, t * sub + r4 * n4:
                                                                          t * sub + (r4 + 1) * n4, :]

        ms = [m1_ref[rs, :], st_ref[0, rs, :], st_ref[2, rs, :]]
        dens = [d1_ref[rs, :], st_ref[1, rs, :], st_ref[3, rs, :]]
        mx = jnp.maximum(jnp.maximum(ms[0], ms[1]), ms[2])
        es = [jnp.exp2(m - mx) for m in ms]
        tot = es[0] * dens[0] + es[1] * dens[1] + es[2] * dens[2]
        ws = [e / tot for e in es]
        for g in range(n_lane_groups):
            sl = slice(g * LANES, (g + 1) * LANES)
            w1, w4, w16 = (jnp.take_along_axis(w, pair_idx + 2 * g, axis=1) for w in ws)
            a = w1 * o1_ref[rs, sl].astype(_F32) + w4 * n4_ref[g, rs, :] + w16 * n16_ref[g, rs, :]
            ac_ref[rs, sl] = (a * ga_ref[rs, sl].astype(_F32)).astype(ac_ref.dtype)

        c0 = slice(t * sub, t * sub + CHUNK)
        c1 = slice(t * sub + CHUNK, (t + 1) * sub)
        for g in range(ws_ref.shape[0]):
            wm = jnp.where(tril, ws_ref[g], 0.0).astype(_BF16)
            gs = slice(g * GROUP_WIDTH_B, (g + 1) * GROUP_WIDTH_B)
            vc2 = jnp.concatenate([vc_ref[c0, gs], vc_ref[c1, gs]], axis=1)
            mixed = jnp.dot(wm, vc2, preferred_element_type=_F32) + bst_ref[:, g:g + 1]
            for c, half in ((c0, slice(0, GROUP_WIDTH_B)), (c1, slice(GROUP_WIDTH_B, 2 * GROUP_WIDTH_B))):
                cv = u_ref[c, gs].astype(_F32) * mixed[:, half] * gb_ref[c, gs].astype(_F32)
                ac_ref[c, d_attn + g * GROUP_WIDTH_B:d_attn + (g + 1) * GROUP_WIDTH_B] = cv.astype(ac_ref.dtype)

    for t in range(tm // sub):
        rs = slice(t * sub, (t + 1) * sub)
        acc = jnp.dot(ac_ref[rs, :], wb_ref[...], preferred_element_type=_F32)
        xo = x_ref[rs, :] + acc
        msq = jnp.mean(xo * xo, axis=-1, keepdims=True)
        y_ref[rs, :] = xo * lax.rsqrt(msq + EPS) * fg_ref[...]


def _out_proj(os, ms, dens, ga, u, vc, gb, x, w_out, w_s, b_s_t, final_g, *, tm):
    m, d_model = x.shape
    d_attn = ga.shape[1]
    d_chunk = u.shape[1]
    rows = lambda n: pl.BlockSpec((tm, n), lambda i: (i, 0))
    res = lambda d, n: pl.BlockSpec((d, tm // d, n), lambda i: (0, i, 0))
    whole = lambda a: pl.BlockSpec(a.shape, lambda i: (0,) * a.ndim)
    by_pattern = lambda n: [rows(n), res(4, n), res(16, n)]
    fg = final_g.reshape(1, -1)
    slab = pltpu.VMEM((d_attn // LANES, tm, LANES), _F32)
    return pl.pallas_call(
        _out_proj_kernel,
        grid=(m // tm,),
        in_specs=by_pattern(d_attn) + by_pattern(LANES) + by_pattern(LANES)
        + [rows(d_attn), rows(d_chunk), rows(d_chunk), rows(d_chunk), rows(d_model),
           pl.BlockSpec(w_out.shape, lambda i: (0, 0), pipeline_mode=pl.Buffered(1)),
           whole(w_s), whole(b_s_t), whole(fg)],
        out_specs=rows(d_model),
        out_shape=jax.ShapeDtypeStruct((m, d_model), _F32),
        scratch_shapes=[pltpu.VMEM((tm, d_attn + d_chunk), _BF16), slab, slab, slab,
                        pltpu.VMEM((4, tm, LANES), _F32)],
        compiler_params=_params("arbitrary"),
        name="out_proj",
    )(*os, *ms, *dens, ga, u, vc, gb, x, w_out, w_s, b_s_t, fg)


def _sample_in_proj_kernel(x_ref, g_ref, w_ref, cos_ref, sin_ref, lng_ref, lnb_ref,
                           q_ref, k_ref, v_ref, ga_ref, u_ref, vc_ref, gb_ref, wqkv_ref, wgate_ref, xn_ref):
    j = pl.program_id(0)

    @pl.when(j == 0)
    def _():
        xf = x_ref[...]
        ms = jnp.mean(xf * xf, axis=-1, keepdims=True)
        xn_ref[...] = (xf * lax.rsqrt(ms + EPS) * g_ref[...]).astype(xn_ref.dtype)

    wb = w_ref[...].astype(_BF16)

    @pl.when(j <= SEG_V)
    def _():
        wqkv_ref[...] = wb

    @pl.when(j > SEG_V)
    def _():
        wgate_ref[...] = wb

    z = jnp.dot(xn_ref[...], wb, preferred_element_type=_F32)

    def rope_to(ref, scale):
        for g in range(z.shape[1] // LANES):
            sl = slice(g * LANES, (g + 1) * LANES)
            ref[:, sl] = _rope_group(z[:, sl], cos_ref[...], sin_ref[...]) * scale

    @pl.when(j == SEG_Q)
    def _():
        rope_to(q_ref, HEAD_DIM ** -0.5)

    @pl.when(j == SEG_K)
    def _():
        rope_to(k_ref, 1.0)

    @pl.when(j == SEG_V)
    def _():
        v_ref[...] = z

    @pl.when(j == SEG_GA)
    def _():
        ga_ref[...] = jax.nn.silu(z)

    @pl.when(j == SEG_U)
    def _():
        u_ref[...] = jax.nn.gelu(z)

    @pl.when(j == SEG_VC)
    def _():
        vc_ref[...] = _gelu_layer_norm(z, lng_ref[...], lnb_ref[...])

    @pl.when(j == SEG_GB)
    def _():
        gb_ref[...] = jax.nn.silu(z)


def _sample_in_proj(x, norm_g, w_in, cos_t, sin_t, ln_g, ln_b):
    m, d_model = x.shape
    d_seg = w_in.shape[1] // N_SEGMENTS
    whole = lambda r, c: pl.BlockSpec((r, c), lambda j: (0, 0))
    return pl.pallas_call(
        _sample_in_proj_kernel,
        grid=(N_SEGMENTS,),
        in_specs=[whole(m, d_model), whole(1, d_model),
                  pl.BlockSpec((d_model, d_seg), lambda j: (0, j)),
                  whole(m, LANES), whole(m, LANES), whole(1, d_seg), whole(1, d_seg)],
        out_specs=[whole(m, d_seg)] * N_SEGMENTS
        + [pl.BlockSpec((d_model, d_seg), lambda j: (0, jnp.minimum(j, SEG_V))),
           pl.BlockSpec((None, d_model, d_seg), lambda j: (jnp.maximum(j - SEG_GA, 0), 0, 0))],
        out_shape=[jax.ShapeDtypeStruct((m, d_seg), _F32)] * N_SEGMENTS
        + [jax.ShapeDtypeStruct((d_model, (SEG_V + 1) * d_seg), _BF16),
           jax.ShapeDtypeStruct((N_SEGMENTS - SEG_GA, d_model, d_seg), _BF16)],
        scratch_shapes=[pltpu.VMEM((m, d_model), _BF16)],
        compiler_params=_params("arbitrary"),
        name="sample_in_proj",
    )(x, norm_g.reshape(1, -1), w_in, cos_t, sin_t, ln_g.reshape(1, -1), ln_b.reshape(1, -1))


def _key_multiplicity(rows, wb, t_new):
    t_row = lax.broadcasted_iota(jnp.int32, (rows, wb), 0) // (rows // t_new)
    pos = lax.broadcasted_iota(jnp.int32, (rows, wb), 1)
    dist = wb + t_row - pos
    mult = jnp.zeros((rows, wb), _F32)
    for window, d in DILATIONS:
        mult = mult + jnp.where((dist % d == 0) & (dist <= window), 1.0, 0.0)
    return mult


def _sample_attn_tile(q_ref, kn_ref, vn_ref, kt_ref, vt_ref, o_ref, mult_ref):
    t_new, width = q_ref.shape
    n_heads = width // HEAD_DIM
    rows = t_new * n_heads
    head_row = lax.broadcasted_iota(jnp.int32, (n_heads, width), 0)
    head_col = lax.broadcasted_iota(jnp.int32, (n_heads, width), 1) // HEAD_DIM
    own = head_row == head_col
    qbd = jnp.concatenate([jnp.where(own, q_ref[t:t + 1, :], 0.0) for t in range(t_new)], axis=0)
    mult = mult_ref[...]

    s = jnp.dot(qbd.astype(_BF16), kt_ref[...].astype(_BF16), preferred_element_type=_F32)
    yield
    s = jnp.where(mult > 0.0, s, NEG_INF)
    m = jnp.max(s, axis=-1, keepdims=True)
    t_col = lax.broadcasted_iota(jnp.int32, (rows, 1), 0) // n_heads
    s_new, mult_new = [], []
    for tp in range(t_new):
        s_new.append(jnp.sum(qbd * kn_ref[tp:tp + 1, :], axis=-1, keepdims=True))
        dn = t_col - tp
        mn = jnp.zeros((rows, 1), _F32)
        for window, d in DILATIONS:
            mn = mn + jnp.where((dn >= 0) & (dn % d == 0), 1.0, 0.0)
        mult_new.append(mn)
        m = jnp.maximum(m, jnp.where(mn > 0.0, s_new[tp], NEG_INF))
    p = jnp.exp(s - m) * mult
    den = jnp.sum(p, axis=-1, keepdims=True)
    yield
    num = lax.dot_general(p.astype(_BF16), vt_ref[...].astype(_BF16), (((1,), (1,)), ((), ())),
                          preferred_element_type=_F32)
    for tp in range(t_new):
        pn = jnp.where(mult_new[tp] > 0.0, jnp.exp(s_new[tp] - m), 0.0) * mult_new[tp]
        den = den + pn
        num = num + pn * vn_ref[tp:tp + 1, :]
    res = num / den
    for t in range(t_new):
        blk = res[t * n_heads:(t + 1) * n_heads, :]
        o_ref[t:t + 1, :] = jnp.sum(jnp.where(own, blk, 0.0), axis=0, keepdims=True)


def _sample_out_kernel(oa_ref, ga_ref, u_ref, vc_ref, gb_ref, x_ref, wout_ref, coef_ref, bias_ref, fg_ref,
                       y_ref, wb_ref, c_ref, *, t_new):
    d_attn = oa_ref.shape[1]
    wb_ref[...] = wout_ref[...].astype(wb_ref.dtype)
    a = (oa_ref[...] * ga_ref[...]).astype(_BF16)
    vc = vc_ref[...]
    mixed = bias_ref[...] + coef_ref[0] * vc
    for delta in range(1, t_new):
        mixed = mixed + coef_ref[delta] * pltpu.roll(vc, delta, 0)
    c_ref[...] = (u_ref[...] * mixed * gb_ref[...]).astype(c_ref.dtype)
    acc = jnp.dot(a, wb_ref[:d_attn, :], preferred_element_type=_F32)
    acc = acc + jnp.dot(c_ref[...], wb_ref[d_attn:, :], preferred_element_type=_F32)
    xo = x_ref[...] + acc
    ms = jnp.mean(xo * xo, axis=-1, keepdims=True)
    y_ref[...] = xo * lax.rsqrt(ms + EPS) * fg_ref[...]


def _sample_out(oa, ga, u, vc, gb, x, w_out, coef, bias, final_g):
    m, d_model = x.shape
    args = (oa, ga, u, vc, gb, x, w_out, coef, bias, final_g.reshape(1, -1))
    whole = lambda a: pl.BlockSpec(a.shape, lambda i: (0,) * a.ndim)
    return pl.pallas_call(
        functools.partial(_sample_out_kernel, t_new=coef.shape[0]),
        grid=(1,),
        in_specs=[whole(a) for a in args],
        out_specs=[pl.BlockSpec((m, d_model), lambda i: (0, 0)), whole(w_out)],
        out_shape=[jax.ShapeDtypeStruct((m, d_model), _F32), jax.ShapeDtypeStruct(w_out.shape, _BF16)],
        scratch_shapes=[pltpu.VMEM((m, u.shape[1]), _BF16)],
        compiler_params=_params("arbitrary"),
        name="sample_out",
    )(*args)


def kernel(x_prompt, x_sample, cache_k, cache_v, norm_g, w_in, ln_g, ln_b, w_s, b_s, w_out, final_g):
    batch, s_len, d_model = x_prompt.shape
    db, t_new, _ = x_sample.shape
    depth, _, wb, n_heads, head_dim = cache_k.shape
    d_attn = n_heads * head_dim
    d_chunk = w_out.shape[1] - d_attn
    assert batch == 1 and depth == 1 and head_dim == HEAD_DIM
    assert w_in.shape[2] == 4 * d_attn + 3 * d_chunk and d_attn == d_chunk
    tail = min(MAX_WINDOW, s_len)
    w_in0, w_out0 = w_in[0], w_out[0]
    ng, lg, lb = norm_g[0], ln_g[0], ln_b[0]

    xs = x_sample.reshape(db * t_new, d_model)
    pos_s = PAST_LEN + jnp.tile(jnp.arange(t_new, dtype=jnp.int32), db)
    cos_s, sin_s = _rope_tables(pos_s)
    qs, ks, vs, gas, us, vcs, gbs, w_qkv_b, w_gate_b = _sample_in_proj(xs, ng, w_in0, cos_s, sin_s, lg, lb)
    qs_b, ks_b, vs_b, vcs_b = (a.reshape(db, t_new, -1) for a in (qs, ks, vs, vcs))
    cache_t = lambda c: jnp.transpose(c[0], (0, 2, 3, 1)).reshape(db, d_attn, wb)
    cache_kt, cache_vt = cache_t(cache_k), cache_t(cache_v)
    assert wb == MAX_WINDOW

    xp = x_prompt.reshape(s_len, d_model)
    (xn, q1, q4, q16, k1, k4, k16, v1, v4, v16, k_tail_t, v_tail_t) = _qkv_proj(
        xp, ng, w_qkv_b, tm=256, q_scale=HEAD_DIM ** -0.5 * math.log2(math.e), tail_rows=tail)
    ga, u, vc, gb, oa = _gates_proj(xn, w_gate_b, (lg, lb), (qs_b, ks_b, vs_b, cache_kt, cache_vt), tm=512)
    o1, m1, d1 = _dilated_attention(q1[None], k1[None], v1[None], q_rows=1024)
    o4, m4, d4 = _dilated_attention(q4, k4, v4, q_rows=1024)
    o16, m16, d16 = _dilated_attention(q16, k16, v16, q_rows=1024)

    w_ts = jnp.transpose(w_s[0][:, :t_new, :t_new], (1, 2, 0))
    coef = jnp.stack([jnp.stack([w_ts[t, t - delta] if t >= delta else jnp.zeros_like(w_ts[0, 0])
                                 for t in range(t_new)]) for delta in range(t_new)])
    coef = jnp.tile(jnp.repeat(coef, GROUP_WIDTH_B, axis=2), (1, db, 1))
    bias = jnp.tile(jnp.repeat(jnp.transpose(b_s[0][:, :t_new], (1, 0)), GROUP_WIDTH_B, axis=1), (db, 1))
    ys, w_out_b = _sample_out(oa.reshape(db * t_new, d_attn), gas, us, vcs, gbs, xs, w_out0, coef, bias, final_g)
    y_sample = ys.reshape(db, t_new, d_model)

    y_prompt = _out_proj((o1[0], o4, o16), (m1[0], m4, m16), (d1[0], d4, d16), ga, u, vc, gb, xp, w_out_b,
                         w_s[0], b_s[0].T, final_g, tm=512)
    untranspose = lambda a: jnp.transpose(a.reshape(n_heads, head_dim, tail), (2, 0, 1))

    hs = (n_heads, head_dim)
    return (
        y_prompt.reshape(batch, s_len, d_model),
        y_sample,
        untranspose(k_tail_t).reshape(depth, batch, tail, *hs),
        untranspose(v_tail_t).reshape(depth, batch, tail, *hs),
        ks_b.reshape(depth, db, t_new, *hs),
        vs_b.reshape(depth, db, t_new, *hs),
        vcs_b.reshape(depth, db, t_new, d_chunk),
    )
```

```python
import functools
import math

import jax
import jax.numpy as jnp
from jax import lax
from jax.experimental import pallas as pl
from jax.experimental.pallas import tpu as pltpu

HEAD_DIM = 64
BLOCK = 128
CHUNK = 128
GROUP_WIDTH_B = 128
DILATIONS = ((128, 1), (512, 4), (2048, 16))
MAX_WINDOW = 2048
PAST_LEN = 16384
ROPE_THETA = 10000.0
EPS = 1e-6
NEG_INF = -1e30
N_SEGMENTS = 7
SEG_Q, SEG_K, SEG_V, SEG_GA, SEG_U, SEG_VC, SEG_GB = range(N_SEGMENTS)

LANES = 128
VMEM_LIMIT_BYTES = 56 * 1024 * 1024
SUB_ROWS = 256
SAMPLE_HEAD_GROUP_WIDTH = 512

_BF16 = jnp.bfloat16
_F32 = jnp.float32


def _params(*semantics):
    return pltpu.CompilerParams(dimension_semantics=semantics, vmem_limit_bytes=VMEM_LIMIT_BYTES)


def _rope_tables(pos):
    half = HEAD_DIM // 2
    inv = jnp.exp(-math.log(ROPE_THETA) * jnp.arange(half, dtype=_F32) / half)
    ang = pos.astype(_F32)[:, None] * inv[None, :]
    cos = jnp.cos(ang)
    sin = jnp.sin(ang)
    cos_t = jnp.concatenate([cos, cos, cos, cos], axis=-1)
    sin_t = jnp.concatenate([-sin, sin, -sin, sin], axis=-1)
    return cos_t, sin_t


def _rope_tables_blocked(s_len, tm):
    half = HEAD_DIM // 2
    inv = jnp.exp(-math.log(ROPE_THETA) * jnp.arange(half, dtype=_F32) / half)
    tile4 = lambda a: jnp.concatenate([a, a, a, a], axis=-1)
    ang_r = jnp.arange(tm, dtype=jnp.int32).astype(_F32)[:, None] * inv[None, :]
    ang_b = (jnp.arange(s_len // tm, dtype=jnp.int32) * tm).astype(_F32)[:, None] * inv[None, :]
    sign = jnp.concatenate([-jnp.ones((1, half), _F32), jnp.ones((1, half), _F32)] * 2, axis=-1)
    return (tile4(jnp.cos(ang_r)), tile4(jnp.sin(ang_r)), tile4(jnp.cos(ang_b)), tile4(jnp.sin(ang_b)), sign)


def _rope_group(zg, cos, sin_signed):
    lane = lax.broadcasted_iota(jnp.int32, zg.shape, 1)
    first_half = (lane % HEAD_DIM) < (HEAD_DIM // 2)
    partner = jnp.where(first_half, pltpu.roll(zg, LANES - HEAD_DIM // 2, 1),
                        pltpu.roll(zg, HEAD_DIM // 2, 1))
    return zg * cos + partner * sin_signed


def _gelu_layer_norm(z, g, b):
    h = jax.nn.gelu(z)
    mu = jnp.mean(h, axis=-1, keepdims=True)
    hc = h - mu
    var = jnp.mean(hc * hc, axis=-1, keepdims=True)
    return hc * lax.rsqrt(var + EPS) * g + b


def _qkv_kernel(x_ref, g_ref, w_ref, cos_r_ref, sin_r_ref, cos_b_ref, sin_b_ref, sign_ref,
                xn_ref, q1_ref, q4_ref, q16_ref, k1_ref, k4_ref, k16_ref, v1_ref, v4_ref, v16_ref,
                ktail_ref, vtail_ref, nat_ref, res4_ref, *, q_scale):
    tm = x_ref.shape[0]
    d_seg = q1_ref.shape[1]
    sub = min(SUB_ROWS, tm)
    n4, n16 = sub // 4, sub // 16
    step = pl.ds(pl.program_id(0), 1)
    cb, sb = cos_b_ref[step, :], sin_b_ref[step, :]
    segments = (
        (SEG_Q, (q1_ref, q4_ref, q16_ref), None, q_scale),
        (SEG_K, (k1_ref, k4_ref, k16_ref), ktail_ref, 1.0),
        (SEG_V, (v1_ref, v4_ref, v16_ref), vtail_ref, 1.0),
    )
    for t in range(tm // sub):
        rs = slice(t * sub, (t + 1) * sub)
        xf = x_ref[rs, :]
        ms = jnp.mean(xf * xf, axis=-1, keepdims=True)
        xb = (xf * lax.rsqrt(ms + EPS) * g_ref[...]).astype(_BF16)
        xn_ref[rs, :] = xb
        cr, sr = cos_r_ref[rs, :], sin_r_ref[rs, :]
        cos = cr * cb - sr * sb
        sin_signed = (sr * cb + cr * sb) * sign_ref[...]
        for seg, (d1_ref, d4_ref, d16_ref), tail_ref, scale in segments:
            z = jnp.dot(xb, w_ref[:, seg * d_seg:(seg + 1) * d_seg], preferred_element_type=_F32)
            for g in range(d_seg // LANES):
                sl = slice(g * LANES, (g + 1) * LANES)
                r = z[:, sl]
                if seg != SEG_V:
                    r = _rope_group(r, cos, sin_signed)
                if tail_ref is not None:
                    tail_ref[sl, rs] = r.T
                if scale != 1.0:
                    r = r * scale
                d1_ref[rs, sl] = r.astype(d1_ref.dtype)
                nat_ref[g] = r
                for r4 in range(4):
                    blk = nat_ref[g, pl.ds(r4, n4, stride=4), :]
                    d4_ref[r4, t * n4:(t + 1) * n4, sl] = blk.astype(d4_ref.dtype)
                    res4_ref[g, r4 * n4:(r4 + 1) * n4, :] = blk
                for r4 in range(4):
                    for a in range(4):
                        blk = res4_ref[g, pl.ds(r4 * n4 + a, n16, stride=4), :]
                        d16_ref[r4 + 4 * a, t * n16:(t + 1) * n16, sl] = blk.astype(d16_ref.dtype)


def _gates_kernel(xn_ref, w_ref, lng_ref, lnb_ref, q_ref, kn_ref, vn_ref, kt_ref, vt_ref,
                  ga_ref, u_ref, vc_ref, gb_ref, oa_ref, mult_ref):
    j = pl.program_id(1)

    @pl.when(jnp.logical_and(pl.program_id(0) == 0, j == 0))
    def _():
        mult_ref[...] = _key_multiplicity(mult_ref.shape[0], mult_ref.shape[1], q_ref.shape[0])

    tm = xn_ref.shape[0]
    sub = min(SUB_ROWS, tm)
    epilogues = (
        (ga_ref, jax.nn.silu),
        (u_ref, jax.nn.gelu),
        (vc_ref, lambda z: _gelu_layer_norm(z, lng_ref[...], lnb_ref[...])),
        (gb_ref, jax.nn.silu),
    )
    for s, (o_ref, epilogue) in enumerate(epilogues):
        @pl.when(j == s)
        def _(s=s, o_ref=o_ref, epilogue=epilogue):
            side = _sample_attn_tile(q_ref, kn_ref, vn_ref, kt_ref, vt_ref, oa_ref, mult_ref)
            for t in range(tm // sub):
                rs = slice(t * sub, (t + 1) * sub)
                z = jnp.dot(xn_ref[rs, :], w_ref[s], preferred_element_type=_F32)
                next(side, None)
                o_ref[rs, :] = epilogue(z).astype(o_ref.dtype)
                if t == 0:
                    next(side, None)
            for _ in side:
                pass


def _gates_proj(xn, w_gate, ln, side, *, tm):
    s_len, d_model = xn.shape
    n_seg, _, d_seg = w_gate.shape
    sq, skn, svn, skt, svt = side
    db, t_new, d_attn = sq.shape
    width, wb = SAMPLE_HEAD_GROUP_WIDTH, skt.shape[2]
    hg = d_attn // width
    n_i = s_len // tm
    assert n_i * n_seg == db * hg
    tile = lambda i, j: i * n_seg + j
    new = pl.BlockSpec((None, t_new, width), lambda i, j: (tile(i, j) // hg, 0, tile(i, j) % hg))
    cache = pl.BlockSpec((None, width, wb), lambda i, j: (tile(i, j) // hg, tile(i, j) % hg, 0))
    row_vec = pl.BlockSpec((1, d_seg), lambda i, j: (0, 0))
    rows = pl.BlockSpec((tm, d_seg), lambda i, j: (i, 0))
    return pl.pallas_call(
        _gates_kernel,
        grid=(n_i, n_seg),
        in_specs=[pl.BlockSpec((tm, d_model), lambda i, j: (i, 0)),
                  pl.BlockSpec(w_gate.shape, lambda i, j: (0, 0, 0), pipeline_mode=pl.Buffered(1)),
                  row_vec, row_vec, new, new, new, cache, cache],
        out_specs=[rows] * n_seg + [new],
        out_shape=[jax.ShapeDtypeStruct((s_len, d_seg), _BF16)] * n_seg
        + [jax.ShapeDtypeStruct((db, t_new, d_attn), _F32)],
        scratch_shapes=[pltpu.VMEM((t_new * width // HEAD_DIM, wb), _F32)],
        compiler_params=_params("arbitrary", "arbitrary"),
        name="gates_proj",
    )(xn, w_gate, ln[0].reshape(1, -1), ln[1].reshape(1, -1), sq, skn, svn, skt, svt)


def _qkv_proj(x, norm_g, w_qkv, *, tm, q_scale, tail_rows):
    s_len, d_model = x.shape
    d_seg = w_qkv.shape[1] // 3
    n_i = s_len // tm
    tail_start = n_i - tail_rows // tm
    tables = _rope_tables_blocked(s_len, tm)
    const = lambda a: pl.BlockSpec(a.shape, lambda i: (0, 0))
    layouts_shape = [jax.ShapeDtypeStruct((s_len, d_seg), _BF16),
                     jax.ShapeDtypeStruct((4, s_len // 4, d_seg), _BF16),
                     jax.ShapeDtypeStruct((16, s_len // 16, d_seg), _BF16)]
    layouts_spec = [pl.BlockSpec((tm, d_seg), lambda i: (i, 0)),
                    pl.BlockSpec((4, tm // 4, d_seg), lambda i: (0, i, 0)),
                    pl.BlockSpec((16, tm // 16, d_seg), lambda i: (0, i, 0))]
    tail_spec = pl.BlockSpec((d_seg, tm), lambda i: (0, jnp.maximum(i - tail_start, 0)),
                             pipeline_mode=pl.Buffered(1))
    tail_shape = jax.ShapeDtypeStruct((d_seg, tail_rows), _F32)
    return pl.pallas_call(
        functools.partial(_qkv_kernel, q_scale=q_scale),
        grid=(n_i,),
        in_specs=[pl.BlockSpec((tm, d_model), lambda i: (i, 0)), pl.BlockSpec((1, d_model), lambda i: (0, 0)),
                  pl.BlockSpec(w_qkv.shape, lambda i: (0, 0), pipeline_mode=pl.Buffered(1))]
        + [const(t) for t in tables],
        out_specs=[pl.BlockSpec((tm, d_model), lambda i: (i, 0))] + layouts_spec * 3 + [tail_spec] * 2,
        out_shape=[jax.ShapeDtypeStruct((s_len, d_model), _BF16)] + layouts_shape * 3 + [tail_shape] * 2,
        scratch_shapes=[pltpu.VMEM((d_seg // LANES, min(SUB_ROWS, tm), LANES), _F32)] * 2,
        compiler_params=_params("arbitrary"),
        name="qkv_proj",
    )(x, norm_g.reshape(1, -1), w_qkv, *tables)


def _attn_kernel(q_ref, kp_ref, kc_ref, vp_ref, vc_ref, o_ref, m_ref, den_ref, k0_ref, *vext_refs, res_blocks):
    n = pl.program_id(1)
    first_step = jnp.logical_and(pl.program_id(0) == 0, n == 0)
    n_pairs = q_ref.shape[1] // LANES
    two = 2 * BLOCK

    @pl.when(first_step)
    def _():
        ones = jnp.ones((two, LANES), k0_ref.dtype)
        for vext_ref in vext_refs:
            for hp in range(n_pairs):
                vext_ref[:, (2 * hp + 1) * LANES:(2 * hp + 2) * LANES] = ones

    k0_ref[0:BLOCK, :] = kp_ref[...]
    k0_ref[BLOCK:, :] = kc_ref[0:BLOCK, :]

    qi = lax.broadcasted_iota(jnp.int32, (two, two), 0) % BLOCK
    si = lax.broadcasted_iota(jnp.int32, (two, two), 1)
    dist = qi + BLOCK - si
    band_bias = jnp.where((dist >= 0) & (dist <= BLOCK), 0.0, NEG_INF).astype(_F32)
    lane = lax.broadcasted_iota(jnp.int32, (BLOCK, LANES), 1)
    head0 = lane < HEAD_DIM
    zero = jnp.zeros((BLOCK, LANES), q_ref.dtype)

    for j, vext_ref in enumerate(vext_refs):
        rows = slice(j * BLOCK, (j + 1) * BLOCK)
        prev_keys = slice((j - 1) * BLOCK, (j + 1) * BLOCK)
        bias = band_bias
        if len(vext_refs) % res_blocks == 0:
            if j % res_blocks == 0:
                bias = band_bias + jnp.where(si < BLOCK, NEG_INF, 0.0)
        elif j == 0:
            bias = band_bias + jnp.where(jnp.logical_and(si < BLOCK, n == 0), NEG_INF, 0.0)
        for hp in range(n_pairs):
            sl = slice(hp * LANES, (hp + 1) * LANES)
            ext = slice(2 * hp * LANES, (2 * hp + 1) * LANES)
            if j == 0:
                vext_ref[0:BLOCK, ext] = vp_ref[:, sl]
                vext_ref[BLOCK:, ext] = vc_ref[0:BLOCK, sl]
            else:
                vext_ref[:, ext] = vc_ref[prev_keys, sl]
        m_acc = jnp.zeros((BLOCK, LANES), _F32)
        den_acc = jnp.ones((BLOCK, LANES), _F32)
        for hp in range(n_pairs):
            sl = slice(hp * LANES, (hp + 1) * LANES)
            q2 = q_ref[rows, sl]
            qs = jnp.concatenate([jnp.where(head0, q2, zero), jnp.where(head0, zero, q2)], axis=0)
            k2 = k0_ref[:, sl] if j == 0 else kc_ref[prev_keys, sl]
            s = lax.dot_general(qs, k2, (((1,), (1,)), ((), ())), preferred_element_type=_F32) + bias
            m = jnp.max(s, axis=-1, keepdims=True)
            p = jnp.exp2(s - m).astype(vext_ref.dtype)
            pv = jnp.dot(p, vext_ref[:, 2 * hp * LANES:(2 * hp + 2) * LANES], preferred_element_type=_F32)
            num, den = pv[:, :LANES], pv[:, LANES:]
            o_ref[rows, sl] = jnp.where(head0, num[:BLOCK], num[BLOCK:]).astype(o_ref.dtype)
            m_acc = jnp.where(lane == 2 * hp, m[:BLOCK], jnp.where(lane == 2 * hp + 1, m[BLOCK:], m_acc))
            den_acc = jnp.where(lane == 2 * hp, den[:BLOCK],
                                jnp.where(lane == 2 * hp + 1, den[BLOCK:], den_acc))
        m_ref[rows, :] = m_acc
        den_ref[rows, :] = den_acc


def _dilated_attention(q, k, v, *, q_rows):
    d_orig, res_len, d_attn = q.shape
    res_blocks = res_len // BLOCK
    if q_rows > res_len:
        q, k, v = (a.reshape(d_orig * res_len // q_rows, q_rows, d_attn) for a in (q, k, v))
    d, sub_len, _ = q.shape
    per = q_rows // BLOCK
    assert per % res_blocks == 0 or res_blocks % per == 0
    cur = pl.BlockSpec((None, q_rows, d_attn), lambda r, n: (r, n, 0))
    prev = pl.BlockSpec((None, BLOCK, d_attn), lambda r, n: (r, jnp.maximum(n * per - 1, 0), 0))
    stat = pl.BlockSpec((None, q_rows, LANES), lambda r, n: (r, n, 0))
    stat_shape = jax.ShapeDtypeStruct((d, sub_len, LANES), _F32)
    o, m, den = pl.pallas_call(
        functools.partial(_attn_kernel, res_blocks=res_blocks),
        grid=(d, sub_len // q_rows),
        in_specs=[cur, prev, cur, prev, cur],
        out_specs=[cur, stat, stat],
        out_shape=[jax.ShapeDtypeStruct((d, sub_len, d_attn), q.dtype), stat_shape, stat_shape],
        scratch_shapes=[pltpu.VMEM((2 * BLOCK, d_attn), q.dtype)]
        + [pltpu.VMEM((2 * BLOCK, 2 * d_attn), q.dtype)] * per,
        compiler_params=_params("arbitrary", "arbitrary"),
        name=f"dilated_attn_d{d_orig}",
    )(q, k, k, v, v)
    return (o.reshape(d_orig, res_len, d_attn), m.reshape(d_orig, res_len, LANES),
            den.reshape(d_orig, res_len, LANES))


def _out_proj_kernel(o1_ref, o4_ref, o16_ref, m1_ref, m4_ref, m16_ref, d1_ref, d4_ref, d16_ref,
                     ga_ref, u_ref, vc_ref, gb_ref, x_ref, wb_ref, ws_ref, bst_ref, fg_ref, y_ref,
                     ac_ref, n4_ref, n16_ref, tmp_ref, st_ref):
    tm, d_attn = o1_ref.shape
    n_lane_groups = d_attn // LANES
    sub = 2 * CHUNK
    n4, n16 = sub // 4, sub // 16

    pair_idx = lax.broadcasted_iota(jnp.int32, (sub, LANES), 1) // HEAD_DIM
    tril = (lax.broadcasted_iota(jnp.int32, (CHUNK, CHUNK), 0)
            >= lax.broadcasted_iota(jnp.int32, (CHUNK, CHUNK), 1))

    for t in range(tm // sub):
        rs = slice(t * sub, (t + 1) * sub)
        for r in range(4):
            dst = pl.ds(t * sub + r, n4, stride=4)
            src = slice(t * n4, (t + 1) * n4)
            st_ref[0, dst, :] = m4_ref[r, src, :]
            st_ref[1, dst, :] = d4_ref[r, src, :]
            for g in range(n_lane_groups):
                n4_ref[g, dst, :] = o4_ref[r, src, g * LANES:(g + 1) * LANES].astype(_F32)
        for r in range(16):
            dst = pl.ds(t * sub + r, n16, stride=16)
            src = slice(t * n16, (t + 1) * n16)
            st_ref[2, dst, :] = m16_ref[r, src, :]
            st_ref[3, dst, :] = d16_ref[r, src, :]
        for g in range(n_lane_groups):
            sl = slice(g * LANES, (g + 1) * LANES)
            for r4 in range(4):
                for a in range(4):
                    blk = o16_ref[r4 + 4 * a, t * n16:(t + 1) * n16, sl].astype(_F32)
                    tmp_ref[g, pl.ds(t * sub + r4 * n4 + a, n16, stride=4), :] = blk
                n16_ref[g, pl.ds(t * sub + r4, n4, stride=4), :] = tmp_ref[g, t * sub + r4 * n4:
                                                                          t * sub + (r4 + 1) * n4, :]

        ms = [m1_ref[rs, :], st_ref[0, rs, :], st_ref[2, rs, :]]
        dens = [d1_ref[rs, :], st_ref[1, rs, :], st_ref[3, rs, :]]
        mx = jnp.maximum(jnp.maximum(ms[0], ms[1]), ms[2])
        es = [jnp.exp2(m - mx) for m in ms]
        tot = es[0] * dens[0] + es[1] * dens[1] + es[2] * dens[2]
        ws = [e / tot for e in es]
        for g in range(n_lane_groups):
            sl = slice(g * LANES, (g + 1) * LANES)
            w1, w4, w16 = (jnp.take_along_axis(w, pair_idx + 2 * g, axis=1) for w in ws)
            a = w1 * o1_ref[rs, sl].astype(_F32) + w4 * n4_ref[g, rs, :] + w16 * n16_ref[g, rs, :]
            ac_ref[rs, sl] = (a * ga_ref[rs, sl].astype(_F32)).astype(ac_ref.dtype)

        c0 = slice(t * sub, t * sub + CHUNK)
        c1 = slice(t * sub + CHUNK, (t + 1) * sub)
        for g in range(ws_ref.shape[0]):
            wm = jnp.where(tril, ws_ref[g], 0.0).astype(_BF16)
            gs = slice(g * GROUP_WIDTH_B, (g + 1) * GROUP_WIDTH_B)
            vc2 = jnp.concatenate([vc_ref[c0, gs], vc_ref[c1, gs]], axis=1)
            mixed = jnp.dot(wm, vc2, preferred_element_type=_F32) + bst_ref[:, g:g + 1]
            for c, half in ((c0, slice(0, GROUP_WIDTH_B)), (c1, slice(GROUP_WIDTH_B, 2 * GROUP_WIDTH_B))):
                cv = u_ref[c, gs].astype(_F32) * mixed[:, half] * gb_ref[c, gs].astype(_F32)
                ac_ref[c, d_attn + g * GROUP_WIDTH_B:d_attn + (g + 1) * GROUP_WIDTH_B] = cv.astype(ac_ref.dtype)

    for t in range(tm // sub):
        rs = slice(t * sub, (t + 1) * sub)
        acc = jnp.dot(ac_ref[rs, :], wb_ref[...], preferred_element_type=_F32)
        xo = x_ref[rs, :] + acc
        msq = jnp.mean(xo * xo, axis=-1, keepdims=True)
        y_ref[rs, :] = xo * lax.rsqrt(msq + EPS) * fg_ref[...]


def _out_proj(os, ms, dens, ga, u, vc, gb, x, w_out, w_s, b_s_t, final_g, *, tm):
    m, d_model = x.shape
    d_attn = ga.shape[1]
    d_chunk = u.shape[1]
    rows = lambda n: pl.BlockSpec((tm, n), lambda i: (i, 0))
    res = lambda d, n: pl.BlockSpec((d, tm // d, n), lambda i: (0, i, 0))
    whole = lambda a: pl.BlockSpec(a.shape, lambda i: (0,) * a.ndim)
    by_pattern = lambda n: [rows(n), res(4, n), res(16, n)]
    fg = final_g.reshape(1, -1)
    slab = pltpu.VMEM((d_attn // LANES, tm, LANES), _F32)
    return pl.pallas_call(
        _out_proj_kernel,
        grid=(m // tm,),
        in_specs=by_pattern(d_attn) + by_pattern(LANES) + by_pattern(LANES)
        + [rows(d_attn), rows(d_chunk), rows(d_chunk), rows(d_chunk), rows(d_model),
           pl.BlockSpec(w_out.shape, lambda i: (0, 0), pipeline_mode=pl.Buffered(1)),
           whole(w_s), whole(b_s_t), whole(fg)],
        out_specs=rows(d_model),
        out_shape=jax.ShapeDtypeStruct((m, d_model), _F32),
        scratch_shapes=[pltpu.VMEM((tm, d_attn + d_chunk), _BF16), slab, slab, slab,
                        pltpu.VMEM((4, tm, LANES), _F32)],
        compiler_params=_params("arbitrary"),
        name="out_proj",
    )(*os, *ms, *dens, ga, u, vc, gb, x, w_out, w_s, b_s_t, fg)


def _sample_in_proj_kernel(x_ref, g_ref, w_ref, cos_ref, sin_ref, lng_ref, lnb_ref,
                           q_ref, k_ref, v_ref, ga_ref, u_ref, vc_ref, gb_ref, wqkv_ref, wgate_ref, xn_ref):
    j = pl.program_id(0)

    @pl.when(j == 0)
    def _():
        xf = x_ref[...]
        ms = jnp.mean(xf * xf, axis=-1, keepdims=True)
        xn_ref[...] = (xf * lax.rsqrt(ms + EPS) * g_ref[...]).astype(xn_ref.dtype)

    wb = w_ref[...].astype(_BF16)

    @pl.when(j <= SEG_V)
    def _():
        wqkv_ref[...] = wb

    @pl.when(j > SEG_V)
    def _():
        wgate_ref[...] = wb

    z = jnp.dot(xn_ref[...], wb, preferred_element_type=_F32)

    def rope_to(ref, scale):
        for g in range(z.shape[1] // LANES):
            sl = slice(g * LANES, (g + 1) * LANES)
            ref[:, sl] = _rope_group(z[:, sl], cos_ref[...], sin_ref[...]) * scale

    @pl.when(j == SEG_Q)
    def _():
        rope_to(q_ref, HEAD_DIM ** -0.5)

    @pl.when(j == SEG_K)
    def _():
        rope_to(k_ref, 1.0)

    @pl.when(j == SEG_V)
    def _():
        v_ref[...] = z

    @pl.when(j == SEG_GA)
    def _():
        ga_ref[...] = jax.nn.silu(z)

    @pl.when(j == SEG_U)
    def _():
        u_ref[...] = jax.nn.gelu(z)

    @pl.when(j == SEG_VC)
    def _():
        vc_ref[...] = _gelu_layer_norm(z, lng_ref[...], lnb_ref[...])

    @pl.when(j == SEG_GB)
    def _():
        gb_ref[...] = jax.nn.silu(z)


def _sample_in_proj(x, norm_g, w_in, cos_t, sin_t, ln_g, ln_b):
    m, d_model = x.shape
    d_seg = w_in.shape[1] // N_SEGMENTS
    whole = lambda r, c: pl.BlockSpec((r, c), lambda j: (0, 0))
    return pl.pallas_call(
        _sample_in_proj_kernel,
        grid=(N_SEGMENTS,),
        in_specs=[whole(m, d_model), whole(1, d_model),
                  pl.BlockSpec((d_model, d_seg), lambda j: (0, j)),
                  whole(m, LANES), whole(m, LANES), whole(1, d_seg), whole(1, d_seg)],
        out_specs=[whole(m, d_seg)] * N_SEGMENTS
        + [pl.BlockSpec((d_model, d_seg), lambda j: (0, jnp.minimum(j, SEG_V))),
           pl.BlockSpec((None, d_model, d_seg), lambda j: (jnp.maximum(j - SEG_GA, 0), 0, 0))],
        out_shape=[jax.ShapeDtypeStruct((m, d_seg), _F32)] * N_SEGMENTS
        + [jax.ShapeDtypeStruct((d_model, (SEG_V + 1) * d_seg), _BF16),
           jax.ShapeDtypeStruct((N_SEGMENTS - SEG_GA, d_model, d_seg), _BF16)],
        scratch_shapes=[pltpu.VMEM((m, d_model), _BF16)],
        compiler_params=_params("arbitrary"),
        name="sample_in_proj",
    )(x, norm_g.reshape(1, -1), w_in, cos_t, sin_t, ln_g.reshape(1, -1), ln_b.reshape(1, -1))


def _key_multiplicity(rows, wb, t_new):
    t_row = lax.broadcasted_iota(jnp.int32, (rows, wb), 0) // (rows // t_new)
    pos = lax.broadcasted_iota(jnp.int32, (rows, wb), 1)
    dist = wb + t_row - pos
    mult = jnp.zeros((rows, wb), _F32)
    for window, d in DILATIONS:
        mult = mult + jnp.where((dist % d == 0) & (dist <= window), 1.0, 0.0)
    return mult


def _sample_attn_tile(q_ref, kn_ref, vn_ref, kt_ref, vt_ref, o_ref, mult_ref):
    t_new, width = q_ref.shape
    n_heads = width // HEAD_DIM
    rows = t_new * n_heads
    head_row = lax.broadcasted_iota(jnp.int32, (n_heads, width), 0)
    head_col = lax.broadcasted_iota(jnp.int32, (n_heads, width), 1) // HEAD_DIM
    own = head_row == head_col
    qbd = jnp.concatenate([jnp.where(own, q_ref[t:t + 1, :], 0.0) for t in range(t_new)], axis=0)
    mult = mult_ref[...]

    s = jnp.dot(qbd.astype(_BF16), kt_ref[...].astype(_BF16), preferred_element_type=_F32)
    yield
    s = jnp.where(mult > 0.0, s, NEG_INF)
    m = jnp.max(s, axis=-1, keepdims=True)
    t_col = lax.broadcasted_iota(jnp.int32, (rows, 1), 0) // n_heads
    s_new, mult_new = [], []
    for tp in range(t_new):
        s_new.append(jnp.sum(qbd * kn_ref[tp:tp + 1, :], axis=-1, keepdims=True))
        dn = t_col - tp
        mn = jnp.zeros((rows, 1), _F32)
        for window, d in DILATIONS:
            mn = mn + jnp.where((dn >= 0) & (dn % d == 0), 1.0, 0.0)
        mult_new.append(mn)
        m = jnp.maximum(m, jnp.where(mn > 0.0, s_new[tp], NEG_INF))
    p = jnp.exp(s - m) * mult
    den = jnp.sum(p, axis=-1, keepdims=True)
    yield
    num = lax.dot_general(p.astype(_BF16), vt_ref[...].astype(_BF16), (((1,), (1,)), ((), ())),
                          preferred_element_type=_F32)
    for tp in range(t_new):
        pn = jnp.where(mult_new[tp] > 0.0, jnp.exp(s_new[tp] - m), 0.0) * mult_new[tp]
        den = den + pn
        num = num + pn * vn_ref[tp:tp + 1, :]
    res = num / den
    for t in range(t_new):
        blk = res[t * n_heads:(t + 1) * n_heads, :]
        o_ref[t:t + 1, :] = jnp.sum(jnp.where(own, blk, 0.0), axis=0, keepdims=True)


def _sample_out_kernel(oa_ref, ga_ref, u_ref, vc_ref, gb_ref, x_ref, wout_ref, coef_ref, bias_ref, fg_ref,
                       y_ref, wb_ref, c_ref, *, t_new):
    d_attn = oa_ref.shape[1]
    wb_ref[...] = wout_ref[...].astype(wb_ref.dtype)
    a = (oa_ref[...] * ga_ref[...]).astype(_BF16)
    vc = vc_ref[...]
    mixed = bias_ref[...] + coef_ref[0] * vc
    for delta in range(1, t_new):
        mixed = mixed + coef_ref[delta] * pltpu.roll(vc, delta, 0)
    c_ref[...] = (u_ref[...] * mixed * gb_ref[...]).astype(c_ref.dtype)
    acc = jnp.dot(a, wb_ref[:d_attn, :], preferred_element_type=_F32)
    acc = acc + jnp.dot(c_ref[...], wb_ref[d_attn:, :], preferred_element_type=_F32)
    xo = x_ref[...] + acc
    ms = jnp.mean(xo * xo, axis=-1, keepdims=True)
    y_ref[...] = xo * lax.rsqrt(ms + EPS) * fg_ref[...]


def _sample_out(oa, ga, u, vc, gb, x, w_out, coef, bias, final_g):
    m, d_model = x.shape
    args = (oa, ga, u, vc, gb, x, w_out, coef, bias, final_g.reshape(1, -1))
    whole = lambda a: pl.BlockSpec(a.shape, lambda i: (0,) * a.ndim)
    return pl.pallas_call(
        functools.partial(_sample_out_kernel, t_new=coef.shape[0]),
        grid=(1,),
        in_specs=[whole(a) for a in args],
        out_specs=[pl.BlockSpec((m, d_model), lambda i: (0, 0)), whole(w_out)],
        out_shape=[jax.ShapeDtypeStruct((m, d_model), _F32), jax.ShapeDtypeStruct(w_out.shape, _BF16)],
        scratch_shapes=[pltpu.VMEM((m, u.shape[1]), _BF16)],
        compiler_params=_params("arbitrary"),
        name="sample_out",
    )(*args)


def kernel(x_prompt, x_sample, cache_k, cache_v, norm_g, w_in, ln_g, ln_b, w_s, b_s, w_out, final_g):
    batch, s_len, d_model = x_prompt.shape
    db, t_new, _ = x_sample.shape
    depth, _, wb, n_heads, head_dim = cache_k.shape
    d_attn = n_heads * head_dim
    d_chunk = w_out.shape[1] - d_attn
    assert batch == 1 and depth == 1 and head_dim == HEAD_DIM
    assert w_in.shape[2] == 4 * d_attn + 3 * d_chunk and d_attn == d_chunk
    tail = min(MAX_WINDOW, s_len)
    w_in0, w_out0 = w_in[0], w_out[0]
    ng, lg, lb = norm_g[0], ln_g[0], ln_b[0]

    xs = x_sample.reshape(db * t_new, d_model)
    pos_s = PAST_LEN + jnp.tile(jnp.arange(t_new, dtype=jnp.int32), db)
    cos_s, sin_s = _rope_tables(pos_s)
    qs, ks, vs, gas, us, vcs, gbs, w_qkv_b, w_gate_b = _sample_in_proj(xs, ng, w_in0, cos_s, sin_s, lg, lb)
    qs_b, ks_b, vs_b, vcs_b = (a.reshape(db, t_new, -1) for a in (qs, ks, vs, vcs))
    cache_t = lambda c: jnp.transpose(c[0], (0, 2, 3, 1)).reshape(db, d_attn, wb)
    cache_kt, cache_vt = cache_t(cache_k), cache_t(cache_v)
    assert wb == MAX_WINDOW

    xp = x_prompt.reshape(s_len, d_model)
    (xn, q1, q4, q16, k1, k4, k16, v1, v4, v16, k_tail_t, v_tail_t) = _qkv_proj(
        xp, ng, w_qkv_b, tm=512, q_scale=HEAD_DIM ** -0.5 * math.log2(math.e), tail_rows=tail)
    ga, u, vc, gb, oa = _gates_proj(xn, w_gate_b, (lg, lb), (qs_b, ks_b, vs_b, cache_kt, cache_vt), tm=512)
    o1, m1, d1 = _dilated_attention(q1[None], k1[None], v1[None], q_rows=1024)
    o4, m4, d4 = _dilated_attention(q4, k4, v4, q_rows=1024)
    o16, m16, d16 = _dilated_attention(q16, k16, v16, q_rows=1024)

    w_ts = jnp.transpose(w_s[0][:, :t_new, :t_new], (1, 2, 0))
    coef = jnp.stack([jnp.stack([w_ts[t, t - delta] if t >= delta else jnp.zeros_like(w_ts[0, 0])
                                 for t in range(t_new)]) for delta in range(t_new)])
    coef = jnp.tile(jnp.repeat(coef, GROUP_WIDTH_B, axis=2), (1, db, 1))
    bias = jnp.tile(jnp.repeat(jnp.transpose(b_s[0][:, :t_new], (1, 0)), GROUP_WIDTH_B, axis=1), (db, 1))
    ys, w_out_b = _sample_out(oa.reshape(db * t_new, d_attn), gas, us, vcs, gbs, xs, w_out0, coef, bias, final_g)
    y_sample = ys.reshape(db, t_new, d_model)

    y_prompt = _out_proj((o1[0], o4, o16), (m1[0], m4, m16), (d1[0], d4, d16), ga, u, vc, gb, xp, w_out_b,
                         w_s[0], b_s[0].T, final_g, tm=512)
    untranspose = lambda a: jnp.transpose(a.reshape(n_heads, head_dim, tail), (2, 0, 1))

    hs = (n_heads, head_dim)
    return (
        y_prompt.reshape(batch, s_len, d_model),
        y_sample,
        untranspose(k_tail_t).reshape(depth, batch, tail, *hs),
        untranspose(v_tail_t).reshape(depth, batch, tail, *hs),
        ks_b.reshape(depth, db, t_new, *hs),
        vs_b.reshape(depth, db, t_new, *hs),
        vcs_b.reshape(depth, db, t_new, d_chunk),
    )
```

```python
import functools
import math

import jax
import jax.numpy as jnp
from jax import lax
from jax.experimental import pallas as pl
from jax.experimental.pallas import tpu as pltpu

HEAD_DIM = 64
BLOCK = 128
CHUNK = 128
GROUP_WIDTH_B = 128
DILATIONS = ((128, 1), (512, 4), (2048, 16))
MAX_WINDOW = 2048
PAST_LEN = 16384
ROPE_THETA = 10000.0
EPS = 1e-6
NEG_INF = -1e30
N_SEGMENTS = 7
SEG_Q, SEG_K, SEG_V, SEG_GA, SEG_U, SEG_VC, SEG_GB = range(N_SEGMENTS)

LANES = 128
VMEM_LIMIT_BYTES = 56 * 1024 * 1024
SUB_ROWS = 256
SAMPLE_HEAD_GROUP_WIDTH = 512

_BF16 = jnp.bfloat16
_F32 = jnp.float32


def _params(*semantics):
    return pltpu.CompilerParams(dimension_semantics=semantics, vmem_limit_bytes=VMEM_LIMIT_BYTES)


def _rope_tables(pos):
    half = HEAD_DIM // 2
    inv = jnp.exp(-math.log(ROPE_THETA) * jnp.arange(half, dtype=_F32) / half)
    ang = pos.astype(_F32)[:, None] * inv[None, :]
    cos = jnp.cos(ang)
    sin = jnp.sin(ang)
    cos_t = jnp.concatenate([cos, cos, cos, cos], axis=-1)
    sin_t = jnp.concatenate([-sin, sin, -sin, sin], axis=-1)
    return cos_t, sin_t


def _rope_tables_blocked(s_len, tm):
    half = HEAD_DIM // 2
    inv = jnp.exp(-math.log(ROPE_THETA) * jnp.arange(half, dtype=_F32) / half)
    tile4 = lambda a: jnp.concatenate([a, a, a, a], axis=-1)
    ang_r = jnp.arange(tm, dtype=jnp.int32).astype(_F32)[:, None] * inv[None, :]
    ang_b = (jnp.arange(s_len // tm, dtype=jnp.int32) * tm).astype(_F32)[:, None] * inv[None, :]
    sign = jnp.concatenate([-jnp.ones((1, half), _F32), jnp.ones((1, half), _F32)] * 2, axis=-1)
    return (tile4(jnp.cos(ang_r)), tile4(jnp.sin(ang_r)), tile4(jnp.cos(ang_b)), tile4(jnp.sin(ang_b)), sign)


def _rope_group(zg, cos, sin_signed):
    lane = lax.broadcasted_iota(jnp.int32, zg.shape, 1)
    first_half = (lane % HEAD_DIM) < (HEAD_DIM // 2)
    partner = jnp.where(first_half, pltpu.roll(zg, LANES - HEAD_DIM // 2, 1),
                        pltpu.roll(zg, HEAD_DIM // 2, 1))
    return zg * cos + partner * sin_signed


def _gelu_layer_norm(z, g, b):
    h = jax.nn.gelu(z)
    mu = jnp.mean(h, axis=-1, keepdims=True)
    hc = h - mu
    var = jnp.mean(hc * hc, axis=-1, keepdims=True)
    return hc * lax.rsqrt(var + EPS) * g + b


def _qkv_kernel(x_ref, g_ref, w_ref, cos_r_ref, sin_r_ref, cos_b_ref, sin_b_ref, sign_ref,
                xn_ref, q1_ref, q4_ref, q16_ref, k1_ref, k4_ref, k16_ref, v1_ref, v4_ref, v16_ref,
                ktail_ref, vtail_ref, nat_ref, res4_ref, *, q_scale):
    tm = x_ref.shape[0]
    d_seg = q1_ref.shape[1]
    sub = min(SUB_ROWS, tm)
    n4, n16 = sub // 4, sub // 16
    step = pl.ds(pl.program_id(0), 1)
    cb, sb = cos_b_ref[step, :], sin_b_ref[step, :]
    segments = (
        (SEG_Q, (q1_ref, q4_ref, q16_ref), None, q_scale),
        (SEG_K, (k1_ref, k4_ref, k16_ref), ktail_ref, 1.0),
        (SEG_V, (v1_ref, v4_ref, v16_ref), vtail_ref, 1.0),
    )
    for t in range(tm // sub):
        rs = slice(t * sub, (t + 1) * sub)
        xf = x_ref[rs, :]
        ms = jnp.mean(xf * xf, axis=-1, keepdims=True)
        xb = (xf * lax.rsqrt(ms + EPS) * g_ref[...]).astype(_BF16)
        xn_ref[rs, :] = xb
        cr, sr = cos_r_ref[rs, :], sin_r_ref[rs, :]
        cos = cr * cb - sr * sb
        sin_signed = (sr * cb + cr * sb) * sign_ref[...]
        for seg, (d1_ref, d4_ref, d16_ref), tail_ref, scale in segments:
            z = jnp.dot(xb, w_ref[:, seg * d_seg:(seg + 1) * d_seg], preferred_element_type=_F32)
            for g in range(d_seg // LANES):
                sl = slice(g * LANES, (g + 1) * LANES)
                r = z[:, sl]
                if seg != SEG_V:
                    r = _rope_group(r, cos, sin_signed)
                if tail_ref is not None:
                    tail_ref[sl, rs] = r.T
                if scale != 1.0:
                    r = r * scale
                d1_ref[rs, sl] = r.astype(d1_ref.dtype)
                nat_ref[g] = r
                for r4 in range(4):
                    blk = nat_ref[g, pl.ds(r4, n4, stride=4), :]
                    d4_ref[r4, t * n4:(t + 1) * n4, sl] = blk.astype(d4_ref.dtype)
                    res4_ref[g, r4 * n4:(r4 + 1) * n4, :] = blk
                for r4 in range(4):
                    for a in range(4):
                        blk = res4_ref[g, pl.ds(r4 * n4 + a, n16, stride=4), :]
                        d16_ref[r4 + 4 * a, t * n16:(t + 1) * n16, sl] = blk.astype(d16_ref.dtype)


def _gates_kernel(xn_ref, w_ref, lng_ref, lnb_ref, ws_ref, bst_ref, q_ref, kn_ref, vn_ref, kt_ref, vt_ref,
                  ga_ref, c_ref, oa_ref, u_ref, vc_ref, mult_ref):
    j = pl.program_id(1)

    @pl.when(jnp.logical_and(pl.program_id(0) == 0, j == 0))
    def _():
        mult_ref[...] = _key_multiplicity(mult_ref.shape[0], mult_ref.shape[1], q_ref.shape[0])

    tm = xn_ref.shape[0]
    sub = 2 * CHUNK
    tril = (lax.broadcasted_iota(jnp.int32, (CHUNK, CHUNK), 0)
            >= lax.broadcasted_iota(jnp.int32, (CHUNK, CHUNK), 1))

    def gated_gmlp(z, rs):
        gb = jax.nn.silu(z)
        c0 = slice(rs.start, rs.start + CHUNK)
        c1 = slice(rs.start + CHUNK, rs.stop)
        for g in range(ws_ref.shape[0]):
            wm = jnp.where(tril, ws_ref[g], 0.0).astype(_BF16)
            gs = slice(g * GROUP_WIDTH_B, (g + 1) * GROUP_WIDTH_B)
            vc2 = jnp.concatenate([vc_ref[c0, gs], vc_ref[c1, gs]], axis=1)
            mixed = jnp.dot(wm, vc2, preferred_element_type=_F32) + bst_ref[:, g:g + 1]
            for c, half in ((c0, slice(0, GROUP_WIDTH_B)), (c1, slice(GROUP_WIDTH_B, 2 * GROUP_WIDTH_B))):
                local = slice(c.start - rs.start, c.stop - rs.start)
                cv = u_ref[c, gs].astype(_F32) * mixed[:, half] * gb[local, gs]
                c_ref[c, gs] = cv.astype(c_ref.dtype)

    def store(ref, fn):
        def write(z, rs):
            ref[rs, :] = fn(z).astype(ref.dtype)
        return write

    epilogues = (
        store(ga_ref, jax.nn.silu),
        store(u_ref, jax.nn.gelu),
        store(vc_ref, lambda z: _gelu_layer_norm(z, lng_ref[...], lnb_ref[...])),
        gated_gmlp,
    )
    for s, epilogue in enumerate(epilogues):
        @pl.when(j == s)
        def _(s=s, epilogue=epilogue):
            side = _sample_attn_tile(q_ref, kn_ref, vn_ref, kt_ref, vt_ref, oa_ref, mult_ref)
            for t in range(tm // sub):
                rs = slice(t * sub, (t + 1) * sub)
                z = jnp.dot(xn_ref[rs, :], w_ref[s], preferred_element_type=_F32)
                next(side, None)
                epilogue(z, rs)
                if t == 0:
                    next(side, None)
            for _ in side:
                pass


def _gates_proj(xn, w_gate, ln, w_s, b_s_t, side, *, tm):
    s_len, d_model = xn.shape
    n_seg, _, d_seg = w_gate.shape
    sq, skn, svn, skt, svt = side
    db, t_new, d_attn = sq.shape
    width, wb = SAMPLE_HEAD_GROUP_WIDTH, skt.shape[2]
    hg = d_attn // width
    n_i = s_len // tm
    assert n_i * n_seg == db * hg
    tile = lambda i, j: i * n_seg + j
    new = pl.BlockSpec((None, t_new, width), lambda i, j: (tile(i, j) // hg, 0, tile(i, j) % hg))
    cache = pl.BlockSpec((None, width, wb), lambda i, j: (tile(i, j) // hg, tile(i, j) % hg, 0))
    row_vec = pl.BlockSpec((1, d_seg), lambda i, j: (0, 0))
    rows = pl.BlockSpec((tm, d_seg), lambda i, j: (i, 0))
    whole = lambda a: pl.BlockSpec(a.shape, lambda i, j: (0,) * a.ndim)
    return pl.pallas_call(
        _gates_kernel,
        grid=(n_i, n_seg),
        in_specs=[pl.BlockSpec((tm, d_model), lambda i, j: (i, 0)),
                  pl.BlockSpec(w_gate.shape, lambda i, j: (0, 0, 0), pipeline_mode=pl.Buffered(1)),
                  row_vec, row_vec, whole(w_s), whole(b_s_t), new, new, new, cache, cache],
        out_specs=[rows, rows, new],
        out_shape=[jax.ShapeDtypeStruct((s_len, d_seg), _BF16)] * 2
        + [jax.ShapeDtypeStruct((db, t_new, d_attn), _F32)],
        scratch_shapes=[pltpu.VMEM((tm, d_seg), _BF16), pltpu.VMEM((tm, d_seg), _BF16),
                        pltpu.VMEM((t_new * width // HEAD_DIM, wb), _F32)],
        compiler_params=_params("arbitrary", "arbitrary"),
        name="gates_proj",
    )(xn, w_gate, ln[0].reshape(1, -1), ln[1].reshape(1, -1), w_s, b_s_t, sq, skn, svn, skt, svt)


def _qkv_proj(x, norm_g, w_qkv, *, tm, q_scale, tail_rows):
    s_len, d_model = x.shape
    d_seg = w_qkv.shape[1] // 3
    n_i = s_len // tm
    tail_start = n_i - tail_rows // tm
    tables = _rope_tables_blocked(s_len, tm)
    const = lambda a: pl.BlockSpec(a.shape, lambda i: (0, 0))
    layouts_shape = [jax.ShapeDtypeStruct((s_len, d_seg), _BF16),
                     jax.ShapeDtypeStruct((4, s_len // 4, d_seg), _BF16),
                     jax.ShapeDtypeStruct((16, s_len // 16, d_seg), _BF16)]
    layouts_spec = [pl.BlockSpec((tm, d_seg), lambda i: (i, 0)),
                    pl.BlockSpec((4, tm // 4, d_seg), lambda i: (0, i, 0)),
                    pl.BlockSpec((16, tm // 16, d_seg), lambda i: (0, i, 0))]
    tail_spec = pl.BlockSpec((d_seg, tm), lambda i: (0, jnp.maximum(i - tail_start, 0)))
    tail_shape = jax.ShapeDtypeStruct((d_seg, tail_rows), _F32)
    return pl.pallas_call(
        functools.partial(_qkv_kernel, q_scale=q_scale),
        grid=(n_i,),
        in_specs=[pl.BlockSpec((tm, d_model), lambda i: (i, 0)), pl.BlockSpec((1, d_model), lambda i: (0, 0)),
                  pl.BlockSpec(w_qkv.shape, lambda i: (0, 0), pipeline_mode=pl.Buffered(1))]
        + [const(t) for t in tables],
        out_specs=[pl.BlockSpec((tm, d_model), lambda i: (i, 0))] + layouts_spec * 3 + [tail_spec] * 2,
        out_shape=[jax.ShapeDtypeStruct((s_len, d_model), _BF16)] + layouts_shape * 3 + [tail_shape] * 2,
        scratch_shapes=[pltpu.VMEM((d_seg // LANES, min(SUB_ROWS, tm), LANES), _F32)] * 2,
        compiler_params=_params("arbitrary"),
        name="qkv_proj",
    )(x, norm_g.reshape(1, -1), w_qkv, *tables)


def _attn_kernel(q_ref, kp_ref, kc_ref, vp_ref, vc_ref, o_ref, m_ref, den_ref, k0_ref, *vext_refs, res_blocks):
    n = pl.program_id(1)
    first_step = jnp.logical_and(pl.program_id(0) == 0, n == 0)
    n_pairs = q_ref.shape[1] // LANES
    two = 2 * BLOCK

    @pl.when(first_step)
    def _():
        ones = jnp.ones((two, LANES), k0_ref.dtype)
        for vext_ref in vext_refs:
            for hp in range(n_pairs):
                vext_ref[:, (2 * hp + 1) * LANES:(2 * hp + 2) * LANES] = ones

    k0_ref[0:BLOCK, :] = kp_ref[...]
    k0_ref[BLOCK:, :] = kc_ref[0:BLOCK, :]

    qi = lax.broadcasted_iota(jnp.int32, (two, two), 0) % BLOCK
    si = lax.broadcasted_iota(jnp.int32, (two, two), 1)
    dist = qi + BLOCK - si
    band_bias = jnp.where((dist >= 0) & (dist <= BLOCK), 0.0, NEG_INF).astype(_F32)
    lane = lax.broadcasted_iota(jnp.int32, (BLOCK, LANES), 1)
    head0 = lane < HEAD_DIM
    zero = jnp.zeros((BLOCK, LANES), q_ref.dtype)

    for j, vext_ref in enumerate(vext_refs):
        rows = slice(j * BLOCK, (j + 1) * BLOCK)
        prev_keys = slice((j - 1) * BLOCK, (j + 1) * BLOCK)
        bias = band_bias
        if len(vext_refs) % res_blocks == 0:
            if j % res_blocks == 0:
                bias = band_bias + jnp.where(si < BLOCK, NEG_INF, 0.0)
        elif j == 0:
            bias = band_bias + jnp.where(jnp.logical_and(si < BLOCK, n == 0), NEG_INF, 0.0)
        for hp in range(n_pairs):
            sl = slice(hp * LANES, (hp + 1) * LANES)
            ext = slice(2 * hp * LANES, (2 * hp + 1) * LANES)
            if j == 0:
                vext_ref[0:BLOCK, ext] = vp_ref[:, sl]
                vext_ref[BLOCK:, ext] = vc_ref[0:BLOCK, sl]
            else:
                vext_ref[:, ext] = vc_ref[prev_keys, sl]
        m_acc = jnp.zeros((BLOCK, LANES), _F32)
        den_acc = jnp.ones((BLOCK, LANES), _F32)
        for hp in range(n_pairs):
            sl = slice(hp * LANES, (hp + 1) * LANES)
            q2 = q_ref[rows, sl]
            qs = jnp.concatenate([jnp.where(head0, q2, zero), jnp.where(head0, zero, q2)], axis=0)
            k2 = k0_ref[:, sl] if j == 0 else kc_ref[prev_keys, sl]
            s = lax.dot_general(qs, k2, (((1,), (1,)), ((), ())), preferred_element_type=_F32) + bias
            m = jnp.max(s, axis=-1, keepdims=True)
            p = jnp.exp2(s - m).astype(vext_ref.dtype)
            pv = jnp.dot(p, vext_ref[:, 2 * hp * LANES:(2 * hp + 2) * LANES], preferred_element_type=_F32)
            num, den = pv[:, :LANES], pv[:, LANES:]
            o_ref[rows, sl] = jnp.where(head0, num[:BLOCK], num[BLOCK:]).astype(o_ref.dtype)
            m_acc = jnp.where(lane == 2 * hp, m[:BLOCK], jnp.where(lane == 2 * hp + 1, m[BLOCK:], m_acc))
            den_acc = jnp.where(lane == 2 * hp, den[:BLOCK],
                                jnp.where(lane == 2 * hp + 1, den[BLOCK:], den_acc))
        m_ref[rows, :] = m_acc
        den_ref[rows, :] = den_acc


def _dilated_attention(q, k, v, *, q_rows):
    d_orig, res_len, d_attn = q.shape
    res_blocks = res_len // BLOCK
    if q_rows > res_len:
        q, k, v = (a.reshape(d_orig * res_len // q_rows, q_rows, d_attn) for a in (q, k, v))
    d, sub_len, _ = q.shape
    per = q_rows // BLOCK
    assert per % res_blocks == 0 or res_blocks % per == 0
    cur = pl.BlockSpec((None, q_rows, d_attn), lambda r, n: (r, n, 0))
    prev = pl.BlockSpec((None, BLOCK, d_attn), lambda r, n: (r, jnp.maximum(n * per - 1, 0), 0))
    stat = pl.BlockSpec((None, q_rows, LANES), lambda r, n: (r, n, 0))
    stat_shape = jax.ShapeDtypeStruct((d, sub_len, LANES), _F32)
    o, m, den = pl.pallas_call(
        functools.partial(_attn_kernel, res_blocks=res_blocks),
        grid=(d, sub_len // q_rows),
        in_specs=[cur, prev, cur, prev, cur],
        out_specs=[cur, stat, stat],
        out_shape=[jax.ShapeDtypeStruct((d, sub_len, d_attn), q.dtype), stat_shape, stat_shape],
        scratch_shapes=[pltpu.VMEM((2 * BLOCK, d_attn), q.dtype)]
        + [pltpu.VMEM((2 * BLOCK, 2 * d_attn), q.dtype)] * per,
        compiler_params=_params("arbitrary", "arbitrary"),
        name=f"dilated_attn_d{d_orig}",
    )(q, k, k, v, v)
    return (o.reshape(d_orig, res_len, d_attn), m.reshape(d_orig, res_len, LANES),
            den.reshape(d_orig, res_len, LANES))


def _out_proj_kernel(o1_ref, o4_ref, o16_ref, m1_ref, m4_ref, m16_ref, d1_ref, d4_ref, d16_ref,
                     ga_ref, c_ref, x_ref, wb_ref, fg_ref, y_ref,
                     a_ref, n4_ref, n16_ref, tmp_ref, st_ref):
    tm, d_attn = o1_ref.shape
    n_lane_groups = d_attn // LANES
    sub = SUB_ROWS
    n4, n16 = sub // 4, sub // 16

    pair_idx = lax.broadcasted_iota(jnp.int32, (sub, LANES), 1) // HEAD_DIM

    for t in range(tm // sub):
        rs = slice(t * sub, (t + 1) * sub)
        for r in range(4):
            dst = pl.ds(t * sub + r, n4, stride=4)
            src = slice(t * n4, (t + 1) * n4)
            st_ref[0, dst, :] = m4_ref[r, src, :]
            st_ref[1, dst, :] = d4_ref[r, src, :]
            for g in range(n_lane_groups):
                n4_ref[g, dst, :] = o4_ref[r, src, g * LANES:(g + 1) * LANES].astype(_F32)
        for r in range(16):
            dst = pl.ds(t * sub + r, n16, stride=16)
            src = slice(t * n16, (t + 1) * n16)
            st_ref[2, dst, :] = m16_ref[r, src, :]
            st_ref[3, dst, :] = d16_ref[r, src, :]
        for g in range(n_lane_groups):
            sl = slice(g * LANES, (g + 1) * LANES)
            for r4 in range(4):
                for a in range(4):
                    blk = o16_ref[r4 + 4 * a, t * n16:(t + 1) * n16, sl].astype(_F32)
                    tmp_ref[g, pl.ds(t * sub + r4 * n4 + a, n16, stride=4), :] = blk
                n16_ref[g, pl.ds(t * sub + r4, n4, stride=4), :] = tmp_ref[g, t * sub + r4 * n4:
                                                                          t * sub + (r4 + 1) * n4, :]

        ms = [m1_ref[rs, :], st_ref[0, rs, :], st_ref[2, rs, :]]
        dens = [d1_ref[rs, :], st_ref[1, rs, :], st_ref[3, rs, :]]
        mx = jnp.maximum(jnp.maximum(ms[0], ms[1]), ms[2])
        es = [jnp.exp2(m - mx) for m in ms]
        tot = es[0] * dens[0] + es[1] * dens[1] + es[2] * dens[2]
        ws = [e / tot for e in es]
        for g in range(n_lane_groups):
            sl = slice(g * LANES, (g + 1) * LANES)
            w1, w4, w16 = (jnp.take_along_axis(w, pair_idx + 2 * g, axis=1) for w in ws)
            a = w1 * o1_ref[rs, sl].astype(_F32) + w4 * n4_ref[g, rs, :] + w16 * n16_ref[g, rs, :]
            a_ref[rs, sl] = (a * ga_ref[rs, sl].astype(_F32)).astype(a_ref.dtype)

    for t in range(tm // sub):
        rs = slice(t * sub, (t + 1) * sub)
        acc = jnp.dot(a_ref[rs, :], wb_ref[:d_attn, :], preferred_element_type=_F32)
        acc = acc + jnp.dot(c_ref[rs, :], wb_ref[d_attn:, :], preferred_element_type=_F32)
        xo = x_ref[rs, :] + acc
        msq = jnp.mean(xo * xo, axis=-1, keepdims=True)
        y_ref[rs, :] = xo * lax.rsqrt(msq + EPS) * fg_ref[...]


def _out_proj(os, ms, dens, ga, c, x, w_out, final_g, *, tm):
    m, d_model = x.shape
    d_attn = ga.shape[1]
    d_chunk = c.shape[1]
    rows = lambda n: pl.BlockSpec((tm, n), lambda i: (i, 0))
    res = lambda d, n: pl.BlockSpec((d, tm // d, n), lambda i: (0, i, 0))
    by_pattern = lambda n: [rows(n), res(4, n), res(16, n)]
    fg = final_g.reshape(1, -1)
    slab = pltpu.VMEM((d_attn // LANES, tm, LANES), _F32)
    return pl.pallas_call(
        _out_proj_kernel,
        grid=(m // tm,),
        in_specs=by_pattern(d_attn) + by_pattern(LANES) + by_pattern(LANES)
        + [rows(d_attn), rows(d_chunk), rows(d_model),
           pl.BlockSpec(w_out.shape, lambda i: (0, 0), pipeline_mode=pl.Buffered(1)),
           pl.BlockSpec(fg.shape, lambda i: (0, 0))],
        out_specs=rows(d_model),
        out_shape=jax.ShapeDtypeStruct((m, d_model), _F32),
        scratch_shapes=[pltpu.VMEM((tm, d_attn), _BF16), slab, slab, slab,
                        pltpu.VMEM((4, tm, LANES), _F32)],
        compiler_params=_params("arbitrary"),
        name="out_proj",
    )(*os, *ms, *dens, ga, c, x, w_out, fg)


def _sample_in_proj_kernel(x_ref, g_ref, w_ref, cos_ref, sin_ref, lng_ref, lnb_ref,
                           q_ref, k_ref, v_ref, ga_ref, u_ref, vc_ref, gb_ref, wqkv_ref, wgate_ref, xn_ref):
    j = pl.program_id(0)

    @pl.when(j == 0)
    def _():
        xf = x_ref[...]
        ms = jnp.mean(xf * xf, axis=-1, keepdims=True)
        xn_ref[...] = (xf * lax.rsqrt(ms + EPS) * g_ref[...]).astype(xn_ref.dtype)

    wb = w_ref[...].astype(_BF16)

    @pl.when(j <= SEG_V)
    def _():
        wqkv_ref[...] = wb

    @pl.when(j > SEG_V)
    def _():
        wgate_ref[...] = wb

    z = jnp.dot(xn_ref[...], wb, preferred_element_type=_F32)

    def rope_to(ref, scale):
        for g in range(z.shape[1] // LANES):
            sl = slice(g * LANES, (g + 1) * LANES)
            ref[:, sl] = _rope_group(z[:, sl], cos_ref[...], sin_ref[...]) * scale

    @pl.when(j == SEG_Q)
    def _():
        rope_to(q_ref, HEAD_DIM ** -0.5)

    @pl.when(j == SEG_K)
    def _():
        rope_to(k_ref, 1.0)

    @pl.when(j == SEG_V)
    def _():
        v_ref[...] = z

    @pl.when(j == SEG_GA)
    def _():
        ga_ref[...] = jax.nn.silu(z)

    @pl.when(j == SEG_U)
    def _():
        u_ref[...] = jax.nn.gelu(z)

    @pl.when(j == SEG_VC)
    def _():
        vc_ref[...] = _gelu_layer_norm(z, lng_ref[...], lnb_ref[...])

    @pl.when(j == SEG_GB)
    def _():
        gb_ref[...] = jax.nn.silu(z)


def _sample_in_proj(x, norm_g, w_in, cos_t, sin_t, ln_g, ln_b):
    m, d_model = x.shape
    d_seg = w_in.shape[1] // N_SEGMENTS
    whole = lambda r, c: pl.BlockSpec((r, c), lambda j: (0, 0))
    return pl.pallas_call(
        _sample_in_proj_kernel,
        grid=(N_SEGMENTS,),
        in_specs=[whole(m, d_model), whole(1, d_model),
                  pl.BlockSpec((d_model, d_seg), lambda j: (0, j)),
                  whole(m, LANES), whole(m, LANES), whole(1, d_seg), whole(1, d_seg)],
        out_specs=[whole(m, d_seg)] * N_SEGMENTS
        + [pl.BlockSpec((d_model, d_seg), lambda j: (0, jnp.minimum(j, SEG_V))),
           pl.BlockSpec((None, d_model, d_seg), lambda j: (jnp.maximum(j - SEG_GA, 0), 0, 0))],
        out_shape=[jax.ShapeDtypeStruct((m, d_seg), _F32)] * N_SEGMENTS
        + [jax.ShapeDtypeStruct((d_model, (SEG_V + 1) * d_seg), _BF16),
           jax.ShapeDtypeStruct((N_SEGMENTS - SEG_GA, d_model, d_seg), _BF16)],
        scratch_shapes=[pltpu.VMEM((m, d_model), _BF16)],
        compiler_params=_params("arbitrary"),
        name="sample_in_proj",
    )(x, norm_g.reshape(1, -1), w_in, cos_t, sin_t, ln_g.reshape(1, -1), ln_b.reshape(1, -1))


def _key_multiplicity(rows, wb, t_new):
    t_row = lax.broadcasted_iota(jnp.int32, (rows, wb), 0) // (rows // t_new)
    pos = lax.broadcasted_iota(jnp.int32, (rows, wb), 1)
    dist = wb + t_row - pos
    mult = jnp.zeros((rows, wb), _F32)
    for window, d in DILATIONS:
        mult = mult + jnp.where((dist % d == 0) & (dist <= window), 1.0, 0.0)
    return mult


def _sample_attn_tile(q_ref, kn_ref, vn_ref, kt_ref, vt_ref, o_ref, mult_ref):
    t_new, width = q_ref.shape
    n_heads = width // HEAD_DIM
    rows = t_new * n_heads
    head_row = lax.broadcasted_iota(jnp.int32, (n_heads, width), 0)
    head_col = lax.broadcasted_iota(jnp.int32, (n_heads, width), 1) // HEAD_DIM
    own = head_row == head_col
    qbd = jnp.concatenate([jnp.where(own, q_ref[t:t + 1, :], 0.0) for t in range(t_new)], axis=0)
    mult = mult_ref[...]

    s = jnp.dot(qbd.astype(_BF16), kt_ref[...].astype(_BF16), preferred_element_type=_F32)
    yield
    s = jnp.where(mult > 0.0, s, NEG_INF)
    m = jnp.max(s, axis=-1, keepdims=True)
    t_col = lax.broadcasted_iota(jnp.int32, (rows, 1), 0) // n_heads
    s_new, mult_new = [], []
    for tp in range(t_new):
        s_new.append(jnp.sum(qbd * kn_ref[tp:tp + 1, :], axis=-1, keepdims=True))
        dn = t_col - tp
        mn = jnp.zeros((rows, 1), _F32)
        for window, d in DILATIONS:
            mn = mn + jnp.where((dn >= 0) & (dn % d == 0), 1.0, 0.0)
        mult_new.append(mn)
        m = jnp.maximum(m, jnp.where(mn > 0.0, s_new[tp], NEG_INF))
    p = jnp.exp(s - m) * mult
    den = jnp.sum(p, axis=-1, keepdims=True)
    yield
    num = lax.dot_general(p.astype(_BF16), vt_ref[...].astype(_BF16), (((1,), (1,)), ((), ())),
                          preferred_element_type=_F32)
    for tp in range(t_new):
        pn = jnp.where(mult_new[tp] > 0.0, jnp.exp(s_new[tp] - m), 0.0) * mult_new[tp]
        den = den + pn
        num = num + pn * vn_ref[tp:tp + 1, :]
    res = num / den
    for t in range(t_new):
        blk = res[t * n_heads:(t + 1) * n_heads, :]
        o_ref[t:t + 1, :] = jnp.sum(jnp.where(own, blk, 0.0), axis=0, keepdims=True)


def _sample_out_kernel(oa_ref, ga_ref, u_ref, vc_ref, gb_ref, x_ref, wout_ref, coef_ref, bias_ref, fg_ref,
                       y_ref, wb_ref, c_ref, *, t_new):
    d_attn = oa_ref.shape[1]
    wb_ref[...] = wout_ref[...].astype(wb_ref.dtype)
    a = (oa_ref[...] * ga_ref[...]).astype(_BF16)
    vc = vc_ref[...]
    mixed = bias_ref[...] + coef_ref[0] * vc
    for delta in range(1, t_new):
        mixed = mixed + coef_ref[delta] * pltpu.roll(vc, delta, 0)
    c_ref[...] = (u_ref[...] * mixed * gb_ref[...]).astype(c_ref.dtype)
    acc = jnp.dot(a, wb_ref[:d_attn, :], preferred_element_type=_F32)
    acc = acc + jnp.dot(c_ref[...], wb_ref[d_attn:, :], preferred_element_type=_F32)
    xo = x_ref[...] + acc
    ms = jnp.mean(xo * xo, axis=-1, keepdims=True)
    y_ref[...] = xo * lax.rsqrt(ms + EPS) * fg_ref[...]


def _sample_out(oa, ga, u, vc, gb, x, w_out, coef, bias, final_g):
    m, d_model = x.shape
    args = (oa, ga, u, vc, gb, x, w_out, coef, bias, final_g.reshape(1, -1))
    whole = lambda a: pl.BlockSpec(a.shape, lambda i: (0,) * a.ndim)
    return pl.pallas_call(
        functools.partial(_sample_out_kernel, t_new=coef.shape[0]),
        grid=(1,),
        in_specs=[whole(a) for a in args],
        out_specs=[pl.BlockSpec((m, d_model), lambda i: (0, 0)), whole(w_out)],
        out_shape=[jax.ShapeDtypeStruct((m, d_model), _F32), jax.ShapeDtypeStruct(w_out.shape, _BF16)],
        scratch_shapes=[pltpu.VMEM((m, u.shape[1]), _BF16)],
        compiler_params=_params("arbitrary"),
        name="sample_out",
    )(*args)


def kernel(x_prompt, x_sample, cache_k, cache_v, norm_g, w_in, ln_g, ln_b, w_s, b_s, w_out, final_g):
    batch, s_len, d_model = x_prompt.shape
    db, t_new, _ = x_sample.shape
    depth, _, wb, n_heads, head_dim = cache_k.shape
    d_attn = n_heads * head_dim
    d_chunk = w_out.shape[1] - d_attn
    assert batch == 1 and depth == 1 and head_dim == HEAD_DIM
    assert w_in.shape[2] == 4 * d_attn + 3 * d_chunk and d_attn == d_chunk
    tail = min(MAX_WINDOW, s_len)
    w_in0, w_out0 = w_in[0], w_out[0]
    ng, lg, lb = norm_g[0], ln_g[0], ln_b[0]

    xs = x_sample.reshape(db * t_new, d_model)
    pos_s = PAST_LEN + jnp.tile(jnp.arange(t_new, dtype=jnp.int32), db)
    cos_s, sin_s = _rope_tables(pos_s)
    qs, ks, vs, gas, us, vcs, gbs, w_qkv_b, w_gate_b = _sample_in_proj(xs, ng, w_in0, cos_s, sin_s, lg, lb)
    qs_b, ks_b, vs_b, vcs_b = (a.reshape(db, t_new, -1) for a in (qs, ks, vs, vcs))
    cache_t = lambda c: jnp.transpose(c[0], (0, 2, 3, 1)).reshape(db, d_attn, wb)
    cache_kt, cache_vt = cache_t(cache_k), cache_t(cache_v)
    assert wb == MAX_WINDOW

    xp = x_prompt.reshape(s_len, d_model)
    (xn, q1, q4, q16, k1, k4, k16, v1, v4, v16, k_tail_t, v_tail_t) = _qkv_proj(
        xp, ng, w_qkv_b, tm=256, q_scale=HEAD_DIM ** -0.5 * math.log2(math.e), tail_rows=tail)
    ga, c, oa = _gates_proj(xn, w_gate_b, (lg, lb), w_s[0], b_s[0].T,
                            (qs_b, ks_b, vs_b, cache_kt, cache_vt), tm=512)
    o1, m1, d1 = _dilated_attention(q1[None], k1[None], v1[None], q_rows=1024)
    o4, m4, d4 = _dilated_attention(q4, k4, v4, q_rows=1024)
    o16, m16, d16 = _dilated_attention(q16, k16, v16, q_rows=1024)

    w_ts = jnp.transpose(w_s[0][:, :t_new, :t_new], (1, 2, 0))
    coef = jnp.stack([jnp.stack([w_ts[t, t - delta] if t >= delta else jnp.zeros_like(w_ts[0, 0])
                                 for t in range(t_new)]) for delta in range(t_new)])
    coef = jnp.tile(jnp.repeat(coef, GROUP_WIDTH_B, axis=2), (1, db, 1))
    bias = jnp.tile(jnp.repeat(jnp.transpose(b_s[0][:, :t_new], (1, 0)), GROUP_WIDTH_B, axis=1), (db, 1))
    ys, w_out_b = _sample_out(oa.reshape(db * t_new, d_attn), gas, us, vcs, gbs, xs, w_out0, coef, bias, final_g)
    y_sample = ys.reshape(db, t_new, d_model)

    y_prompt = _out_proj((o1[0], o4, o16), (m1[0], m4, m16), (d1[0], d4, d16), ga, c, xp, w_out_b, final_g, tm=512)
    untranspose = lambda a: jnp.transpose(a.reshape(n_heads, head_dim, tail), (2, 0, 1))

    hs = (n_heads, head_dim)
    return (
        y_prompt.reshape(batch, s_len, d_model),
        y_sample,
        untranspose(k_tail_t).reshape(depth, batch, tail, *hs),
        untranspose(v_tail_t).reshape(depth, batch, tail, *hs),
        ks_b.reshape(depth, db, t_new, *hs),
        vs_b.reshape(depth, db, t_new, *hs),
        vcs_b.reshape(depth, db, t_new, d_chunk),
    )
```

```python
import functools
import math

import jax
import jax.numpy as jnp
from jax import lax
from jax.experimental import pallas as pl
from jax.experimental.pallas import tpu as pltpu

HEAD_DIM = 64
BLOCK = 128
CHUNK = 128
GROUP_WIDTH_B = 128
DILATIONS = ((128, 1), (512, 4), (2048, 16))
MAX_WINDOW = 2048
PAST_LEN = 16384
ROPE_THETA = 10000.0
EPS = 1e-6
NEG_INF = -1e30
N_SEGMENTS = 7
SEG_Q, SEG_K, SEG_V, SEG_GA, SEG_U, SEG_VC, SEG_GB = range(N_SEGMENTS)

LANES = 128
VMEM_LIMIT_BYTES = 56 * 1024 * 1024
SUB_ROWS = 256
SAMPLE_HEAD_GROUP_WIDTH = 512
SAMPLE_OUT_K_CHUNKS = 4

_BF16 = jnp.bfloat16
_F32 = jnp.float32


def _params(*semantics):
    return pltpu.CompilerParams(dimension_semantics=semantics, vmem_limit_bytes=VMEM_LIMIT_BYTES)


def _rope_tables(pos):
    half = HEAD_DIM // 2
    inv = jnp.exp(-math.log(ROPE_THETA) * jnp.arange(half, dtype=_F32) / half)
    ang = pos.astype(_F32)[:, None] * inv[None, :]
    cos = jnp.cos(ang)
    sin = jnp.sin(ang)
    cos_t = jnp.concatenate([cos, cos, cos, cos], axis=-1)
    sin_t = jnp.concatenate([-sin, sin, -sin, sin], axis=-1)
    return cos_t, sin_t


def _rope_tables_blocked(s_len, tm):
    half = HEAD_DIM // 2
    inv = jnp.exp(-math.log(ROPE_THETA) * jnp.arange(half, dtype=_F32) / half)
    tile4 = lambda a: jnp.concatenate([a, a, a, a], axis=-1)
    ang_r = jnp.arange(tm, dtype=jnp.int32).astype(_F32)[:, None] * inv[None, :]
    ang_b = (jnp.arange(s_len // tm, dtype=jnp.int32) * tm).astype(_F32)[:, None] * inv[None, :]
    sign = jnp.concatenate([-jnp.ones((1, half), _F32), jnp.ones((1, half), _F32)] * 2, axis=-1)
    return (tile4(jnp.cos(ang_r)), tile4(jnp.sin(ang_r)), tile4(jnp.cos(ang_b)), tile4(jnp.sin(ang_b)), sign)


def _rope_group(zg, cos, sin_signed):
    lane = lax.broadcasted_iota(jnp.int32, zg.shape, 1)
    first_half = (lane % HEAD_DIM) < (HEAD_DIM // 2)
    partner = jnp.where(first_half, pltpu.roll(zg, LANES - HEAD_DIM // 2, 1),
                        pltpu.roll(zg, HEAD_DIM // 2, 1))
    return zg * cos + partner * sin_signed


def _gelu_layer_norm(z, g, b):
    h = jax.nn.gelu(z)
    mu = jnp.mean(h, axis=-1, keepdims=True)
    hc = h - mu
    var = jnp.mean(hc * hc, axis=-1, keepdims=True)
    return hc * lax.rsqrt(var + EPS) * g + b


def _qkv_kernel(x_ref, g_ref, w_ref, cos_r_ref, sin_r_ref, cos_b_ref, sin_b_ref, sign_ref,
                xn_ref, q1_ref, q4_ref, q16_ref, k1_ref, k4_ref, k16_ref, v1_ref, v4_ref, v16_ref,
                ktail_ref, vtail_ref, nat_ref, res4_ref, *, q_scale):
    tm = x_ref.shape[0]
    d_seg = q1_ref.shape[1]
    sub = min(SUB_ROWS, tm)
    n4, n16 = sub // 4, sub // 16
    step = pl.ds(pl.program_id(0), 1)
    cb, sb = cos_b_ref[step, :], sin_b_ref[step, :]
    segments = (
        (SEG_Q, (q1_ref, q4_ref, q16_ref), None, q_scale),
        (SEG_K, (k1_ref, k4_ref, k16_ref), ktail_ref, 1.0),
        (SEG_V, (v1_ref, v4_ref, v16_ref), vtail_ref, 1.0),
    )
    for t in range(tm // sub):
        rs = slice(t * sub, (t + 1) * sub)
        xf = x_ref[rs, :]
        ms = jnp.mean(xf * xf, axis=-1, keepdims=True)
        xb = (xf * lax.rsqrt(ms + EPS) * g_ref[...]).astype(_BF16)
        xn_ref[rs, :] = xb
        cr, sr = cos_r_ref[rs, :], sin_r_ref[rs, :]
        cos = cr * cb - sr * sb
        sin_signed = (sr * cb + cr * sb) * sign_ref[...]
        for seg, (d1_ref, d4_ref, d16_ref), tail_ref, scale in segments:
            z = jnp.dot(xb, w_ref[:, seg * d_seg:(seg + 1) * d_seg], preferred_element_type=_F32)
            for g in range(d_seg // LANES):
                sl = slice(g * LANES, (g + 1) * LANES)
                r = z[:, sl]
                if seg != SEG_V:
                    r = _rope_group(r, cos, sin_signed)
                if tail_ref is not None:
                    tail_ref[sl, rs] = r.T
                if scale != 1.0:
                    r = r * scale
                d1_ref[rs, sl] = r.astype(d1_ref.dtype)
                nat_ref[g] = r
                for r4 in range(4):
                    blk = nat_ref[g, pl.ds(r4, n4, stride=4), :]
                    d4_ref[r4, t * n4:(t + 1) * n4, sl] = blk.astype(d4_ref.dtype)
                    res4_ref[g, r4 * n4:(r4 + 1) * n4, :] = blk
                for r4 in range(4):
                    for a in range(4):
                        blk = res4_ref[g, pl.ds(r4 * n4 + a, n16, stride=4), :]
                        d16_ref[r4 + 4 * a, t * n16:(t + 1) * n16, sl] = blk.astype(d16_ref.dtype)


def _gates_kernel(xn_ref, w_ref, lng_ref, lnb_ref, ws_ref, bst_ref, q_ref, kn_ref, vn_ref, kt_ref, vt_ref,
                  ga_ref, c_ref, oa_ref, u_ref, vc_ref, mult_ref):
    j = pl.program_id(1)

    @pl.when(jnp.logical_and(pl.program_id(0) == 0, j == 0))
    def _():
        mult_ref[...] = _key_multiplicity(mult_ref.shape[0], mult_ref.shape[1], q_ref.shape[0])

    tm = xn_ref.shape[0]
    sub = 2 * CHUNK
    tril = (lax.broadcasted_iota(jnp.int32, (CHUNK, CHUNK), 0)
            >= lax.broadcasted_iota(jnp.int32, (CHUNK, CHUNK), 1))

    def gated_gmlp(z, rs):
        gb = jax.nn.silu(z)
        c0 = slice(rs.start, rs.start + CHUNK)
        c1 = slice(rs.start + CHUNK, rs.stop)
        for g in range(ws_ref.shape[0]):
            wm = jnp.where(tril, ws_ref[g], 0.0).astype(_BF16)
            gs = slice(g * GROUP_WIDTH_B, (g + 1) * GROUP_WIDTH_B)
            vc2 = jnp.concatenate([vc_ref[c0, gs], vc_ref[c1, gs]], axis=1)
            mixed = jnp.dot(wm, vc2, preferred_element_type=_F32) + bst_ref[:, g:g + 1]
            for c, half in ((c0, slice(0, GROUP_WIDTH_B)), (c1, slice(GROUP_WIDTH_B, 2 * GROUP_WIDTH_B))):
                local = slice(c.start - rs.start, c.stop - rs.start)
                cv = u_ref[c, gs].astype(_F32) * mixed[:, half] * gb[local, gs]
                c_ref[c, gs] = cv.astype(c_ref.dtype)

    def store(ref, fn):
        def write(z, rs):
            ref[rs, :] = fn(z).astype(ref.dtype)
        return write

    epilogues = (
        store(ga_ref, jax.nn.silu),
        store(u_ref, jax.nn.gelu),
        store(vc_ref, lambda z: _gelu_layer_norm(z, lng_ref[...], lnb_ref[...])),
        gated_gmlp,
    )
    for s, epilogue in enumerate(epilogues):
        @pl.when(j == s)
        def _(s=s, epilogue=epilogue):
            side = _sample_attn_tile(q_ref, kn_ref, vn_ref, kt_ref, vt_ref, oa_ref, mult_ref)
            for t in range(tm // sub):
                rs = slice(t * sub, (t + 1) * sub)
                z = jnp.dot(xn_ref[rs, :], w_ref[s], preferred_element_type=_F32)
                next(side, None)
                epilogue(z, rs)
                if t == 0:
                    next(side, None)
            for _ in side:
                pass


def _gates_proj(xn, w_gate, ln, w_s, b_s_t, side, *, tm):
    s_len, d_model = xn.shape
    n_seg, _, d_seg = w_gate.shape
    sq, skn, svn, skt, svt = side
    db, t_new, d_attn = sq.shape
    width, wb = SAMPLE_HEAD_GROUP_WIDTH, skt.shape[2]
    hg = d_attn // width
    n_i = s_len // tm
    assert n_i * n_seg == db * hg
    tile = lambda i, j: i * n_seg + j
    new = pl.BlockSpec((None, t_new, width), lambda i, j: (tile(i, j) // hg, 0, tile(i, j) % hg))
    cache = pl.BlockSpec((None, width, wb), lambda i, j: (tile(i, j) // hg, tile(i, j) % hg, 0))
    row_vec = pl.BlockSpec((1, d_seg), lambda i, j: (0, 0))
    rows = pl.BlockSpec((tm, d_seg), lambda i, j: (i, 0))
    whole = lambda a: pl.BlockSpec(a.shape, lambda i, j: (0,) * a.ndim)
    return pl.pallas_call(
        _gates_kernel,
        grid=(n_i, n_seg),
        in_specs=[pl.BlockSpec((tm, d_model), lambda i, j: (i, 0)),
                  pl.BlockSpec(w_gate.shape, lambda i, j: (0, 0, 0), pipeline_mode=pl.Buffered(1)),
                  row_vec, row_vec, whole(w_s), whole(b_s_t), new, new, new, cache, cache],
        out_specs=[rows, rows, new],
        out_shape=[jax.ShapeDtypeStruct((s_len, d_seg), _BF16)] * 2
        + [jax.ShapeDtypeStruct((db, t_new, d_attn), _F32)],
        scratch_shapes=[pltpu.VMEM((tm, d_seg), _BF16), pltpu.VMEM((tm, d_seg), _BF16),
                        pltpu.VMEM((t_new * width // HEAD_DIM, wb), _F32)],
        compiler_params=_params("arbitrary", "arbitrary"),
        name="gates_proj",
    )(xn, w_gate, ln[0].reshape(1, -1), ln[1].reshape(1, -1), w_s, b_s_t, sq, skn, svn, skt, svt)


def _qkv_proj(x, norm_g, w_qkv, *, tm, q_scale, tail_rows):
    s_len, d_model = x.shape
    d_seg = w_qkv.shape[1] // 3
    n_i = s_len // tm
    tail_start = n_i - tail_rows // tm
    tables = _rope_tables_blocked(s_len, tm)
    const = lambda a: pl.BlockSpec(a.shape, lambda i: (0, 0))
    layouts_shape = [jax.ShapeDtypeStruct((s_len, d_seg), _BF16),
                     jax.ShapeDtypeStruct((4, s_len // 4, d_seg), _BF16),
                     jax.ShapeDtypeStruct((16, s_len // 16, d_seg), _BF16)]
    layouts_spec = [pl.BlockSpec((tm, d_seg), lambda i: (i, 0)),
                    pl.BlockSpec((4, tm // 4, d_seg), lambda i: (0, i, 0)),
                    pl.BlockSpec((16, tm // 16, d_seg), lambda i: (0, i, 0))]
    tail_spec = pl.BlockSpec((d_seg, tm), lambda i: (0, jnp.maximum(i - tail_start, 0)))
    tail_shape = jax.ShapeDtypeStruct((d_seg, tail_rows), _F32)
    return pl.pallas_call(
        functools.partial(_qkv_kernel, q_scale=q_scale),
        grid=(n_i,),
        in_specs=[pl.BlockSpec((tm, d_model), lambda i: (i, 0)), pl.BlockSpec((1, d_model), lambda i: (0, 0)),
                  pl.BlockSpec(w_qkv.shape, lambda i: (0, 0), pipeline_mode=pl.Buffered(1))]
        + [const(t) for t in tables],
        out_specs=[pl.BlockSpec((tm, d_model), lambda i: (i, 0))] + layouts_spec * 3 + [tail_spec] * 2,
        out_shape=[jax.ShapeDtypeStruct((s_len, d_model), _BF16)] + layouts_shape * 3 + [tail_shape] * 2,
        scratch_shapes=[pltpu.VMEM((d_seg // LANES, min(SUB_ROWS, tm), LANES), _F32)] * 2,
        compiler_params=_params("arbitrary"),
        name="qkv_proj",
    )(x, norm_g.reshape(1, -1), w_qkv, *tables)


def _attn_kernel(q_ref, kp_ref, kc_ref, vp_ref, vc_ref, o_ref, stat_ref, k0_ref, *vext_refs, res_blocks):
    n = pl.program_id(1)
    first_step = jnp.logical_and(pl.program_id(0) == 0, n == 0)
    n_pairs = q_ref.shape[1] // LANES
    two = 2 * BLOCK

    @pl.when(first_step)
    def _():
        ones = jnp.ones((two, LANES), k0_ref.dtype)
        for vext_ref in vext_refs:
            for hp in range(n_pairs):
                vext_ref[:, (2 * hp + 1) * LANES:(2 * hp + 2) * LANES] = ones

    k0_ref[0:BLOCK, :] = kp_ref[...]
    k0_ref[BLOCK:, :] = kc_ref[0:BLOCK, :]

    qi = lax.broadcasted_iota(jnp.int32, (two, two), 0) % BLOCK
    si = lax.broadcasted_iota(jnp.int32, (two, two), 1)
    dist = qi + BLOCK - si
    band_bias = jnp.where((dist >= 0) & (dist <= BLOCK), 0.0, NEG_INF).astype(_F32)
    lane = lax.broadcasted_iota(jnp.int32, (BLOCK, LANES), 1)
    head0 = lane < HEAD_DIM
    zero = jnp.zeros((BLOCK, LANES), q_ref.dtype)

    for j, vext_ref in enumerate(vext_refs):
        rows = slice(j * BLOCK, (j + 1) * BLOCK)
        prev_keys = slice((j - 1) * BLOCK, (j + 1) * BLOCK)
        bias = band_bias
        if len(vext_refs) % res_blocks == 0:
            if j % res_blocks == 0:
                bias = band_bias + jnp.where(si < BLOCK, NEG_INF, 0.0)
        elif j == 0:
            bias = band_bias + jnp.where(jnp.logical_and(si < BLOCK, n == 0), NEG_INF, 0.0)
        for hp in range(n_pairs):
            sl = slice(hp * LANES, (hp + 1) * LANES)
            ext = slice(2 * hp * LANES, (2 * hp + 1) * LANES)
            if j == 0:
                vext_ref[0:BLOCK, ext] = vp_ref[:, sl]
                vext_ref[BLOCK:, ext] = vc_ref[0:BLOCK, sl]
            else:
                vext_ref[:, ext] = vc_ref[prev_keys, sl]
        stat_acc = jnp.where(lane < 2 * n_pairs, 0.0, 1.0)
        for hp in range(n_pairs):
            sl = slice(hp * LANES, (hp + 1) * LANES)
            q2 = q_ref[rows, sl]
            qs = jnp.concatenate([jnp.where(head0, q2, zero), jnp.where(head0, zero, q2)], axis=0)
            k2 = k0_ref[:, sl] if j == 0 else kc_ref[prev_keys, sl]
            s = lax.dot_general(qs, k2, (((1,), (1,)), ((), ())), preferred_element_type=_F32) + bias
            m = jnp.max(s, axis=-1, keepdims=True)
            p = jnp.exp2(s - m).astype(vext_ref.dtype)
            pv = jnp.dot(p, vext_ref[:, 2 * hp * LANES:(2 * hp + 2) * LANES], preferred_element_type=_F32)
            num, den = pv[:, :LANES], pv[:, LANES:]
            o_ref[rows, sl] = jnp.where(head0, num[:BLOCK], num[BLOCK:]).astype(o_ref.dtype)
            for h, (mh, dh) in ((2 * hp, (m[:BLOCK], den[:BLOCK])), (2 * hp + 1, (m[BLOCK:], den[BLOCK:]))):
                stat_acc = jnp.where(lane == h, mh, jnp.where(lane == 2 * n_pairs + h, dh, stat_acc))
        stat_ref[rows, :] = stat_acc


def _dilated_attention(q, k, v, *, q_rows):
    d_orig, res_len, d_attn = q.shape
    res_blocks = res_len // BLOCK
    if q_rows > res_len:
        q, k, v = (a.reshape(d_orig * res_len // q_rows, q_rows, d_attn) for a in (q, k, v))
    d, sub_len, _ = q.shape
    per = q_rows // BLOCK
    assert per % res_blocks == 0 or res_blocks % per == 0
    cur = pl.BlockSpec((None, q_rows, d_attn), lambda r, n: (r, n, 0))
    prev = pl.BlockSpec((None, BLOCK, d_attn), lambda r, n: (r, jnp.maximum(n * per - 1, 0), 0))
    stat = pl.BlockSpec((None, q_rows, LANES), lambda r, n: (r, n, 0))
    stat_shape = jax.ShapeDtypeStruct((d, sub_len, LANES), _F32)
    o, st = pl.pallas_call(
        functools.partial(_attn_kernel, res_blocks=res_blocks),
        grid=(d, sub_len // q_rows),
        in_specs=[cur, prev, cur, prev, cur],
        out_specs=[cur, stat],
        out_shape=[jax.ShapeDtypeStruct((d, sub_len, d_attn), q.dtype), stat_shape],
        scratch_shapes=[pltpu.VMEM((2 * BLOCK, d_attn), q.dtype)]
        + [pltpu.VMEM((2 * BLOCK, 2 * d_attn), q.dtype)] * per,
        compiler_params=_params("arbitrary", "arbitrary"),
        name=f"dilated_attn_d{d_orig}",
    )(q, k, k, v, v)
    return o.reshape(d_orig, res_len, d_attn), st.reshape(d_orig, res_len, LANES)


def _out_proj_kernel(o1_ref, o4_ref, o16_ref, s1_ref, s4_ref, s16_ref,
                     ga_ref, c_ref, x_ref, wb_ref, fg_ref, y_ref,
                     a_ref, n4_ref, n16_ref, tmp_ref, st_ref):
    tm, d_attn = o1_ref.shape
    n_heads = d_attn // HEAD_DIM
    n_lane_groups = d_attn // LANES
    sub = SUB_ROWS
    n4, n16 = sub // 4, sub // 16

    pair_idx = lax.broadcasted_iota(jnp.int32, (sub, LANES), 1) // HEAD_DIM
    head_lane = lax.broadcasted_iota(jnp.int32, (sub, LANES), 1) < n_heads

    for t in range(tm // sub):
        rs = slice(t * sub, (t + 1) * sub)
        for r in range(4):
            dst = pl.ds(t * sub + r, n4, stride=4)
            src = slice(t * n4, (t + 1) * n4)
            st_ref[0, dst, :] = s4_ref[r, src, :]
            for g in range(n_lane_groups):
                n4_ref[g, dst, :] = o4_ref[r, src, g * LANES:(g + 1) * LANES].astype(_F32)
        for r in range(16):
            dst = pl.ds(t * sub + r, n16, stride=16)
            src = slice(t * n16, (t + 1) * n16)
            st_ref[1, dst, :] = s16_ref[r, src, :]
        for g in range(n_lane_groups):
            sl = slice(g * LANES, (g + 1) * LANES)
            for r4 in range(4):
                for a in range(4):
                    blk = o16_ref[r4 + 4 * a, t * n16:(t + 1) * n16, sl].astype(_F32)
                    tmp_ref[g, pl.ds(t * sub + r4 * n4 + a, n16, stride=4), :] = blk
                n16_ref[g, pl.ds(t * sub + r4, n4, stride=4), :] = tmp_ref[g, t * sub + r4 * n4:
                                                                          t * sub + (r4 + 1) * n4, :]

        ms = [s1_ref[rs, :], st_ref[0, rs, :], st_ref[1, rs, :]]
        dens = [pltpu.roll(s, LANES - n_heads, 1) for s in ms]
        mx = jnp.maximum(jnp.maximum(ms[0], ms[1]), ms[2])
        es = [jnp.exp2(m - mx) for m in ms]
        tot = es[0] * dens[0] + es[1] * dens[1] + es[2] * dens[2]
        tot = jnp.where(head_lane, tot, 1.0)
        ws = [e / tot for e in es]
        for g in range(n_lane_groups):
            sl = slice(g * LANES, (g + 1) * LANES)
            w1, w4, w16 = (jnp.take_along_axis(w, pair_idx + 2 * g, axis=1) for w in ws)
            a = w1 * o1_ref[rs, sl].astype(_F32) + w4 * n4_ref[g, rs, :] + w16 * n16_ref[g, rs, :]
            a_ref[rs, sl] = (a * ga_ref[rs, sl].astype(_F32)).astype(a_ref.dtype)

    for t in range(tm // sub):
        rs = slice(t * sub, (t + 1) * sub)
        acc = jnp.dot(a_ref[rs, :], wb_ref[:d_attn, :], preferred_element_type=_F32)
        acc = acc + jnp.dot(c_ref[rs, :], wb_ref[d_attn:, :], preferred_element_type=_F32)
        xo = x_ref[rs, :] + acc
        msq = jnp.mean(xo * xo, axis=-1, keepdims=True)
        y_ref[rs, :] = xo * lax.rsqrt(msq + EPS) * fg_ref[...]


def _out_proj(os, stats, ga, c, x, w_out, final_g, *, tm):
    m, d_model = x.shape
    d_attn = ga.shape[1]
    d_chunk = c.shape[1]
    rows = lambda n: pl.BlockSpec((tm, n), lambda i: (i, 0))
    res = lambda d, n: pl.BlockSpec((d, tm // d, n), lambda i: (0, i, 0))
    by_pattern = lambda n: [rows(n), res(4, n), res(16, n)]
    fg = final_g.reshape(1, -1)
    slab = pltpu.VMEM((d_attn // LANES, tm, LANES), _F32)
    return pl.pallas_call(
        _out_proj_kernel,
        grid=(m // tm,),
        in_specs=by_pattern(d_attn) + by_pattern(LANES)
        + [rows(d_attn), rows(d_chunk), rows(d_model),
           pl.BlockSpec(w_out.shape, lambda i: (0, 0), pipeline_mode=pl.Buffered(1)),
           pl.BlockSpec(fg.shape, lambda i: (0, 0))],
        out_specs=rows(d_model),
        out_shape=jax.ShapeDtypeStruct((m, d_model), _F32),
        scratch_shapes=[pltpu.VMEM((tm, d_attn), _BF16), slab, slab, slab,
                        pltpu.VMEM((2, tm, LANES), _F32)],
        compiler_params=_params("arbitrary"),
        name="out_proj",
    )(*os, *stats, ga, c, x, w_out, fg)


def _sample_in_proj_kernel(x_ref, g_ref, w_ref, cos_ref, sin_ref, lng_ref, lnb_ref,
                           q_ref, k_ref, v_ref, ga_ref, u_ref, vc_ref, gb_ref, wqkv_ref, wgate_ref, xn_ref):
    j = pl.program_id(0)

    @pl.when(j == 0)
    def _():
        xf = x_ref[...]
        ms = jnp.mean(xf * xf, axis=-1, keepdims=True)
        xn_ref[...] = (xf * lax.rsqrt(ms + EPS) * g_ref[...]).astype(xn_ref.dtype)

    wb = w_ref[...].astype(_BF16)

    @pl.when(j <= SEG_V)
    def _():
        wqkv_ref[...] = wb

    @pl.when(j > SEG_V)
    def _():
        wgate_ref[...] = wb

    z = jnp.dot(xn_ref[...], wb, preferred_element_type=_F32)

    def rope_to(ref, scale):
        for g in range(z.shape[1] // LANES):
            sl = slice(g * LANES, (g + 1) * LANES)
            ref[:, sl] = _rope_group(z[:, sl], cos_ref[...], sin_ref[...]) * scale

    @pl.when(j == SEG_Q)
    def _():
        rope_to(q_ref, HEAD_DIM ** -0.5)

    @pl.when(j == SEG_K)
    def _():
        rope_to(k_ref, 1.0)

    @pl.when(j == SEG_V)
    def _():
        v_ref[...] = z

    @pl.when(j == SEG_GA)
    def _():
        ga_ref[...] = jax.nn.silu(z)

    @pl.when(j == SEG_U)
    def _():
        u_ref[...] = jax.nn.gelu(z)

    @pl.when(j == SEG_VC)
    def _():
        vc_ref[...] = _gelu_layer_norm(z, lng_ref[...], lnb_ref[...])

    @pl.when(j == SEG_GB)
    def _():
        gb_ref[...] = jax.nn.silu(z)


def _sample_in_proj(x, norm_g, w_in, cos_t, sin_t, ln_g, ln_b):
    m, d_model = x.shape
    d_seg = w_in.shape[1] // N_SEGMENTS
    whole = lambda r, c: pl.BlockSpec((r, c), lambda j: (0, 0))
    return pl.pallas_call(
        _sample_in_proj_kernel,
        grid=(N_SEGMENTS,),
        in_specs=[whole(m, d_model), whole(1, d_model),
                  pl.BlockSpec((d_model, d_seg), lambda j: (0, j)),
                  whole(m, LANES), whole(m, LANES), whole(1, d_seg), whole(1, d_seg)],
        out_specs=[whole(m, d_seg)] * N_SEGMENTS
        + [pl.BlockSpec((d_model, d_seg), lambda j: (0, jnp.minimum(j, SEG_V))),
           pl.BlockSpec((None, d_model, d_seg), lambda j: (jnp.maximum(j - SEG_GA, 0), 0, 0))],
        out_shape=[jax.ShapeDtypeStruct((m, d_seg), _F32)] * N_SEGMENTS
        + [jax.ShapeDtypeStruct((d_model, (SEG_V + 1) * d_seg), _BF16),
           jax.ShapeDtypeStruct((N_SEGMENTS - SEG_GA, d_model, d_seg), _BF16)],
        scratch_shapes=[pltpu.VMEM((m, d_model), _BF16)],
        compiler_params=_params("arbitrary"),
        name="sample_in_proj",
    )(x, norm_g.reshape(1, -1), w_in, cos_t, sin_t, ln_g.reshape(1, -1), ln_b.reshape(1, -1))


def _key_multiplicity(rows, wb, t_new):
    t_row = lax.broadcasted_iota(jnp.int32, (rows, wb), 0) // (rows // t_new)
    pos = lax.broadcasted_iota(jnp.int32, (rows, wb), 1)
    dist = wb + t_row - pos
    mult = jnp.zeros((rows, wb), _F32)
    for window, d in DILATIONS:
        mult = mult + jnp.where((dist % d == 0) & (dist <= window), 1.0, 0.0)
    return mult


def _sample_attn_tile(q_ref, kn_ref, vn_ref, kt_ref, vt_ref, o_ref, mult_ref):
    t_new, width = q_ref.shape
    n_heads = width // HEAD_DIM
    rows = t_new * n_heads
    head_row = lax.broadcasted_iota(jnp.int32, (n_heads, width), 0)
    head_col = lax.broadcasted_iota(jnp.int32, (n_heads, width), 1) // HEAD_DIM
    own = head_row == head_col
    qbd = jnp.concatenate([jnp.where(own, q_ref[t:t + 1, :], 0.0) for t in range(t_new)], axis=0)
    mult = mult_ref[...]

    s = jnp.dot(qbd.astype(_BF16), kt_ref[...].astype(_BF16), preferred_element_type=_F32)
    yield
    s = jnp.where(mult > 0.0, s, NEG_INF)
    m = jnp.max(s, axis=-1, keepdims=True)
    t_col = lax.broadcasted_iota(jnp.int32, (rows, 1), 0) // n_heads
    s_new, mult_new = [], []
    for tp in range(t_new):
        s_new.append(jnp.sum(qbd * kn_ref[tp:tp + 1, :], axis=-1, keepdims=True))
        dn = t_col - tp
        mn = jnp.zeros((rows, 1), _F32)
        for window, d in DILATIONS:
            mn = mn + jnp.where((dn >= 0) & (dn % d == 0), 1.0, 0.0)
        mult_new.append(mn)
        m = jnp.maximum(m, jnp.where(mn > 0.0, s_new[tp], NEG_INF))
    p = jnp.exp(s - m) * mult
    den = jnp.sum(p, axis=-1, keepdims=True)
    yield
    num = lax.dot_general(p.astype(_BF16), vt_ref[...].astype(_BF16), (((1,), (1,)), ((), ())),
                          preferred_element_type=_F32)
    for tp in range(t_new):
        pn = jnp.where(mult_new[tp] > 0.0, jnp.exp(s_new[tp] - m), 0.0) * mult_new[tp]
        den = den + pn
        num = num + pn * vn_ref[tp:tp + 1, :]
    res = num / den
    for t in range(t_new):
        blk = res[t * n_heads:(t + 1) * n_heads, :]
        o_ref[t:t + 1, :] = jnp.sum(jnp.where(own, blk, 0.0), axis=0, keepdims=True)


def _sample_out_kernel(oa_ref, ga_ref, u_ref, vc_ref, gb_ref, x_ref, wout_ref, coef_ref, bias_ref, fg_ref,
                       y_ref, wb_ref, ac_ref, acc_ref, *, t_new):
    k = pl.program_id(0)
    d_attn = oa_ref.shape[1]
    kc = wout_ref.shape[0]

    @pl.when(k == 0)
    def _():
        ac_ref[:, :d_attn] = (oa_ref[...] * ga_ref[...]).astype(ac_ref.dtype)
        vc = vc_ref[...]
        mixed = bias_ref[...] + coef_ref[0] * vc
        for delta in range(1, t_new):
            mixed = mixed + coef_ref[delta] * pltpu.roll(vc, delta, 0)
        ac_ref[:, d_attn:] = (u_ref[...] * mixed * gb_ref[...]).astype(ac_ref.dtype)
        acc_ref[...] = x_ref[...]

    wb = wout_ref[...].astype(wb_ref.dtype)
    wb_ref[...] = wb
    col0 = pl.multiple_of(k * kc, kc)
    acc_ref[...] += jnp.dot(ac_ref[:, pl.ds(col0, kc)], wb, preferred_element_type=_F32)

    @pl.when(k == pl.num_programs(0) - 1)
    def _():
        xo = acc_ref[...]
        ms = jnp.mean(xo * xo, axis=-1, keepdims=True)
        y_ref[...] = xo * lax.rsqrt(ms + EPS) * fg_ref[...]


def _sample_out(oa, ga, u, vc, gb, x, w_out, coef, bias, final_g):
    m, d_model = x.shape
    args = (oa, ga, u, vc, gb, x, w_out, coef, bias, final_g.reshape(1, -1))
    whole = lambda a: pl.BlockSpec(a.shape, lambda k: (0,) * a.ndim)
    k_chunk = w_out.shape[0] // SAMPLE_OUT_K_CHUNKS
    w_spec = pl.BlockSpec((k_chunk, d_model), lambda k: (k, 0))
    return pl.pallas_call(
        functools.partial(_sample_out_kernel, t_new=coef.shape[0]),
        grid=(SAMPLE_OUT_K_CHUNKS,),
        in_specs=[w_spec if a is w_out else whole(a) for a in args],
        out_specs=[pl.BlockSpec((m, d_model), lambda k: (0, 0)), w_spec],
        out_shape=[jax.ShapeDtypeStruct((m, d_model), _F32), jax.ShapeDtypeStruct(w_out.shape, _BF16)],
        scratch_shapes=[pltpu.VMEM((m, w_out.shape[0]), _BF16), pltpu.VMEM((m, d_model), _F32)],
        compiler_params=_params("arbitrary"),
        name="sample_out",
    )(*args)


def kernel(x_prompt, x_sample, cache_k, cache_v, norm_g, w_in, ln_g, ln_b, w_s, b_s, w_out, final_g):
    batch, s_len, d_model = x_prompt.shape
    db, t_new, _ = x_sample.shape
    depth, _, wb, n_heads, head_dim = cache_k.shape
    d_attn = n_heads * head_dim
    d_chunk = w_out.shape[1] - d_attn
    assert batch == 1 and depth == 1 and head_dim == HEAD_DIM
    assert w_in.shape[2] == 4 * d_attn + 3 * d_chunk and d_attn == d_chunk
    tail = min(MAX_WINDOW, s_len)
    w_in0, w_out0 = w_in[0], w_out[0]
    ng, lg, lb = norm_g[0], ln_g[0], ln_b[0]

    xs = x_sample.reshape(db * t_new, d_model)
    pos_s = PAST_LEN + jnp.tile(jnp.arange(t_new, dtype=jnp.int32), db)
    cos_s, sin_s = _rope_tables(pos_s)
    qs, ks, vs, gas, us, vcs, gbs, w_qkv_b, w_gate_b = _sample_in_proj(xs, ng, w_in0, cos_s, sin_s, lg, lb)
    qs_b, ks_b, vs_b, vcs_b = (a.reshape(db, t_new, -1) for a in (qs, ks, vs, vcs))
    cache_t = lambda c: jnp.transpose(c[0], (0, 2, 3, 1)).reshape(db, d_attn, wb)
    cache_kt, cache_vt = cache_t(cache_k), cache_t(cache_v)
    assert wb == MAX_WINDOW

    xp = x_prompt.reshape(s_len, d_model)
    (xn, q1, q4, q16, k1, k4, k16, v1, v4, v16, k_tail_t, v_tail_t) = _qkv_proj(
        xp, ng, w_qkv_b, tm=256, q_scale=HEAD_DIM ** -0.5 * math.log2(math.e), tail_rows=tail)
    ga, c, oa = _gates_proj(xn, w_gate_b, (lg, lb), w_s[0], b_s[0].T,
                            (qs_b, ks_b, vs_b, cache_kt, cache_vt), tm=512)
    o1, s1 = _dilated_attention(q1[None], k1[None], v1[None], q_rows=1024)
    o4, s4 = _dilated_attention(q4, k4, v4, q_rows=1024)
    o16, s16 = _dilated_attention(q16, k16, v16, q_rows=1024)

    w_ts = jnp.transpose(w_s[0][:, :t_new, :t_new], (1, 2, 0))
    coef = jnp.stack([jnp.stack([w_ts[t, t - delta] if t >= delta else jnp.zeros_like(w_ts[0, 0])
                                 for t in range(t_new)]) for delta in range(t_new)])
    coef = jnp.tile(jnp.repeat(coef, GROUP_WIDTH_B, axis=2), (1, db, 1))
    bias = jnp.tile(jnp.repeat(jnp.transpose(b_s[0][:, :t_new], (1, 0)), GROUP_WIDTH_B, axis=1), (db, 1))
    ys, w_out_b = _sample_out(oa.reshape(db * t_new, d_attn), gas, us, vcs, gbs, xs, w_out0, coef, bias, final_g)
    y_sample = ys.reshape(db, t_new, d_model)

    y_prompt = _out_proj((o1[0], o4, o16), (s1[0], s4, s16), ga, c, xp, w_out_b, final_g, tm=512)
    untranspose = lambda a: jnp.transpose(a.reshape(n_heads, head_dim, tail), (2, 0, 1))

    hs = (n_heads, head_dim)
    return (
        y_prompt.reshape(batch, s_len, d_model),
        y_sample,
        untranspose(k_tail_t).reshape(depth, batch, tail, *hs),
        untranspose(v_tail_t).reshape(depth, batch, tail, *hs),
        ks_b.reshape(depth, db, t_new, *hs),
        vs_b.reshape(depth, db, t_new, *hs),
        vcs_b.reshape(depth, db, t_new, d_chunk),
    )
```

```python
import functools
import math

import jax
import jax.numpy as jnp
from jax import lax
from jax.experimental import pallas as pl
from jax.experimental.pallas import tpu as pltpu

HEAD_DIM = 64
BLOCK = 128
CHUNK = 128
GROUP_WIDTH_B = 128
DILATIONS = ((128, 1), (512, 4), (2048, 16))
MAX_WINDOW = 2048
PAST_LEN = 16384
ROPE_THETA = 10000.0
EPS = 1e-6
NEG_INF = -1e30
N_SEGMENTS = 7
SEG_Q, SEG_K, SEG_V, SEG_GA, SEG_U, SEG_VC, SEG_GB = range(N_SEGMENTS)

LANES = 128
VMEM_LIMIT_BYTES = 56 * 1024 * 1024
SUB_ROWS = 256
SAMPLE_HEAD_GROUP_WIDTH = 512
SAMPLE_OUT_K_CHUNKS = 4

_BF16 = jnp.bfloat16
_F32 = jnp.float32


def _params(*semantics):
    return pltpu.CompilerParams(dimension_semantics=semantics, vmem_limit_bytes=VMEM_LIMIT_BYTES)


def _rope_tables(pos):
    half = HEAD_DIM // 2
    inv = jnp.exp(-math.log(ROPE_THETA) * jnp.arange(half, dtype=_F32) / half)
    ang = pos.astype(_F32)[:, None] * inv[None, :]
    cos = jnp.cos(ang)
    sin = jnp.sin(ang)
    cos_t = jnp.concatenate([cos, cos, cos, cos], axis=-1)
    sin_t = jnp.concatenate([-sin, sin, -sin, sin], axis=-1)
    return cos_t, sin_t


def _rope_tables_blocked(s_len, tm):
    half = HEAD_DIM // 2
    inv = jnp.exp(-math.log(ROPE_THETA) * jnp.arange(half, dtype=_F32) / half)
    tile4 = lambda a: jnp.concatenate([a, a, a, a], axis=-1)
    ang_r = jnp.arange(tm, dtype=jnp.int32).astype(_F32)[:, None] * inv[None, :]
    ang_b = (jnp.arange(s_len // tm, dtype=jnp.int32) * tm).astype(_F32)[:, None] * inv[None, :]
    sign = jnp.concatenate([-jnp.ones((1, half), _F32), jnp.ones((1, half), _F32)] * 2, axis=-1)
    return (tile4(jnp.cos(ang_r)), tile4(jnp.sin(ang_r)), tile4(jnp.cos(ang_b)), tile4(jnp.sin(ang_b)), sign)


def _rope_group(zg, cos, sin_signed):
    lane = lax.broadcasted_iota(jnp.int32, zg.shape, 1)
    first_half = (lane % HEAD_DIM) < (HEAD_DIM // 2)
    partner = jnp.where(first_half, pltpu.roll(zg, LANES - HEAD_DIM // 2, 1),
                        pltpu.roll(zg, HEAD_DIM // 2, 1))
    return zg * cos + partner * sin_signed


def _gelu_layer_norm(z, g, b):
    h = jax.nn.gelu(z)
    mu = jnp.mean(h, axis=-1, keepdims=True)
    hc = h - mu
    var = jnp.mean(hc * hc, axis=-1, keepdims=True)
    return hc * lax.rsqrt(var + EPS) * g + b


def _qkv_kernel(x_ref, g_ref, w_ref, cos_r_ref, sin_r_ref, cos_b_ref, sin_b_ref, sign_ref,
                xn_ref, q1_ref, q4_ref, q16_ref, k1_ref, k4_ref, k16_ref, v1_ref, v4_ref, v16_ref,
                ktail_ref, vtail_ref, nat_ref, res4_ref, *, q_scale):
    tm = x_ref.shape[0]
    d_seg = q1_ref.shape[1]
    sub = min(SUB_ROWS, tm)
    n4, n16 = sub // 4, sub // 16
    step = pl.ds(pl.program_id(0), 1)
    cb, sb = cos_b_ref[step, :], sin_b_ref[step, :]
    segments = (
        (SEG_Q, (q1_ref, q4_ref, q16_ref), None, q_scale),
        (SEG_K, (k1_ref, k4_ref, k16_ref), ktail_ref, 1.0),
        (SEG_V, (v1_ref, v4_ref, v16_ref), vtail_ref, 1.0),
    )
    for t in range(tm // sub):
        rs = slice(t * sub, (t + 1) * sub)
        xf = x_ref[rs, :]
        ms = jnp.mean(xf * xf, axis=-1, keepdims=True)
        xb = (xf * lax.rsqrt(ms + EPS) * g_ref[...]).astype(_BF16)
        xn_ref[rs, :] = xb
        cr, sr = cos_r_ref[rs, :], sin_r_ref[rs, :]
        cos = cr * cb - sr * sb
        sin_signed = (sr * cb + cr * sb) * sign_ref[...]
        for seg, (d1_ref, d4_ref, d16_ref), tail_ref, scale in segments:
            z = jnp.dot(xb, w_ref[:, seg * d_seg:(seg + 1) * d_seg], preferred_element_type=_F32)
            for g in range(d_seg // LANES):
                sl = slice(g * LANES, (g + 1) * LANES)
                r = z[:, sl]
                if seg != SEG_V:
                    r = _rope_group(r, cos, sin_signed)
                if tail_ref is not None:
                    tail_ref[sl, rs] = r.T
                if scale != 1.0:
                    r = r * scale
                d1_ref[rs, sl] = r.astype(d1_ref.dtype)
                nat_ref[g] = r
                for r4 in range(4):
                    blk = nat_ref[g, pl.ds(r4, n4, stride=4), :]
                    d4_ref[r4, t * n4:(t + 1) * n4, sl] = blk.astype(d4_ref.dtype)
                    res4_ref[g, r4 * n4:(r4 + 1) * n4, :] = blk
                for r4 in range(4):
                    for a in range(4):
                        blk = res4_ref[g, pl.ds(r4 * n4 + a, n16, stride=4), :]
                        d16_ref[r4 + 4 * a, t * n16:(t + 1) * n16, sl] = blk.astype(d16_ref.dtype)


def _gates_kernel(xn_ref, w_ref, lng_ref, lnb_ref, ws_ref, bst_ref, q_ref, kn_ref, vn_ref, kt_ref, vt_ref,
                  ga_ref, c_ref, oa_ref, u_ref, vc_ref, mult_ref):
    j = pl.program_id(1)
    t_new = q_ref.shape[0] // 2

    @pl.when(jnp.logical_and(pl.program_id(0) == 0, j == 0))
    def _():
        mult_ref[...] = _key_multiplicity(mult_ref.shape[0], mult_ref.shape[1], t_new)

    tm = xn_ref.shape[0]
    sub = 2 * CHUNK
    tril = (lax.broadcasted_iota(jnp.int32, (CHUNK, CHUNK), 0)
            >= lax.broadcasted_iota(jnp.int32, (CHUNK, CHUNK), 1))

    def gated_gmlp(z, rs):
        gb = jax.nn.silu(z)
        c0 = slice(rs.start, rs.start + CHUNK)
        c1 = slice(rs.start + CHUNK, rs.stop)
        for g in range(ws_ref.shape[0]):
            wm = jnp.where(tril, ws_ref[g], 0.0).astype(_BF16)
            gs = slice(g * GROUP_WIDTH_B, (g + 1) * GROUP_WIDTH_B)
            vc2 = jnp.concatenate([vc_ref[c0, gs], vc_ref[c1, gs]], axis=1)
            mixed = jnp.dot(wm, vc2, preferred_element_type=_F32) + bst_ref[:, g:g + 1]
            for c, half in ((c0, slice(0, GROUP_WIDTH_B)), (c1, slice(GROUP_WIDTH_B, 2 * GROUP_WIDTH_B))):
                local = slice(c.start - rs.start, c.stop - rs.start)
                cv = u_ref[c, gs].astype(_F32) * mixed[:, half] * gb[local, gs]
                c_ref[c, gs] = cv.astype(c_ref.dtype)

    def store(ref, fn):
        def write(z, rs):
            ref[rs, :] = fn(z).astype(ref.dtype)
        return write

    epilogues = (
        store(ga_ref, jax.nn.silu),
        store(u_ref, jax.nn.gelu),
        store(vc_ref, lambda z: _gelu_layer_norm(z, lng_ref[...], lnb_ref[...])),
        gated_gmlp,
    )
    for s, epilogue in enumerate(epilogues):
        @pl.when(j == s)
        def _(s=s, epilogue=epilogue):
            side = _sample_attn_tile(q_ref, kn_ref, vn_ref, kt_ref, vt_ref, oa_ref, mult_ref,
                                     row0=(s % 2) * t_new, t_new=t_new)
            for t in range(tm // sub):
                rs = slice(t * sub, (t + 1) * sub)
                z = jnp.dot(xn_ref[rs, :], w_ref[s], preferred_element_type=_F32)
                next(side, None)
                epilogue(z, rs)
                if t == 0:
                    next(side, None)
            for _ in side:
                pass


def _gates_proj(xn, w_gate, ln, w_s, b_s_t, side, *, tm):
    s_len, d_model = xn.shape
    n_seg, _, d_seg = w_gate.shape
    sq, skn, svn, skt, svt = side
    db, _, wb = skt.shape
    d_attn = sq.shape[1]
    t_new = sq.shape[0] // db
    width = SAMPLE_HEAD_GROUP_WIDTH
    hg = d_attn // width
    n_i = s_len // tm
    assert n_seg == 2 * hg and n_i * 2 == db and 2 * t_new == 8
    new = pl.BlockSpec((2 * t_new, width), lambda i, j: (i, j // 2))
    cache = pl.BlockSpec((None, width, wb), lambda i, j: (2 * i + j % 2, j // 2, 0))
    row_vec = pl.BlockSpec((1, d_seg), lambda i, j: (0, 0))
    rows = pl.BlockSpec((tm, d_seg), lambda i, j: (i, 0))
    whole = lambda a: pl.BlockSpec(a.shape, lambda i, j: (0,) * a.ndim)
    return pl.pallas_call(
        _gates_kernel,
        grid=(n_i, n_seg),
        in_specs=[pl.BlockSpec((tm, d_model), lambda i, j: (i, 0)),
                  pl.BlockSpec(w_gate.shape, lambda i, j: (0, 0, 0), pipeline_mode=pl.Buffered(1)),
                  row_vec, row_vec, whole(w_s), whole(b_s_t), new, new, new, cache, cache],
        out_specs=[rows, rows, new],
        out_shape=[jax.ShapeDtypeStruct((s_len, d_seg), _BF16)] * 2
        + [jax.ShapeDtypeStruct((db * t_new, d_attn), _F32)],
        scratch_shapes=[pltpu.VMEM((tm, d_seg), _BF16), pltpu.VMEM((tm, d_seg), _BF16),
                        pltpu.VMEM((t_new * width // HEAD_DIM, wb), _F32)],
        compiler_params=_params("arbitrary", "arbitrary"),
        name="gates_proj",
    )(xn, w_gate, ln[0].reshape(1, -1), ln[1].reshape(1, -1), w_s, b_s_t, sq, skn, svn, skt, svt)


def _qkv_proj(x, norm_g, w_qkv, *, tm, q_scale, tail_rows):
    s_len, d_model = x.shape
    d_seg = w_qkv.shape[1] // 3
    n_i = s_len // tm
    tail_start = n_i - tail_rows // tm
    tables = _rope_tables_blocked(s_len, tm)
    const = lambda a: pl.BlockSpec(a.shape, lambda i: (0, 0))
    layouts_shape = [jax.ShapeDtypeStruct((s_len, d_seg), _BF16),
                     jax.ShapeDtypeStruct((4, s_len // 4, d_seg), _BF16),
                     jax.ShapeDtypeStruct((16, s_len // 16, d_seg), _BF16)]
    layouts_spec = [pl.BlockSpec((tm, d_seg), lambda i: (i, 0)),
                    pl.BlockSpec((4, tm // 4, d_seg), lambda i: (0, i, 0)),
                    pl.BlockSpec((16, tm // 16, d_seg), lambda i: (0, i, 0))]
    tail_spec = pl.BlockSpec((d_seg, tm), lambda i: (0, jnp.maximum(i - tail_start, 0)))
    tail_shape = jax.ShapeDtypeStruct((d_seg, tail_rows), _F32)
    return pl.pallas_call(
        functools.partial(_qkv_kernel, q_scale=q_scale),
        grid=(n_i,),
        in_specs=[pl.BlockSpec((tm, d_model), lambda i: (i, 0)), pl.BlockSpec((1, d_model), lambda i: (0, 0)),
                  pl.BlockSpec(w_qkv.shape, lambda i: (0, 0), pipeline_mode=pl.Buffered(1))]
        + [const(t) for t in tables],
        out_specs=[pl.BlockSpec((tm, d_model), lambda i: (i, 0))] + layouts_spec * 3 + [tail_spec] * 2,
        out_shape=[jax.ShapeDtypeStruct((s_len, d_model), _BF16)] + layouts_shape * 3 + [tail_shape] * 2,
        scratch_shapes=[pltpu.VMEM((d_seg // LANES, min(SUB_ROWS, tm), LANES), _F32)] * 2,
        compiler_params=_params("arbitrary"),
        name="qkv_proj",
    )(x, norm_g.reshape(1, -1), w_qkv, *tables)


def _attn_kernel(q_ref, kp_ref, kc_ref, vp_ref, vc_ref, o_ref, stat_ref, k0_ref, *vext_refs, res_blocks):
    n = pl.program_id(1)
    first_step = jnp.logical_and(pl.program_id(0) == 0, n == 0)
    n_pairs = q_ref.shape[1] // LANES
    two = 2 * BLOCK

    @pl.when(first_step)
    def _():
        ones = jnp.ones((two, LANES), k0_ref.dtype)
        for vext_ref in vext_refs:
            for hp in range(n_pairs):
                vext_ref[:, (2 * hp + 1) * LANES:(2 * hp + 2) * LANES] = ones

    k0_ref[0:BLOCK, :] = kp_ref[...]
    k0_ref[BLOCK:, :] = kc_ref[0:BLOCK, :]

    qi = lax.broadcasted_iota(jnp.int32, (two, two), 0) % BLOCK
    si = lax.broadcasted_iota(jnp.int32, (two, two), 1)
    dist = qi + BLOCK - si
    band_bias = jnp.where((dist >= 0) & (dist <= BLOCK), 0.0, NEG_INF).astype(_F32)
    lane = lax.broadcasted_iota(jnp.int32, (BLOCK, LANES), 1)
    head0 = lane < HEAD_DIM
    zero = jnp.zeros((BLOCK, LANES), q_ref.dtype)

    for j, vext_ref in enumerate(vext_refs):
        rows = slice(j * BLOCK, (j + 1) * BLOCK)
        prev_keys = slice((j - 1) * BLOCK, (j + 1) * BLOCK)
        bias = band_bias
        if len(vext_refs) % res_blocks == 0:
            if j % res_blocks == 0:
                bias = band_bias + jnp.where(si < BLOCK, NEG_INF, 0.0)
        elif j == 0:
            bias = band_bias + jnp.where(jnp.logical_and(si < BLOCK, n == 0), NEG_INF, 0.0)
        for hp in range(n_pairs):
            sl = slice(hp * LANES, (hp + 1) * LANES)
            ext = slice(2 * hp * LANES, (2 * hp + 1) * LANES)
            if j == 0:
                vext_ref[0:BLOCK, ext] = vp_ref[:, sl]
                vext_ref[BLOCK:, ext] = vc_ref[0:BLOCK, sl]
            else:
                vext_ref[:, ext] = vc_ref[prev_keys, sl]
        stat_acc = jnp.where(lane < 2 * n_pairs, 0.0, 1.0)
        for hp in range(n_pairs):
            sl = slice(hp * LANES, (hp + 1) * LANES)
            q2 = q_ref[rows, sl]
            qs = jnp.concatenate([jnp.where(head0, q2, zero), jnp.where(head0, zero, q2)], axis=0)
            k2 = k0_ref[:, sl] if j == 0 else kc_ref[prev_keys, sl]
            s = lax.dot_general(qs, k2, (((1,), (1,)), ((), ())), preferred_element_type=_F32) + bias
            m = jnp.max(s, axis=-1, keepdims=True)
            p = jnp.exp2(s - m).astype(vext_ref.dtype)
            pv = jnp.dot(p, vext_ref[:, 2 * hp * LANES:(2 * hp + 2) * LANES], preferred_element_type=_F32)
            num, den = pv[:, :LANES], pv[:, LANES:]
            o_ref[rows, sl] = jnp.where(head0, num[:BLOCK], num[BLOCK:]).astype(o_ref.dtype)
            for h, (mh, dh) in ((2 * hp, (m[:BLOCK], den[:BLOCK])), (2 * hp + 1, (m[BLOCK:], den[BLOCK:]))):
                stat_acc = jnp.where(lane == h, mh, jnp.where(lane == 2 * n_pairs + h, dh, stat_acc))
        stat_ref[rows, :] = stat_acc


def _dilated_attention(q, k, v, *, q_rows):
    d_orig, res_len, d_attn = q.shape
    res_blocks = res_len // BLOCK
    if q_rows > res_len:
        q, k, v = (a.reshape(d_orig * res_len // q_rows, q_rows, d_attn) for a in (q, k, v))
    d, sub_len, _ = q.shape
    per = q_rows // BLOCK
    assert per % res_blocks == 0 or res_blocks % per == 0
    cur = pl.BlockSpec((None, q_rows, d_attn), lambda r, n: (r, n, 0))
    prev = pl.BlockSpec((None, BLOCK, d_attn), lambda r, n: (r, jnp.maximum(n * per - 1, 0), 0))
    stat = pl.BlockSpec((None, q_rows, LANES), lambda r, n: (r, n, 0))
    stat_shape = jax.ShapeDtypeStruct((d, sub_len, LANES), _F32)
    o, st = pl.pallas_call(
        functools.partial(_attn_kernel, res_blocks=res_blocks),
        grid=(d, sub_len // q_rows),
        in_specs=[cur, prev, cur, prev, cur],
        out_specs=[cur, stat],
        out_shape=[jax.ShapeDtypeStruct((d, sub_len, d_attn), q.dtype), stat_shape],
        scratch_shapes=[pltpu.VMEM((2 * BLOCK, d_attn), q.dtype)]
        + [pltpu.VMEM((2 * BLOCK, 2 * d_attn), q.dtype)] * per,
        compiler_params=_params("arbitrary", "arbitrary"),
        name=f"dilated_attn_d{d_orig}",
    )(q, k, k, v, v)
    return o.reshape(d_orig, res_len, d_attn), st.reshape(d_orig, res_len, LANES)


def _out_proj_kernel(o1_ref, o4_ref, o16_ref, s1_ref, s4_ref, s16_ref,
                     ga_ref, c_ref, x_ref, wb_ref, fg_ref, y_ref,
                     a_ref, n4_ref, n16_ref, tmp_ref, st_ref):
    tm, d_attn = o1_ref.shape
    n_heads = d_attn // HEAD_DIM
    n_lane_groups = d_attn // LANES
    sub = SUB_ROWS
    n4, n16 = sub // 4, sub // 16

    pair_idx = lax.broadcasted_iota(jnp.int32, (sub, LANES), 1) // HEAD_DIM
    head_lane = lax.broadcasted_iota(jnp.int32, (sub, LANES), 1) < n_heads

    for t in range(tm // sub):
        rs = slice(t * sub, (t + 1) * sub)
        for r in range(4):
            dst = pl.ds(t * sub + r, n4, stride=4)
            src = slice(t * n4, (t + 1) * n4)
            st_ref[0, dst, :] = s4_ref[r, src, :]
            for g in range(n_lane_groups):
                n4_ref[g, dst, :] = o4_ref[r, src, g * LANES:(g + 1) * LANES].astype(_F32)
        for r in range(16):
            dst = pl.ds(t * sub + r, n16, stride=16)
            src = slice(t * n16, (t + 1) * n16)
            st_ref[1, dst, :] = s16_ref[r, src, :]
        for g in range(n_lane_groups):
            sl = slice(g * LANES, (g + 1) * LANES)
            for r4 in range(4):
                for a in range(4):
                    blk = o16_ref[r4 + 4 * a, t * n16:(t + 1) * n16, sl].astype(_F32)
                    tmp_ref[g, pl.ds(t * sub + r4 * n4 + a, n16, stride=4), :] = blk
                n16_ref[g, pl.ds(t * sub + r4, n4, stride=4), :] = tmp_ref[g, t * sub + r4 * n4:
                                                                          t * sub + (r4 + 1) * n4, :]

        ms = [s1_ref[rs, :], st_ref[0, rs, :], st_ref[1, rs, :]]
        dens = [pltpu.roll(s, LANES - n_heads, 1) for s in ms]
        mx = jnp.maximum(jnp.maximum(ms[0], ms[1]), ms[2])
        es = [jnp.exp2(m - mx) for m in ms]
        tot = es[0] * dens[0] + es[1] * dens[1] + es[2] * dens[2]
        tot = jnp.where(head_lane, tot, 1.0)
        ws = [e / tot for e in es]
        for g in range(n_lane_groups):
            sl = slice(g * LANES, (g + 1) * LANES)
            w1, w4, w16 = (jnp.take_along_axis(w, pair_idx + 2 * g, axis=1) for w in ws)
            a = w1 * o1_ref[rs, sl].astype(_F32) + w4 * n4_ref[g, rs, :] + w16 * n16_ref[g, rs, :]
            a_ref[rs, sl] = (a * ga_ref[rs, sl].astype(_F32)).astype(a_ref.dtype)

    for t in range(tm // sub):
        rs = slice(t * sub, (t + 1) * sub)
        acc = jnp.dot(a_ref[rs, :], wb_ref[:d_attn, :], preferred_element_type=_F32)
        acc = acc + jnp.dot(c_ref[rs, :], wb_ref[d_attn:, :], preferred_element_type=_F32)
        xo = x_ref[rs, :] + acc
        msq = jnp.mean(xo * xo, axis=-1, keepdims=True)
        y_ref[rs, :] = xo * lax.rsqrt(msq + EPS) * fg_ref[...]


def _out_proj(os, stats, ga, c, x, w_out, final_g, *, tm):
    m, d_model = x.shape
    d_attn = ga.shape[1]
    d_chunk = c.shape[1]
    rows = lambda n: pl.BlockSpec((tm, n), lambda i: (i, 0))
    res = lambda d, n: pl.BlockSpec((d, tm // d, n), lambda i: (0, i, 0))
    by_pattern = lambda n: [rows(n), res(4, n), res(16, n)]
    fg = final_g.reshape(1, -1)
    slab = pltpu.VMEM((d_attn // LANES, tm, LANES), _F32)
    return pl.pallas_call(
        _out_proj_kernel,
        grid=(m // tm,),
        in_specs=by_pattern(d_attn) + by_pattern(LANES)
        + [rows(d_attn), rows(d_chunk), rows(d_model),
           pl.BlockSpec(w_out.shape, lambda i: (0, 0), pipeline_mode=pl.Buffered(1)),
           pl.BlockSpec(fg.shape, lambda i: (0, 0))],
        out_specs=rows(d_model),
        out_shape=jax.ShapeDtypeStruct((m, d_model), _F32),
        scratch_shapes=[pltpu.VMEM((tm, d_attn), _BF16), slab, slab, slab,
                        pltpu.VMEM((2, tm, LANES), _F32)],
        compiler_params=_params("arbitrary"),
        name="out_proj",
    )(*os, *stats, ga, c, x, w_out, fg)


def _sample_in_proj_kernel(x_ref, g_ref, w_ref, cos_ref, sin_ref, lng_ref, lnb_ref,
                           q_ref, k_ref, v_ref, ga_ref, u_ref, vc_ref, gb_ref, wqkv_ref, wgate_ref, xn_ref):
    j = pl.program_id(0)

    @pl.when(j == 0)
    def _():
        xf = x_ref[...]
        ms = jnp.mean(xf * xf, axis=-1, keepdims=True)
        xn_ref[...] = (xf * lax.rsqrt(ms + EPS) * g_ref[...]).astype(xn_ref.dtype)

    wb = w_ref[...].astype(_BF16)

    @pl.when(j <= SEG_V)
    def _():
        wqkv_ref[...] = wb

    @pl.when(j > SEG_V)
    def _():
        wgate_ref[...] = wb

    z = jnp.dot(xn_ref[...], wb, preferred_element_type=_F32)

    def rope_to(ref, scale):
        for g in range(z.shape[1] // LANES):
            sl = slice(g * LANES, (g + 1) * LANES)
            ref[:, sl] = _rope_group(z[:, sl], cos_ref[...], sin_ref[...]) * scale

    @pl.when(j == SEG_Q)
    def _():
        rope_to(q_ref, HEAD_DIM ** -0.5)

    @pl.when(j == SEG_K)
    def _():
        rope_to(k_ref, 1.0)

    @pl.when(j == SEG_V)
    def _():
        v_ref[...] = z

    @pl.when(j == SEG_GA)
    def _():
        ga_ref[...] = jax.nn.silu(z)

    @pl.when(j == SEG_U)
    def _():
        u_ref[...] = jax.nn.gelu(z)

    @pl.when(j == SEG_VC)
    def _():
        vc_ref[...] = _gelu_layer_norm(z, lng_ref[...], lnb_ref[...])

    @pl.when(j == SEG_GB)
    def _():
        gb_ref[...] = jax.nn.silu(z)


def _sample_in_proj(x, norm_g, w_in, cos_t, sin_t, ln_g, ln_b):
    m, d_model = x.shape
    d_seg = w_in.shape[1] // N_SEGMENTS
    whole = lambda r, c: pl.BlockSpec((r, c), lambda j: (0, 0))
    return pl.pallas_call(
        _sample_in_proj_kernel,
        grid=(N_SEGMENTS,),
        in_specs=[whole(m, d_model), whole(1, d_model),
                  pl.BlockSpec((d_model, d_seg), lambda j: (0, j)),
                  whole(m, LANES), whole(m, LANES), whole(1, d_seg), whole(1, d_seg)],
        out_specs=[whole(m, d_seg)] * N_SEGMENTS
        + [pl.BlockSpec((d_model, d_seg), lambda j: (0, jnp.minimum(j, SEG_V))),
           pl.BlockSpec((None, d_model, d_seg), lambda j: (jnp.maximum(j - SEG_GA, 0), 0, 0))],
        out_shape=[jax.ShapeDtypeStruct((m, d_seg), _F32)] * N_SEGMENTS
        + [jax.ShapeDtypeStruct((d_model, (SEG_V + 1) * d_seg), _BF16),
           jax.ShapeDtypeStruct((N_SEGMENTS - SEG_GA, d_model, d_seg), _BF16)],
        scratch_shapes=[pltpu.VMEM((m, d_model), _BF16)],
        compiler_params=_params("arbitrary"),
        name="sample_in_proj",
    )(x, norm_g.reshape(1, -1), w_in, cos_t, sin_t, ln_g.reshape(1, -1), ln_b.reshape(1, -1))


def _key_multiplicity(rows, wb, t_new):
    t_row = lax.broadcasted_iota(jnp.int32, (rows, wb), 0) // (rows // t_new)
    pos = lax.broadcasted_iota(jnp.int32, (rows, wb), 1)
    dist = wb + t_row - pos
    mult = jnp.zeros((rows, wb), _F32)
    for window, d in DILATIONS:
        mult = mult + jnp.where((dist % d == 0) & (dist <= window), 1.0, 0.0)
    return mult


def _sample_attn_tile(q_ref, kn_ref, vn_ref, kt_ref, vt_ref, o_ref, mult_ref, *, row0, t_new):
    width = q_ref.shape[1]
    n_heads = width // HEAD_DIM
    rows = t_new * n_heads
    new_row = lambda ref, t: ref[row0 + t:row0 + t + 1, :]
    head_row = lax.broadcasted_iota(jnp.int32, (n_heads, width), 0)
    head_col = lax.broadcasted_iota(jnp.int32, (n_heads, width), 1) // HEAD_DIM
    own = head_row == head_col
    qbd = jnp.concatenate([jnp.where(own, new_row(q_ref, t), 0.0) for t in range(t_new)], axis=0)
    mult = mult_ref[...]

    s = jnp.dot(qbd.astype(_BF16), kt_ref[...].astype(_BF16), preferred_element_type=_F32)
    yield
    s = jnp.where(mult > 0.0, s, NEG_INF)
    m = jnp.max(s, axis=-1, keepdims=True)
    t_col = lax.broadcasted_iota(jnp.int32, (rows, 1), 0) // n_heads
    s_new, mult_new = [], []
    for tp in range(t_new):
        s_new.append(jnp.sum(qbd * new_row(kn_ref, tp), axis=-1, keepdims=True))
        dn = t_col - tp
        mn = jnp.zeros((rows, 1), _F32)
        for window, d in DILATIONS:
            mn = mn + jnp.where((dn >= 0) & (dn % d == 0), 1.0, 0.0)
        mult_new.append(mn)
        m = jnp.maximum(m, jnp.where(mn > 0.0, s_new[tp], NEG_INF))
    p = jnp.exp(s - m) * mult
    den = jnp.sum(p, axis=-1, keepdims=True)
    yield
    num = lax.dot_general(p.astype(_BF16), vt_ref[...].astype(_BF16), (((1,), (1,)), ((), ())),
                          preferred_element_type=_F32)
    for tp in range(t_new):
        pn = jnp.where(mult_new[tp] > 0.0, jnp.exp(s_new[tp] - m), 0.0) * mult_new[tp]
        den = den + pn
        num = num + pn * new_row(vn_ref, tp)
    res = num / den
    for t in range(t_new):
        blk = res[t * n_heads:(t + 1) * n_heads, :]
        o_ref[row0 + t:row0 + t + 1, :] = jnp.sum(jnp.where(own, blk, 0.0), axis=0, keepdims=True)


def _sample_out_kernel(oa_ref, ga_ref, u_ref, vc_ref, gb_ref, x_ref, wout_ref, coef_ref, bias_ref, fg_ref,
                       y_ref, wb_ref, ac_ref, acc_ref, *, t_new):
    k = pl.program_id(0)
    d_attn = oa_ref.shape[1]
    kc = wout_ref.shape[0]

    @pl.when(k == 0)
    def _():
        ac_ref[:, :d_attn] = (oa_ref[...] * ga_ref[...]).astype(ac_ref.dtype)
        vc = vc_ref[...]
        mixed = bias_ref[...] + coef_ref[0] * vc
        for delta in range(1, t_new):
            mixed = mixed + coef_ref[delta] * pltpu.roll(vc, delta, 0)
        ac_ref[:, d_attn:] = (u_ref[...] * mixed * gb_ref[...]).astype(ac_ref.dtype)
        acc_ref[...] = x_ref[...]

    wb = wout_ref[...].astype(wb_ref.dtype)
    wb_ref[...] = wb
    col0 = pl.multiple_of(k * kc, kc)
    acc_ref[...] += jnp.dot(ac_ref[:, pl.ds(col0, kc)], wb, preferred_element_type=_F32)

    @pl.when(k == pl.num_programs(0) - 1)
    def _():
        xo = acc_ref[...]
        ms = jnp.mean(xo * xo, axis=-1, keepdims=True)
        y_ref[...] = xo * lax.rsqrt(ms + EPS) * fg_ref[...]


def _sample_out(oa, ga, u, vc, gb, x, w_out, coef, bias, final_g):
    m, d_model = x.shape
    args = (oa, ga, u, vc, gb, x, w_out, coef, bias, final_g.reshape(1, -1))
    whole = lambda a: pl.BlockSpec(a.shape, lambda k: (0,) * a.ndim)
    k_chunk = w_out.shape[0] // SAMPLE_OUT_K_CHUNKS
    w_spec = pl.BlockSpec((k_chunk, d_model), lambda k: (k, 0))
    return pl.pallas_call(
        functools.partial(_sample_out_kernel, t_new=coef.shape[0]),
        grid=(SAMPLE_OUT_K_CHUNKS,),
        in_specs=[w_spec if a is w_out else whole(a) for a in args],
        out_specs=[pl.BlockSpec((m, d_model), lambda k: (0, 0)), w_spec],
        out_shape=[jax.ShapeDtypeStruct((m, d_model), _F32), jax.ShapeDtypeStruct(w_out.shape, _BF16)],
        scratch_shapes=[pltpu.VMEM((m, w_out.shape[0]), _BF16), pltpu.VMEM((m, d_model), _F32)],
        compiler_params=_params("arbitrary"),
        name="sample_out",
    )(*args)


def kernel(x_prompt, x_sample, cache_k, cache_v, norm_g, w_in, ln_g, ln_b, w_s, b_s, w_out, final_g):
    batch, s_len, d_model = x_prompt.shape
    db, t_new, _ = x_sample.shape
    depth, _, wb, n_heads, head_dim = cache_k.shape
    d_attn = n_heads * head_dim
    d_chunk = w_out.shape[1] - d_attn
    assert batch == 1 and depth == 1 and head_dim == HEAD_DIM
    assert w_in.shape[2] == 4 * d_attn + 3 * d_chunk and d_attn == d_chunk
    tail = min(MAX_WINDOW, s_len)
    w_in0, w_out0 = w_in[0], w_out[0]
    ng, lg, lb = norm_g[0], ln_g[0], ln_b[0]

    xs = x_sample.reshape(db * t_new, d_model)
    pos_s = PAST_LEN + jnp.tile(jnp.arange(t_new, dtype=jnp.int32), db)
    cos_s, sin_s = _rope_tables(pos_s)
    qs, ks, vs, gas, us, vcs, gbs, w_qkv_b, w_gate_b = _sample_in_proj(xs, ng, w_in0, cos_s, sin_s, lg, lb)
    cache_t = lambda c: jnp.transpose(c[0], (0, 2, 3, 1)).reshape(db, d_attn, wb)
    cache_kt, cache_vt = cache_t(cache_k), cache_t(cache_v)
    assert wb == MAX_WINDOW

    xp = x_prompt.reshape(s_len, d_model)
    (xn, q1, q4, q16, k1, k4, k16, v1, v4, v16, k_tail_t, v_tail_t) = _qkv_proj(
        xp, ng, w_qkv_b, tm=256, q_scale=HEAD_DIM ** -0.5 * math.log2(math.e), tail_rows=tail)
    ga, c, oa = _gates_proj(xn, w_gate_b, (lg, lb), w_s[0], b_s[0].T,
                            (qs, ks, vs, cache_kt, cache_vt), tm=512)
    o1, s1 = _dilated_attention(q1[None], k1[None], v1[None], q_rows=1024)
    o4, s4 = _dilated_attention(q4, k4, v4, q_rows=1024)
    o16, s16 = _dilated_attention(q16, k16, v16, q_rows=1024)

    w_ts = jnp.transpose(w_s[0][:, :t_new, :t_new], (1, 2, 0))
    coef = jnp.stack([jnp.stack([w_ts[t, t - delta] if t >= delta else jnp.zeros_like(w_ts[0, 0])
                                 for t in range(t_new)]) for delta in range(t_new)])
    coef = jnp.tile(jnp.repeat(coef, GROUP_WIDTH_B, axis=2), (1, db, 1))
    bias = jnp.tile(jnp.repeat(jnp.transpose(b_s[0][:, :t_new], (1, 0)), GROUP_WIDTH_B, axis=1), (db, 1))
    ys, w_out_b = _sample_out(oa, gas, us, vcs, gbs, xs, w_out0, coef, bias, final_g)
    y_sample = ys.reshape(db, t_new, d_model)

    y_prompt = _out_proj((o1[0], o4, o16), (s1[0], s4, s16), ga, c, xp, w_out_b, final_g, tm=512)
    untranspose = lambda a: jnp.transpose(a.reshape(n_heads, head_dim, tail), (2, 0, 1))

    hs = (n_heads, head_dim)
    return (
        y_prompt.reshape(batch, s_len, d_model),
        y_sample,
        untranspose(k_tail_t).reshape(depth, batch, tail, *hs),
        untranspose(v_tail_t).reshape(depth, batch, tail, *hs),
        ks.reshape(depth, db, t_new, *hs),
        vs.reshape(depth, db, t_new, *hs),
        vcs.reshape(depth, db, t_new, d_chunk),
    )
```

```python
import functools
import math

import jax
import jax.numpy as jnp
from jax import lax
from jax.experimental import pallas as pl
from jax.experimental.pallas import tpu as pltpu

HEAD_DIM = 64
BLOCK = 128
CHUNK = 128
GROUP_WIDTH_B = 128
DILATIONS = ((128, 1), (512, 4), (2048, 16))
MAX_WINDOW = 2048
PAST_LEN = 16384
ROPE_THETA = 10000.0
EPS = 1e-6
NEG_INF = -1e30
N_SEGMENTS = 7
SEG_Q, SEG_K, SEG_V, SEG_GA, SEG_U, SEG_VC, SEG_GB = range(N_SEGMENTS)

LANES = 128
VMEM_LIMIT_BYTES = 56 * 1024 * 1024
SUB_ROWS = 256
SAMPLE_HEAD_GROUP_WIDTH = 512
SAMPLE_OUT_K_CHUNKS = 4

_BF16 = jnp.bfloat16
_F32 = jnp.float32


def _params(*semantics):
    return pltpu.CompilerParams(dimension_semantics=semantics, vmem_limit_bytes=VMEM_LIMIT_BYTES)


def _rope_tables(pos):
    half = HEAD_DIM // 2
    inv = jnp.exp(-math.log(ROPE_THETA) * jnp.arange(half, dtype=_F32) / half)
    ang = pos.astype(_F32)[:, None] * inv[None, :]
    cos = jnp.cos(ang)
    sin = jnp.sin(ang)
    cos_t = jnp.concatenate([cos, cos, cos, cos], axis=-1)
    sin_t = jnp.concatenate([-sin, sin, -sin, sin], axis=-1)
    return cos_t, sin_t


def _rope_tables_blocked(s_len, tm):
    half = HEAD_DIM // 2
    inv = jnp.exp(-math.log(ROPE_THETA) * jnp.arange(half, dtype=_F32) / half)
    tile4 = lambda a: jnp.concatenate([a, a, a, a], axis=-1)
    ang_r = jnp.arange(tm, dtype=jnp.int32).astype(_F32)[:, None] * inv[None, :]
    ang_b = (jnp.arange(s_len // tm, dtype=jnp.int32) * tm).astype(_F32)[:, None] * inv[None, :]
    sign = jnp.concatenate([-jnp.ones((1, half), _F32), jnp.ones((1, half), _F32)] * 2, axis=-1)
    return (tile4(jnp.cos(ang_r)), tile4(jnp.sin(ang_r)), tile4(jnp.cos(ang_b)), tile4(jnp.sin(ang_b)), sign)


def _rope_group(zg, cos, sin_signed):
    lane = lax.broadcasted_iota(jnp.int32, zg.shape, 1)
    first_half = (lane % HEAD_DIM) < (HEAD_DIM // 2)
    partner = jnp.where(first_half, pltpu.roll(zg, LANES - HEAD_DIM // 2, 1),
                        pltpu.roll(zg, HEAD_DIM // 2, 1))
    return zg * cos + partner * sin_signed


def _gelu_layer_norm(z, g, b):
    h = jax.nn.gelu(z)
    mu = jnp.mean(h, axis=-1, keepdims=True)
    hc = h - mu
    var = jnp.mean(hc * hc, axis=-1, keepdims=True)
    return hc * lax.rsqrt(var + EPS) * g + b


def _qkv_kernel(x_ref, g_ref, w_ref, cos_r_ref, sin_r_ref, cos_b_ref, sin_b_ref, sign_ref,
                xn_ref, q1_ref, q4_ref, q16_ref, k1_ref, k4_ref, k16_ref, v1_ref, v4_ref, v16_ref,
                ktail_ref, vtail_ref, nat_ref, res4_ref, *, q_scale):
    tm = x_ref.shape[0]
    d_seg = q1_ref.shape[1]
    sub = min(SUB_ROWS, tm)
    n4, n16 = sub // 4, sub // 16
    step = pl.ds(pl.program_id(0), 1)
    cb, sb = cos_b_ref[step, :], sin_b_ref[step, :]
    segments = (
        (SEG_Q, (q1_ref, q4_ref, q16_ref), None, q_scale),
        (SEG_K, (k1_ref, k4_ref, k16_ref), ktail_ref, 1.0),
        (SEG_V, (v1_ref, v4_ref, v16_ref), vtail_ref, 1.0),
    )
    for t in range(tm // sub):
        rs = slice(t * sub, (t + 1) * sub)
        xf = x_ref[rs, :]
        ms = jnp.mean(xf * xf, axis=-1, keepdims=True)
        xb = (xf * lax.rsqrt(ms + EPS) * g_ref[...]).astype(_BF16)
        xn_ref[rs, :] = xb
        cr, sr = cos_r_ref[rs, :], sin_r_ref[rs, :]
        cos = cr * cb - sr * sb
        sin_signed = (sr * cb + cr * sb) * sign_ref[...]
        for seg, (d1_ref, d4_ref, d16_ref), tail_ref, scale in segments:
            z = jnp.dot(xb, w_ref[:, seg * d_seg:(seg + 1) * d_seg], preferred_element_type=_F32)
            for g in range(d_seg // LANES):
                sl = slice(g * LANES, (g + 1) * LANES)
                r = z[:, sl]
                if seg != SEG_V:
                    r = _rope_group(r, cos, sin_signed)
                if tail_ref is not None:
                    tail_ref[sl, rs] = r.T
                if scale != 1.0:
                    r = r * scale
                d1_ref[rs, sl] = r.astype(d1_ref.dtype)
                nat_ref[g] = r
                for r4 in range(4):
                    blk = nat_ref[g, pl.ds(r4, n4, stride=4), :]
                    d4_ref[r4, t * n4:(t + 1) * n4, sl] = blk.astype(d4_ref.dtype)
                    res4_ref[g, r4 * n4:(r4 + 1) * n4, :] = blk
                for r4 in range(4):
                    for a in range(4):
                        blk = res4_ref[g, pl.ds(r4 * n4 + a, n16, stride=4), :]
                        d16_ref[r4 + 4 * a, t * n16:(t + 1) * n16, sl] = blk.astype(d16_ref.dtype)


def _gates_kernel(xn_ref, w_ref, lng_ref, lnb_ref, ws_ref, bst_ref, q_ref, kn_ref, vn_ref, kt_ref, vt_ref,
                  ga_ref, c_ref, oa_ref, u_ref, vc_ref, mult_ref):
    j = pl.program_id(1)
    t_new = q_ref.shape[0] // 2

    @pl.when(jnp.logical_and(pl.program_id(0) == 0, j == 0))
    def _():
        mult_ref[...] = _key_multiplicity(mult_ref.shape[0], mult_ref.shape[1], t_new)

    tm = xn_ref.shape[0]
    sub = 2 * CHUNK
    tril = (lax.broadcasted_iota(jnp.int32, (CHUNK, CHUNK), 0)
            >= lax.broadcasted_iota(jnp.int32, (CHUNK, CHUNK), 1))

    def gated_gmlp(z, rs):
        gb = jax.nn.silu(z)
        c0 = slice(rs.start, rs.start + CHUNK)
        c1 = slice(rs.start + CHUNK, rs.stop)
        for g in range(ws_ref.shape[0]):
            wm = jnp.where(tril, ws_ref[g], 0.0).astype(_BF16)
            gs = slice(g * GROUP_WIDTH_B, (g + 1) * GROUP_WIDTH_B)
            vc2 = jnp.concatenate([vc_ref[c0, gs], vc_ref[c1, gs]], axis=1)
            mixed = jnp.dot(wm, vc2, preferred_element_type=_F32) + bst_ref[:, g:g + 1]
            for c, half in ((c0, slice(0, GROUP_WIDTH_B)), (c1, slice(GROUP_WIDTH_B, 2 * GROUP_WIDTH_B))):
                local = slice(c.start - rs.start, c.stop - rs.start)
                cv = u_ref[c, gs].astype(_F32) * mixed[:, half] * gb[local, gs]
                c_ref[c, gs] = cv.astype(c_ref.dtype)

    def store(ref, fn):
        def write(z, rs):
            ref[rs, :] = fn(z).astype(ref.dtype)
        return write

    epilogues = (
        store(ga_ref, jax.nn.silu),
        store(u_ref, jax.nn.gelu),
        store(vc_ref, lambda z: _gelu_layer_norm(z, lng_ref[...], lnb_ref[...])),
        gated_gmlp,
    )
    for s, epilogue in enumerate(epilogues):
        @pl.when(j == s)
        def _(s=s, epilogue=epilogue):
            side = _sample_attn_tile(q_ref, kn_ref, vn_ref, kt_ref, vt_ref, oa_ref, mult_ref,
                                     row0=(s % 2) * t_new, t_new=t_new)
            for t in range(tm // sub):
                rs = slice(t * sub, (t + 1) * sub)
                z = jnp.dot(xn_ref[rs, :], w_ref[s], preferred_element_type=_F32)
                next(side, None)
                epilogue(z, rs)
                if t == 0:
                    next(side, None)
            for _ in side:
                pass


def _gates_proj(xn, w_gate, ln, w_s, b_s_t, side, *, tm):
    s_len, d_model = xn.shape
    n_seg, _, d_seg = w_gate.shape
    sq, skn, svn, skt, svt = side
    db, _, wb = skt.shape
    d_attn = sq.shape[1]
    t_new = sq.shape[0] // db
    width = SAMPLE_HEAD_GROUP_WIDTH
    hg = d_attn // width
    n_i = s_len // tm
    assert n_seg == 2 * hg and n_i * 2 == db and 2 * t_new == 8
    new = pl.BlockSpec((2 * t_new, width), lambda i, j: (i, j // 2))
    cache = pl.BlockSpec((None, width, wb), lambda i, j: (2 * i + j % 2, j // 2, 0))
    row_vec = pl.BlockSpec((1, d_seg), lambda i, j: (0, 0))
    rows = pl.BlockSpec((tm, d_seg), lambda i, j: (i, 0))
    whole = lambda a: pl.BlockSpec(a.shape, lambda i, j: (0,) * a.ndim)
    return pl.pallas_call(
        _gates_kernel,
        grid=(n_i, n_seg),
        in_specs=[pl.BlockSpec((tm, d_model), lambda i, j: (i, 0)),
                  pl.BlockSpec(w_gate.shape, lambda i, j: (0, 0, 0), pipeline_mode=pl.Buffered(1)),
                  row_vec, row_vec, whole(w_s), whole(b_s_t), new, new, new, cache, cache],
        out_specs=[rows, rows, new],
        out_shape=[jax.ShapeDtypeStruct((s_len, d_seg), _BF16)] * 2
        + [jax.ShapeDtypeStruct((db * t_new, d_attn), _F32)],
        scratch_shapes=[pltpu.VMEM((tm, d_seg), _BF16), pltpu.VMEM((tm, d_seg), _BF16),
                        pltpu.VMEM((t_new * width // HEAD_DIM, wb), _F32)],
        compiler_params=_params("arbitrary", "arbitrary"),
        name="gates_proj",
    )(xn, w_gate, ln[0].reshape(1, -1), ln[1].reshape(1, -1), w_s, b_s_t, sq, skn, svn, skt, svt)


def _qkv_proj(x, norm_g, w_qkv, *, tm, q_scale, tail_rows):
    s_len, d_model = x.shape
    d_seg = w_qkv.shape[1] // 3
    n_i = s_len // tm
    tail_start = n_i - tail_rows // tm
    tables = _rope_tables_blocked(s_len, tm)
    const = lambda a: pl.BlockSpec(a.shape, lambda i: (0, 0))
    layouts_shape = [jax.ShapeDtypeStruct((s_len, d_seg), _BF16),
                     jax.ShapeDtypeStruct((4, s_len // 4, d_seg), _BF16),
                     jax.ShapeDtypeStruct((16, s_len // 16, d_seg), _BF16)]
    layouts_spec = [pl.BlockSpec((tm, d_seg), lambda i: (i, 0)),
                    pl.BlockSpec((4, tm // 4, d_seg), lambda i: (0, i, 0)),
                    pl.BlockSpec((16, tm // 16, d_seg), lambda i: (0, i, 0))]
    tail_spec = pl.BlockSpec((d_seg, tm), lambda i: (0, jnp.maximum(i - tail_start, 0)))
    tail_shape = jax.ShapeDtypeStruct((d_seg, tail_rows), _F32)
    return pl.pallas_call(
        functools.partial(_qkv_kernel, q_scale=q_scale),
        grid=(n_i,),
        in_specs=[pl.BlockSpec((tm, d_model), lambda i: (i, 0)), pl.BlockSpec((1, d_model), lambda i: (0, 0)),
                  pl.BlockSpec(w_qkv.shape, lambda i: (0, 0), pipeline_mode=pl.Buffered(1))]
        + [const(t) for t in tables],
        out_specs=[pl.BlockSpec((tm, d_model), lambda i: (i, 0))] + layouts_spec * 3 + [tail_spec] * 2,
        out_shape=[jax.ShapeDtypeStruct((s_len, d_model), _BF16)] + layouts_shape * 3 + [tail_shape] * 2,
        scratch_shapes=[pltpu.VMEM((d_seg // LANES, min(SUB_ROWS, tm), LANES), _F32)] * 2,
        compiler_params=_params("arbitrary"),
        name="qkv_proj",
    )(x, norm_g.reshape(1, -1), w_qkv, *tables)


def _attn_kernel(q_ref, kp_ref, kc_ref, vp_ref, vc_ref, o_ref, stat_ref, k0_ref, *vext_refs, res_blocks):
    n = pl.program_id(1)
    first_step = jnp.logical_and(pl.program_id(0) == 0, n == 0)
    n_pairs = q_ref.shape[1] // LANES
    two = 2 * BLOCK

    @pl.when(first_step)
    def _():
        ones = jnp.ones((two, LANES), k0_ref.dtype)
        for vext_ref in vext_refs:
            for hp in range(n_pairs):
                vext_ref[:, (2 * hp + 1) * LANES:(2 * hp + 2) * LANES] = ones

    k0_ref[0:BLOCK, :] = kp_ref[...]
    k0_ref[BLOCK:, :] = kc_ref[0:BLOCK, :]

    qi = lax.broadcasted_iota(jnp.int32, (two, two), 0) % BLOCK
    si = lax.broadcasted_iota(jnp.int32, (two, two), 1)
    dist = qi + BLOCK - si
    band_bias = jnp.where((dist >= 0) & (dist <= BLOCK), 0.0, NEG_INF).astype(_F32)
    lane = lax.broadcasted_iota(jnp.int32, (BLOCK, LANES), 1)
    head0 = lane < HEAD_DIM
    zero = jnp.zeros((BLOCK, LANES), q_ref.dtype)

    for j, vext_ref in enumerate(vext_refs):
        rows = slice(j * BLOCK, (j + 1) * BLOCK)
        prev_keys = slice((j - 1) * BLOCK, (j + 1) * BLOCK)
        bias = band_bias
        if len(vext_refs) % res_blocks == 0:
            if j % res_blocks == 0:
                bias = band_bias + jnp.where(si < BLOCK, NEG_INF, 0.0)
        elif j == 0:
            bias = band_bias + jnp.where(jnp.logical_and(si < BLOCK, n == 0), NEG_INF, 0.0)
        for hp in range(n_pairs):
            sl = slice(hp * LANES, (hp + 1) * LANES)
            ext = slice(2 * hp * LANES, (2 * hp + 1) * LANES)
            if j == 0:
                vext_ref[0:BLOCK, ext] = vp_ref[:, sl]
                vext_ref[BLOCK:, ext] = vc_ref[0:BLOCK, sl]
            else:
                vext_ref[:, ext] = vc_ref[prev_keys, sl]
        stat_acc = jnp.where(lane < 2 * n_pairs, 0.0, 1.0)
        for hp in range(n_pairs):
            sl = slice(hp * LANES, (hp + 1) * LANES)
            q2 = q_ref[rows, sl]
            qs = jnp.concatenate([jnp.where(head0, q2, zero), jnp.where(head0, zero, q2)], axis=0)
            k2 = k0_ref[:, sl] if j == 0 else kc_ref[prev_keys, sl]
            s = lax.dot_general(qs, k2, (((1,), (1,)), ((), ())), preferred_element_type=_F32) + bias
            m = jnp.max(s, axis=-1, keepdims=True)
            p = jnp.exp2(s - m).astype(vext_ref.dtype)
            pv = jnp.dot(p, vext_ref[:, 2 * hp * LANES:(2 * hp + 2) * LANES], preferred_element_type=_F32)
            num, den = pv[:, :LANES], pv[:, LANES:]
            o_ref[rows, sl] = jnp.where(head0, num[:BLOCK], num[BLOCK:]).astype(o_ref.dtype)
            for h, (mh, dh) in ((2 * hp, (m[:BLOCK], den[:BLOCK])), (2 * hp + 1, (m[BLOCK:], den[BLOCK:]))):
                stat_acc = jnp.where(lane == h, mh, jnp.where(lane == 2 * n_pairs + h, dh, stat_acc))
        stat_ref[rows, :] = stat_acc


def _dilated_attention(q, k, v, *, q_rows):
    d_orig, res_len, d_attn = q.shape
    res_blocks = res_len // BLOCK
    if q_rows > res_len:
        q, k, v = (a.reshape(d_orig * res_len // q_rows, q_rows, d_attn) for a in (q, k, v))
    d, sub_len, _ = q.shape
    per = q_rows // BLOCK
    assert per % res_blocks == 0 or res_blocks % per == 0
    cur = pl.BlockSpec((None, q_rows, d_attn), lambda r, n: (r, n, 0))
    prev = pl.BlockSpec((None, BLOCK, d_attn), lambda r, n: (r, jnp.maximum(n * per - 1, 0), 0))
    stat = pl.BlockSpec((None, q_rows, LANES), lambda r, n: (r, n, 0))
    stat_shape = jax.ShapeDtypeStruct((d, sub_len, LANES), _F32)
    o, st = pl.pallas_call(
        functools.partial(_attn_kernel, res_blocks=res_blocks),
        grid=(d, sub_len // q_rows),
        in_specs=[cur, prev, cur, prev, cur],
        out_specs=[cur, stat],
        out_shape=[jax.ShapeDtypeStruct((d, sub_len, d_attn), q.dtype), stat_shape],
        scratch_shapes=[pltpu.VMEM((2 * BLOCK, d_attn), q.dtype)]
        + [pltpu.VMEM((2 * BLOCK, 2 * d_attn), q.dtype)] * per,
        compiler_params=_params("arbitrary", "arbitrary"),
        name=f"dilated_attn_d{d_orig}",
    )(q, k, k, v, v)
    return o.reshape(d_orig, res_len, d_attn), st.reshape(d_orig, res_len, LANES)


def _out_proj_kernel(o1_ref, o4_ref, o16_ref, s1_ref, s4_ref, s16_ref,
                     ga_ref, c_ref, x_ref, wb_ref, fg_ref, y_ref,
                     a_ref, n4_ref, n16_ref, tmp_ref, st_ref):
    tm, d_attn = o1_ref.shape
    n_heads = d_attn // HEAD_DIM
    n_lane_groups = d_attn // LANES
    sub = SUB_ROWS
    n4, n16 = sub // 4, sub // 16

    pair_idx = lax.broadcasted_iota(jnp.int32, (sub, LANES), 1) // HEAD_DIM
    head_lane = lax.broadcasted_iota(jnp.int32, (sub, LANES), 1) < n_heads

    for t in range(tm // sub):
        rs = slice(t * sub, (t + 1) * sub)
        for r in range(4):
            dst = pl.ds(t * sub + r, n4, stride=4)
            src = slice(t * n4, (t + 1) * n4)
            st_ref[0, dst, :] = s4_ref[r, src, :]
            for g in range(n_lane_groups):
                n4_ref[g, dst, :] = o4_ref[r, src, g * LANES:(g + 1) * LANES].astype(_F32)
        for r in range(16):
            dst = pl.ds(t * sub + r, n16, stride=16)
            src = slice(t * n16, (t + 1) * n16)
            st_ref[1, dst, :] = s16_ref[r, src, :]
        for g in range(n_lane_groups):
            sl = slice(g * LANES, (g + 1) * LANES)
            for r4 in range(4):
                for a in range(4):
                    blk = o16_ref[r4 + 4 * a, t * n16:(t + 1) * n16, sl].astype(_F32)
                    tmp_ref[g, pl.ds(t * sub + r4 * n4 + a, n16, stride=4), :] = blk
                n16_ref[g, pl.ds(t * sub + r4, n4, stride=4), :] = tmp_ref[g, t * sub + r4 * n4:
                                                                          t * sub + (r4 + 1) * n4, :]

        ms = [s1_ref[rs, :], st_ref[0, rs, :], st_ref[1, rs, :]]
        dens = [pltpu.roll(s, LANES - n_heads, 1) for s in ms]
        mx = jnp.maximum(jnp.maximum(ms[0], ms[1]), ms[2])
        es = [jnp.exp2(m - mx) for m in ms]
        tot = es[0] * dens[0] + es[1] * dens[1] + es[2] * dens[2]
        tot = jnp.where(head_lane, tot, 1.0)
        ws = [e / tot for e in es]
        for g in range(n_lane_groups):
            sl = slice(g * LANES, (g + 1) * LANES)
            w1, w4, w16 = (jnp.take_along_axis(w, pair_idx + 2 * g, axis=1) for w in ws)
            a = w1 * o1_ref[rs, sl].astype(_F32) + w4 * n4_ref[g, rs, :] + w16 * n16_ref[g, rs, :]
            a_ref[rs, sl] = (a * ga_ref[rs, sl].astype(_F32)).astype(a_ref.dtype)

    for t in range(tm // sub):
        rs = slice(t * sub, (t + 1) * sub)
        acc = jnp.dot(a_ref[rs, :], wb_ref[:d_attn, :], preferred_element_type=_F32)
        acc = acc + jnp.dot(c_ref[rs, :], wb_ref[d_attn:, :], preferred_element_type=_F32)
        xo = x_ref[rs, :] + acc
        msq = jnp.mean(xo * xo, axis=-1, keepdims=True)
        y_ref[rs, :] = xo * lax.rsqrt(msq + EPS) * fg_ref[...]


def _out_proj(os, stats, ga, c, x, w_out, final_g, *, tm):
    m, d_model = x.shape
    d_attn = ga.shape[1]
    d_chunk = c.shape[1]
    rows = lambda n: pl.BlockSpec((tm, n), lambda i: (i, 0))
    res = lambda d, n: pl.BlockSpec((d, tm // d, n), lambda i: (0, i, 0))
    by_pattern = lambda n: [rows(n), res(4, n), res(16, n)]
    fg = final_g.reshape(1, -1)
    slab = pltpu.VMEM((d_attn // LANES, tm, LANES), _F32)
    return pl.pallas_call(
        _out_proj_kernel,
        grid=(m // tm,),
        in_specs=by_pattern(d_attn) + by_pattern(LANES)
        + [rows(d_attn), rows(d_chunk), rows(d_model),
           pl.BlockSpec(w_out.shape, lambda i: (0, 0), pipeline_mode=pl.Buffered(1)),
           pl.BlockSpec(fg.shape, lambda i: (0, 0))],
        out_specs=rows(d_model),
        out_shape=jax.ShapeDtypeStruct((m, d_model), _F32),
        scratch_shapes=[pltpu.VMEM((tm, d_attn), _BF16), slab, slab, slab,
                        pltpu.VMEM((2, tm, LANES), _F32)],
        compiler_params=_params("arbitrary"),
        name="out_proj",
    )(*os, *stats, ga, c, x, w_out, fg)


def _sample_in_proj_kernel(x_ref, g_ref, w_ref, cos_ref, sin_ref, lng_ref, lnb_ref,
                           q_ref, k_ref, v_ref, ga_ref, u_ref, vc_ref, gb_ref, wqkv_ref, wgate_ref, xn_ref):
    j = pl.program_id(0)

    @pl.when(j == 0)
    def _():
        xf = x_ref[...]
        ms = jnp.mean(xf * xf, axis=-1, keepdims=True)
        xn_ref[...] = (xf * lax.rsqrt(ms + EPS) * g_ref[...]).astype(xn_ref.dtype)

    wb = w_ref[...].astype(_BF16)

    @pl.when(j <= SEG_V)
    def _():
        wqkv_ref[...] = wb

    @pl.when(j > SEG_V)
    def _():
        wgate_ref[...] = wb

    z = jnp.dot(xn_ref[...], wb, preferred_element_type=_F32)

    def rope_to(ref, scale):
        for g in range(z.shape[1] // LANES):
            sl = slice(g * LANES, (g + 1) * LANES)
            ref[:, sl] = _rope_group(z[:, sl], cos_ref[...], sin_ref[...]) * scale

    @pl.when(j == SEG_Q)
    def _():
        rope_to(q_ref, HEAD_DIM ** -0.5)

    @pl.when(j == SEG_K)
    def _():
        rope_to(k_ref, 1.0)

    @pl.when(j == SEG_V)
    def _():
        v_ref[...] = z

    @pl.when(j == SEG_GA)
    def _():
        ga_ref[...] = jax.nn.silu(z)

    @pl.when(j == SEG_U)
    def _():
        u_ref[...] = jax.nn.gelu(z)

    @pl.when(j == SEG_VC)
    def _():
        vc_ref[...] = _gelu_layer_norm(z, lng_ref[...], lnb_ref[...])

    @pl.when(j == SEG_GB)
    def _():
        gb_ref[...] = jax.nn.silu(z)


def _sample_in_proj(x, norm_g, w_in, cos_t, sin_t, ln_g, ln_b):
    m, d_model = x.shape
    d_seg = w_in.shape[1] // N_SEGMENTS
    whole = lambda r, c: pl.BlockSpec((r, c), lambda j: (0, 0))
    return pl.pallas_call(
        _sample_in_proj_kernel,
        grid=(N_SEGMENTS,),
        in_specs=[whole(m, d_model), whole(1, d_model),
                  pl.BlockSpec((d_model, d_seg), lambda j: (0, j)),
                  whole(m, LANES), whole(m, LANES), whole(1, d_seg), whole(1, d_seg)],
        out_specs=[whole(m, d_seg)] * N_SEGMENTS
        + [pl.BlockSpec((d_model, d_seg), lambda j: (0, jnp.minimum(j, SEG_V))),
           pl.BlockSpec((None, d_model, d_seg), lambda j: (jnp.maximum(j - SEG_GA, 0), 0, 0))],
        out_shape=[jax.ShapeDtypeStruct((m, d_seg), _F32)] * N_SEGMENTS
        + [jax.ShapeDtypeStruct((d_model, (SEG_V + 1) * d_seg), _BF16),
           jax.ShapeDtypeStruct((N_SEGMENTS - SEG_GA, d_model, d_seg), _BF16)],
        scratch_shapes=[pltpu.VMEM((m, d_model), _BF16)],
        compiler_params=_params("arbitrary"),
        name="sample_in_proj",
    )(x, norm_g.reshape(1, -1), w_in, cos_t, sin_t, ln_g.reshape(1, -1), ln_b.reshape(1, -1))


def _key_multiplicity(rows, wb, t_new):
    t_row = lax.broadcasted_iota(jnp.int32, (rows, wb), 0) // (rows // t_new)
    pos = lax.broadcasted_iota(jnp.int32, (rows, wb), 1)
    dist = wb + t_row - pos
    mult = jnp.zeros((rows, wb), _F32)
    for window, d in DILATIONS:
        mult = mult + jnp.where((dist % d == 0) & (dist <= window), 1.0, 0.0)
    return mult


def _sample_attn_tile(q_ref, kn_ref, vn_ref, kt_ref, vt_ref, o_ref, mult_ref, *, row0, t_new):
    width = q_ref.shape[1]
    n_heads = width // HEAD_DIM
    rows = t_new * n_heads
    new_row = lambda ref, t: ref[row0 + t:row0 + t + 1, :]
    head_row = lax.broadcasted_iota(jnp.int32, (n_heads, width), 0)
    head_col = lax.broadcasted_iota(jnp.int32, (n_heads, width), 1) // HEAD_DIM
    own = head_row == head_col
    qbd = jnp.concatenate([jnp.where(own, new_row(q_ref, t), 0.0) for t in range(t_new)], axis=0)
    mult = mult_ref[...]

    s = jnp.dot(qbd.astype(_BF16), kt_ref[...].astype(_BF16), preferred_element_type=_F32)
    yield
    s = jnp.where(mult > 0.0, s, NEG_INF)
    m = jnp.max(s, axis=-1, keepdims=True)
    t_col = lax.broadcasted_iota(jnp.int32, (rows, 1), 0) // n_heads
    s_new, mult_new = [], []
    for tp in range(t_new):
        s_new.append(jnp.sum(qbd * new_row(kn_ref, tp), axis=-1, keepdims=True))
        dn = t_col - tp
        mn = jnp.zeros((rows, 1), _F32)
        for window, d in DILATIONS:
            mn = mn + jnp.where((dn >= 0) & (dn % d == 0), 1.0, 0.0)
        mult_new.append(mn)
        m = jnp.maximum(m, jnp.where(mn > 0.0, s_new[tp], NEG_INF))
    p = jnp.exp(s - m) * mult
    den = jnp.sum(p, axis=-1, keepdims=True)
    yield
    num = lax.dot_general(p.astype(_BF16), vt_ref[...].astype(_BF16), (((1,), (1,)), ((), ())),
                          preferred_element_type=_F32)
    for tp in range(t_new):
        pn = jnp.where(mult_new[tp] > 0.0, jnp.exp(s_new[tp] - m), 0.0) * mult_new[tp]
        den = den + pn
        num = num + pn * new_row(vn_ref, tp)
    res = num / den
    for t in range(t_new):
        blk = res[t * n_heads:(t + 1) * n_heads, :]
        o_ref[row0 + t:row0 + t + 1, :] = jnp.sum(jnp.where(own, blk, 0.0), axis=0, keepdims=True)


def _sample_out_kernel(oa_ref, ga_ref, u_ref, vc_ref, gb_ref, x_ref, wout_ref, coef_ref, bias_ref, fg_ref,
                       y_ref, wb_ref, ac_ref, acc_ref, *, t_new):
    k = pl.program_id(0)
    d_attn = oa_ref.shape[1]
    kc = wout_ref.shape[0]

    @pl.when(k == 0)
    def _():
        ac_ref[:, :d_attn] = (oa_ref[...] * ga_ref[...]).astype(ac_ref.dtype)
        vc = vc_ref[...]
        mixed = bias_ref[...] + coef_ref[0] * vc
        for delta in range(1, t_new):
            mixed = mixed + coef_ref[delta] * pltpu.roll(vc, delta, 0)
        ac_ref[:, d_attn:] = (u_ref[...] * mixed * gb_ref[...]).astype(ac_ref.dtype)
        acc_ref[...] = x_ref[...]

    wb = wout_ref[...].astype(wb_ref.dtype)
    wb_ref[...] = wb
    col0 = pl.multiple_of(k * kc, kc)
    acc_ref[...] += jnp.dot(ac_ref[:, pl.ds(col0, kc)], wb, preferred_element_type=_F32)

    @pl.when(k == pl.num_programs(0) - 1)
    def _():
        xo = acc_ref[...]
        ms = jnp.mean(xo * xo, axis=-1, keepdims=True)
        y_ref[...] = xo * lax.rsqrt(ms + EPS) * fg_ref[...]


def _sample_out(oa, ga, u, vc, gb, x, w_out, coef, bias, final_g):
    m, d_model = x.shape
    args = (oa, ga, u, vc, gb, x, w_out, coef, bias, final_g.reshape(1, -1))
    whole = lambda a: pl.BlockSpec(a.shape, lambda k: (0,) * a.ndim)
    k_chunk = w_out.shape[0] // SAMPLE_OUT_K_CHUNKS
    w_spec = pl.BlockSpec((k_chunk, d_model), lambda k: (k, 0))
    return pl.pallas_call(
        functools.partial(_sample_out_kernel, t_new=coef.shape[0]),
        grid=(SAMPLE_OUT_K_CHUNKS,),
        in_specs=[w_spec if a is w_out else whole(a) for a in args],
        out_specs=[pl.BlockSpec((m, d_model), lambda k: (0, 0)), w_spec],
        out_shape=[jax.ShapeDtypeStruct((m, d_model), _F32), jax.ShapeDtypeStruct(w_out.shape, _BF16)],
        scratch_shapes=[pltpu.VMEM((m, w_out.shape[0]), _BF16), pltpu.VMEM((m, d_model), _F32)],
        compiler_params=_params("arbitrary"),
        name="sample_out",
    )(*args)


def kernel(x_prompt, x_sample, cache_k, cache_v, norm_g, w_in, ln_g, ln_b, w_s, b_s, w_out, final_g):
    batch, s_len, d_model = x_prompt.shape
    db, t_new, _ = x_sample.shape
    depth, _, wb, n_heads, head_dim = cache_k.shape
    d_attn = n_heads * head_dim
    d_chunk = w_out.shape[1] - d_attn
    assert batch == 1 and depth == 1 and head_dim == HEAD_DIM
    assert w_in.shape[2] == 4 * d_attn + 3 * d_chunk and d_attn == d_chunk
    tail = min(MAX_WINDOW, s_len)
    w_in0, w_out0 = w_in[0], w_out[0]
    ng, lg, lb = norm_g[0], ln_g[0], ln_b[0]

    xs = x_sample.reshape(db * t_new, d_model)
    pos_s = PAST_LEN + jnp.tile(jnp.arange(t_new, dtype=jnp.int32), db)
    cos_s, sin_s = _rope_tables(pos_s)
    qs, ks, vs, gas, us, vcs, gbs, w_qkv_b, w_gate_b = _sample_in_proj(xs, ng, w_in0, cos_s, sin_s, lg, lb)
    cache_t = lambda c: jnp.transpose(c[0], (0, 2, 3, 1)).reshape(db, d_attn, wb)
    cache_kt, cache_vt = cache_t(cache_k), cache_t(cache_v)
    assert wb == MAX_WINDOW

    xp = x_prompt.reshape(s_len, d_model)
    (xn, q1, q4, q16, k1, k4, k16, v1, v4, v16, k_tail_t, v_tail_t) = _qkv_proj(
        xp, ng, w_qkv_b, tm=256, q_scale=HEAD_DIM ** -0.5 * math.log2(math.e), tail_rows=tail)
    ga, c, oa = _gates_proj(xn, w_gate_b, (lg, lb), w_s[0], b_s[0].T,
                            (qs, ks, vs, cache_kt, cache_vt), tm=512)
    o1, s1 = _dilated_attention(q1[None], k1[None], v1[None], q_rows=2048)
    o4, s4 = _dilated_attention(q4, k4, v4, q_rows=2048)
    o16, s16 = _dilated_attention(q16, k16, v16, q_rows=2048)

    w_ts = jnp.transpose(w_s[0][:, :t_new, :t_new], (1, 2, 0))
    coef = jnp.stack([jnp.stack([w_ts[t, t - delta] if t >= delta else jnp.zeros_like(w_ts[0, 0])
                                 for t in range(t_new)]) for delta in range(t_new)])
    coef = jnp.tile(jnp.repeat(coef, GROUP_WIDTH_B, axis=2), (1, db, 1))
    bias = jnp.tile(jnp.repeat(jnp.transpose(b_s[0][:, :t_new], (1, 0)), GROUP_WIDTH_B, axis=1), (db, 1))
    ys, w_out_b = _sample_out(oa, gas, us, vcs, gbs, xs, w_out0, coef, bias, final_g)
    y_sample = ys.reshape(db, t_new, d_model)

    y_prompt = _out_proj((o1[0], o4, o16), (s1[0], s4, s16), ga, c, xp, w_out_b, final_g, tm=512)
    untranspose = lambda a: jnp.transpose(a.reshape(n_heads, head_dim, tail), (2, 0, 1))

    hs = (n_heads, head_dim)
    return (
        y_prompt.reshape(batch, s_len, d_model),
        y_sample,
        untranspose(k_tail_t).reshape(depth, batch, tail, *hs),
        untranspose(v_tail_t).reshape(depth, batch, tail, *hs),
        ks.reshape(depth, db, t_new, *hs),
        vs.reshape(depth, db, t_new, *hs),
        vcs.reshape(depth, db, t_new, d_chunk),
    )
```

```python
import functools
import math

import jax
import jax.numpy as jnp
from jax import lax
from jax.experimental import pallas as pl
from jax.experimental.pallas import tpu as pltpu

HEAD_DIM = 64
BLOCK = 128
CHUNK = 128
GROUP_WIDTH_B = 128
DILATIONS = ((128, 1), (512, 4), (2048, 16))
MAX_WINDOW = 2048
PAST_LEN = 16384
ROPE_THETA = 10000.0
EPS = 1e-6
NEG_INF = -1e30
N_SEGMENTS = 7
SEG_Q, SEG_K, SEG_V, SEG_GA, SEG_U, SEG_VC, SEG_GB = range(N_SEGMENTS)

LANES = 128
VMEM_LIMIT_BYTES = 56 * 1024 * 1024
SUB_ROWS = 256
SAMPLE_HEAD_GROUP_WIDTH = 512
SAMPLE_OUT_K_CHUNKS = 4
SAMPLE_POS_CHUNK = 512

_BF16 = jnp.bfloat16
_F32 = jnp.float32


def _params(*semantics):
    return pltpu.CompilerParams(dimension_semantics=semantics, vmem_limit_bytes=VMEM_LIMIT_BYTES)


def _rope_tables(pos):
    half = HEAD_DIM // 2
    inv = jnp.exp(-math.log(ROPE_THETA) * jnp.arange(half, dtype=_F32) / half)
    ang = pos.astype(_F32)[:, None] * inv[None, :]
    cos = jnp.cos(ang)
    sin = jnp.sin(ang)
    cos_t = jnp.concatenate([cos, cos, cos, cos], axis=-1)
    sin_t = jnp.concatenate([-sin, sin, -sin, sin], axis=-1)
    return cos_t, sin_t


def _rope_tables_blocked(s_len, tm):
    half = HEAD_DIM // 2
    inv = jnp.exp(-math.log(ROPE_THETA) * jnp.arange(half, dtype=_F32) / half)
    tile4 = lambda a: jnp.concatenate([a, a, a, a], axis=-1)
    ang_r = jnp.arange(tm, dtype=jnp.int32).astype(_F32)[:, None] * inv[None, :]
    ang_b = (jnp.arange(s_len // tm, dtype=jnp.int32) * tm).astype(_F32)[:, None] * inv[None, :]
    sign = jnp.concatenate([-jnp.ones((1, half), _F32), jnp.ones((1, half), _F32)] * 2, axis=-1)
    return (tile4(jnp.cos(ang_r)), tile4(jnp.sin(ang_r)), tile4(jnp.cos(ang_b)), tile4(jnp.sin(ang_b)), sign)


def _rope_group(zg, cos, sin_signed):
    lane = lax.broadcasted_iota(jnp.int32, zg.shape, 1)
    first_half = (lane % HEAD_DIM) < (HEAD_DIM // 2)
    partner = jnp.where(first_half, pltpu.roll(zg, LANES - HEAD_DIM // 2, 1),
                        pltpu.roll(zg, HEAD_DIM // 2, 1))
    return zg * cos + partner * sin_signed


def _gelu_layer_norm(z, g, b):
    h = jax.nn.gelu(z)
    mu = jnp.mean(h, axis=-1, keepdims=True)
    hc = h - mu
    var = jnp.mean(hc * hc, axis=-1, keepdims=True)
    return hc * lax.rsqrt(var + EPS) * g + b


def _qkv_kernel(x_ref, g_ref, w_ref, cos_r_ref, sin_r_ref, cos_b_ref, sin_b_ref, sign_ref,
                xn_ref, q1_ref, q4_ref, q16_ref, k1_ref, k4_ref, k16_ref, v1_ref, v4_ref, v16_ref,
                ktail_ref, vtail_ref, nat_ref, res4_ref, *, q_scale):
    tm = x_ref.shape[0]
    d_seg = q1_ref.shape[1]
    sub = min(SUB_ROWS, tm)
    n4, n16 = sub // 4, sub // 16
    step = pl.ds(pl.program_id(0), 1)
    cb, sb = cos_b_ref[step, :], sin_b_ref[step, :]
    segments = (
        (SEG_Q, (q1_ref, q4_ref, q16_ref), None, q_scale),
        (SEG_K, (k1_ref, k4_ref, k16_ref), ktail_ref, 1.0),
        (SEG_V, (v1_ref, v4_ref, v16_ref), vtail_ref, 1.0),
    )
    for t in range(tm // sub):
        rs = slice(t * sub, (t + 1) * sub)
        xf = x_ref[rs, :]
        ms = jnp.mean(xf * xf, axis=-1, keepdims=True)
        xb = (xf * lax.rsqrt(ms + EPS) * g_ref[...]).astype(_BF16)
        xn_ref[rs, :] = xb
        cr, sr = cos_r_ref[rs, :], sin_r_ref[rs, :]
        cos = cr * cb - sr * sb
        sin_signed = (sr * cb + cr * sb) * sign_ref[...]
        for seg, (d1_ref, d4_ref, d16_ref), tail_ref, scale in segments:
            z = jnp.dot(xb, w_ref[:, seg * d_seg:(seg + 1) * d_seg], preferred_element_type=_F32)
            for g in range(d_seg // LANES):
                sl = slice(g * LANES, (g + 1) * LANES)
                r = z[:, sl]
                if seg != SEG_V:
                    r = _rope_group(r, cos, sin_signed)
                if tail_ref is not None:
                    tail_ref[sl, rs] = r.T
                if scale != 1.0:
                    r = r * scale
                d1_ref[rs, sl] = r.astype(d1_ref.dtype)
                nat_ref[g] = r
                for r4 in range(4):
                    blk = nat_ref[g, pl.ds(r4, n4, stride=4), :]
                    d4_ref[r4, t * n4:(t + 1) * n4, sl] = blk.astype(d4_ref.dtype)
                    res4_ref[g, r4 * n4:(r4 + 1) * n4, :] = blk
                for r4 in range(4):
                    for a in range(4):
                        blk = res4_ref[g, pl.ds(r4 * n4 + a, n16, stride=4), :]
                        d16_ref[r4 + 4 * a, t * n16:(t + 1) * n16, sl] = blk.astype(d16_ref.dtype)


def _gates_kernel(xn_ref, w_ref, lng_ref, lnb_ref, ws_ref, bst_ref, q_ref, kn_ref, vn_ref, kt_ref, vt_ref,
                  ga_ref, c_ref, oa_ref, u_ref, vc_ref, mult_ref):
    j = pl.program_id(1)
    t_new = q_ref.shape[0] // 2

    @pl.when(jnp.logical_and(pl.program_id(0) == 0, j == 0))
    def _():
        mult_ref[...] = _key_multiplicity(mult_ref.shape[0], mult_ref.shape[1], t_new)

    tm = xn_ref.shape[0]
    sub = 2 * CHUNK
    tril = (lax.broadcasted_iota(jnp.int32, (CHUNK, CHUNK), 0)
            >= lax.broadcasted_iota(jnp.int32, (CHUNK, CHUNK), 1))

    def gated_gmlp(z, rs):
        gb = jax.nn.silu(z)
        c0 = slice(rs.start, rs.start + CHUNK)
        c1 = slice(rs.start + CHUNK, rs.stop)
        for g in range(ws_ref.shape[0]):
            wm = jnp.where(tril, ws_ref[g], 0.0).astype(_BF16)
            gs = slice(g * GROUP_WIDTH_B, (g + 1) * GROUP_WIDTH_B)
            vc2 = jnp.concatenate([vc_ref[c0, gs], vc_ref[c1, gs]], axis=1)
            mixed = jnp.dot(wm, vc2, preferred_element_type=_F32) + bst_ref[:, g:g + 1]
            for c, half in ((c0, slice(0, GROUP_WIDTH_B)), (c1, slice(GROUP_WIDTH_B, 2 * GROUP_WIDTH_B))):
                local = slice(c.start - rs.start, c.stop - rs.start)
                cv = u_ref[c, gs].astype(_F32) * mixed[:, half] * gb[local, gs]
                c_ref[c, gs] = cv.astype(c_ref.dtype)

    def store(ref, fn):
        def write(z, rs):
            ref[rs, :] = fn(z).astype(ref.dtype)
        return write

    epilogues = (
        store(ga_ref, jax.nn.silu),
        store(u_ref, jax.nn.gelu),
        store(vc_ref, lambda z: _gelu_layer_norm(z, lng_ref[...], lnb_ref[...])),
        gated_gmlp,
    )
    for s, epilogue in enumerate(epilogues):
        @pl.when(j == s)
        def _(s=s, epilogue=epilogue):
            side = _sample_attn_tile(q_ref, kn_ref, vn_ref, kt_ref, vt_ref, oa_ref, mult_ref,
                                     row0=(s % 2) * t_new, t_new=t_new)
            for t in range(tm // sub):
                rs = slice(t * sub, (t + 1) * sub)
                z = jnp.dot(xn_ref[rs, :], w_ref[s], preferred_element_type=_F32)
                next(side, None)
                epilogue(z, rs)
                if t == 0:
                    next(side, None)
            for _ in side:
                pass


def _gates_proj(xn, w_gate, ln, w_s, b_s_t, side, *, tm):
    s_len, d_model = xn.shape
    n_seg, _, d_seg = w_gate.shape
    sq, skn, svn, skt, svt = side
    db, _, wb = skt.shape
    d_attn = sq.shape[1]
    t_new = sq.shape[0] // db
    width = SAMPLE_HEAD_GROUP_WIDTH
    hg = d_attn // width
    n_i = s_len // tm
    assert n_seg == 2 * hg and n_i * 2 == db and 2 * t_new == 8
    new = pl.BlockSpec((2 * t_new, width), lambda i, j: (i, j // 2))
    cache = pl.BlockSpec((None, width, wb), lambda i, j: (2 * i + j % 2, j // 2, 0))
    row_vec = pl.BlockSpec((1, d_seg), lambda i, j: (0, 0))
    rows = pl.BlockSpec((tm, d_seg), lambda i, j: (i, 0))
    whole = lambda a: pl.BlockSpec(a.shape, lambda i, j: (0,) * a.ndim)
    return pl.pallas_call(
        _gates_kernel,
        grid=(n_i, n_seg),
        in_specs=[pl.BlockSpec((tm, d_model), lambda i, j: (i, 0)),
                  pl.BlockSpec(w_gate.shape, lambda i, j: (0, 0, 0), pipeline_mode=pl.Buffered(1)),
                  row_vec, row_vec, whole(w_s), whole(b_s_t), new, new, new, cache, cache],
        out_specs=[rows, rows, new],
        out_shape=[jax.ShapeDtypeStruct((s_len, d_seg), _BF16)] * 2
        + [jax.ShapeDtypeStruct((db * t_new, d_attn), _F32)],
        scratch_shapes=[pltpu.VMEM((tm, d_seg), _BF16), pltpu.VMEM((tm, d_seg), _BF16),
                        pltpu.VMEM((t_new * width // HEAD_DIM, wb), _F32)],
        compiler_params=_params("arbitrary", "arbitrary"),
        name="gates_proj",
    )(xn, w_gate, ln[0].reshape(1, -1), ln[1].reshape(1, -1), w_s, b_s_t, sq, skn, svn, skt, svt)


def _qkv_proj(x, norm_g, w_qkv, *, tm, q_scale, tail_rows):
    s_len, d_model = x.shape
    d_seg = w_qkv.shape[1] // 3
    n_i = s_len // tm
    tail_start = n_i - tail_rows // tm
    tables = _rope_tables_blocked(s_len, tm)
    const = lambda a: pl.BlockSpec(a.shape, lambda i: (0, 0))
    layouts_shape = [jax.ShapeDtypeStruct((s_len, d_seg), _BF16),
                     jax.ShapeDtypeStruct((4, s_len // 4, d_seg), _BF16),
                     jax.ShapeDtypeStruct((16, s_len // 16, d_seg), _BF16)]
    layouts_spec = [pl.BlockSpec((tm, d_seg), lambda i: (i, 0)),
                    pl.BlockSpec((4, tm // 4, d_seg), lambda i: (0, i, 0)),
                    pl.BlockSpec((16, tm // 16, d_seg), lambda i: (0, i, 0))]
    tail_spec = pl.BlockSpec((d_seg, tm), lambda i: (0, jnp.maximum(i - tail_start, 0)))
    tail_shape = jax.ShapeDtypeStruct((d_seg, tail_rows), _F32)
    return pl.pallas_call(
        functools.partial(_qkv_kernel, q_scale=q_scale),
        grid=(n_i,),
        in_specs=[pl.BlockSpec((tm, d_model), lambda i: (i, 0)), pl.BlockSpec((1, d_model), lambda i: (0, 0)),
                  pl.BlockSpec(w_qkv.shape, lambda i: (0, 0), pipeline_mode=pl.Buffered(1))]
        + [const(t) for t in tables],
        out_specs=[pl.BlockSpec((tm, d_model), lambda i: (i, 0))] + layouts_spec * 3 + [tail_spec] * 2,
        out_shape=[jax.ShapeDtypeStruct((s_len, d_model), _BF16)] + layouts_shape * 3 + [tail_shape] * 2,
        scratch_shapes=[pltpu.VMEM((d_seg // LANES, min(SUB_ROWS, tm), LANES), _F32)] * 2,
        compiler_params=_params("arbitrary"),
        name="qkv_proj",
    )(x, norm_g.reshape(1, -1), w_qkv, *tables)


def _attn_kernel(q_ref, kp_ref, kc_ref, vp_ref, vc_ref, o_ref, stat_ref, k0_ref, *vext_refs, res_blocks):
    n = pl.program_id(1)
    first_step = jnp.logical_and(pl.program_id(0) == 0, n == 0)
    n_pairs = q_ref.shape[1] // LANES
    two = 2 * BLOCK

    @pl.when(first_step)
    def _():
        ones = jnp.ones((two, LANES), k0_ref.dtype)
        for vext_ref in vext_refs:
            for hp in range(n_pairs):
                vext_ref[:, (2 * hp + 1) * LANES:(2 * hp + 2) * LANES] = ones

    k0_ref[0:BLOCK, :] = kp_ref[...]
    k0_ref[BLOCK:, :] = kc_ref[0:BLOCK, :]

    qi = lax.broadcasted_iota(jnp.int32, (two, two), 0) % BLOCK
    si = lax.broadcasted_iota(jnp.int32, (two, two), 1)
    dist = qi + BLOCK - si
    band_bias = jnp.where((dist >= 0) & (dist <= BLOCK), 0.0, NEG_INF).astype(_F32)
    lane = lax.broadcasted_iota(jnp.int32, (BLOCK, LANES), 1)
    head0 = lane < HEAD_DIM
    zero = jnp.zeros((BLOCK, LANES), q_ref.dtype)

    for j, vext_ref in enumerate(vext_refs):
        rows = slice(j * BLOCK, (j + 1) * BLOCK)
        prev_keys = slice((j - 1) * BLOCK, (j + 1) * BLOCK)
        bias = band_bias
        if len(vext_refs) % res_blocks == 0:
            if j % res_blocks == 0:
                bias = band_bias + jnp.where(si < BLOCK, NEG_INF, 0.0)
        elif j == 0:
            bias = band_bias + jnp.where(jnp.logical_and(si < BLOCK, n == 0), NEG_INF, 0.0)
        for hp in range(n_pairs):
            sl = slice(hp * LANES, (hp + 1) * LANES)
            ext = slice(2 * hp * LANES, (2 * hp + 1) * LANES)
            if j == 0:
                vext_ref[0:BLOCK, ext] = vp_ref[:, sl]
                vext_ref[BLOCK:, ext] = vc_ref[0:BLOCK, sl]
            else:
                vext_ref[:, ext] = vc_ref[prev_keys, sl]
        stat_acc = jnp.where(lane < 2 * n_pairs, 0.0, 1.0)
        for hp in range(n_pairs):
            sl = slice(hp * LANES, (hp + 1) * LANES)
            q2 = q_ref[rows, sl]
            qs = jnp.concatenate([jnp.where(head0, q2, zero), jnp.where(head0, zero, q2)], axis=0)
            k2 = k0_ref[:, sl] if j == 0 else kc_ref[prev_keys, sl]
            s = lax.dot_general(qs, k2, (((1,), (1,)), ((), ())), preferred_element_type=_F32) + bias
            m = jnp.max(s, axis=-1, keepdims=True)
            p = jnp.exp2(s - m).astype(vext_ref.dtype)
            pv = jnp.dot(p, vext_ref[:, 2 * hp * LANES:(2 * hp + 2) * LANES], preferred_element_type=_F32)
            num, den = pv[:, :LANES], pv[:, LANES:]
            o_ref[rows, sl] = jnp.where(head0, num[:BLOCK], num[BLOCK:]).astype(o_ref.dtype)
            for h, (mh, dh) in ((2 * hp, (m[:BLOCK], den[:BLOCK])), (2 * hp + 1, (m[BLOCK:], den[BLOCK:]))):
                stat_acc = jnp.where(lane == h, mh, jnp.where(lane == 2 * n_pairs + h, dh, stat_acc))
        stat_ref[rows, :] = stat_acc


def _dilated_attention(q, k, v, *, q_rows):
    d_orig, res_len, d_attn = q.shape
    res_blocks = res_len // BLOCK
    if q_rows > res_len:
        q, k, v = (a.reshape(d_orig * res_len // q_rows, q_rows, d_attn) for a in (q, k, v))
    d, sub_len, _ = q.shape
    per = q_rows // BLOCK
    assert per % res_blocks == 0 or res_blocks % per == 0
    cur = pl.BlockSpec((None, q_rows, d_attn), lambda r, n: (r, n, 0))
    prev = pl.BlockSpec((None, BLOCK, d_attn), lambda r, n: (r, jnp.maximum(n * per - 1, 0), 0))
    stat = pl.BlockSpec((None, q_rows, LANES), lambda r, n: (r, n, 0))
    stat_shape = jax.ShapeDtypeStruct((d, sub_len, LANES), _F32)
    o, st = pl.pallas_call(
        functools.partial(_attn_kernel, res_blocks=res_blocks),
        grid=(d, sub_len // q_rows),
        in_specs=[cur, prev, cur, prev, cur],
        out_specs=[cur, stat],
        out_shape=[jax.ShapeDtypeStruct((d, sub_len, d_attn), q.dtype), stat_shape],
        scratch_shapes=[pltpu.VMEM((2 * BLOCK, d_attn), q.dtype)]
        + [pltpu.VMEM((2 * BLOCK, 2 * d_attn), q.dtype)] * per,
        compiler_params=_params("arbitrary", "arbitrary"),
        name=f"dilated_attn_d{d_orig}",
    )(q, k, k, v, v)
    return o.reshape(d_orig, res_len, d_attn), st.reshape(d_orig, res_len, LANES)


def _out_proj_kernel(o1_ref, o4_ref, o16_ref, s1_ref, s4_ref, s16_ref,
                     ga_ref, c_ref, x_ref, wb_ref, fg_ref, y_ref,
                     a_ref, n4_ref, n16_ref, tmp_ref, st_ref):
    tm, d_attn = o1_ref.shape
    n_heads = d_attn // HEAD_DIM
    n_lane_groups = d_attn // LANES
    sub = SUB_ROWS
    n4, n16 = sub // 4, sub // 16

    pair_idx = lax.broadcasted_iota(jnp.int32, (sub, LANES), 1) // HEAD_DIM
    head_lane = lax.broadcasted_iota(jnp.int32, (sub, LANES), 1) < n_heads

    for t in range(tm // sub):
        rs = slice(t * sub, (t + 1) * sub)
        for r in range(4):
            dst = pl.ds(t * sub + r, n4, stride=4)
            src = slice(t * n4, (t + 1) * n4)
            st_ref[0, dst, :] = s4_ref[r, src, :]
            for g in range(n_lane_groups):
                n4_ref[g, dst, :] = o4_ref[r, src, g * LANES:(g + 1) * LANES].astype(_F32)
        for r in range(16):
            dst = pl.ds(t * sub + r, n16, stride=16)
            src = slice(t * n16, (t + 1) * n16)
            st_ref[1, dst, :] = s16_ref[r, src, :]
        for g in range(n_lane_groups):
            sl = slice(g * LANES, (g + 1) * LANES)
            for r4 in range(4):
                for a in range(4):
                    blk = o16_ref[r4 + 4 * a, t * n16:(t + 1) * n16, sl].astype(_F32)
                    tmp_ref[g, pl.ds(t * sub + r4 * n4 + a, n16, stride=4), :] = blk
                n16_ref[g, pl.ds(t * sub + r4, n4, stride=4), :] = tmp_ref[g, t * sub + r4 * n4:
                                                                          t * sub + (r4 + 1) * n4, :]

        ms = [s1_ref[rs, :], st_ref[0, rs, :], st_ref[1, rs, :]]
        dens = [pltpu.roll(s, LANES - n_heads, 1) for s in ms]
        mx = jnp.maximum(jnp.maximum(ms[0], ms[1]), ms[2])
        es = [jnp.exp2(m - mx) for m in ms]
        tot = es[0] * dens[0] + es[1] * dens[1] + es[2] * dens[2]
        tot = jnp.where(head_lane, tot, 1.0)
        ws = [e / tot for e in es]
        for g in range(n_lane_groups):
            sl = slice(g * LANES, (g + 1) * LANES)
            w1, w4, w16 = (jnp.take_along_axis(w, pair_idx + 2 * g, axis=1) for w in ws)
            a = w1 * o1_ref[rs, sl].astype(_F32) + w4 * n4_ref[g, rs, :] + w16 * n16_ref[g, rs, :]
            a_ref[rs, sl] = (a * ga_ref[rs, sl].astype(_F32)).astype(a_ref.dtype)

    for t in range(tm // sub):
        rs = slice(t * sub, (t + 1) * sub)
        acc = jnp.dot(a_ref[rs, :], wb_ref[:d_attn, :], preferred_element_type=_F32)
        acc = acc + jnp.dot(c_ref[rs, :], wb_ref[d_attn:, :], preferred_element_type=_F32)
        xo = x_ref[rs, :] + acc
        msq = jnp.mean(xo * xo, axis=-1, keepdims=True)
        y_ref[rs, :] = xo * lax.rsqrt(msq + EPS) * fg_ref[...]


def _out_proj(os, stats, ga, c, x, w_out, final_g, *, tm):
    m, d_model = x.shape
    d_attn = ga.shape[1]
    d_chunk = c.shape[1]
    rows = lambda n: pl.BlockSpec((tm, n), lambda i: (i, 0))
    res = lambda d, n: pl.BlockSpec((d, tm // d, n), lambda i: (0, i, 0))
    by_pattern = lambda n: [rows(n), res(4, n), res(16, n)]
    fg = final_g.reshape(1, -1)
    slab = pltpu.VMEM((d_attn // LANES, tm, LANES), _F32)
    return pl.pallas_call(
        _out_proj_kernel,
        grid=(m // tm,),
        in_specs=by_pattern(d_attn) + by_pattern(LANES)
        + [rows(d_attn), rows(d_chunk), rows(d_model),
           pl.BlockSpec(w_out.shape, lambda i: (0, 0), pipeline_mode=pl.Buffered(1)),
           pl.BlockSpec(fg.shape, lambda i: (0, 0))],
        out_specs=rows(d_model),
        out_shape=jax.ShapeDtypeStruct((m, d_model), _F32),
        scratch_shapes=[pltpu.VMEM((tm, d_attn), _BF16), slab, slab, slab,
                        pltpu.VMEM((2, tm, LANES), _F32)],
        compiler_params=_params("arbitrary"),
        name="out_proj",
    )(*os, *stats, ga, c, x, w_out, fg)


def _sample_in_proj_kernel(x_ref, g_ref, w_ref, cos_ref, sin_ref, lng_ref, lnb_ref,
                           q_ref, k_ref, v_ref, ga_ref, u_ref, vc_ref, gb_ref, wqkv_ref, wgate_ref, xn_ref):
    j = pl.program_id(0)

    @pl.when(j == 0)
    def _():
        xf = x_ref[...]
        ms = jnp.mean(xf * xf, axis=-1, keepdims=True)
        xn_ref[...] = (xf * lax.rsqrt(ms + EPS) * g_ref[...]).astype(xn_ref.dtype)

    wb = w_ref[...].astype(_BF16)

    @pl.when(j <= SEG_V)
    def _():
        wqkv_ref[...] = wb

    @pl.when(j > SEG_V)
    def _():
        wgate_ref[...] = wb

    z = jnp.dot(xn_ref[...], wb, preferred_element_type=_F32)

    def rope_to(ref, scale):
        for g in range(z.shape[1] // LANES):
            sl = slice(g * LANES, (g + 1) * LANES)
            ref[:, sl] = _rope_group(z[:, sl], cos_ref[...], sin_ref[...]) * scale

    @pl.when(j == SEG_Q)
    def _():
        rope_to(q_ref, HEAD_DIM ** -0.5)

    @pl.when(j == SEG_K)
    def _():
        rope_to(k_ref, 1.0)

    @pl.when(j == SEG_V)
    def _():
        v_ref[...] = z

    @pl.when(j == SEG_GA)
    def _():
        ga_ref[...] = jax.nn.silu(z)

    @pl.when(j == SEG_U)
    def _():
        u_ref[...] = jax.nn.gelu(z)

    @pl.when(j == SEG_VC)
    def _():
        vc_ref[...] = _gelu_layer_norm(z, lng_ref[...], lnb_ref[...])

    @pl.when(j == SEG_GB)
    def _():
        gb_ref[...] = jax.nn.silu(z)


def _sample_in_proj(x, norm_g, w_in, cos_t, sin_t, ln_g, ln_b):
    m, d_model = x.shape
    d_seg = w_in.shape[1] // N_SEGMENTS
    whole = lambda r, c: pl.BlockSpec((r, c), lambda j: (0, 0))
    return pl.pallas_call(
        _sample_in_proj_kernel,
        grid=(N_SEGMENTS,),
        in_specs=[whole(m, d_model), whole(1, d_model),
                  pl.BlockSpec((d_model, d_seg), lambda j: (0, j)),
                  whole(m, LANES), whole(m, LANES), whole(1, d_seg), whole(1, d_seg)],
        out_specs=[whole(m, d_seg)] * N_SEGMENTS
        + [pl.BlockSpec((d_model, d_seg), lambda j: (0, jnp.minimum(j, SEG_V))),
           pl.BlockSpec((None, d_model, d_seg), lambda j: (jnp.maximum(j - SEG_GA, 0), 0, 0))],
        out_shape=[jax.ShapeDtypeStruct((m, d_seg), _F32)] * N_SEGMENTS
        + [jax.ShapeDtypeStruct((d_model, (SEG_V + 1) * d_seg), _BF16),
           jax.ShapeDtypeStruct((N_SEGMENTS - SEG_GA, d_model, d_seg), _BF16)],
        scratch_shapes=[pltpu.VMEM((m, d_model), _BF16)],
        compiler_params=_params("arbitrary"),
        name="sample_in_proj",
    )(x, norm_g.reshape(1, -1), w_in, cos_t, sin_t, ln_g.reshape(1, -1), ln_b.reshape(1, -1))


def _key_multiplicity(rows, wb, t_new):
    t_row = lax.broadcasted_iota(jnp.int32, (rows, wb), 0) // (rows // t_new)
    pos = lax.broadcasted_iota(jnp.int32, (rows, wb), 1)
    dist = wb + t_row - pos
    mult = jnp.zeros((rows, wb), _F32)
    for window, d in DILATIONS:
        mult = mult + jnp.where((dist % d == 0) & (dist <= window), 1.0, 0.0)
    return mult


def _sample_attn_tile(q_ref, kn_ref, vn_ref, kt_ref, vt_ref, o_ref, mult_ref, *, row0, t_new):
    width = q_ref.shape[1]
    n_heads = width // HEAD_DIM
    rows = t_new * n_heads
    new_row = lambda ref, t: ref[row0 + t:row0 + t + 1, :]
    head_row = lax.broadcasted_iota(jnp.int32, (n_heads, width), 0)
    head_col = lax.broadcasted_iota(jnp.int32, (n_heads, width), 1) // HEAD_DIM
    own = head_row == head_col
    qbd = jnp.concatenate([jnp.where(own, new_row(q_ref, t), 0.0) for t in range(t_new)], axis=0)
    mult = mult_ref[...]

    wb = kt_ref.shape[1]
    chunks = [slice(c * SAMPLE_POS_CHUNK, (c + 1) * SAMPLE_POS_CHUNK) for c in range(wb // SAMPLE_POS_CHUNK)]
    qbd_b = qbd.astype(_BF16)
    s = jnp.concatenate([jnp.dot(qbd_b, kt_ref[:, c].astype(_BF16), preferred_element_type=_F32)
                         for c in chunks], axis=1)
    yield
    s = jnp.where(mult > 0.0, s, NEG_INF)
    m = jnp.max(s, axis=-1, keepdims=True)
    t_col = lax.broadcasted_iota(jnp.int32, (rows, 1), 0) // n_heads
    s_new, mult_new = [], []
    for tp in range(t_new):
        s_new.append(jnp.sum(qbd * new_row(kn_ref, tp), axis=-1, keepdims=True))
        dn = t_col - tp
        mn = jnp.zeros((rows, 1), _F32)
        for window, d in DILATIONS:
            mn = mn + jnp.where((dn >= 0) & (dn % d == 0), 1.0, 0.0)
        mult_new.append(mn)
        m = jnp.maximum(m, jnp.where(mn > 0.0, s_new[tp], NEG_INF))
    p = jnp.exp(s - m) * mult
    den = jnp.sum(p, axis=-1, keepdims=True)
    yield
    p_b = p.astype(_BF16)
    num = sum(lax.dot_general(p_b[:, c], vt_ref[:, c].astype(_BF16), (((1,), (1,)), ((), ())),
                              preferred_element_type=_F32) for c in chunks)
    for tp in range(t_new):
        pn = jnp.where(mult_new[tp] > 0.0, jnp.exp(s_new[tp] - m), 0.0) * mult_new[tp]
        den = den + pn
        num = num + pn * new_row(vn_ref, tp)
    res = num / den
    for t in range(t_new):
        blk = res[t * n_heads:(t + 1) * n_heads, :]
        o_ref[row0 + t:row0 + t + 1, :] = jnp.sum(jnp.where(own, blk, 0.0), axis=0, keepdims=True)


def _sample_out_kernel(oa_ref, ga_ref, u_ref, vc_ref, gb_ref, x_ref, wout_ref, coef_ref, bias_ref, fg_ref,
                       y_ref, wb_ref, ac_ref, acc_ref, *, t_new):
    k = pl.program_id(0)
    d_attn = oa_ref.shape[1]
    kc = wout_ref.shape[0]

    @pl.when(k == 0)
    def _():
        ac_ref[:, :d_attn] = (oa_ref[...] * ga_ref[...]).astype(ac_ref.dtype)
        vc = vc_ref[...]
        mixed = bias_ref[...] + coef_ref[0] * vc
        for delta in range(1, t_new):
            mixed = mixed + coef_ref[delta] * pltpu.roll(vc, delta, 0)
        ac_ref[:, d_attn:] = (u_ref[...] * mixed * gb_ref[...]).astype(ac_ref.dtype)
        acc_ref[...] = x_ref[...]

    wb = wout_ref[...].astype(wb_ref.dtype)
    wb_ref[...] = wb
    col0 = pl.multiple_of(k * kc, kc)
    acc_ref[...] += jnp.dot(ac_ref[:, pl.ds(col0, kc)], wb, preferred_element_type=_F32)

    @pl.when(k == pl.num_programs(0) - 1)
    def _():
        xo = acc_ref[...]
        ms = jnp.mean(xo * xo, axis=-1, keepdims=True)
        y_ref[...] = xo * lax.rsqrt(ms + EPS) * fg_ref[...]


def _sample_out(oa, ga, u, vc, gb, x, w_out, coef, bias, final_g):
    m, d_model = x.shape
    args = (oa, ga, u, vc, gb, x, w_out, coef, bias, final_g.reshape(1, -1))
    whole = lambda a: pl.BlockSpec(a.shape, lambda k: (0,) * a.ndim)
    k_chunk = w_out.shape[0] // SAMPLE_OUT_K_CHUNKS
    w_spec = pl.BlockSpec((k_chunk, d_model), lambda k: (k, 0))
    return pl.pallas_call(
        functools.partial(_sample_out_kernel, t_new=coef.shape[0]),
        grid=(SAMPLE_OUT_K_CHUNKS,),
        in_specs=[w_spec if a is w_out else whole(a) for a in args],
        out_specs=[pl.BlockSpec((m, d_model), lambda k: (0, 0)), w_spec],
        out_shape=[jax.ShapeDtypeStruct((m, d_model), _F32), jax.ShapeDtypeStruct(w_out.shape, _BF16)],
        scratch_shapes=[pltpu.VMEM((m, w_out.shape[0]), _BF16), pltpu.VMEM((m, d_model), _F32)],
        compiler_params=_params("arbitrary"),
        name="sample_out",
    )(*args)


def kernel(x_prompt, x_sample, cache_k, cache_v, norm_g, w_in, ln_g, ln_b, w_s, b_s, w_out, final_g):
    batch, s_len, d_model = x_prompt.shape
    db, t_new, _ = x_sample.shape
    depth, _, wb, n_heads, head_dim = cache_k.shape
    d_attn = n_heads * head_dim
    d_chunk = w_out.shape[1] - d_attn
    assert batch == 1 and depth == 1 and head_dim == HEAD_DIM
    assert w_in.shape[2] == 4 * d_attn + 3 * d_chunk and d_attn == d_chunk
    tail = min(MAX_WINDOW, s_len)
    w_in0, w_out0 = w_in[0], w_out[0]
    ng, lg, lb = norm_g[0], ln_g[0], ln_b[0]

    xs = x_sample.reshape(db * t_new, d_model)
    pos_s = PAST_LEN + jnp.tile(jnp.arange(t_new, dtype=jnp.int32), db)
    cos_s, sin_s = _rope_tables(pos_s)
    qs, ks, vs, gas, us, vcs, gbs, w_qkv_b, w_gate_b = _sample_in_proj(xs, ng, w_in0, cos_s, sin_s, lg, lb)
    cache_t = lambda c: jnp.transpose(c[0], (0, 2, 3, 1)).reshape(db, d_attn, wb)
    cache_kt, cache_vt = cache_t(cache_k), cache_t(cache_v)
    assert wb == MAX_WINDOW

    xp = x_prompt.reshape(s_len, d_model)
    (xn, q1, q4, q16, k1, k4, k16, v1, v4, v16, k_tail_t, v_tail_t) = _qkv_proj(
        xp, ng, w_qkv_b, tm=256, q_scale=HEAD_DIM ** -0.5 * math.log2(math.e), tail_rows=tail)
    ga, c, oa = _gates_proj(xn, w_gate_b, (lg, lb), w_s[0], b_s[0].T,
                            (qs, ks, vs, cache_kt, cache_vt), tm=512)
    o1, s1 = _dilated_attention(q1[None], k1[None], v1[None], q_rows=1024)
    o4, s4 = _dilated_attention(q4, k4, v4, q_rows=1024)
    o16, s16 = _dilated_attention(q16, k16, v16, q_rows=1024)

    w_ts = jnp.transpose(w_s[0][:, :t_new, :t_new], (1, 2, 0))
    coef = jnp.stack([jnp.stack([w_ts[t, t - delta] if t >= delta else jnp.zeros_like(w_ts[0, 0])
                                 for t in range(t_new)]) for delta in range(t_new)])
    coef = jnp.tile(jnp.repeat(coef, GROUP_WIDTH_B, axis=2), (1, db, 1))
    bias = jnp.tile(jnp.repeat(jnp.transpose(b_s[0][:, :t_new], (1, 0)), GROUP_WIDTH_B, axis=1), (db, 1))
    ys, w_out_b = _sample_out(oa, gas, us, vcs, gbs, xs, w_out0, coef, bias, final_g)
    y_sample = ys.reshape(db, t_new, d_model)

    y_prompt = _out_proj((o1[0], o4, o16), (s1[0], s4, s16), ga, c, xp, w_out_b, final_g, tm=512)
    untranspose = lambda a: jnp.transpose(a.reshape(n_heads, head_dim, tail), (2, 0, 1))

    hs = (n_heads, head_dim)
    return (
        y_prompt.reshape(batch, s_len, d_model),
        y_sample,
        untranspose(k_tail_t).reshape(depth, batch, tail, *hs),
        untranspose(v_tail_t).reshape(depth, batch, tail, *hs),
        ks.reshape(depth, db, t_new, *hs),
        vs.reshape(depth, db, t_new, *hs),
        vcs.reshape(depth, db, t_new, d_chunk),
    )
```
